```python
import math
import jax, jax.numpy as jnp
from jax import lax
import numpy as np

D_MODEL = 2048
BATCH = 1
SEQ = 8192
DEPTH = 2
DEC_BATCH = 128
DEC_SEQ = 1
PAST_LEN = 2048
PAGE_SIZE = 128

A_HEADS = 16
A_KV_HEADS = 4
A_GROUP = A_HEADS // A_KV_HEADS
HEAD_DIM = 64
A_WIDTH = A_HEADS * HEAD_DIM
CMP_BLOCK = 32
CMP_STRIDE = 16
CMP_RATIO = CMP_BLOCK // CMP_STRIDE
CMP_HIDDEN = 128
SLC_BLOCK = 64
SLC_TOP = 16
WINDOW = 512
Q_BLOCK = 128
N_KV_BRANCH = 6
B_HEADS = 8
QK_DIM = 64
V_DIM = 128
B_WIDTH = B_HEADS * V_DIM
MLSTM_CHUNK = 64
FORGET_BIAS = 3.0
D_FF = 4 * D_MODEL
EPS = 1e-6
FORCE = 1e4
NEG = -1e30

SPLITS = (A_WIDTH, N_KV_BRANCH * A_KV_HEADS * HEAD_DIM, 3 * A_HEADS,
          B_HEADS * QK_DIM, B_HEADS * QK_DIM, B_WIDTH, B_HEADS, B_HEADS, B_WIDTH, 2 * D_MODEL)
N_IN = sum(SPLITS)

kernel_name = 'nsa_mlstm_parallel_decoder_step'


def rmsnorm(x, g):
    xf = x.astype(jnp.float32)
    y = xf * lax.rsqrt(jnp.mean(xf * xf, axis=-1, keepdims=True) + EPS)
    return (y * g.astype(jnp.float32)).astype(x.dtype)


def alibi_slopes():
    return 2.0 ** (-8.0 * jnp.arange(1, A_HEADS + 1, dtype=jnp.float32) / A_HEADS)


def masked_softmax(s, mask):
    s = jnp.where(mask, s.astype(jnp.float32), NEG)
    return jnp.where(mask, jax.nn.softmax(s, axis=-1), 0.0)


def split_in(z):
    bounds = [int(b) for b in np.cumsum(SPLITS)[:-1]]
    return jnp.split(z, bounds, axis=-1)


def project_in(x, g, w_in, b_in):
    B, T = x.shape[:2]
    qa, kva, ga, qb, kb, vb, ib, fb, ob, gm = split_in(rmsnorm(x, g) @ w_in + b_in)
    return (qa.reshape(B, T, A_HEADS, HEAD_DIM), ga.reshape(B, T, A_HEADS, 3),
            kva.reshape(B, T, N_KV_BRANCH, A_KV_HEADS, HEAD_DIM),
            qb.reshape(B, T, B_HEADS, QK_DIM), kb.reshape(B, T, B_HEADS, QK_DIM),
            vb.reshape(B, T, B_HEADS, V_DIM), ib, fb, ob, gm)


def compress(rows, pe, w1, b1, w2):
    B, L = rows.shape[:2]
    nc = (L - CMP_BLOCK) // CMP_STRIDE + 1
    nseg = nc + CMP_RATIO - 1
    seg = rows[:, :nseg * CMP_STRIDE].reshape(B, nseg, CMP_STRIDE, 2, A_KV_HEADS, HEAD_DIM)
    w1r = w1.reshape(2, CMP_RATIO, CMP_STRIDE, HEAD_DIM, CMP_HIDDEN)
    y = jnp.einsum('bnjckd,crjde->rbncke', seg, w1r)
    pre = sum(y[r][:, r:r + nc] for r in range(CMP_RATIO))
    pe_bias = jnp.einsum('cjd,cjde->ce', pe, w1.reshape(2, CMP_BLOCK, HEAD_DIM, CMP_HIDDEN)) + b1
    hid = jax.nn.relu(pre + pe_bias[:, None, :])
    out = jnp.einsum('bncke,ced->bnckd', hid, w2)
    c_end = jnp.arange(nc) * CMP_STRIDE + (CMP_BLOCK - 1)
    return out, c_end


def nsa_attend(q, gl, q_pos, ck, cv, c_end, sk, sv, wk, wv, w_pos):
    B, T = q.shape[:2]
    qg = q.reshape(B, T, A_KV_HEADS, A_GROUP, HEAD_DIM) * (HEAD_DIM ** -0.5)
    slope = alibi_slopes().reshape(A_KV_HEADS, A_GROUP)
    t = q_pos
    dist_c = (t[:, None] - c_end[None, :]).astype(jnp.float32)
    s_c = jnp.einsum('btkgd,bnkd->btkgn', qg, ck) - slope[None, None, :, :, None] * dist_c[None, :, None, None, :]
    p_c = masked_softmax(s_c, (dist_c >= 0)[None, :, None, None, :])
    o_c = jnp.einsum('btkgn,bnkd->btkgd', p_c, cv)
    ns = sk.shape[2]
    c_start = c_end - (CMP_BLOCK - 1)
    s_start = jnp.arange(ns) * SLC_BLOCK
    overlap = ((c_start[:, None] < s_start[None, :] + SLC_BLOCK) & (c_end[:, None] >= s_start[None, :])).astype(jnp.float32)
    imp = jnp.einsum('btkgn,ns->btks', p_c, overlap)
    blk = jnp.arange(ns)[None, :]
    cur = (t // SLC_BLOCK)[:, None]
    forced = (blk == 0) | (blk == cur) | (blk == cur - 1)
    causal = s_start[None, :] <= t[:, None]
    score = jnp.where(forced[None, :, None, :], FORCE, jnp.where(causal[None, :, None, :], imp, -FORCE))
    _, idx = lax.top_k(score, min(SLC_TOP, ns))
    b_i = jnp.arange(B)[:, None, None, None]
    k_i = jnp.arange(A_KV_HEADS)[None, None, :, None]
    sel_k = sk[b_i, k_i, idx]
    sel_v = sv[b_i, k_i, idx]
    pos_s = idx[..., None] * SLC_BLOCK + jnp.arange(SLC_BLOCK)
    dist_s = (t[None, :, None, None, None] - pos_s).astype(jnp.float32)
    s_s = jnp.einsum('btkgd,btknjd->btkgnj', qg, sel_k) - slope[None, None, :, :, None, None] * dist_s[:, :, :, None]
    mask_s = jnp.broadcast_to((dist_s >= 0)[:, :, :, None], s_s.shape).reshape(B, T, A_KV_HEADS, A_GROUP, -1)
    p_s = masked_softmax(s_s.reshape(B, T, A_KV_HEADS, A_GROUP, -1), mask_s)
    o_s = jnp.einsum('btkgm,btkmd->btkgd', p_s, sel_v.reshape(B, T, A_KV_HEADS, -1, HEAD_DIM))
    dist_w = t[:, None] - w_pos[None, :]
    mask_w = (dist_w >= 0) & (dist_w < WINDOW) & (w_pos[None, :] >= 0)
    s_w = jnp.einsum('btkgd,blkd->btkgl', qg, wk) - slope[None, None, :, :, None] * dist_w.astype(jnp.float32)[None, :, None, None, :]
    p_w = masked_softmax(s_w, mask_w[None, :, None, None, :])
    o_w = jnp.einsum('btkgl,blkd->btkgd', p_w, wv)
    g = jax.nn.sigmoid(gl.astype(jnp.float32)).reshape(B, T, A_KV_HEADS, A_GROUP, 3)
    o = g[..., 0:1] * o_c + g[..., 1:2] * o_s + g[..., 2:3] * o_w
    return o.reshape(B, T, A_WIDTH).astype(q.dtype)


def nsa_prompt(q, gl, kva, pe, w1, b1, w2):
    B, T = q.shape[:2]
    ckv, c_end = compress(kva[:, :, 0:2], pe, w1, b1, w2)
    ck, cv = ckv[:, :, 0], ckv[:, :, 1]
    ns = T // SLC_BLOCK
    sk = kva[:, :, 2].reshape(B, ns, SLC_BLOCK, A_KV_HEADS, HEAD_DIM).transpose(0, 3, 1, 2, 4)
    sv = kva[:, :, 3].reshape(B, ns, SLC_BLOCK, A_KV_HEADS, HEAD_DIM).transpose(0, 3, 1, 2, 4)
    pad = ((0, 0), (WINDOW, 0), (0, 0), (0, 0))
    wk_pad = jnp.pad(kva[:, :, 4], pad)
    wv_pad = jnp.pad(kva[:, :, 5], pad)
    nqb = T // Q_BLOCK
    qb = q.reshape(B, nqb, Q_BLOCK, A_HEADS, HEAD_DIM).swapaxes(0, 1)
    gb = gl.reshape(B, nqb, Q_BLOCK, A_HEADS, 3).swapaxes(0, 1)
    starts = jnp.arange(nqb) * Q_BLOCK

    def block(args):
        qi, gi, s0 = args
        q_pos = s0 + jnp.arange(Q_BLOCK)
        wk = lax.dynamic_slice_in_dim(wk_pad, s0, WINDOW + Q_BLOCK, axis=1)
        wv = lax.dynamic_slice_in_dim(wv_pad, s0, WINDOW + Q_BLOCK, axis=1)
        w_pos = s0 - WINDOW + jnp.arange(WINDOW + Q_BLOCK)
        return nsa_attend(qi, gi, q_pos, ck, cv, c_end, sk, sv, wk, wv, w_pos)

    o = lax.map(block, (qb, gb, starts))
    return o.swapaxes(0, 1).reshape(B, T, A_WIDTH)


def nsa_sample(q, gl, kva, pool, page_table, win_buf, pe, w1, b1, w2):
    B, T = q.shape[:2]
    past = pool[page_table].reshape(B, PAST_LEN, 4, A_KV_HEADS, HEAD_DIM)
    full = jnp.concatenate([past, kva[:, :, 0:4]], axis=1)
    L = PAST_LEN + T
    ckv, c_end = compress(full[:, :, 0:2], pe, w1, b1, w2)
    ns = -(-L // SLC_BLOCK)
    slc = jnp.pad(full[:, :, 2:4], ((0, 0), (0, ns * SLC_BLOCK - L), (0, 0), (0, 0), (0, 0)))
    slc = slc.reshape(B, ns, SLC_BLOCK, 2, A_KV_HEADS, HEAD_DIM).transpose(3, 0, 4, 1, 2, 5)
    W = win_buf.shape[1]
    win = jnp.concatenate([win_buf, kva[:, :, 4:6]], axis=1)
    w_pos = PAST_LEN - W + jnp.arange(W + T)
    q_pos = PAST_LEN + jnp.arange(T)
    o = nsa_attend(q, gl, q_pos, ckv[:, :, 0], ckv[:, :, 1], c_end, slc[0], slc[1], win[:, :, 0], win[:, :, 1], w_pos)
    return o, win[:, -W:]


def mlstm(q, k, v, i_pre, f_pre, o_pre, C0, n0, m0):
    B, T = q.shape[:2]
    L = math.gcd(T, MLSTM_CHUNK)
    nch = T // L

    def chunks(a):
        a = a.astype(jnp.float32).reshape(B, nch, L, *a.shape[2:])
        return jnp.moveaxis(a, 1, 0).swapaxes(2, 3)

    xs = (chunks(q), chunks(k) * (QK_DIM ** -0.5), chunks(v), chunks(i_pre), chunks(f_pre))
    causal = jnp.tril(jnp.ones((L, L), dtype=bool))

    def step(carry, xc):
        C, n, m = carry
        qc, kc, vc, ic, fc = xc
        b = jnp.cumsum(jax.nn.log_sigmoid(fc), axis=-1)
        log_d = jnp.where(causal, b[..., :, None] - b[..., None, :] + ic[..., None, :], -jnp.inf)
        m_new = jnp.maximum(b + m[..., None], jnp.max(log_d, axis=-1))
        d = jnp.exp(log_d - m_new[..., None])
        inter = jnp.exp(b + m[..., None] - m_new)
        s = jnp.einsum('bhtd,bhsd->bhts', qc, kc) * d
        num = inter[..., None] * jnp.einsum('bhtd,bhde->bhte', qc, C) + jnp.einsum('bhts,bhse->bhte', s, vc)
        den = inter * jnp.einsum('bhtd,bhd->bht', qc, n) + jnp.sum(s, axis=-1)
        h = num / jnp.maximum(jnp.abs(den), jnp.exp(-m_new))[..., None]
        w_end = d[..., -1, :]
        C_new = inter[..., -1, None, None] * C + jnp.einsum('bhs,bhsd,bhse->bhde', w_end, kc, vc)
        n_new = inter[..., -1, None] * n + jnp.einsum('bhs,bhsd->bhd', w_end, kc)
        return (C_new, n_new, m_new[..., -1]), h

    init = (C0.astype(jnp.float32), n0.astype(jnp.float32), m0.astype(jnp.float32))
    (C, n, m), hs = lax.scan(step, init, xs)
    h = hs.transpose(1, 0, 3, 2, 4).reshape(B, T, B_WIDTH)
    h = jax.nn.sigmoid(o_pre.astype(jnp.float32)) * h
    return h.astype(q.dtype), C, n, m


def merge_and_mlp(x, ya, yb, gm, g, w_up_a, w_up_b, w_out, w_mlp1, w_mlp2):
    g_a, g_b = jnp.split(gm, 2, axis=-1)
    mix = jax.nn.sigmoid(g_a) * (ya @ w_up_a) + jax.nn.sigmoid(g_b) * (yb @ w_up_b)
    x = x + rmsnorm(mix @ w_out, g[1])
    h = rmsnorm(x, g[2])
    return x + rmsnorm(jnp.square(jax.nn.relu(h @ w_mlp1)) @ w_mlp2, g[3])


def setup_inputs(seed: int = 0) -> dict:
    key = jax.random.key(seed)
    ks = jax.random.split(key, 24)
    f32 = jnp.float32

    def nrm(k, shape, s):
        return jax.random.normal(k, shape, f32) * s

    n_pages = PAST_LEN // PAGE_SIZE
    n_pool = (DEC_BATCH * n_pages * 5) // 4
    w_sz = min(WINDOW, PAST_LEN)
    page_table = jax.random.permutation(ks[7], n_pool)[:DEC_BATCH * n_pages].reshape(DEC_BATCH, n_pages).astype(jnp.int32)
    f_lo = sum(SPLITS[:7])
    b_in = nrm(ks[10], (DEPTH, N_IN), 0.02).at[:, f_lo:f_lo + B_HEADS].add(FORGET_BIAS)
    return {
        'x_prompt': nrm(ks[0], (BATCH, SEQ, D_MODEL), 1.0),
        'x_sample': nrm(ks[1], (DEC_BATCH, DEC_SEQ, D_MODEL), 1.0),
        'cache_nsa_kv': nrm(ks[2], (DEPTH, n_pool, PAGE_SIZE, 4, A_KV_HEADS, HEAD_DIM), 1.0),
        'cache_win_kv': nrm(ks[3], (DEPTH, DEC_BATCH, w_sz, 2, A_KV_HEADS, HEAD_DIM), 1.0),
        'state_C': nrm(ks[4], (DEPTH, DEC_BATCH, B_HEADS, QK_DIM, V_DIM), 0.5),
        'state_n': nrm(ks[5], (DEPTH, DEC_BATCH, B_HEADS, QK_DIM), 0.5),
        'state_m': nrm(ks[6], (DEPTH, DEC_BATCH, B_HEADS), 0.5),
        'page_table': page_table,
        'norm_g': 1.0 + nrm(ks[8], (DEPTH, 4, D_MODEL), 0.05),
        'w_in': nrm(ks[9], (DEPTH, D_MODEL, N_IN), D_MODEL ** -0.5),
        'b_in': b_in,
        'cmp_pe': nrm(ks[11], (DEPTH, 2, CMP_BLOCK, HEAD_DIM), 0.1),
        'cmp_w1': nrm(ks[12], (DEPTH, 2, CMP_BLOCK * HEAD_DIM, CMP_HIDDEN), (CMP_BLOCK * HEAD_DIM) ** -0.5),
        'cmp_b1': nrm(ks[13], (DEPTH, 2, CMP_HIDDEN), 0.02),
        'cmp_w2': nrm(ks[14], (DEPTH, 2, CMP_HIDDEN, HEAD_DIM), (2.0 / CMP_HIDDEN) ** 0.5),
        'w_up_a': nrm(ks[15], (DEPTH, A_WIDTH, D_MODEL), A_WIDTH ** -0.5),
        'w_up_b': nrm(ks[16], (DEPTH, B_WIDTH, D_MODEL), B_WIDTH ** -0.5),
        'w_out': nrm(ks[17], (DEPTH, D_MODEL, D_MODEL), D_MODEL ** -0.5),
        'w_mlp1': nrm(ks[18], (DEPTH, D_MODEL, D_FF), D_MODEL ** -0.5),
        'w_mlp2': nrm(ks[19], (DEPTH, D_FF, D_MODEL), D_FF ** -0.5),
    }


def reference(x_prompt, x_sample, cache_nsa_kv, cache_win_kv, state_C, state_n, state_m, page_table,
              norm_g, w_in, b_in, cmp_pe, cmp_w1, cmp_b1, cmp_w2, w_up_a, w_up_b, w_out, w_mlp1, w_mlp2):
    xp, xs = x_prompt, x_sample
    kv_p, kv_s, win_p, win_s = [], [], [], []
    C_p, C_s, n_p, n_s, m_p, m_s = [], [], [], [], [], []
    for l in range(DEPTH):
        cmp = (cmp_pe[l], cmp_w1[l], cmp_b1[l], cmp_w2[l])
        mlp = (norm_g[l], w_up_a[l], w_up_b[l], w_out[l], w_mlp1[l], w_mlp2[l])
        qa, ga, kva, qb, kb, vb, ib, fb, ob, gm = project_in(xp, norm_g[l, 0], w_in[l], b_in[l])
        ya = nsa_prompt(qa, ga, kva, *cmp)
        B = xp.shape[0]
        C0 = jnp.zeros((B, B_HEADS, QK_DIM, V_DIM), jnp.float32)
        n0 = jnp.zeros((B, B_HEADS, QK_DIM), jnp.float32)
        m0 = jnp.zeros((B, B_HEADS), jnp.float32)
        yb, C, n, m = mlstm(qb, kb, vb, ib, fb, ob, C0, n0, m0)
        kv_p.append(kva[:, :, 0:4])
        win_p.append(kva[:, -min(WINDOW, xp.shape[1]):, 4:6])
        C_p.append(C)
        n_p.append(n)
        m_p.append(m)
        xp = merge_and_mlp(xp, ya, yb, gm, *mlp)
        qa, ga, kva, qb, kb, vb, ib, fb, ob, gm = project_in(xs, norm_g[l, 0], w_in[l], b_in[l])
        ya, win_new = nsa_sample(qa, ga, kva, cache_nsa_kv[l], page_table, cache_win_kv[l], *cmp)
        yb, C, n, m = mlstm(qb, kb, vb, ib, fb, ob, state_C[l], state_n[l], state_m[l])
        kv_s.append(kva[:, :, 0:4])
        win_s.append(win_new)
        C_s.append(C)
        n_s.append(n)
        m_s.append(m)
        xs = merge_and_mlp(xs, ya, yb, gm, *mlp)
    return (xp, xs, jnp.stack(kv_p), jnp.stack(kv_s), jnp.stack(win_p), jnp.stack(win_s),
            jnp.stack(C_p), jnp.stack(C_s), jnp.stack(n_p), jnp.stack(n_s), jnp.stack(m_p), jnp.stack(m_s))
```

```python
import functools
import math

import jax
import jax.numpy as jnp
import numpy as np
from jax import lax
from jax.experimental import pallas as pl
from jax.experimental.pallas import tpu as pltpu

D_MODEL = 2048
DEPTH = 2
PAST_LEN = 2048
A_HEADS = 16
A_KV_HEADS = 4
A_GROUP = A_HEADS // A_KV_HEADS
HEAD_DIM = 64
A_WIDTH = A_HEADS * HEAD_DIM
CMP_BLOCK = 32
CMP_STRIDE = 16
CMP_RATIO = CMP_BLOCK // CMP_STRIDE
CMP_HIDDEN = 128
SLC_BLOCK = 64
SLC_TOP = 16
WINDOW = 512
Q_BLOCK = 128
N_KV_BRANCH = 6
B_HEADS = 8
QK_DIM = 64
V_DIM = 128
B_WIDTH = B_HEADS * V_DIM
MLSTM_CHUNK = 64
D_FF = 4 * D_MODEL
EPS = 1e-6
FORCE = 1e4
NEG = -1e30

SPLITS = (A_WIDTH, N_KV_BRANCH * A_KV_HEADS * HEAD_DIM, 3 * A_HEADS,
          B_HEADS * QK_DIM, B_HEADS * QK_DIM, B_WIDTH, B_HEADS, B_HEADS, B_WIDTH, 2 * D_MODEL)
BOUNDS = tuple(int(b) for b in np.cumsum((0,) + SPLITS))

VMEM_LIMIT_BYTES = 56 * 1024 * 1024


def _params(*sem):
    return pltpu.CompilerParams(dimension_semantics=sem, vmem_limit_bytes=VMEM_LIMIT_BYTES)


def _norm_matmul_kernel(x_ref, g_ref, w_ref, b_ref, o_ref, h_ref, *, act):
    @pl.when(pl.program_id(1) == 0)
    def _():
        x = x_ref[...]
        r = lax.rsqrt(jnp.mean(x * x, axis=-1, keepdims=True) + EPS)
        h_ref[...] = (x * r * g_ref[...]).astype(h_ref.dtype)

    z = jnp.dot(h_ref[...], w_ref[...], preferred_element_type=jnp.float32) + b_ref[...]
    if act == "relu2":
        z = jnp.square(jnp.maximum(z, 0.0))
    o_ref[...] = z.astype(o_ref.dtype)


def norm_matmul(x, g, w, b, *, act=None, out_dtype=jnp.float32, tm=1024, tn=512):
    M, K = x.shape
    N = w.shape[1]
    tm = min(tm, M)
    tn = min(tn, N)
    assert M % tm == 0 and N % tn == 0, (M, N, tm, tn)
    return pl.pallas_call(
        functools.partial(_norm_matmul_kernel, act=act),
        grid=(M // tm, N // tn),
        in_specs=[
            pl.BlockSpec((tm, K), lambda i, j: (i, 0)),
            pl.BlockSpec((1, K), lambda i, j: (0, 0)),
            pl.BlockSpec((K, tn), lambda i, j: (0, j)),
            pl.BlockSpec((1, tn), lambda i, j: (0, j)),
        ],
        out_specs=pl.BlockSpec((tm, tn), lambda i, j: (i, j)),
        out_shape=jax.ShapeDtypeStruct((M, N), out_dtype),
        scratch_shapes=[pltpu.VMEM((tm, K), jnp.bfloat16)],
        compiler_params=_params("parallel", "arbitrary"),
        name="norm_matmul",
    )(x, g.reshape(1, K), w, b.reshape(1, N))


def _mix_kernel(ya_ref, yb_ref, wa_ref, wb_ref, ga_ref, gb_ref, o_ref):
    a = jnp.dot(ya_ref[...].astype(jnp.bfloat16), wa_ref[...], preferred_element_type=jnp.float32)
    b = jnp.dot(yb_ref[...].astype(jnp.bfloat16), wb_ref[...], preferred_element_type=jnp.float32)
    o_ref[...] = (jax.nn.sigmoid(ga_ref[...]) * a + jax.nn.sigmoid(gb_ref[...]) * b).astype(o_ref.dtype)


def mix_matmul(ya, yb, wa, wb, gm, *, tm=1024, tn=512):
    M, Ka = ya.shape
    Kb = yb.shape[1]
    N = wa.shape[1]
    tm = min(tm, M)
    nb = N // tn
    return pl.pallas_call(
        _mix_kernel,
        grid=(M // tm, nb),
        in_specs=[
            pl.BlockSpec((tm, Ka), lambda i, j: (i, 0)),
            pl.BlockSpec((tm, Kb), lambda i, j: (i, 0)),
            pl.BlockSpec((Ka, tn), lambda i, j: (0, j)),
            pl.BlockSpec((Kb, tn), lambda i, j: (0, j)),
            pl.BlockSpec((tm, tn), lambda i, j: (i, j)),
            pl.BlockSpec((tm, tn), lambda i, j: (i, j + nb)),
        ],
        out_specs=pl.BlockSpec((tm, tn), lambda i, j: (i, j)),
        out_shape=jax.ShapeDtypeStruct((M, N), jnp.bfloat16),
        compiler_params=_params("parallel", "arbitrary"),
        name="mix_matmul",
    )(ya, yb, wa, wb, gm, gm)


def _matmul_norm_res_kernel(a_ref, w_ref, x_ref, g_ref, o_ref, acc_ref):
    k = pl.program_id(1)

    @pl.when(k == 0)
    def _():
        acc_ref[...] = jnp.zeros_like(acc_ref)

    acc_ref[...] += jnp.dot(a_ref[...], w_ref[...], preferred_element_type=jnp.float32)

    @pl.when(k == pl.num_programs(1) - 1)
    def _():
        y = acc_ref[...]
        r = lax.rsqrt(jnp.mean(y * y, axis=-1, keepdims=True) + EPS)
        o_ref[...] = x_ref[...] + y * r * g_ref[...]


def matmul_norm_res(a, w, x, g, *, tm=1024, tk=512):
    M, K = a.shape
    N = w.shape[1]
    tm = min(tm, M)
    tk = min(tk, K)
    return pl.pallas_call(
        _matmul_norm_res_kernel,
        grid=(M // tm, K // tk),
        in_specs=[
            pl.BlockSpec((tm, tk), lambda i, k: (i, k)),
            pl.BlockSpec((tk, N), lambda i, k: (k, 0)),
            pl.BlockSpec((tm, N), lambda i, k: (i, 0)),
            pl.BlockSpec((1, N), lambda i, k: (0, 0)),
        ],
        out_specs=pl.BlockSpec((tm, N), lambda i, k: (i, 0)),
        out_shape=jax.ShapeDtypeStruct((M, N), jnp.float32),
        scratch_shapes=[pltpu.VMEM((tm, N), jnp.float32)],
        compiler_params=_params("parallel", "arbitrary"),
        name="matmul_norm_res",
    )(a, w, x, g.reshape(1, N))


def alibi_slopes():
    return 2.0 ** (-8.0 * jnp.arange(1, A_HEADS + 1, dtype=jnp.float32) / A_HEADS)


def masked_softmax(s, mask):
    s = jnp.where(mask, s.astype(jnp.float32), NEG)
    return jnp.where(mask, jax.nn.softmax(s, axis=-1), 0.0)


def compress(rows, pe, w1, b1, w2):
    B, L = rows.shape[:2]
    nc = (L - CMP_BLOCK) // CMP_STRIDE + 1
    nseg = nc + CMP_RATIO - 1
    seg = rows[:, :nseg * CMP_STRIDE].reshape(B, nseg, CMP_STRIDE, 2, A_KV_HEADS, HEAD_DIM)
    w1r = w1.reshape(2, CMP_RATIO, CMP_STRIDE, HEAD_DIM, CMP_HIDDEN)
    y = jnp.einsum('bnjckd,crjde->rbncke', seg, w1r)
    pre = sum(y[r][:, r:r + nc] for r in range(CMP_RATIO))
    pe_bias = jnp.einsum('cjd,cjde->ce', pe, w1.reshape(2, CMP_BLOCK, HEAD_DIM, CMP_HIDDEN)) + b1
    hid = jax.nn.relu(pre + pe_bias[:, None, :])
    out = jnp.einsum('bncke,ced->bnckd', hid, w2)
    c_end = jnp.arange(nc) * CMP_STRIDE + (CMP_BLOCK - 1)
    return out, c_end


def nsa_attend(q, gl, q_pos, ck, cv, c_end, sk, sv, wk, wv, w_pos):
    B, T = q.shape[:2]
    qg = q.reshape(B, T, A_KV_HEADS, A_GROUP, HEAD_DIM) * (HEAD_DIM ** -0.5)
    slope = alibi_slopes().reshape(A_KV_HEADS, A_GROUP)
    t = q_pos
    dist_c = (t[:, None] - c_end[None, :]).astype(jnp.float32)
    s_c = jnp.einsum('btkgd,bnkd->btkgn', qg, ck) - slope[None, None, :, :, None] * dist_c[None, :, None, None, :]
    p_c = masked_softmax(s_c, (dist_c >= 0)[None, :, None, None, :])
    o_c = jnp.einsum('btkgn,bnkd->btkgd', p_c, cv)
    ns = sk.shape[2]
    c_start = c_end - (CMP_BLOCK - 1)
    s_start = jnp.arange(ns) * SLC_BLOCK
    overlap = ((c_start[:, None] < s_start[None, :] + SLC_BLOCK) & (c_end[:, None] >= s_start[None, :])).astype(jnp.float32)
    imp = jnp.einsum('btkgn,ns->btks', p_c, overlap)
    blk = jnp.arange(ns)[None, :]
    cur = (t // SLC_BLOCK)[:, None]
    forced = (blk == 0) | (blk == cur) | (blk == cur - 1)
    causal = s_start[None, :] <= t[:, None]
    score = jnp.where(forced[None, :, None, :], FORCE, jnp.where(causal[None, :, None, :], imp, -FORCE))
    _, idx = lax.top_k(score, min(SLC_TOP, ns))
    b_i = jnp.arange(B)[:, None, None, None]
    k_i = jnp.arange(A_KV_HEADS)[None, None, :, None]
    sel_k = sk[b_i, k_i, idx]
    sel_v = sv[b_i, k_i, idx]
    pos_s = idx[..., None] * SLC_BLOCK + jnp.arange(SLC_BLOCK)
    dist_s = (t[None, :, None, None, None] - pos_s).astype(jnp.float32)
    s_s = jnp.einsum('btkgd,btknjd->btkgnj', qg, sel_k) - slope[None, None, :, :, None, None] * dist_s[:, :, :, None]
    mask_s = jnp.broadcast_to((dist_s >= 0)[:, :, :, None], s_s.shape).reshape(B, T, A_KV_HEADS, A_GROUP, -1)
    p_s = masked_softmax(s_s.reshape(B, T, A_KV_HEADS, A_GROUP, -1), mask_s)
    o_s = jnp.einsum('btkgm,btkmd->btkgd', p_s, sel_v.reshape(B, T, A_KV_HEADS, -1, HEAD_DIM))
    dist_w = t[:, None] - w_pos[None, :]
    mask_w = (dist_w >= 0) & (dist_w < WINDOW) & (w_pos[None, :] >= 0)
    s_w = jnp.einsum('btkgd,blkd->btkgl', qg, wk) - slope[None, None, :, :, None] * dist_w.astype(jnp.float32)[None, :, None, None, :]
    p_w = masked_softmax(s_w, mask_w[None, :, None, None, :])
    o_w = jnp.einsum('btkgl,blkd->btkgd', p_w, wv)
    g = jax.nn.sigmoid(gl.astype(jnp.float32)).reshape(B, T, A_KV_HEADS, A_GROUP, 3)
    o = g[..., 0:1] * o_c + g[..., 1:2] * o_s + g[..., 2:3] * o_w
    return o.reshape(B, T, A_WIDTH).astype(q.dtype)


def nsa_prompt(q, gl, kva, pe, w1, b1, w2):
    B, T = q.shape[:2]
    ckv, c_end = compress(kva[:, :, 0:2], pe, w1, b1, w2)
    ck, cv = ckv[:, :, 0], ckv[:, :, 1]
    ns = T // SLC_BLOCK
    sk = kva[:, :, 2].reshape(B, ns, SLC_BLOCK, A_KV_HEADS, HEAD_DIM).transpose(0, 3, 1, 2, 4)
    sv = kva[:, :, 3].reshape(B, ns, SLC_BLOCK, A_KV_HEADS, HEAD_DIM).transpose(0, 3, 1, 2, 4)
    pad = ((0, 0), (WINDOW, 0), (0, 0), (0, 0))
    wk_pad = jnp.pad(kva[:, :, 4], pad)
    wv_pad = jnp.pad(kva[:, :, 5], pad)
    nqb = T // Q_BLOCK
    qb = q.reshape(B, nqb, Q_BLOCK, A_HEADS, HEAD_DIM).swapaxes(0, 1)
    gb = gl.reshape(B, nqb, Q_BLOCK, A_HEADS, 3).swapaxes(0, 1)
    starts = jnp.arange(nqb) * Q_BLOCK

    def block(args):
        qi, gi, s0 = args
        q_pos = s0 + jnp.arange(Q_BLOCK)
        wk = lax.dynamic_slice_in_dim(wk_pad, s0, WINDOW + Q_BLOCK, axis=1)
        wv = lax.dynamic_slice_in_dim(wv_pad, s0, WINDOW + Q_BLOCK, axis=1)
        w_pos = s0 - WINDOW + jnp.arange(WINDOW + Q_BLOCK)
        return nsa_attend(qi, gi, q_pos, ck, cv, c_end, sk, sv, wk, wv, w_pos)

    o = lax.map(block, (qb, gb, starts))
    return o.swapaxes(0, 1).reshape(B, T, A_WIDTH)


def nsa_sample(q, gl, kva, pool, page_table, win_buf, pe, w1, b1, w2):
    B, T = q.shape[:2]
    past = pool[page_table].reshape(B, PAST_LEN, 4, A_KV_HEADS, HEAD_DIM)
    full = jnp.concatenate([past, kva[:, :, 0:4]], axis=1)
    L = PAST_LEN + T
    ckv, c_end = compress(full[:, :, 0:2], pe, w1, b1, w2)
    ns = -(-L // SLC_BLOCK)
    slc = jnp.pad(full[:, :, 2:4], ((0, 0), (0, ns * SLC_BLOCK - L), (0, 0), (0, 0), (0, 0)))
    slc = slc.reshape(B, ns, SLC_BLOCK, 2, A_KV_HEADS, HEAD_DIM).transpose(3, 0, 4, 1, 2, 5)
    W = win_buf.shape[1]
    win = jnp.concatenate([win_buf, kva[:, :, 4:6]], axis=1)
    w_pos = PAST_LEN - W + jnp.arange(W + T)
    q_pos = PAST_LEN + jnp.arange(T)
    o = nsa_attend(q, gl, q_pos, ckv[:, :, 0], ckv[:, :, 1], c_end, slc[0], slc[1], win[:, :, 0], win[:, :, 1], w_pos)
    return o, win[:, -W:]


def mlstm(q, k, v, i_pre, f_pre, o_pre, C0, n0, m0):
    B, T = q.shape[:2]
    L = math.gcd(T, MLSTM_CHUNK)
    nch = T // L

    def chunks(a):
        a = a.astype(jnp.float32).reshape(B, nch, L, *a.shape[2:])
        return jnp.moveaxis(a, 1, 0).swapaxes(2, 3)

    xs = (chunks(q), chunks(k) * (QK_DIM ** -0.5), chunks(v), chunks(i_pre), chunks(f_pre))
    causal = jnp.tril(jnp.ones((L, L), dtype=bool))

    def step(carry, xc):
        C, n, m = carry
        qc, kc, vc, ic, fc = xc
        b = jnp.cumsum(jax.nn.log_sigmoid(fc), axis=-1)
        log_d = jnp.where(causal, b[..., :, None] - b[..., None, :] + ic[..., None, :], -jnp.inf)
        m_new = jnp.maximum(b + m[..., None], jnp.max(log_d, axis=-1))
        d = jnp.exp(log_d - m_new[..., None])
        inter = jnp.exp(b + m[..., None] - m_new)
        s = jnp.einsum('bhtd,bhsd->bhts', qc, kc) * d
        num = inter[..., None] * jnp.einsum('bhtd,bhde->bhte', qc, C) + jnp.einsum('bhts,bhse->bhte', s, vc)
        den = inter * jnp.einsum('bhtd,bhd->bht', qc, n) + jnp.sum(s, axis=-1)
        h = num / jnp.maximum(jnp.abs(den), jnp.exp(-m_new))[..., None]
        w_end = d[..., -1, :]
        C_new = inter[..., -1, None, None] * C + jnp.einsum('bhs,bhsd,bhse->bhde', w_end, kc, vc)
        n_new = inter[..., -1, None] * n + jnp.einsum('bhs,bhsd->bhd', w_end, kc)
        return (C_new, n_new, m_new[..., -1]), h

    init = (C0.astype(jnp.float32), n0.astype(jnp.float32), m0.astype(jnp.float32))
    (C, n, m), hs = lax.scan(step, init, xs)
    h = hs.transpose(1, 0, 3, 2, 4).reshape(B, T, B_WIDTH)
    h = jax.nn.sigmoid(o_pre.astype(jnp.float32)) * h
    return h.astype(q.dtype), C, n, m


def project_in(x2d, g, w_in_bf, b_in):
    outs = []
    for lo, hi in zip(BOUNDS[:-1], BOUNDS[1:]):
        width = hi - lo
        pad = (-width) % 128
        w = w_in_bf[:, lo:hi]
        b = b_in[lo:hi]
        if pad:
            w = jnp.pad(w, ((0, 0), (0, pad)))
            b = jnp.pad(b, (0, pad))
        z = norm_matmul(x2d, g, w, b)
        outs.append(z[:, :width] if pad else z)
    return outs


def merge_and_mlp(x2d, ya, yb, gm, g, w_up_a, w_up_b, w_out, w_mlp1, w_mlp2):
    mix = mix_matmul(ya, yb, w_up_a, w_up_b, gm)
    x2d = matmul_norm_res(mix, w_out, x2d, g[1], tm=512, tk=D_MODEL)
    h = norm_matmul(x2d, g[2], w_mlp1, jnp.zeros((D_FF,), jnp.float32), act="relu2", out_dtype=jnp.bfloat16)
    return matmul_norm_res(h, w_mlp2, x2d, g[3])


def kernel(x_prompt, x_sample, cache_nsa_kv, cache_win_kv, state_C, state_n, state_m, page_table,
           norm_g, w_in, b_in, cmp_pe, cmp_w1, cmp_b1, cmp_w2, w_up_a, w_up_b, w_out, w_mlp1, w_mlp2):
    bf = jnp.bfloat16
    Bp, Tp = x_prompt.shape[:2]
    Bs, Ts = x_sample.shape[:2]
    xp = x_prompt.reshape(Bp * Tp, D_MODEL)
    xs = x_sample.reshape(Bs * Ts, D_MODEL)
    kv_p, kv_s, win_p, win_s = [], [], [], []
    C_p, C_s, n_p, n_s, m_p, m_s = [], [], [], [], [], []
    for l in range(DEPTH):
        cmp = (cmp_pe[l], cmp_w1[l], cmp_b1[l], cmp_w2[l])
        w_in_bf = w_in[l].astype(bf)
        mlp = (norm_g[l], w_up_a[l].astype(bf), w_up_b[l].astype(bf), w_out[l].astype(bf),
               w_mlp1[l].astype(bf), w_mlp2[l].astype(bf))

        def split(x2d, B, T):
            qa, kva, ga, qb, kb, vb, ib, fb, ob, gm = project_in(x2d, norm_g[l, 0], w_in_bf, b_in[l])
            return (qa.reshape(B, T, A_HEADS, HEAD_DIM), ga.reshape(B, T, A_HEADS, 3),
                    kva.reshape(B, T, N_KV_BRANCH, A_KV_HEADS, HEAD_DIM),
                    qb.reshape(B, T, B_HEADS, QK_DIM), kb.reshape(B, T, B_HEADS, QK_DIM),
                    vb.reshape(B, T, B_HEADS, V_DIM), ib.reshape(B, T, B_HEADS), fb.reshape(B, T, B_HEADS),
                    ob.reshape(B, T, B_WIDTH), gm)

        qa, ga, kva, qb, kb, vb, ib, fb, ob, gm = split(xp, Bp, Tp)
        ya = nsa_prompt(qa, ga, kva, *cmp)
        C0 = jnp.zeros((Bp, B_HEADS, QK_DIM, V_DIM), jnp.float32)
        n0 = jnp.zeros((Bp, B_HEADS, QK_DIM), jnp.float32)
        m0 = jnp.zeros((Bp, B_HEADS), jnp.float32)
        yb, C, n, m = mlstm(qb, kb, vb, ib, fb, ob, C0, n0, m0)
        kv_p.append(kva[:, :, 0:4])
        win_p.append(kva[:, -min(WINDOW, Tp):, 4:6])
        C_p.append(C)
        n_p.append(n)
        m_p.append(m)
        xp = merge_and_mlp(xp, ya.reshape(Bp * Tp, A_WIDTH), yb.reshape(Bp * Tp, B_WIDTH), gm, *mlp)

        qa, ga, kva, qb, kb, vb, ib, fb, ob, gm = split(xs, Bs, Ts)
        ya, win_new = nsa_sample(qa, ga, kva, cache_nsa_kv[l], page_table, cache_win_kv[l], *cmp)
        yb, C, n, m = mlstm(qb, kb, vb, ib, fb, ob, state_C[l], state_n[l], state_m[l])
        kv_s.append(kva[:, :, 0:4])
        win_s.append(win_new)
        C_s.append(C)
        n_s.append(n)
        m_s.append(m)
        xs = merge_and_mlp(xs, ya.reshape(Bs * Ts, A_WIDTH), yb.reshape(Bs * Ts, B_WIDTH), gm, *mlp)

    return (xp.reshape(Bp, Tp, D_MODEL), xs.reshape(Bs, Ts, D_MODEL),
            jnp.stack(kv_p), jnp.stack(kv_s), jnp.stack(win_p), jnp.stack(win_s),
            jnp.stack(C_p), jnp.stack(C_s), jnp.stack(n_p), jnp.stack(n_s), jnp.stack(m_p), jnp.stack(m_s))
```

```python
import functools
import math

import jax
import jax.numpy as jnp
import numpy as np
from jax import lax
from jax.experimental import pallas as pl
from jax.experimental.pallas import tpu as pltpu

D_MODEL = 2048
DEPTH = 2
PAST_LEN = 2048
A_HEADS = 16
A_KV_HEADS = 4
A_GROUP = A_HEADS // A_KV_HEADS
HEAD_DIM = 64
A_WIDTH = A_HEADS * HEAD_DIM
CMP_BLOCK = 32
CMP_STRIDE = 16
CMP_RATIO = CMP_BLOCK // CMP_STRIDE
CMP_HIDDEN = 128
SLC_BLOCK = 64
SLC_TOP = 16
WINDOW = 512
Q_BLOCK = 128
N_KV_BRANCH = 6
B_HEADS = 8
QK_DIM = 64
V_DIM = 128
B_WIDTH = B_HEADS * V_DIM
MLSTM_CHUNK = 64
D_FF = 4 * D_MODEL
EPS = 1e-6
FORCE = 1e4
NEG = -1e30

SPLITS = (A_WIDTH, N_KV_BRANCH * A_KV_HEADS * HEAD_DIM, 3 * A_HEADS,
          B_HEADS * QK_DIM, B_HEADS * QK_DIM, B_WIDTH, B_HEADS, B_HEADS, B_WIDTH, 2 * D_MODEL)
BOUNDS = tuple(int(b) for b in np.cumsum((0,) + SPLITS))

VMEM_LIMIT_BYTES = 56 * 1024 * 1024


def _params(*sem):
    return pltpu.CompilerParams(dimension_semantics=sem, vmem_limit_bytes=VMEM_LIMIT_BYTES)


def _norm_matmul_kernel(x_ref, g_ref, w_ref, b_ref, o_ref, h_ref, *, act):
    @pl.when(pl.program_id(1) == 0)
    def _():
        x = x_ref[...]
        r = lax.rsqrt(jnp.mean(x * x, axis=-1, keepdims=True) + EPS)
        h_ref[...] = (x * r * g_ref[...]).astype(h_ref.dtype)

    z = jnp.dot(h_ref[...], w_ref[...], preferred_element_type=jnp.float32) + b_ref[...]
    if act == "relu2":
        z = jnp.square(jnp.maximum(z, 0.0))
    o_ref[...] = z.astype(o_ref.dtype)


def norm_matmul(x, g, w, b, *, act=None, out_dtype=jnp.float32, tm=1024, tn=512):
    M, K = x.shape
    N = w.shape[1]
    tm = min(tm, M)
    tn = min(tn, N)
    assert M % tm == 0 and N % tn == 0, (M, N, tm, tn)
    return pl.pallas_call(
        functools.partial(_norm_matmul_kernel, act=act),
        grid=(M // tm, N // tn),
        in_specs=[
            pl.BlockSpec((tm, K), lambda i, j: (i, 0)),
            pl.BlockSpec((1, K), lambda i, j: (0, 0)),
            pl.BlockSpec((K, tn), lambda i, j: (0, j)),
            pl.BlockSpec((1, tn), lambda i, j: (0, j)),
        ],
        out_specs=pl.BlockSpec((tm, tn), lambda i, j: (i, j)),
        out_shape=jax.ShapeDtypeStruct((M, N), out_dtype),
        scratch_shapes=[pltpu.VMEM((tm, K), jnp.bfloat16)],
        compiler_params=_params("parallel", "arbitrary"),
        name="norm_matmul",
    )(x, g.reshape(1, K), w, b.reshape(1, N))


def _mix_kernel(ya_ref, yb_ref, wa_ref, wb_ref, ga_ref, gb_ref, o_ref):
    a = jnp.dot(ya_ref[...].astype(jnp.bfloat16), wa_ref[...], preferred_element_type=jnp.float32)
    b = jnp.dot(yb_ref[...].astype(jnp.bfloat16), wb_ref[...], preferred_element_type=jnp.float32)
    o_ref[...] = (jax.nn.sigmoid(ga_ref[...]) * a + jax.nn.sigmoid(gb_ref[...]) * b).astype(o_ref.dtype)


def mix_matmul(ya, yb, wa, wb, gm, *, tm=1024, tn=512):
    M, Ka = ya.shape
    Kb = yb.shape[1]
    N = wa.shape[1]
    tm = min(tm, M)
    nb = N // tn
    return pl.pallas_call(
        _mix_kernel,
        grid=(M // tm, nb),
        in_specs=[
            pl.BlockSpec((tm, Ka), lambda i, j: (i, 0)),
            pl.BlockSpec((tm, Kb), lambda i, j: (i, 0)),
            pl.BlockSpec((Ka, tn), lambda i, j: (0, j)),
            pl.BlockSpec((Kb, tn), lambda i, j: (0, j)),
            pl.BlockSpec((tm, tn), lambda i, j: (i, j)),
            pl.BlockSpec((tm, tn), lambda i, j: (i, j + nb)),
        ],
        out_specs=pl.BlockSpec((tm, tn), lambda i, j: (i, j)),
        out_shape=jax.ShapeDtypeStruct((M, N), jnp.bfloat16),
        compiler_params=_params("parallel", "arbitrary"),
        name="mix_matmul",
    )(ya, yb, wa, wb, gm, gm)


def _matmul_norm_res_kernel(a_ref, w_ref, x_ref, g_ref, o_ref, acc_ref):
    k = pl.program_id(1)

    @pl.when(k == 0)
    def _():
        acc_ref[...] = jnp.zeros_like(acc_ref)

    acc_ref[...] += jnp.dot(a_ref[...], w_ref[...], preferred_element_type=jnp.float32)

    @pl.when(k == pl.num_programs(1) - 1)
    def _():
        y = acc_ref[...]
        r = lax.rsqrt(jnp.mean(y * y, axis=-1, keepdims=True) + EPS)
        o_ref[...] = x_ref[...] + y * r * g_ref[...]


def matmul_norm_res(a, w, x, g, *, tm=1024, tk=512):
    M, K = a.shape
    N = w.shape[1]
    tm = min(tm, M)
    tk = min(tk, K)
    return pl.pallas_call(
        _matmul_norm_res_kernel,
        grid=(M // tm, K // tk),
        in_specs=[
            pl.BlockSpec((tm, tk), lambda i, k: (i, k)),
            pl.BlockSpec((tk, N), lambda i, k: (k, 0)),
            pl.BlockSpec((tm, N), lambda i, k: (i, 0)),
            pl.BlockSpec((1, N), lambda i, k: (0, 0)),
        ],
        out_specs=pl.BlockSpec((tm, N), lambda i, k: (i, 0)),
        out_shape=jax.ShapeDtypeStruct((M, N), jnp.float32),
        scratch_shapes=[pltpu.VMEM((tm, N), jnp.float32)],
        compiler_params=_params("parallel", "arbitrary"),
        name="matmul_norm_res",
    )(a, w, x, g.reshape(1, N))


NSA_LANES = A_GROUP * Q_BLOCK
KEY_CHUNK = 128
SLC_SLOTS = 128
POS_PIECES = 6
AUG = 128
MASK_BIG = 30000.0


def _pos_pieces(pos):
    pos = np.asarray(pos, np.int64)
    hi = (pos // 64) * 64
    lo = pos % 64
    return np.stack([hi, lo] * 3, axis=-1).astype(np.float32)


def _slope_rows():
    slopes = 2.0 ** (-8.0 * np.arange(1, A_HEADS + 1, dtype=np.float64) / A_HEADS)
    s = jnp.asarray(slopes.astype(np.float32))
    hi = s.astype(jnp.bfloat16)
    r1 = s - hi.astype(jnp.float32)
    mid = r1.astype(jnp.bfloat16)
    lo = (r1 - mid.astype(jnp.float32)).astype(jnp.bfloat16)
    rows = jnp.stack([hi, hi, mid, mid, lo, lo], axis=0)
    rows = rows.reshape(POS_PIECES, A_KV_HEADS, A_GROUP).transpose(1, 0, 2)
    rows = jnp.repeat(rows, Q_BLOCK, axis=-1)
    return jnp.pad(rows, ((0, 0), (0, 16 - POS_PIECES), (0, 0)))


def _split3(x):
    hi = x.astype(jnp.bfloat16)
    r1 = x - hi.astype(jnp.float32)
    mid = r1.astype(jnp.bfloat16)
    lo = (r1 - mid.astype(jnp.float32)).astype(jnp.bfloat16)
    return hi, mid, lo


def _tile4(x):
    return jnp.concatenate([x] * A_GROUP, axis=1)


def _softmax_step(carry, s, vT):
    m, l, acc = carry
    m_new = jnp.maximum(m, jnp.max(s, axis=0, keepdims=True))
    alpha = jnp.exp(m - m_new)
    p = jnp.exp(s - m_new)
    l = alpha * l + jnp.sum(p, axis=0, keepdims=True)
    acc = alpha * acc + jnp.dot(vT, p.astype(jnp.bfloat16), preferred_element_type=jnp.float32)
    return m_new, l, acc


def _nsa_prompt_kernel(qT_ref, slope_ref, gT_ref, ck_ref, cvT_ref, ovT_ref, ks_ref, vsT_ref,
                       kw_ref, vwT_ref, o_ref, qa_ref):
    f32, bf = jnp.float32, jnp.bfloat16
    i = pl.program_id(1)
    s0 = i * Q_BLOCK
    ncp = ck_ref.shape[1]

    qT = qT_ref[...] * (HEAD_DIM ** -0.5)
    qcat = jnp.concatenate([qT[g * HEAD_DIM:(g + 1) * HEAD_DIM, :] for g in range(A_GROUP)], axis=1)
    qa_ref[0:HEAD_DIM, :] = qcat.astype(bf)
    qa_ref[HEAD_DIM:HEAD_DIM + 16, :] = slope_ref[0]
    qa_ref[HEAD_DIM + 16:AUG, :] = jnp.zeros((AUG - HEAD_DIM - 16, NSA_LANES), bf)
    qc = qa_ref[0:AUG, :]

    tt = lax.broadcasted_iota(jnp.int32, (Q_BLOCK, Q_BLOCK), 1)
    jj = lax.broadcasted_iota(jnp.int32, (Q_BLOCK, Q_BLOCK), 0)

    sc = jnp.dot(ck_ref[0], qc, preferred_element_type=f32)
    n_idx = lax.broadcasted_iota(jnp.int32, (ncp, Q_BLOCK), 0)
    t_idx = s0 + lax.broadcasted_iota(jnp.int32, (ncp, Q_BLOCK), 1)
    valid = _tile4(n_idx * CMP_STRIDE + (CMP_BLOCK - 1) <= t_idx)
    sc = jnp.where(valid, sc, NEG)
    ec = jnp.where(valid, jnp.exp(sc - jnp.max(sc, axis=0, keepdims=True)), 0.0)
    lc = jnp.sum(ec, axis=0, keepdims=True)
    pc = ec * jnp.where(lc > 0.0, 1.0 / lc, 0.0)
    o_c = jnp.dot(cvT_ref[0], pc.astype(bf), preferred_element_type=f32)

    pg = pc[:, 0:Q_BLOCK]
    for g in range(1, A_GROUP):
        pg = pg + pc[:, g * Q_BLOCK:(g + 1) * Q_BLOCK]
    imp = sum(jnp.dot(ovT_ref[...], piece, preferred_element_type=f32) for piece in _split3(pg))

    blk = lax.broadcasted_iota(jnp.int32, (SLC_SLOTS, Q_BLOCK), 0)
    cur = (s0 + lax.broadcasted_iota(jnp.int32, (SLC_SLOTS, Q_BLOCK), 1)) // SLC_BLOCK
    forced = (blk == 0) | (blk == cur) | (blk == cur - 1)
    causal = blk <= cur
    score = jnp.where(forced, FORCE, jnp.where(causal, imp, -FORCE))
    blk_f = blk.astype(f32)
    sel = jnp.zeros((SLC_SLOTS, Q_BLOCK), f32)
    for _ in range(SLC_TOP):
        top = jnp.max(score, axis=0, keepdims=True)
        first = jnp.min(jnp.where(score == top, blk_f, float(SLC_SLOTS)), axis=0, keepdims=True)
        pick = blk_f == first
        sel = jnp.where(pick, 1.0, sel)
        score = jnp.where(pick, -jnp.inf, score)
    selneg = jnp.where((sel > 0.0) & causal, 0.0, -MASK_BIG)
    qa_ref[AUG:AUG + SLC_SLOTS, :] = _tile4(selneg).astype(bf)

    def sel_scores(c):
        kc = ks_ref[0, pl.ds(pl.multiple_of(c * KEY_CHUNK, KEY_CHUNK), KEY_CHUNK), :]
        return jnp.dot(kc, qa_ref[...], preferred_element_type=f32)

    init = (jnp.full((1, NSA_LANES), NEG, f32), jnp.zeros((1, NSA_LANES), f32),
            jnp.zeros((HEAD_DIM, NSA_LANES), f32))
    carry = lax.fori_loop(0, i, lambda c, cr: _softmax_step(cr, sel_scores(c), vsT_ref[0, c]), init)
    diag = _tile4(jj <= tt)
    m_s, l_s, acc_s = _softmax_step(carry, jnp.where(diag, sel_scores(i), NEG), vsT_ref[0, i])
    o_s = acc_s * (1.0 / l_s)

    carry = init
    nwc = WINDOW // KEY_CHUNK
    for d in range(nwc + 1):
        c = i - nwc + d
        cc = jnp.maximum(c, 0)
        kc = kw_ref[0, pl.ds(pl.multiple_of(cc * KEY_CHUNK, KEY_CHUNK), KEY_CHUNK), :]
        s = jnp.dot(kc, qc, preferred_element_type=f32)
        if d == 0:
            s = jnp.where(_tile4(jj > tt) & (c >= 0), s, NEG)
        elif d == nwc:
            s = jnp.where(diag, s, NEG)
        else:
            s = jnp.where(c >= 0, s, NEG)
        carry = _softmax_step(carry, s, vwT_ref[0, cc])
    m_w, l_w, acc_w = carry
    o_w = acc_w * (1.0 / l_w)

    gate = jax.nn.sigmoid(gT_ref[0])
    for g in range(A_GROUP):
        cols = slice(g * Q_BLOCK, (g + 1) * Q_BLOCK)
        o = (gate[3 * g:3 * g + 1, :] * o_c[:, cols] + gate[3 * g + 1:3 * g + 2, :] * o_s[:, cols]
             + gate[3 * g + 2:3 * g + 3, :] * o_w[:, cols])
        o_ref[g * HEAD_DIM:(g + 1) * HEAD_DIM, :] = o


def _cmp_prompt_kernel(x_ref, pe_ref, w1_ref, b1_ref, w2T_ref, ck_ref, cvT_ref):
    f32, bf = jnp.float32, jnp.bfloat16
    nseg = ck_ref.shape[1]
    row_w = 2 * A_KV_HEADS * HEAD_DIM
    for c in range(2):
        bias = jnp.dot(pe_ref[c], w1_ref[c], preferred_element_type=f32,
                       precision=lax.Precision.HIGHEST) + b1_ref[c]
        for k in range(A_KV_HEADS):
            col = (c * A_KV_HEADS + k) * HEAD_DIM
            y0 = jnp.zeros((nseg, CMP_HIDDEN), f32)
            y1 = jnp.zeros((nseg, CMP_HIDDEN), f32)
            for j in range(CMP_STRIDE):
                xj = x_ref[:, j * row_w + col:j * row_w + col + HEAD_DIM].astype(bf)
                w0 = w1_ref[c, j * HEAD_DIM:(j + 1) * HEAD_DIM, :].astype(bf)
                w1 = w1_ref[c, (CMP_STRIDE + j) * HEAD_DIM:(CMP_STRIDE + j + 1) * HEAD_DIM, :].astype(bf)
                y0 = y0 + jnp.dot(xj, w0, preferred_element_type=f32)
                y1 = y1 + jnp.dot(xj, w1, preferred_element_type=f32)
            pre = y0 + pltpu.roll(y1, nseg - 1, 0)
            hid = jnp.maximum(pre + bias, 0.0).astype(bf)
            if c == 0:
                ck_ref[k] = lax.dot_general(hid, w2T_ref[c].astype(bf), (((1,), (1,)), ((), ())),
                                            preferred_element_type=f32)
            else:
                cvT_ref[k] = lax.dot_general(w2T_ref[c].astype(bf), hid, (((1,), (1,)), ((), ())),
                                             preferred_element_type=f32)


def compress_prompt(x_cmp, pe, w1, b1, w2):
    T = x_cmp.shape[0]
    nseg = T // CMP_STRIDE
    return pl.pallas_call(
        _cmp_prompt_kernel,
        out_shape=(jax.ShapeDtypeStruct((A_KV_HEADS, nseg, HEAD_DIM), jnp.float32),
                   jax.ShapeDtypeStruct((A_KV_HEADS, HEAD_DIM, nseg), jnp.float32)),
        compiler_params=pltpu.CompilerParams(vmem_limit_bytes=VMEM_LIMIT_BYTES),
        name="compress_prompt",
    )(x_cmp.reshape(nseg, CMP_STRIDE * x_cmp.shape[1]), pe.reshape(2, 1, CMP_BLOCK * HEAD_DIM), w1, b1.reshape(2, 1, CMP_HIDDEN), w2.transpose(0, 2, 1))


def nsa_prompt_pallas(qa, ga, kva, pe, w1, b1, w2):
    bf = jnp.bfloat16
    T = qa.shape[0]
    nqb = T // Q_BLOCK
    nch = T // KEY_CHUNK
    ncp = T // CMP_STRIDE
    kvw = A_KV_HEADS * HEAD_DIM
    ck, cvT = compress_prompt(kva[:, 0:2 * kvw], pe, w1, b1, w2)

    def heads(x):
        return x.reshape(T, A_KV_HEADS, HEAD_DIM).transpose(1, 0, 2)

    def chunksT(x):
        return x.reshape(nch, KEY_CHUNK, A_KV_HEADS, HEAD_DIM).transpose(2, 0, 3, 1).astype(bf)

    pos = np.arange(T)
    key_pos = jnp.broadcast_to(jnp.asarray(_pos_pieces(pos)), (A_KV_HEADS, T, POS_PIECES))
    zpad = jnp.zeros((A_KV_HEADS, T, AUG - HEAD_DIM - POS_PIECES), jnp.float32)
    onehot = jnp.broadcast_to(jnp.asarray((pos[:, None] // SLC_BLOCK == np.arange(SLC_SLOTS)[None, :]).astype(np.float32)),
                              (A_KV_HEADS, T, SLC_SLOTS))
    ks_aug = jnp.concatenate([heads(kva[:, 2 * kvw:3 * kvw]), key_pos, zpad, onehot], axis=-1).astype(bf)
    kw_aug = jnp.concatenate([heads(kva[:, 4 * kvw:5 * kvw]), key_pos, zpad], axis=-1).astype(bf)
    c_end = np.arange(ncp) * CMP_STRIDE + (CMP_BLOCK - 1)
    ck_aug = jnp.concatenate([ck, jnp.broadcast_to(jnp.asarray(_pos_pieces(c_end)), (A_KV_HEADS, ncp, POS_PIECES)),
                              jnp.zeros((A_KV_HEADS, ncp, AUG - HEAD_DIM - POS_PIECES), jnp.float32)], axis=-1).astype(bf)
    vsT = chunksT(kva[:, 3 * kvw:4 * kvw])
    vwT = chunksT(kva[:, 5 * kvw:6 * kvw])
    c_start = c_end - (CMP_BLOCK - 1)
    s_start = np.arange(SLC_SLOTS) * SLC_BLOCK
    ovT = ((c_start[None, :] < s_start[:, None] + SLC_BLOCK) & (c_end[None, :] >= s_start[:, None])
           & (np.arange(ncp)[None, :] < ncp - 1))
    ovT = jnp.asarray(ovT.astype(np.float32)).astype(bf)
    gT = jnp.pad(ga.T.reshape(A_KV_HEADS, 3 * A_GROUP, T), ((0, 0), (0, 16 - 3 * A_GROUP), (0, 0)))

    yT = pl.pallas_call(
        _nsa_prompt_kernel,
        grid=(A_KV_HEADS, nqb),
        in_specs=[
            pl.BlockSpec((A_GROUP * HEAD_DIM, Q_BLOCK), lambda k, i: (k, i)),
            pl.BlockSpec((1, 16, NSA_LANES), lambda k, i: (k, 0, 0)),
            pl.BlockSpec((1, 16, Q_BLOCK), lambda k, i: (k, 0, i)),
            pl.BlockSpec((1, ncp, AUG), lambda k, i: (k, 0, 0)),
            pl.BlockSpec((1, HEAD_DIM, ncp), lambda k, i: (k, 0, 0)),
            pl.BlockSpec((SLC_SLOTS, ncp), lambda k, i: (0, 0)),
            pl.BlockSpec((1, T, AUG + SLC_SLOTS), lambda k, i: (k, 0, 0)),
            pl.BlockSpec((1, nch, HEAD_DIM, KEY_CHUNK), lambda k, i: (k, 0, 0, 0)),
            pl.BlockSpec((1, T, AUG), lambda k, i: (k, 0, 0)),
            pl.BlockSpec((1, nch, HEAD_DIM, KEY_CHUNK), lambda k, i: (k, 0, 0, 0)),
        ],
        out_specs=pl.BlockSpec((A_GROUP * HEAD_DIM, Q_BLOCK), lambda k, i: (k, i)),
        out_shape=jax.ShapeDtypeStruct((A_WIDTH, T), jnp.float32),
        scratch_shapes=[pltpu.VMEM((AUG + SLC_SLOTS, NSA_LANES), bf)],
        compiler_params=_params("parallel", "arbitrary"),
        name="nsa_prompt",
    )(qa.T, _slope_rows(), gT, ck_aug, cvT.astype(bf), ovT, ks_aug, vsT, kw_aug, vwT)
    return yT.T


def alibi_slopes():
    return 2.0 ** (-8.0 * jnp.arange(1, A_HEADS + 1, dtype=jnp.float32) / A_HEADS)


def masked_softmax(s, mask):
    s = jnp.where(mask, s.astype(jnp.float32), NEG)
    return jnp.where(mask, jax.nn.softmax(s, axis=-1), 0.0)


def compress(rows, pe, w1, b1, w2):
    B, L = rows.shape[:2]
    nc = (L - CMP_BLOCK) // CMP_STRIDE + 1
    nseg = nc + CMP_RATIO - 1
    seg = rows[:, :nseg * CMP_STRIDE].reshape(B, nseg, CMP_STRIDE, 2, A_KV_HEADS, HEAD_DIM)
    w1r = w1.reshape(2, CMP_RATIO, CMP_STRIDE, HEAD_DIM, CMP_HIDDEN)
    y = jnp.einsum('bnjckd,crjde->rbncke', seg, w1r)
    pre = sum(y[r][:, r:r + nc] for r in range(CMP_RATIO))
    pe_bias = jnp.einsum('cjd,cjde->ce', pe, w1.reshape(2, CMP_BLOCK, HEAD_DIM, CMP_HIDDEN)) + b1
    hid = jax.nn.relu(pre + pe_bias[:, None, :])
    out = jnp.einsum('bncke,ced->bnckd', hid, w2)
    c_end = jnp.arange(nc) * CMP_STRIDE + (CMP_BLOCK - 1)
    return out, c_end


def nsa_attend(q, gl, q_pos, ck, cv, c_end, sk, sv, wk, wv, w_pos):
    B, T = q.shape[:2]
    qg = q.reshape(B, T, A_KV_HEADS, A_GROUP, HEAD_DIM) * (HEAD_DIM ** -0.5)
    slope = alibi_slopes().reshape(A_KV_HEADS, A_GROUP)
    t = q_pos
    dist_c = (t[:, None] - c_end[None, :]).astype(jnp.float32)
    s_c = jnp.einsum('btkgd,bnkd->btkgn', qg, ck) - slope[None, None, :, :, None] * dist_c[None, :, None, None, :]
    p_c = masked_softmax(s_c, (dist_c >= 0)[None, :, None, None, :])
    o_c = jnp.einsum('btkgn,bnkd->btkgd', p_c, cv)
    ns = sk.shape[2]
    c_start = c_end - (CMP_BLOCK - 1)
    s_start = jnp.arange(ns) * SLC_BLOCK
    overlap = ((c_start[:, None] < s_start[None, :] + SLC_BLOCK) & (c_end[:, None] >= s_start[None, :])).astype(jnp.float32)
    imp = jnp.einsum('btkgn,ns->btks', p_c, overlap)
    blk = jnp.arange(ns)[None, :]
    cur = (t // SLC_BLOCK)[:, None]
    forced = (blk == 0) | (blk == cur) | (blk == cur - 1)
    causal = s_start[None, :] <= t[:, None]
    score = jnp.where(forced[None, :, None, :], FORCE, jnp.where(causal[None, :, None, :], imp, -FORCE))
    _, idx = lax.top_k(score, min(SLC_TOP, ns))
    b_i = jnp.arange(B)[:, None, None, None]
    k_i = jnp.arange(A_KV_HEADS)[None, None, :, None]
    sel_k = sk[b_i, k_i, idx]
    sel_v = sv[b_i, k_i, idx]
    pos_s = idx[..., None] * SLC_BLOCK + jnp.arange(SLC_BLOCK)
    dist_s = (t[None, :, None, None, None] - pos_s).astype(jnp.float32)
    s_s = jnp.einsum('btkgd,btknjd->btkgnj', qg, sel_k) - slope[None, None, :, :, None, None] * dist_s[:, :, :, None]
    mask_s = jnp.broadcast_to((dist_s >= 0)[:, :, :, None], s_s.shape).reshape(B, T, A_KV_HEADS, A_GROUP, -1)
    p_s = masked_softmax(s_s.reshape(B, T, A_KV_HEADS, A_GROUP, -1), mask_s)
    o_s = jnp.einsum('btkgm,btkmd->btkgd', p_s, sel_v.reshape(B, T, A_KV_HEADS, -1, HEAD_DIM))
    dist_w = t[:, None] - w_pos[None, :]
    mask_w = (dist_w >= 0) & (dist_w < WINDOW) & (w_pos[None, :] >= 0)
    s_w = jnp.einsum('btkgd,blkd->btkgl', qg, wk) - slope[None, None, :, :, None] * dist_w.astype(jnp.float32)[None, :, None, None, :]
    p_w = masked_softmax(s_w, mask_w[None, :, None, None, :])
    o_w = jnp.einsum('btkgl,blkd->btkgd', p_w, wv)
    g = jax.nn.sigmoid(gl.astype(jnp.float32)).reshape(B, T, A_KV_HEADS, A_GROUP, 3)
    o = g[..., 0:1] * o_c + g[..., 1:2] * o_s + g[..., 2:3] * o_w
    return o.reshape(B, T, A_WIDTH).astype(q.dtype)


def nsa_prompt(q, gl, kva, pe, w1, b1, w2):
    B, T = q.shape[:2]
    ckv, c_end = compress(kva[:, :, 0:2], pe, w1, b1, w2)
    ck, cv = ckv[:, :, 0], ckv[:, :, 1]
    ns = T // SLC_BLOCK
    sk = kva[:, :, 2].reshape(B, ns, SLC_BLOCK, A_KV_HEADS, HEAD_DIM).transpose(0, 3, 1, 2, 4)
    sv = kva[:, :, 3].reshape(B, ns, SLC_BLOCK, A_KV_HEADS, HEAD_DIM).transpose(0, 3, 1, 2, 4)
    pad = ((0, 0), (WINDOW, 0), (0, 0), (0, 0))
    wk_pad = jnp.pad(kva[:, :, 4], pad)
    wv_pad = jnp.pad(kva[:, :, 5], pad)
    nqb = T // Q_BLOCK
    qb = q.reshape(B, nqb, Q_BLOCK, A_HEADS, HEAD_DIM).swapaxes(0, 1)
    gb = gl.reshape(B, nqb, Q_BLOCK, A_HEADS, 3).swapaxes(0, 1)
    starts = jnp.arange(nqb) * Q_BLOCK

    def block(args):
        qi, gi, s0 = args
        q_pos = s0 + jnp.arange(Q_BLOCK)
        wk = lax.dynamic_slice_in_dim(wk_pad, s0, WINDOW + Q_BLOCK, axis=1)
        wv = lax.dynamic_slice_in_dim(wv_pad, s0, WINDOW + Q_BLOCK, axis=1)
        w_pos = s0 - WINDOW + jnp.arange(WINDOW + Q_BLOCK)
        return nsa_attend(qi, gi, q_pos, ck, cv, c_end, sk, sv, wk, wv, w_pos)

    o = lax.map(block, (qb, gb, starts))
    return o.swapaxes(0, 1).reshape(B, T, A_WIDTH)


def nsa_sample(q, gl, kva, pool, page_table, win_buf, pe, w1, b1, w2):
    B, T = q.shape[:2]
    past = pool[page_table].reshape(B, PAST_LEN, 4, A_KV_HEADS, HEAD_DIM)
    full = jnp.concatenate([past, kva[:, :, 0:4]], axis=1)
    L = PAST_LEN + T
    ckv, c_end = compress(full[:, :, 0:2], pe, w1, b1, w2)
    ns = -(-L // SLC_BLOCK)
    slc = jnp.pad(full[:, :, 2:4], ((0, 0), (0, ns * SLC_BLOCK - L), (0, 0), (0, 0), (0, 0)))
    slc = slc.reshape(B, ns, SLC_BLOCK, 2, A_KV_HEADS, HEAD_DIM).transpose(3, 0, 4, 1, 2, 5)
    W = win_buf.shape[1]
    win = jnp.concatenate([win_buf, kva[:, :, 4:6]], axis=1)
    w_pos = PAST_LEN - W + jnp.arange(W + T)
    q_pos = PAST_LEN + jnp.arange(T)
    o = nsa_attend(q, gl, q_pos, ckv[:, :, 0], ckv[:, :, 1], c_end, slc[0], slc[1], win[:, :, 0], win[:, :, 1], w_pos)
    return o, win[:, -W:]


def mlstm(q, k, v, i_pre, f_pre, o_pre, C0, n0, m0):
    B, T = q.shape[:2]
    L = math.gcd(T, MLSTM_CHUNK)
    nch = T // L

    def chunks(a):
        a = a.astype(jnp.float32).reshape(B, nch, L, *a.shape[2:])
        return jnp.moveaxis(a, 1, 0).swapaxes(2, 3)

    xs = (chunks(q), chunks(k) * (QK_DIM ** -0.5), chunks(v), chunks(i_pre), chunks(f_pre))
    causal = jnp.tril(jnp.ones((L, L), dtype=bool))

    def step(carry, xc):
        C, n, m = carry
        qc, kc, vc, ic, fc = xc
        b = jnp.cumsum(jax.nn.log_sigmoid(fc), axis=-1)
        log_d = jnp.where(causal, b[..., :, None] - b[..., None, :] + ic[..., None, :], -jnp.inf)
        m_new = jnp.maximum(b + m[..., None], jnp.max(log_d, axis=-1))
        d = jnp.exp(log_d - m_new[..., None])
        inter = jnp.exp(b + m[..., None] - m_new)
        s = jnp.einsum('bhtd,bhsd->bhts', qc, kc) * d
        num = inter[..., None] * jnp.einsum('bhtd,bhde->bhte', qc, C) + jnp.einsum('bhts,bhse->bhte', s, vc)
        den = inter * jnp.einsum('bhtd,bhd->bht', qc, n) + jnp.sum(s, axis=-1)
        h = num / jnp.maximum(jnp.abs(den), jnp.exp(-m_new))[..., None]
        w_end = d[..., -1, :]
        C_new = inter[..., -1, None, None] * C + jnp.einsum('bhs,bhsd,bhse->bhde', w_end, kc, vc)
        n_new = inter[..., -1, None] * n + jnp.einsum('bhs,bhsd->bhd', w_end, kc)
        return (C_new, n_new, m_new[..., -1]), h

    init = (C0.astype(jnp.float32), n0.astype(jnp.float32), m0.astype(jnp.float32))
    (C, n, m), hs = lax.scan(step, init, xs)
    h = hs.transpose(1, 0, 3, 2, 4).reshape(B, T, B_WIDTH)
    h = jax.nn.sigmoid(o_pre.astype(jnp.float32)) * h
    return h.astype(q.dtype), C, n, m


def project_in(x2d, g, w_in_bf, b_in):
    outs = []
    for lo, hi in zip(BOUNDS[:-1], BOUNDS[1:]):
        width = hi - lo
        pad = (-width) % 128
        w = w_in_bf[:, lo:hi]
        b = b_in[lo:hi]
        if pad:
            w = jnp.pad(w, ((0, 0), (0, pad)))
            b = jnp.pad(b, (0, pad))
        z = norm_matmul(x2d, g, w, b)
        outs.append(z[:, :width] if pad else z)
    return outs


def merge_and_mlp(x2d, ya, yb, gm, g, w_up_a, w_up_b, w_out, w_mlp1, w_mlp2):
    mix = mix_matmul(ya, yb, w_up_a, w_up_b, gm)
    x2d = matmul_norm_res(mix, w_out, x2d, g[1], tm=512, tk=D_MODEL)
    h = norm_matmul(x2d, g[2], w_mlp1, jnp.zeros((D_FF,), jnp.float32), act="relu2", out_dtype=jnp.bfloat16)
    return matmul_norm_res(h, w_mlp2, x2d, g[3])


def kernel(x_prompt, x_sample, cache_nsa_kv, cache_win_kv, state_C, state_n, state_m, page_table,
           norm_g, w_in, b_in, cmp_pe, cmp_w1, cmp_b1, cmp_w2, w_up_a, w_up_b, w_out, w_mlp1, w_mlp2):
    bf = jnp.bfloat16
    Bp, Tp = x_prompt.shape[:2]
    Bs, Ts = x_sample.shape[:2]
    xp = x_prompt.reshape(Bp * Tp, D_MODEL)
    xs = x_sample.reshape(Bs * Ts, D_MODEL)
    kv_p, kv_s, win_p, win_s = [], [], [], []
    C_p, C_s, n_p, n_s, m_p, m_s = [], [], [], [], [], []
    for l in range(DEPTH):
        cmp = (cmp_pe[l], cmp_w1[l], cmp_b1[l], cmp_w2[l])
        w_in_bf = w_in[l].astype(bf)
        mlp = (norm_g[l], w_up_a[l].astype(bf), w_up_b[l].astype(bf), w_out[l].astype(bf),
               w_mlp1[l].astype(bf), w_mlp2[l].astype(bf))

        def split(x2d, B, T):
            qa, kva, ga, qb, kb, vb, ib, fb, ob, gm = project_in(x2d, norm_g[l, 0], w_in_bf, b_in[l])
            return (qa.reshape(B, T, A_HEADS, HEAD_DIM), ga.reshape(B, T, A_HEADS, 3),
                    kva.reshape(B, T, N_KV_BRANCH, A_KV_HEADS, HEAD_DIM),
                    qb.reshape(B, T, B_HEADS, QK_DIM), kb.reshape(B, T, B_HEADS, QK_DIM),
                    vb.reshape(B, T, B_HEADS, V_DIM), ib.reshape(B, T, B_HEADS), fb.reshape(B, T, B_HEADS),
                    ob.reshape(B, T, B_WIDTH), gm)

        qa, ga, kva, qb, kb, vb, ib, fb, ob, gm = split(xp, Bp, Tp)
        ya = nsa_prompt_pallas(qa.reshape(Tp, A_WIDTH), ga.reshape(Tp, 3 * A_HEADS),
                               kva.reshape(Tp, N_KV_BRANCH * A_KV_HEADS * HEAD_DIM), *cmp)
        C0 = jnp.zeros((Bp, B_HEADS, QK_DIM, V_DIM), jnp.float32)
        n0 = jnp.zeros((Bp, B_HEADS, QK_DIM), jnp.float32)
        m0 = jnp.zeros((Bp, B_HEADS), jnp.float32)
        yb, C, n, m = mlstm(qb, kb, vb, ib, fb, ob, C0, n0, m0)
        kv_p.append(kva[:, :, 0:4])
        win_p.append(kva[:, -min(WINDOW, Tp):, 4:6])
        C_p.append(C)
        n_p.append(n)
        m_p.append(m)
        xp = merge_and_mlp(xp, ya.reshape(Bp * Tp, A_WIDTH), yb.reshape(Bp * Tp, B_WIDTH), gm, *mlp)

        qa, ga, kva, qb, kb, vb, ib, fb, ob, gm = split(xs, Bs, Ts)
        ya, win_new = nsa_sample(qa, ga, kva, cache_nsa_kv[l], page_table, cache_win_kv[l], *cmp)
        yb, C, n, m = mlstm(qb, kb, vb, ib, fb, ob, state_C[l], state_n[l], state_m[l])
        kv_s.append(kva[:, :, 0:4])
        win_s.append(win_new)
        C_s.append(C)
        n_s.append(n)
        m_s.append(m)
        xs = merge_and_mlp(xs, ya.reshape(Bs * Ts, A_WIDTH), yb.reshape(Bs * Ts, B_WIDTH), gm, *mlp)

    return (xp.reshape(Bp, Tp, D_MODEL), xs.reshape(Bs, Ts, D_MODEL),
            jnp.stack(kv_p), jnp.stack(kv_s), jnp.stack(win_p), jnp.stack(win_s),
            jnp.stack(C_p), jnp.stack(C_s), jnp.stack(n_p), jnp.stack(n_s), jnp.stack(m_p), jnp.stack(m_s))
```

```python
import functools
import math

import jax
import jax.numpy as jnp
import numpy as np
from jax import lax
from jax.experimental import pallas as pl
from jax.experimental.pallas import tpu as pltpu

D_MODEL = 2048
DEPTH = 2
PAST_LEN = 2048
A_HEADS = 16
A_KV_HEADS = 4
A_GROUP = A_HEADS // A_KV_HEADS
HEAD_DIM = 64
A_WIDTH = A_HEADS * HEAD_DIM
CMP_BLOCK = 32
CMP_STRIDE = 16
CMP_RATIO = CMP_BLOCK // CMP_STRIDE
CMP_HIDDEN = 128
SLC_BLOCK = 64
SLC_TOP = 16
WINDOW = 512
Q_BLOCK = 128
N_KV_BRANCH = 6
B_HEADS = 8
QK_DIM = 64
V_DIM = 128
B_WIDTH = B_HEADS * V_DIM
MLSTM_CHUNK = 64
D_FF = 4 * D_MODEL
EPS = 1e-6
FORCE = 1e4
NEG = -1e30

SPLITS = (A_WIDTH, N_KV_BRANCH * A_KV_HEADS * HEAD_DIM, 3 * A_HEADS,
          B_HEADS * QK_DIM, B_HEADS * QK_DIM, B_WIDTH, B_HEADS, B_HEADS, B_WIDTH, 2 * D_MODEL)
BOUNDS = tuple(int(b) for b in np.cumsum((0,) + SPLITS))

VMEM_LIMIT_BYTES = 56 * 1024 * 1024


def _params(*sem):
    return pltpu.CompilerParams(dimension_semantics=sem, vmem_limit_bytes=VMEM_LIMIT_BYTES)


def _norm_matmul_kernel(x_ref, g_ref, w_ref, b_ref, o_ref, h_ref, *, act):
    @pl.when(pl.program_id(1) == 0)
    def _():
        x = x_ref[...]
        r = lax.rsqrt(jnp.mean(x * x, axis=-1, keepdims=True) + EPS)
        h_ref[...] = (x * r * g_ref[...]).astype(h_ref.dtype)

    z = jnp.dot(h_ref[...], w_ref[...], preferred_element_type=jnp.float32) + b_ref[...]
    if act == "relu2":
        z = jnp.square(jnp.maximum(z, 0.0))
    o_ref[...] = z.astype(o_ref.dtype)


def norm_matmul(x, g, w, b, *, act=None, out_dtype=jnp.float32, tm=1024, tn=512):
    M, K = x.shape
    N = w.shape[1]
    tm = min(tm, M)
    tn = min(tn, N)
    assert M % tm == 0 and N % tn == 0, (M, N, tm, tn)
    return pl.pallas_call(
        functools.partial(_norm_matmul_kernel, act=act),
        grid=(M // tm, N // tn),
        in_specs=[
            pl.BlockSpec((tm, K), lambda i, j: (i, 0)),
            pl.BlockSpec((1, K), lambda i, j: (0, 0)),
            pl.BlockSpec((K, tn), lambda i, j: (0, j)),
            pl.BlockSpec((1, tn), lambda i, j: (0, j)),
        ],
        out_specs=pl.BlockSpec((tm, tn), lambda i, j: (i, j)),
        out_shape=jax.ShapeDtypeStruct((M, N), out_dtype),
        scratch_shapes=[pltpu.VMEM((tm, K), jnp.bfloat16)],
        compiler_params=_params("parallel", "arbitrary"),
        name="norm_matmul",
    )(x, g.reshape(1, K), w, b.reshape(1, N))


def _mix_kernel(ya_ref, yb_ref, wa_ref, wb_ref, ga_ref, gb_ref, o_ref):
    a = jnp.dot(ya_ref[...].astype(jnp.bfloat16), wa_ref[...], preferred_element_type=jnp.float32)
    b = jnp.dot(yb_ref[...].astype(jnp.bfloat16), wb_ref[...], preferred_element_type=jnp.float32)
    o_ref[...] = (jax.nn.sigmoid(ga_ref[...]) * a + jax.nn.sigmoid(gb_ref[...]) * b).astype(o_ref.dtype)


def mix_matmul(ya, yb, wa, wb, gm, *, tm=1024, tn=512):
    M, Ka = ya.shape
    Kb = yb.shape[1]
    N = wa.shape[1]
    tm = min(tm, M)
    nb = N // tn
    return pl.pallas_call(
        _mix_kernel,
        grid=(M // tm, nb),
        in_specs=[
            pl.BlockSpec((tm, Ka), lambda i, j: (i, 0)),
            pl.BlockSpec((tm, Kb), lambda i, j: (i, 0)),
            pl.BlockSpec((Ka, tn), lambda i, j: (0, j)),
            pl.BlockSpec((Kb, tn), lambda i, j: (0, j)),
            pl.BlockSpec((tm, tn), lambda i, j: (i, j)),
            pl.BlockSpec((tm, tn), lambda i, j: (i, j + nb)),
        ],
        out_specs=pl.BlockSpec((tm, tn), lambda i, j: (i, j)),
        out_shape=jax.ShapeDtypeStruct((M, N), jnp.bfloat16),
        compiler_params=_params("parallel", "arbitrary"),
        name="mix_matmul",
    )(ya, yb, wa, wb, gm, gm)


def _matmul_norm_res_kernel(a_ref, w_ref, x_ref, g_ref, o_ref, acc_ref):
    k = pl.program_id(1)

    @pl.when(k == 0)
    def _():
        acc_ref[...] = jnp.zeros_like(acc_ref)

    acc_ref[...] += jnp.dot(a_ref[...], w_ref[...], preferred_element_type=jnp.float32)

    @pl.when(k == pl.num_programs(1) - 1)
    def _():
        y = acc_ref[...]
        r = lax.rsqrt(jnp.mean(y * y, axis=-1, keepdims=True) + EPS)
        o_ref[...] = x_ref[...] + y * r * g_ref[...]


def matmul_norm_res(a, w, x, g, *, tm=1024, tk=512):
    M, K = a.shape
    N = w.shape[1]
    tm = min(tm, M)
    tk = min(tk, K)
    return pl.pallas_call(
        _matmul_norm_res_kernel,
        grid=(M // tm, K // tk),
        in_specs=[
            pl.BlockSpec((tm, tk), lambda i, k: (i, k)),
            pl.BlockSpec((tk, N), lambda i, k: (k, 0)),
            pl.BlockSpec((tm, N), lambda i, k: (i, 0)),
            pl.BlockSpec((1, N), lambda i, k: (0, 0)),
        ],
        out_specs=pl.BlockSpec((tm, N), lambda i, k: (i, 0)),
        out_shape=jax.ShapeDtypeStruct((M, N), jnp.float32),
        scratch_shapes=[pltpu.VMEM((tm, N), jnp.float32)],
        compiler_params=_params("parallel", "arbitrary"),
        name="matmul_norm_res",
    )(a, w, x, g.reshape(1, N))


NSA_LANES = A_GROUP * Q_BLOCK
KEY_CHUNK = 128
SLC_SLOTS = 128
POS_PIECES = 6
AUG = 128
MASK_BIG = 30000.0


def _pos_pieces(pos):
    pos = np.asarray(pos, np.int64)
    hi = (pos // 64) * 64
    lo = pos % 64
    return np.stack([hi, lo] * 3, axis=-1).astype(np.float32)


def _slope_rows():
    slopes = 2.0 ** (-8.0 * np.arange(1, A_HEADS + 1, dtype=np.float64) / A_HEADS)
    s = jnp.asarray(slopes.astype(np.float32))
    hi = s.astype(jnp.bfloat16)
    r1 = s - hi.astype(jnp.float32)
    mid = r1.astype(jnp.bfloat16)
    lo = (r1 - mid.astype(jnp.float32)).astype(jnp.bfloat16)
    rows = jnp.stack([hi, hi, mid, mid, lo, lo], axis=0)
    rows = rows.reshape(POS_PIECES, A_KV_HEADS, A_GROUP).transpose(1, 0, 2)
    rows = jnp.repeat(rows, Q_BLOCK, axis=-1)
    return jnp.pad(rows, ((0, 0), (0, 16 - POS_PIECES), (0, 0)))


def _split3(x):
    hi = x.astype(jnp.bfloat16)
    r1 = x - hi.astype(jnp.float32)
    mid = r1.astype(jnp.bfloat16)
    lo = (r1 - mid.astype(jnp.float32)).astype(jnp.bfloat16)
    return hi, mid, lo


def _tile4(x):
    return jnp.concatenate([x] * A_GROUP, axis=1)


def _softmax_step(carry, s, vT):
    m, l, acc = carry
    m_new = jnp.maximum(m, jnp.max(s, axis=0, keepdims=True))
    alpha = jnp.exp(m - m_new)
    p = jnp.exp(s - m_new)
    l = alpha * l + jnp.sum(p, axis=0, keepdims=True)
    acc = alpha * acc + jnp.dot(vT, p.astype(jnp.bfloat16), preferred_element_type=jnp.float32)
    return m_new, l, acc


def _nsa_prompt_kernel(qT_ref, slope_ref, gT_ref, ck_ref, cvT_ref, ovT_ref, ks_ref, vsT_ref,
                       kw_ref, vwT_ref, o_ref, qa_ref):
    f32, bf = jnp.float32, jnp.bfloat16
    i = pl.program_id(1)
    s0 = i * Q_BLOCK
    ncp = ck_ref.shape[1]

    qT = qT_ref[...] * (HEAD_DIM ** -0.5)
    qcat = jnp.concatenate([qT[g * HEAD_DIM:(g + 1) * HEAD_DIM, :] for g in range(A_GROUP)], axis=1)
    qa_ref[0:HEAD_DIM, :] = qcat.astype(bf)
    qa_ref[HEAD_DIM:HEAD_DIM + 16, :] = slope_ref[0]
    qa_ref[HEAD_DIM + 16:AUG, :] = jnp.zeros((AUG - HEAD_DIM - 16, NSA_LANES), bf)
    qc = qa_ref[0:AUG, :]

    tt = lax.broadcasted_iota(jnp.int32, (Q_BLOCK, Q_BLOCK), 1)
    jj = lax.broadcasted_iota(jnp.int32, (Q_BLOCK, Q_BLOCK), 0)

    sc = jnp.dot(ck_ref[0], qc, preferred_element_type=f32)
    n_idx = lax.broadcasted_iota(jnp.int32, (ncp, Q_BLOCK), 0)
    t_idx = s0 + lax.broadcasted_iota(jnp.int32, (ncp, Q_BLOCK), 1)
    valid = _tile4(n_idx * CMP_STRIDE + (CMP_BLOCK - 1) <= t_idx)
    sc = jnp.where(valid, sc, NEG)
    ec = jnp.where(valid, jnp.exp(sc - jnp.max(sc, axis=0, keepdims=True)), 0.0)
    lc = jnp.sum(ec, axis=0, keepdims=True)
    pc = ec * jnp.where(lc > 0.0, 1.0 / lc, 0.0)
    o_c = jnp.dot(cvT_ref[0], pc.astype(bf), preferred_element_type=f32)

    pg = pc[:, 0:Q_BLOCK]
    for g in range(1, A_GROUP):
        pg = pg + pc[:, g * Q_BLOCK:(g + 1) * Q_BLOCK]
    imp = sum(jnp.dot(ovT_ref[...], piece, preferred_element_type=f32) for piece in _split3(pg))

    blk = lax.broadcasted_iota(jnp.int32, (SLC_SLOTS, Q_BLOCK), 0)
    cur = (s0 + lax.broadcasted_iota(jnp.int32, (SLC_SLOTS, Q_BLOCK), 1)) // SLC_BLOCK
    forced = (blk == 0) | (blk == cur) | (blk == cur - 1)
    causal = blk <= cur
    score = jnp.where(forced, FORCE, jnp.where(causal, imp, -FORCE))
    blk_f = blk.astype(f32)
    sel = jnp.zeros((SLC_SLOTS, Q_BLOCK), f32)
    for _ in range(SLC_TOP):
        top = jnp.max(score, axis=0, keepdims=True)
        first = jnp.min(jnp.where(score == top, blk_f, float(SLC_SLOTS)), axis=0, keepdims=True)
        pick = blk_f == first
        sel = jnp.where(pick, 1.0, sel)
        score = jnp.where(pick, -jnp.inf, score)
    selneg = jnp.where((sel > 0.0) & causal, 0.0, -MASK_BIG)
    qa_ref[AUG:AUG + SLC_SLOTS, :] = _tile4(selneg).astype(bf)

    def sel_scores(c):
        kc = ks_ref[0, pl.ds(pl.multiple_of(c * KEY_CHUNK, KEY_CHUNK), KEY_CHUNK), :]
        return jnp.dot(kc, qa_ref[...], preferred_element_type=f32)

    init = (jnp.full((1, NSA_LANES), NEG, f32), jnp.zeros((1, NSA_LANES), f32),
            jnp.zeros((HEAD_DIM, NSA_LANES), f32))
    carry = lax.fori_loop(0, i, lambda c, cr: _softmax_step(cr, sel_scores(c), vsT_ref[0, c]), init)
    diag = _tile4(jj <= tt)
    m_s, l_s, acc_s = _softmax_step(carry, jnp.where(diag, sel_scores(i), NEG), vsT_ref[0, i])
    o_s = acc_s * (1.0 / l_s)

    carry = init
    nwc = WINDOW // KEY_CHUNK
    for d in range(nwc + 1):
        c = i - nwc + d
        cc = jnp.maximum(c, 0)
        kc = kw_ref[0, pl.ds(pl.multiple_of(cc * KEY_CHUNK, KEY_CHUNK), KEY_CHUNK), :]
        s = jnp.dot(kc, qc, preferred_element_type=f32)
        if d == 0:
            s = jnp.where(_tile4(jj > tt) & (c >= 0), s, NEG)
        elif d == nwc:
            s = jnp.where(diag, s, NEG)
        else:
            s = jnp.where(c >= 0, s, NEG)
        carry = _softmax_step(carry, s, vwT_ref[0, cc])
    m_w, l_w, acc_w = carry
    o_w = acc_w * (1.0 / l_w)

    gate = jax.nn.sigmoid(gT_ref[0])
    for g in range(A_GROUP):
        cols = slice(g * Q_BLOCK, (g + 1) * Q_BLOCK)
        o = (gate[3 * g:3 * g + 1, :] * o_c[:, cols] + gate[3 * g + 1:3 * g + 2, :] * o_s[:, cols]
             + gate[3 * g + 2:3 * g + 3, :] * o_w[:, cols])
        o_ref[g * HEAD_DIM:(g + 1) * HEAD_DIM, :] = o


def _cmp_prompt_kernel(x_ref, pe_ref, w1_ref, b1_ref, w2T_ref, ck_ref, cvT_ref, bias_ref):
    f32, bf = jnp.float32, jnp.bfloat16
    nseg = ck_ref.shape[1]
    row_w = 2 * A_KV_HEADS * HEAD_DIM
    for c in range(2):
        bias = jnp.dot(pe_ref[c], w1_ref[c], preferred_element_type=f32,
                       precision=lax.Precision.HIGHEST) + b1_ref[c]
        bias_ref[c] = bias
        for k in range(A_KV_HEADS):
            col = (c * A_KV_HEADS + k) * HEAD_DIM
            y0 = jnp.zeros((nseg, CMP_HIDDEN), f32)
            y1 = jnp.zeros((nseg, CMP_HIDDEN), f32)
            for j in range(CMP_STRIDE):
                xj = x_ref[:, j * row_w + col:j * row_w + col + HEAD_DIM].astype(bf)
                w0 = w1_ref[c, j * HEAD_DIM:(j + 1) * HEAD_DIM, :].astype(bf)
                w1 = w1_ref[c, (CMP_STRIDE + j) * HEAD_DIM:(CMP_STRIDE + j + 1) * HEAD_DIM, :].astype(bf)
                y0 = y0 + jnp.dot(xj, w0, preferred_element_type=f32)
                y1 = y1 + jnp.dot(xj, w1, preferred_element_type=f32)
            pre = y0 + pltpu.roll(y1, nseg - 1, 0)
            hid = jnp.maximum(pre + bias, 0.0).astype(bf)
            if c == 0:
                ck_ref[k] = lax.dot_general(hid, w2T_ref[c].astype(bf), (((1,), (1,)), ((), ())),
                                            preferred_element_type=f32)
            else:
                cvT_ref[k] = lax.dot_general(w2T_ref[c].astype(bf), hid, (((1,), (1,)), ((), ())),
                                             preferred_element_type=f32)


def compress_prompt(x_cmp, pe, w1, b1, w2):
    T = x_cmp.shape[0]
    nseg = T // CMP_STRIDE
    return pl.pallas_call(
        _cmp_prompt_kernel,
        out_shape=(jax.ShapeDtypeStruct((A_KV_HEADS, nseg, HEAD_DIM), jnp.float32),
                   jax.ShapeDtypeStruct((A_KV_HEADS, HEAD_DIM, nseg), jnp.float32),
                   jax.ShapeDtypeStruct((2, 1, CMP_HIDDEN), jnp.float32)),
        compiler_params=pltpu.CompilerParams(vmem_limit_bytes=VMEM_LIMIT_BYTES),
        name="compress_prompt",
    )(x_cmp.reshape(nseg, CMP_STRIDE * x_cmp.shape[1]), pe.reshape(2, 1, CMP_BLOCK * HEAD_DIM), w1, b1.reshape(2, 1, CMP_HIDDEN), w2.transpose(0, 2, 1))


def nsa_prompt_pallas(qa, ga, kva, pe, w1, b1, w2):
    bf = jnp.bfloat16
    T = qa.shape[0]
    nqb = T // Q_BLOCK
    nch = T // KEY_CHUNK
    ncp = T // CMP_STRIDE
    kvw = A_KV_HEADS * HEAD_DIM
    ck, cvT, pe_bias = compress_prompt(kva[:, 0:2 * kvw], pe, w1, b1, w2)

    def heads(x):
        return x.reshape(T, A_KV_HEADS, HEAD_DIM).transpose(1, 0, 2)

    def chunksT(x):
        return x.reshape(nch, KEY_CHUNK, A_KV_HEADS, HEAD_DIM).transpose(2, 0, 3, 1).astype(bf)

    pos = np.arange(T)
    key_pos = jnp.broadcast_to(jnp.asarray(_pos_pieces(pos)), (A_KV_HEADS, T, POS_PIECES))
    zpad = jnp.zeros((A_KV_HEADS, T, AUG - HEAD_DIM - POS_PIECES), jnp.float32)
    onehot = jnp.broadcast_to(jnp.asarray((pos[:, None] // SLC_BLOCK == np.arange(SLC_SLOTS)[None, :]).astype(np.float32)),
                              (A_KV_HEADS, T, SLC_SLOTS))
    ks_aug = jnp.concatenate([heads(kva[:, 2 * kvw:3 * kvw]), key_pos, zpad, onehot], axis=-1).astype(bf)
    kw_aug = jnp.concatenate([heads(kva[:, 4 * kvw:5 * kvw]), key_pos, zpad], axis=-1).astype(bf)
    c_end = np.arange(ncp) * CMP_STRIDE + (CMP_BLOCK - 1)
    ck_aug = jnp.concatenate([ck, jnp.broadcast_to(jnp.asarray(_pos_pieces(c_end)), (A_KV_HEADS, ncp, POS_PIECES)),
                              jnp.zeros((A_KV_HEADS, ncp, AUG - HEAD_DIM - POS_PIECES), jnp.float32)], axis=-1).astype(bf)
    vsT = chunksT(kva[:, 3 * kvw:4 * kvw])
    vwT = chunksT(kva[:, 5 * kvw:6 * kvw])
    c_start = c_end - (CMP_BLOCK - 1)
    s_start = np.arange(SLC_SLOTS) * SLC_BLOCK
    ovT = ((c_start[None, :] < s_start[:, None] + SLC_BLOCK) & (c_end[None, :] >= s_start[:, None])
           & (np.arange(ncp)[None, :] < ncp - 1))
    ovT = jnp.asarray(ovT.astype(np.float32)).astype(bf)
    gT = jnp.pad(ga.T.reshape(A_KV_HEADS, 3 * A_GROUP, T), ((0, 0), (0, 16 - 3 * A_GROUP), (0, 0)))

    yT = pl.pallas_call(
        _nsa_prompt_kernel,
        grid=(A_KV_HEADS, nqb),
        in_specs=[
            pl.BlockSpec((A_GROUP * HEAD_DIM, Q_BLOCK), lambda k, i: (k, i)),
            pl.BlockSpec((1, 16, NSA_LANES), lambda k, i: (k, 0, 0)),
            pl.BlockSpec((1, 16, Q_BLOCK), lambda k, i: (k, 0, i)),
            pl.BlockSpec((1, ncp, AUG), lambda k, i: (k, 0, 0)),
            pl.BlockSpec((1, HEAD_DIM, ncp), lambda k, i: (k, 0, 0)),
            pl.BlockSpec((SLC_SLOTS, ncp), lambda k, i: (0, 0)),
            pl.BlockSpec((1, T, AUG + SLC_SLOTS), lambda k, i: (k, 0, 0)),
            pl.BlockSpec((1, nch, HEAD_DIM, KEY_CHUNK), lambda k, i: (k, 0, 0, 0)),
            pl.BlockSpec((1, T, AUG), lambda k, i: (k, 0, 0)),
            pl.BlockSpec((1, nch, HEAD_DIM, KEY_CHUNK), lambda k, i: (k, 0, 0, 0)),
        ],
        out_specs=pl.BlockSpec((A_GROUP * HEAD_DIM, Q_BLOCK), lambda k, i: (k, i)),
        out_shape=jax.ShapeDtypeStruct((A_WIDTH, T), jnp.float32),
        scratch_shapes=[pltpu.VMEM((AUG + SLC_SLOTS, NSA_LANES), bf)],
        compiler_params=_params("parallel", "arbitrary"),
        name="nsa_prompt",
    )(qa.T, _slope_rows(), gT, ck_aug, cvT.astype(bf), ovT, ks_aug, vsT, kw_aug, vwT)
    return yT.T, pe_bias


PAGE_SIZE = 128
N_PAGES = PAST_LEN // PAGE_SIZE
PAGE_SEGS = PAGE_SIZE // CMP_STRIDE
PAST_SEGS = PAST_LEN // CMP_STRIDE
ROW_W = 4 * A_KV_HEADS * HEAD_DIM
KVW = A_KV_HEADS * HEAD_DIM
NS_SAMPLE = PAST_LEN // SLC_BLOCK + 1


def _nt(a, b):
    return lax.dot_general(a, b, (((1,), (1,)), ((), ())), preferred_element_type=jnp.float32)


def _dot3(x, w):
    return sum(jnp.dot(p, w, preferred_element_type=jnp.float32) for p in _split3(x))


def _nsa_sample_kernel(pt_ref, q_ref, g_ref, new_ref, *refs):
    del pt_ref
    pages = refs[:N_PAGES]
    (win_ref, w1_ref, bias_ref, w2bd_ref, bmask_ref, basec_ref, bases_ref, basew_ref, slope_ref,
     grp_ref, ov_ref, exp_ref, o_ref, wout_ref, xc_ref) = refs[N_PAGES:]
    f32, bf = jnp.float32, jnp.bfloat16

    q = q_ref[0] * (HEAD_DIM ** -0.5)
    qbd = jnp.where(bmask_ref[...] > 0.0, jnp.concatenate([q] * A_KV_HEADS, axis=1), 0.0).astype(bf)
    qbd_f = qbd.astype(f32)
    new = new_ref[0]

    def slab(off, width):
        return jnp.concatenate([pg[0, :, off:off + width] for pg in pages], axis=0)

    ckv = []
    for c in range(2):
        for j in range(CMP_STRIDE):
            s = slab(j * ROW_W + c * KVW, KVW).astype(bf)
            for k in range(A_KV_HEADS):
                xc_ref[k * PAST_SEGS:(k + 1) * PAST_SEGS, j * HEAD_DIM:(j + 1) * HEAD_DIM] = (
                    s[:, k * HEAD_DIM:(k + 1) * HEAD_DIM])
        y = jnp.dot(xc_ref[...], w1_ref[c], preferred_element_type=f32)
        pre = y[:, 0:CMP_HIDDEN] + pltpu.roll(y[:, CMP_HIDDEN:2 * CMP_HIDDEN], A_KV_HEADS * PAST_SEGS - 1, 0)
        hid = jnp.maximum(pre + bias_ref[c], 0.0).astype(bf)
        hid = jnp.concatenate([hid[k * PAST_SEGS:(k + 1) * PAST_SEGS] for k in range(A_KV_HEADS)], axis=1)
        ckv.append(jnp.dot(hid, w2bd_ref[c], preferred_element_type=f32).astype(bf))
    ck, cv = ckv

    sc = _nt(qbd, ck) + basec_ref[...]
    ec = jnp.exp(sc - jnp.max(sc, axis=1, keepdims=True))
    pc = ec / jnp.sum(ec, axis=1, keepdims=True)
    o_c = jnp.dot(pc.astype(bf), cv, preferred_element_type=f32)

    pov = _dot3(pc, ov_ref[...])
    p_hi, p_mid, p_lo = _split3(pov)
    imp = sum(jnp.dot(grp_ref[...], p, preferred_element_type=f32) for p in (p_hi, p_mid, p_lo))

    lane = lax.broadcasted_iota(jnp.int32, (A_HEADS, PAST_SEGS), 1)
    score = jnp.where((lane == 0) | (lane == NS_SAMPLE - 1) | (lane == NS_SAMPLE - 2), FORCE,
                      jnp.where(lane < NS_SAMPLE, imp, -jnp.inf))
    rr = lax.broadcasted_iota(jnp.int32, (PAST_SEGS, PAST_SEGS), 0)
    cc = lax.broadcasted_iota(jnp.int32, (PAST_SEGS, PAST_SEGS), 1)
    row_id = lax.broadcasted_iota(jnp.int32, (A_HEADS, PAST_SEGS), 0)
    selrows = jnp.zeros((A_HEADS, PAST_SEGS), f32)
    for k in range(A_KV_HEADS):
        row = jnp.broadcast_to(score[A_GROUP * k:A_GROUP * k + 1, :], (PAST_SEGS, PAST_SEGS))
        col = jnp.max(jnp.where(rr == cc, row, -jnp.inf), axis=1, keepdims=True)
        ahead = (col > row) | ((col == row) & (rr < cc))
        rank = jnp.sum(jnp.where(ahead, 1.0, 0.0), axis=0, keepdims=True)
        sel_k = jnp.where(rank < float(SLC_TOP), 1.0, 0.0)
        selrows = jnp.where(row_id // A_GROUP == k, jnp.broadcast_to(sel_k, (A_HEADS, PAST_SEGS)), selrows)
    selseg = jnp.dot(selrows.astype(bf), exp_ref[...], preferred_element_type=f32)
    maskbias = jnp.where(selseg > 0.5, 0.0, NEG)

    slope = slope_ref[...]
    scores = []
    for j in range(CMP_STRIDE):
        kj = slab(j * ROW_W + 2 * KVW, KVW).astype(bf)
        scores.append(_nt(qbd, kj) + (bases_ref[...] + slope * float(j)) + maskbias)
    s_new = jnp.sum(qbd_f * new[:, 2 * KVW:3 * KVW], axis=1, keepdims=True)
    mx = scores[0]
    for s in scores[1:]:
        mx = jnp.maximum(mx, s)
    m = jnp.maximum(jnp.max(mx, axis=1, keepdims=True), s_new)
    p_new = jnp.exp(s_new - m)
    acc = p_new * new[:, 3 * KVW:4 * KVW]
    psum = jnp.zeros((A_HEADS, PAST_SEGS), f32)
    for j in range(CMP_STRIDE):
        p = jnp.exp(scores[j] - m)
        psum = psum + p
        vj = slab(j * ROW_W + 3 * KVW, KVW).astype(bf)
        acc = acc + jnp.dot(p.astype(bf), vj, preferred_element_type=f32)
    o_s = acc / (jnp.sum(psum, axis=1, keepdims=True) + p_new)

    kw = win_ref[0, :, 0:KVW].astype(bf)
    vw = win_ref[0, :, KVW:2 * KVW].astype(bf)
    sw = _nt(qbd, kw) + basew_ref[...]
    s_neww = jnp.sum(qbd_f * new[:, 4 * KVW:5 * KVW], axis=1, keepdims=True)
    m = jnp.maximum(jnp.max(sw, axis=1, keepdims=True), s_neww)
    pw = jnp.exp(sw - m)
    p_new = jnp.exp(s_neww - m)
    o_w = (jnp.dot(pw.astype(bf), vw, preferred_element_type=f32) + p_new * new[:, 5 * KVW:6 * KVW]) / (
        jnp.sum(pw, axis=1, keepdims=True) + p_new)

    gate = jax.nn.sigmoid(g_ref[0])
    o = gate[:, 0:1] * o_c + gate[:, 1:2] * o_s + gate[:, 2:3] * o_w
    for k in range(A_KV_HEADS):
        o_ref[0, A_GROUP * k:A_GROUP * (k + 1), :] = o[A_GROUP * k:A_GROUP * (k + 1), k * HEAD_DIM:(k + 1) * HEAD_DIM]

    wout_ref[0, 0:WINDOW - 1, :] = win_ref[0, 1:WINDOW, :]
    wout_ref[0, WINDOW - 1:WINDOW, :] = new[:, 4 * KVW:6 * KVW]


def _sample_constants():
    slopes = (2.0 ** (-8.0 * np.arange(1, A_HEADS + 1, dtype=np.float64) / A_HEADS)).astype(np.float32)[:, None]
    heads = np.arange(A_HEADS)[:, None]
    bmask = (np.arange(KVW)[None, :] // HEAD_DIM == heads // A_GROUP).astype(np.float32)
    n = np.arange(PAST_SEGS)[None, :]
    c_end = n * CMP_STRIDE + (CMP_BLOCK - 1)
    basec = np.where(n < PAST_SEGS - 1, slopes * (c_end - PAST_LEN), NEG).astype(np.float32)
    bases = (slopes * (n * CMP_STRIDE - PAST_LEN)).astype(np.float32)
    r = np.arange(WINDOW)[None, :]
    basew = np.where(r >= 1, slopes * (r - WINDOW), NEG).astype(np.float32)
    slope = np.broadcast_to(slopes, (A_HEADS, PAST_SEGS)).astype(np.float32)
    grp = (heads // A_GROUP == heads.T // A_GROUP).astype(np.float32)
    c_start = np.arange(PAST_SEGS) * CMP_STRIDE
    s_start = np.arange(PAST_SEGS) * SLC_BLOCK
    ov = ((c_start[:, None] < s_start[None, :] + SLC_BLOCK) & (c_start[:, None] + CMP_BLOCK - 1 >= s_start[None, :])
          & (np.arange(PAST_SEGS)[:, None] < PAST_SEGS - 1) & (np.arange(PAST_SEGS)[None, :] < NS_SAMPLE))
    expand = (np.arange(PAST_SEGS)[None, :] * CMP_STRIDE // SLC_BLOCK == np.arange(PAST_SEGS)[:, None])
    as_bf = lambda a: jnp.asarray(a.astype(np.float32)).astype(jnp.bfloat16)
    return (jnp.asarray(bmask), jnp.asarray(basec), jnp.asarray(bases), jnp.asarray(basew), jnp.asarray(slope),
            as_bf(grp), as_bf(ov), as_bf(expand))


def nsa_sample_pallas(qa, ga, kva, pool, page_table, win_buf, layer, w1, w2, pe_bias):
    bf = jnp.bfloat16
    B = qa.shape[0]
    npool = pool.shape[1]
    page_table = page_table + layer * npool
    seg_w = CMP_STRIDE * ROW_W
    w1r = w1.reshape(2, CMP_RATIO, CMP_STRIDE * HEAD_DIM, CMP_HIDDEN).transpose(0, 2, 1, 3)
    w1r = w1r.reshape(2, CMP_STRIDE * HEAD_DIM, CMP_RATIO * CMP_HIDDEN).astype(bf)
    w2bd = jnp.einsum('kl,ced->ckeld', jnp.eye(A_KV_HEADS, dtype=w2.dtype), w2)
    w2bd = w2bd.reshape(2, A_KV_HEADS * CMP_HIDDEN, KVW).astype(bf)
    consts = _sample_constants()
    full = lambda shape: pl.BlockSpec(shape, lambda b, pt: (0,) * len(shape))
    page_specs = [pl.BlockSpec((1, PAGE_SEGS, seg_w), functools.partial(lambda p, b, pt: (pt[b, p], 0, 0), p))
                  for p in range(N_PAGES)]
    grid_spec = pltpu.PrefetchScalarGridSpec(
        num_scalar_prefetch=1,
        grid=(B,),
        in_specs=[
            pl.BlockSpec((1, A_HEADS, HEAD_DIM), lambda b, pt: (b, 0, 0)),
            pl.BlockSpec((1, A_HEADS, 3), lambda b, pt: (b, 0, 0)),
            pl.BlockSpec((1, 1, N_KV_BRANCH * KVW), lambda b, pt: (b, 0, 0)),
            *page_specs,
            pl.BlockSpec((1, WINDOW, 2 * KVW), lambda b, pt: (layer * B + b, 0, 0)),
            full(w1r.shape), full((2, 1, CMP_HIDDEN)), full(w2bd.shape),
            *[full(c.shape) for c in consts],
        ],
        out_specs=[
            pl.BlockSpec((1, A_HEADS, HEAD_DIM), lambda b, pt: (b, 0, 0)),
            pl.BlockSpec((1, WINDOW, 2 * KVW), lambda b, pt: (b, 0, 0)),
        ],
        scratch_shapes=[pltpu.VMEM((A_KV_HEADS * PAST_SEGS, CMP_STRIDE * HEAD_DIM), bf)],
    )
    pool3 = pool.reshape(pool.shape[0] * npool, PAGE_SEGS, seg_w)
    o, wout = pl.pallas_call(
        _nsa_sample_kernel,
        grid_spec=grid_spec,
        out_shape=(jax.ShapeDtypeStruct((B, A_HEADS, HEAD_DIM), jnp.float32),
                   jax.ShapeDtypeStruct((B, WINDOW, 2 * KVW), jnp.float32)),
        compiler_params=_params("arbitrary"),
        name="nsa_sample",
    )(page_table, qa.reshape(B, A_HEADS, HEAD_DIM), ga.reshape(B, A_HEADS, 3), kva.reshape(B, 1, N_KV_BRANCH * KVW),
      *([pool3] * N_PAGES), win_buf.reshape(win_buf.shape[0] * B, WINDOW, 2 * KVW), w1r, pe_bias.reshape(2, 1, CMP_HIDDEN), w2bd, *consts)
    return o.reshape(B, A_WIDTH), wout


MLSTM_L = 128
VAUG = 2 * V_DIM


def _mlstm_prompt_kernel(q_ref, kT_ref, v_ref, o_ref, f_ref, iT_ref, fT_ref, y_ref, c_ref, m_ref):
    f32, bf = jnp.float32, jnp.bfloat16
    L = MLSTM_L

    @pl.when(pl.program_id(0) == 0)
    def _():
        c_ref[...] = jnp.zeros_like(c_ref)
        m_ref[...] = jnp.zeros_like(m_ref)

    rr = lax.broadcasted_iota(jnp.int32, (L, L), 0)
    cc = lax.broadcasted_iota(jnp.int32, (L, L), 1)
    lower = rr >= cc
    tril = jnp.where(lower, 1.0, 0.0).astype(bf)
    triu = jnp.where(rr <= cc, 1.0, 0.0).astype(bf)
    b_col = sum(jnp.dot(tril, p, preferred_element_type=f32) for p in _split3(jax.nn.log_sigmoid(f_ref[...])))
    b_row = _dot3(jax.nn.log_sigmoid(fT_ref[...]), triu)
    a_row = iT_ref[...] - b_row
    ones_col = jnp.where(lax.broadcasted_iota(jnp.int32, (L, V_DIM), 1) == 0, 1.0, 0.0)

    for h in range(B_HEADS):
        m_prev = m_ref[h, 0:1, 0:1]
        a = a_row[h:h + 1, :]
        amat = jnp.where(lower, jnp.broadcast_to(a, (L, L)), -jnp.inf)
        big_m = jnp.maximum(m_prev, jnp.max(amat, axis=1, keepdims=True))
        dmat = jnp.exp(amat - big_m)
        inter = jnp.exp(m_prev - big_m)
        q = q_ref[:, h * QK_DIM:(h + 1) * QK_DIM].astype(bf)
        kT = kT_ref[h * QK_DIM:(h + 1) * QK_DIM, :] * (QK_DIM ** -0.5)
        vaug = jnp.concatenate([v_ref[:, h * V_DIM:(h + 1) * V_DIM], ones_col], axis=1).astype(bf)
        s = jnp.dot(q, kT.astype(bf), preferred_element_type=f32) * dmat
        r = (jnp.dot(s.astype(bf), vaug, preferred_element_type=f32)
             + inter * jnp.dot(q, c_ref[h].astype(bf), preferred_element_type=f32))
        m_new = b_col[:, h:h + 1] + big_m
        den = jnp.maximum(jnp.abs(r[:, V_DIM:V_DIM + 1]), jnp.exp(-m_new))
        y_ref[:, h * V_DIM:(h + 1) * V_DIM] = (jax.nn.sigmoid(o_ref[:, h * V_DIM:(h + 1) * V_DIM])
                                               * (r[:, 0:V_DIM] / den))
        m_end = big_m[L - 1:L, :]
        w_end = jnp.exp(a - m_end)
        c_ref[h] = inter[L - 1:L, :] * c_ref[h] + jnp.dot((kT * w_end).astype(bf), vaug, preferred_element_type=f32)
        m_ref[h] = jnp.broadcast_to(m_new[L - 1:L, :], m_ref.shape[1:])


def mlstm_prompt_pallas(qb, kb, vb, ib, fb, ob):
    T = qb.shape[0]
    L = MLSTM_L
    y, caug, m = pl.pallas_call(
        _mlstm_prompt_kernel,
        grid=(T // L,),
        in_specs=[
            pl.BlockSpec((L, B_HEADS * QK_DIM), lambda c: (c, 0)),
            pl.BlockSpec((B_HEADS * QK_DIM, L), lambda c: (0, c)),
            pl.BlockSpec((L, B_WIDTH), lambda c: (c, 0)),
            pl.BlockSpec((L, B_WIDTH), lambda c: (c, 0)),
            pl.BlockSpec((L, B_HEADS), lambda c: (c, 0)),
            pl.BlockSpec((B_HEADS, L), lambda c: (0, c)),
            pl.BlockSpec((B_HEADS, L), lambda c: (0, c)),
        ],
        out_specs=[
            pl.BlockSpec((L, B_WIDTH), lambda c: (c, 0)),
            pl.BlockSpec((B_HEADS, QK_DIM, VAUG), lambda c: (0, 0, 0)),
            pl.BlockSpec((B_HEADS, 8, 128), lambda c: (0, 0, 0)),
        ],
        out_shape=(jax.ShapeDtypeStruct((T, B_WIDTH), jnp.float32),
                   jax.ShapeDtypeStruct((B_HEADS, QK_DIM, VAUG), jnp.float32),
                   jax.ShapeDtypeStruct((B_HEADS, 8, 128), jnp.float32)),
        compiler_params=_params("arbitrary"),
        name="mlstm_prompt",
    )(qb, kb.T, vb, ob, fb, ib.T, fb.T)
    return y, caug[:, :, 0:V_DIM], caug[:, :, V_DIM], m[:, 0, 0]


def _mlstm_sample_kernel(q_ref, k_ref, qT_ref, kT_ref, v_ref, o_ref, i_ref, f_ref, m_ref, n_ref, c_ref,
                         y_ref, cn_ref, nn_ref, mn_ref):
    bb = q_ref.shape[0]
    scale = QK_DIM ** -0.5
    for b in range(bb):
        logf = jax.nn.log_sigmoid(f_ref[b])
        m_new = jnp.maximum(logf + m_ref[b], i_ref[b])
        d_all = jnp.exp(i_ref[b] - m_new)
        inter_all = jnp.exp(logf + m_ref[b] - m_new)
        floor_all = jnp.exp(-m_new)
        mn_ref[b] = m_new
        qk_all = jnp.sum(q_ref[b] * k_ref[b], axis=1, keepdims=True) * scale
        qn_all = jnp.sum(q_ref[b] * n_ref[b], axis=1, keepdims=True)
        for h in range(B_HEADS):
            d = d_all[:, h:h + 1]
            inter = inter_all[:, h:h + 1]
            s = qk_all[h:h + 1, :] * d
            c = c_ref[b, h]
            v = v_ref[b, h:h + 1, :]
            qc = qT_ref[b, :, h:h + 1]
            kc = kT_ref[b, :, h:h + 1] * scale
            num = inter * jnp.sum(qc * c, axis=0, keepdims=True) + s * v
            den = inter * qn_all[h:h + 1, :] + s
            hout = num / jnp.maximum(jnp.abs(den), floor_all[:, h:h + 1])
            y_ref[b, h:h + 1, :] = jax.nn.sigmoid(o_ref[b, h:h + 1, :]) * hout
            cn_ref[b, h] = inter * c + d * (kc * v)
            nn_ref[b, h:h + 1, :] = inter * n_ref[b, h:h + 1, :] + d * (k_ref[b, h:h + 1, :] * scale)


def mlstm_sample_pallas(qb, kb, vb, ib, fb, ob, state_C, state_n, state_m, layer, *, bb=8):
    B = qb.shape[0]
    nb = B // bb
    q3 = qb.reshape(B, B_HEADS, QK_DIM)
    k3 = kb.reshape(B, B_HEADS, QK_DIM)
    row8 = lambda x: x.reshape(-1, 1, B_HEADS)
    lay = layer * nb
    y, cn, nn, mn = pl.pallas_call(
        _mlstm_sample_kernel,
        grid=(nb,),
        in_specs=[
            pl.BlockSpec((bb, B_HEADS, QK_DIM), lambda i: (i, 0, 0)),
            pl.BlockSpec((bb, B_HEADS, QK_DIM), lambda i: (i, 0, 0)),
            pl.BlockSpec((bb, QK_DIM, B_HEADS), lambda i: (i, 0, 0)),
            pl.BlockSpec((bb, QK_DIM, B_HEADS), lambda i: (i, 0, 0)),
            pl.BlockSpec((bb, B_HEADS, V_DIM), lambda i: (i, 0, 0)),
            pl.BlockSpec((bb, B_HEADS, V_DIM), lambda i: (i, 0, 0)),
            pl.BlockSpec((bb, 1, B_HEADS), lambda i: (i, 0, 0)),
            pl.BlockSpec((bb, 1, B_HEADS), lambda i: (i, 0, 0)),
            pl.BlockSpec((bb, 1, B_HEADS), lambda i: (lay + i, 0, 0)),
            pl.BlockSpec((bb, B_HEADS, QK_DIM), lambda i: (lay + i, 0, 0)),
            pl.BlockSpec((bb, B_HEADS, QK_DIM, V_DIM), lambda i: (lay + i, 0, 0, 0)),
        ],
        out_specs=[
            pl.BlockSpec((bb, B_HEADS, V_DIM), lambda i: (i, 0, 0)),
            pl.BlockSpec((bb, B_HEADS, QK_DIM, V_DIM), lambda i: (i, 0, 0, 0)),
            pl.BlockSpec((bb, B_HEADS, QK_DIM), lambda i: (i, 0, 0)),
            pl.BlockSpec((bb, 1, B_HEADS), lambda i: (i, 0, 0)),
        ],
        out_shape=(jax.ShapeDtypeStruct((B, B_HEADS, V_DIM), jnp.float32),
                   jax.ShapeDtypeStruct((B, B_HEADS, QK_DIM, V_DIM), jnp.float32),
                   jax.ShapeDtypeStruct((B, B_HEADS, QK_DIM), jnp.float32),
                   jax.ShapeDtypeStruct((B, 1, B_HEADS), jnp.float32)),
        compiler_params=_params("arbitrary"),
        name="mlstm_sample",
    )(q3, k3, q3.transpose(0, 2, 1), k3.transpose(0, 2, 1), vb.reshape(B, B_HEADS, V_DIM),
      ob.reshape(B, B_HEADS, V_DIM), row8(ib), row8(fb), row8(state_m),
      state_n.reshape(-1, B_HEADS, QK_DIM), state_C.reshape(-1, B_HEADS, QK_DIM, V_DIM))
    return y.reshape(B, B_WIDTH), cn, nn, mn.reshape(B, B_HEADS)


def project_in(x2d, g, w_in_bf, b_in):
    outs = []
    for lo, hi in zip(BOUNDS[:-1], BOUNDS[1:]):
        width = hi - lo
        pad = (-width) % 128
        w = w_in_bf[:, lo:hi]
        b = b_in[lo:hi]
        if pad:
            w = jnp.pad(w, ((0, 0), (0, pad)))
            b = jnp.pad(b, (0, pad))
        z = norm_matmul(x2d, g, w, b)
        outs.append(z[:, :width] if pad else z)
    return outs


def merge_and_mlp(x2d, ya, yb, gm, g, w_up_a, w_up_b, w_out, w_mlp1, w_mlp2):
    mix = mix_matmul(ya, yb, w_up_a, w_up_b, gm)
    x2d = matmul_norm_res(mix, w_out, x2d, g[1], tm=512, tk=D_MODEL)
    h = norm_matmul(x2d, g[2], w_mlp1, jnp.zeros((D_FF,), jnp.float32), act="relu2", out_dtype=jnp.bfloat16)
    return matmul_norm_res(h, w_mlp2, x2d, g[3])


def kernel(x_prompt, x_sample, cache_nsa_kv, cache_win_kv, state_C, state_n, state_m, page_table,
           norm_g, w_in, b_in, cmp_pe, cmp_w1, cmp_b1, cmp_w2, w_up_a, w_up_b, w_out, w_mlp1, w_mlp2):
    bf = jnp.bfloat16
    Bp, Tp = x_prompt.shape[:2]
    Bs, Ts = x_sample.shape[:2]
    assert Bp == 1 and Ts == 1 and Tp >= WINDOW, (x_prompt.shape, x_sample.shape)
    xp = x_prompt.reshape(Bp * Tp, D_MODEL)
    xs = x_sample.reshape(Bs * Ts, D_MODEL)
    kv_p, kv_s, win_p, win_s = [], [], [], []
    C_p, C_s, n_p, n_s, m_p, m_s = [], [], [], [], [], []
    for l in range(DEPTH):
        cmp = (cmp_pe[l], cmp_w1[l], cmp_b1[l], cmp_w2[l])
        w_in_bf = w_in[l].astype(bf)
        mlp = (norm_g[l], w_up_a[l].astype(bf), w_up_b[l].astype(bf), w_out[l].astype(bf),
               w_mlp1[l].astype(bf), w_mlp2[l].astype(bf))

        kv4 = 4 * KVW

        qa, kva, ga, qb, kb, vb, ib, fb, ob, gm = project_in(xp, norm_g[l, 0], w_in_bf, b_in[l])
        ya, pe_bias = nsa_prompt_pallas(qa, ga, kva, *cmp)
        yb, C, n, m = mlstm_prompt_pallas(qb, kb, vb, ib, fb, ob)
        kv_p.append(kva[:, 0:kv4].reshape(Bp, Tp, 4, A_KV_HEADS, HEAD_DIM))
        win_p.append(kva[Tp - WINDOW:, kv4:].reshape(Bp, WINDOW, 2, A_KV_HEADS, HEAD_DIM))
        C_p.append(C[None])
        n_p.append(n[None])
        m_p.append(m[None])
        xp = merge_and_mlp(xp, ya, yb, gm, *mlp)

        qa, kva, ga, qb, kb, vb, ib, fb, ob, gm = project_in(xs, norm_g[l, 0], w_in_bf, b_in[l])
        ya, win_new = nsa_sample_pallas(qa, ga, kva, cache_nsa_kv, page_table, cache_win_kv, l,
                                        cmp_w1[l], cmp_w2[l], pe_bias)
        yb, C, n, m = mlstm_sample_pallas(qb, kb, vb, ib, fb, ob, state_C, state_n, state_m, l)
        kv_s.append(kva[:, 0:kv4].reshape(Bs, Ts, 4, A_KV_HEADS, HEAD_DIM))
        win_s.append(win_new.reshape(Bs, WINDOW, 2, A_KV_HEADS, HEAD_DIM))
        C_s.append(C)
        n_s.append(n)
        m_s.append(m)
        xs = merge_and_mlp(xs, ya, yb, gm, *mlp)

    return (xp.reshape(Bp, Tp, D_MODEL), xs.reshape(Bs, Ts, D_MODEL),
            jnp.stack(kv_p), jnp.stack(kv_s), jnp.stack(win_p), jnp.stack(win_s),
            jnp.stack(C_p), jnp.stack(C_s), jnp.stack(n_p), jnp.stack(n_s), jnp.stack(m_p), jnp.stack(m_s))
```

```python
import functools
import math

import jax
import jax.numpy as jnp
import numpy as np
from jax import lax
from jax.experimental import pallas as pl
from jax.experimental.pallas import tpu as pltpu

D_MODEL = 2048
DEPTH = 2
PAST_LEN = 2048
A_HEADS = 16
A_KV_HEADS = 4
A_GROUP = A_HEADS // A_KV_HEADS
HEAD_DIM = 64
A_WIDTH = A_HEADS * HEAD_DIM
CMP_BLOCK = 32
CMP_STRIDE = 16
CMP_RATIO = CMP_BLOCK // CMP_STRIDE
CMP_HIDDEN = 128
SLC_BLOCK = 64
SLC_TOP = 16
WINDOW = 512
Q_BLOCK = 128
N_KV_BRANCH = 6
B_HEADS = 8
QK_DIM = 64
V_DIM = 128
B_WIDTH = B_HEADS * V_DIM
MLSTM_CHUNK = 64
D_FF = 4 * D_MODEL
EPS = 1e-6
FORCE = 1e4
NEG = -1e30

SPLITS = (A_WIDTH, N_KV_BRANCH * A_KV_HEADS * HEAD_DIM, 3 * A_HEADS,
          B_HEADS * QK_DIM, B_HEADS * QK_DIM, B_WIDTH, B_HEADS, B_HEADS, B_WIDTH, 2 * D_MODEL)
BOUNDS = tuple(int(b) for b in np.cumsum((0,) + SPLITS))

VMEM_LIMIT_BYTES = 56 * 1024 * 1024


def _params(*sem):
    return pltpu.CompilerParams(dimension_semantics=sem, vmem_limit_bytes=VMEM_LIMIT_BYTES)


def _row_tile(m, want):
    for t in range(min(want, m), 15, -1):
        if m % t == 0 and t % 16 == 0:
            return t
    return m


def _rmsnorm_kernel(x_ref, g_ref, o_ref):
    x = x_ref[...]
    r = lax.rsqrt(jnp.mean(x * x, axis=-1, keepdims=True) + EPS)
    o_ref[...] = (x * r * g_ref[...]).astype(o_ref.dtype)


def rmsnorm_cast(x, g, *, tm=640):
    M, K = x.shape
    tm = _row_tile(M, tm)
    return pl.pallas_call(
        _rmsnorm_kernel,
        grid=(M // tm,),
        in_specs=[pl.BlockSpec((tm, K), lambda i: (i, 0)), pl.BlockSpec((1, K), lambda i: (0, 0))],
        out_specs=pl.BlockSpec((tm, K), lambda i: (i, 0)),
        out_shape=jax.ShapeDtypeStruct((M, K), jnp.bfloat16),
        compiler_params=_params("parallel"),
        name="rmsnorm_cast",
    )(x, g.reshape(1, K))


def _matmul_act_kernel(h_ref, w_ref, b_ref, o_ref, *, nt, act):
    w = w_ref[...].astype(jnp.bfloat16)
    z = (_nt(h_ref[...], w) if nt else jnp.dot(h_ref[...], w, preferred_element_type=jnp.float32)) + b_ref[...]
    if act == "relu2":
        z = jnp.square(jnp.maximum(z, 0.0))
    o_ref[...] = z.astype(o_ref.dtype)


def matmul_act(h, w, b, *, nt, act=None, out_dtype=jnp.float32, tm=1664, tn=512):
    M, K = h.shape
    N = w.shape[0] if nt else w.shape[1]
    tm = _row_tile(M, tm)
    tn = min(tn, N)
    assert N % tn == 0, (N, tn)
    w_spec = pl.BlockSpec((tn, K), lambda i, j: (j, 0)) if nt else pl.BlockSpec((K, tn), lambda i, j: (0, j))
    return pl.pallas_call(
        functools.partial(_matmul_act_kernel, nt=nt, act=act),
        grid=(M // tm, N // tn),
        in_specs=[pl.BlockSpec((tm, K), lambda i, j: (i, 0)), w_spec, pl.BlockSpec((1, tn), lambda i, j: (0, j))],
        out_specs=pl.BlockSpec((tm, tn), lambda i, j: (i, j)),
        out_shape=jax.ShapeDtypeStruct((M, N), out_dtype),
        compiler_params=_params("parallel", "arbitrary"),
        name="matmul_act",
    )(h, w, b.reshape(1, N))


def _mix_kernel(ya_ref, yb_ref, wa_ref, wb_ref, ga_ref, gb_ref, o_ref):
    bf = jnp.bfloat16
    a = jnp.dot(ya_ref[...].astype(bf), wa_ref[...].astype(bf), preferred_element_type=jnp.float32)
    b = jnp.dot(yb_ref[...].astype(bf), wb_ref[...].astype(bf), preferred_element_type=jnp.float32)
    o_ref[...] = (jax.nn.sigmoid(ga_ref[...]) * a + jax.nn.sigmoid(gb_ref[...]) * b).astype(o_ref.dtype)


def mix_matmul(ya, yb, wa, wb, gm_src, *, tm=832, tn=512):
    M, Ka = ya.shape
    Kb = yb.shape[1]
    N = wa.shape[1]
    gm, off = gm_src
    tm = _row_tile(M, tm)
    nb = N // tn
    assert off % tn == 0 and N % tn == 0, (off, N, tn)
    ja, jb = off // tn, off // tn + nb
    return pl.pallas_call(
        _mix_kernel,
        grid=(M // tm, nb),
        in_specs=[
            pl.BlockSpec((tm, Ka), lambda i, j: (i, 0)),
            pl.BlockSpec((tm, Kb), lambda i, j: (i, 0)),
            pl.BlockSpec((Ka, tn), lambda i, j: (0, j)),
            pl.BlockSpec((Kb, tn), lambda i, j: (0, j)),
            pl.BlockSpec((tm, tn), lambda i, j: (i, ja + j)),
            pl.BlockSpec((tm, tn), lambda i, j: (i, jb + j)),
        ],
        out_specs=pl.BlockSpec((tm, tn), lambda i, j: (i, j)),
        out_shape=jax.ShapeDtypeStruct((M, N), jnp.bfloat16),
        compiler_params=_params("parallel", "arbitrary"),
        name="mix_matmul",
    )(ya, yb, wa, wb, gm, gm)


def _matmul_norm_res_kernel(a_ref, w_ref, x_ref, g_ref, o_ref, acc_ref):
    k = pl.program_id(1)

    @pl.when(k == 0)
    def _():
        acc_ref[...] = jnp.zeros_like(acc_ref)

    acc_ref[...] += jnp.dot(a_ref[...], w_ref[...].astype(jnp.bfloat16), preferred_element_type=jnp.float32)

    @pl.when(k == pl.num_programs(1) - 1)
    def _():
        y = acc_ref[...]
        r = lax.rsqrt(jnp.mean(y * y, axis=-1, keepdims=True) + EPS)
        o_ref[...] = x_ref[...] + y * r * g_ref[...]


def matmul_norm_res(a, w, x, g, *, tm=640, tk=512):
    M, K = a.shape
    N = w.shape[1]
    tm = _row_tile(M, tm)
    tk = min(tk, K)
    return pl.pallas_call(
        _matmul_norm_res_kernel,
        grid=(M // tm, K // tk),
        in_specs=[
            pl.BlockSpec((tm, tk), lambda i, k: (i, k)),
            pl.BlockSpec((tk, N), lambda i, k: (k, 0)),
            pl.BlockSpec((tm, N), lambda i, k: (i, 0)),
            pl.BlockSpec((1, N), lambda i, k: (0, 0)),
        ],
        out_specs=pl.BlockSpec((tm, N), lambda i, k: (i, 0)),
        out_shape=jax.ShapeDtypeStruct((M, N), jnp.float32),
        scratch_shapes=[pltpu.VMEM((tm, N), jnp.float32)],
        compiler_params=_params("parallel", "arbitrary"),
        name="matmul_norm_res",
    )(a, w, x, g.reshape(1, N))


NSA_LANES = A_GROUP * Q_BLOCK
KEY_CHUNK = 128
SLC_SLOTS = 128
POS_PIECES = 6
AUG = 128
MASK_BIG = 30000.0


def _pos_pieces(pos):
    pos = np.asarray(pos, np.int64)
    hi = (pos // 64) * 64
    lo = pos % 64
    return np.stack([hi, lo] * 3, axis=-1).astype(np.float32)


def _slope_rows():
    slopes = 2.0 ** (-8.0 * np.arange(1, A_HEADS + 1, dtype=np.float64) / A_HEADS)
    s = jnp.asarray(slopes.astype(np.float32))
    hi = s.astype(jnp.bfloat16)
    r1 = s - hi.astype(jnp.float32)
    mid = r1.astype(jnp.bfloat16)
    lo = (r1 - mid.astype(jnp.float32)).astype(jnp.bfloat16)
    rows = jnp.stack([hi, hi, mid, mid, lo, lo], axis=0)
    rows = rows.reshape(POS_PIECES, A_KV_HEADS, A_GROUP).transpose(1, 0, 2)
    rows = jnp.repeat(rows, Q_BLOCK, axis=-1)
    return jnp.pad(rows, ((0, 0), (0, 16 - POS_PIECES), (0, 0)))


def _split3(x):
    hi = x.astype(jnp.bfloat16)
    r1 = x - hi.astype(jnp.float32)
    mid = r1.astype(jnp.bfloat16)
    lo = (r1 - mid.astype(jnp.float32)).astype(jnp.bfloat16)
    return hi, mid, lo


def _tile4(x):
    return jnp.concatenate([x] * A_GROUP, axis=1)


def _softmax_step(carry, s, vT):
    m, l, acc = carry
    m_new = jnp.maximum(m, jnp.max(s, axis=0, keepdims=True))
    alpha = jnp.exp(m - m_new)
    p = jnp.exp(s - m_new)
    l = alpha * l + jnp.sum(p, axis=0, keepdims=True)
    acc = alpha * acc + jnp.dot(vT, p.astype(jnp.bfloat16), preferred_element_type=jnp.float32)
    return m_new, l, acc


def _nsa_prompt_kernel(qT_ref, slope_ref, gT_ref, ck_ref, cvT_ref, ovT_ref, ks_ref, vsT_ref,
                       kw_ref, vwT_ref, o_ref, qa_ref):
    f32, bf = jnp.float32, jnp.bfloat16
    i = pl.program_id(1)
    s0 = i * Q_BLOCK
    ncp = ck_ref.shape[1]

    qT = qT_ref[...] * (HEAD_DIM ** -0.5)
    qcat = jnp.concatenate([qT[g * HEAD_DIM:(g + 1) * HEAD_DIM, :] for g in range(A_GROUP)], axis=1)
    qa_ref[0:HEAD_DIM, :] = qcat.astype(bf)
    qa_ref[HEAD_DIM:HEAD_DIM + 16, :] = slope_ref[0]
    qa_ref[HEAD_DIM + 16:AUG, :] = jnp.zeros((AUG - HEAD_DIM - 16, NSA_LANES), bf)
    qc = qa_ref[0:AUG, :]

    tt = lax.broadcasted_iota(jnp.int32, (Q_BLOCK, Q_BLOCK), 1)
    jj = lax.broadcasted_iota(jnp.int32, (Q_BLOCK, Q_BLOCK), 0)

    sc = jnp.dot(ck_ref[0], qc, preferred_element_type=f32)
    n_idx = lax.broadcasted_iota(jnp.int32, (ncp, Q_BLOCK), 0)
    t_idx = s0 + lax.broadcasted_iota(jnp.int32, (ncp, Q_BLOCK), 1)
    valid = _tile4(n_idx * CMP_STRIDE + (CMP_BLOCK - 1) <= t_idx)
    sc = jnp.where(valid, sc, NEG)
    ec = jnp.where(valid, jnp.exp(sc - jnp.max(sc, axis=0, keepdims=True)), 0.0)
    lc = jnp.sum(ec, axis=0, keepdims=True)
    pc = ec * jnp.where(lc > 0.0, 1.0 / lc, 0.0)
    o_c = jnp.dot(cvT_ref[0], pc.astype(bf), preferred_element_type=f32)

    pg = pc[:, 0:Q_BLOCK]
    for g in range(1, A_GROUP):
        pg = pg + pc[:, g * Q_BLOCK:(g + 1) * Q_BLOCK]
    imp = sum(jnp.dot(ovT_ref[...], piece, preferred_element_type=f32) for piece in _split3(pg))

    blk = lax.broadcasted_iota(jnp.int32, (SLC_SLOTS, Q_BLOCK), 0)
    cur = (s0 + lax.broadcasted_iota(jnp.int32, (SLC_SLOTS, Q_BLOCK), 1)) // SLC_BLOCK
    forced = (blk == 0) | (blk == cur) | (blk == cur - 1)
    causal = blk <= cur
    score = jnp.where(forced, FORCE, jnp.where(causal, imp, -FORCE))
    blk_f = blk.astype(f32)
    sel = jnp.zeros((SLC_SLOTS, Q_BLOCK), f32)
    for _ in range(SLC_TOP):
        top = jnp.max(score, axis=0, keepdims=True)
        first = jnp.min(jnp.where(score == top, blk_f, float(SLC_SLOTS)), axis=0, keepdims=True)
        pick = blk_f == first
        sel = jnp.where(pick, 1.0, sel)
        score = jnp.where(pick, -jnp.inf, score)
    selneg = jnp.where((sel > 0.0) & causal, 0.0, -MASK_BIG)
    qa_ref[AUG:AUG + SLC_SLOTS, :] = _tile4(selneg).astype(bf)

    def sel_scores(c):
        kc = ks_ref[0, pl.ds(pl.multiple_of(c * KEY_CHUNK, KEY_CHUNK), KEY_CHUNK), :]
        return jnp.dot(kc, qa_ref[...], preferred_element_type=f32)

    init = (jnp.full((1, NSA_LANES), NEG, f32), jnp.zeros((1, NSA_LANES), f32),
            jnp.zeros((HEAD_DIM, NSA_LANES), f32))
    carry = lax.fori_loop(0, i, lambda c, cr: _softmax_step(cr, sel_scores(c), vsT_ref[0, c]), init)
    diag = _tile4(jj <= tt)
    m_s, l_s, acc_s = _softmax_step(carry, jnp.where(diag, sel_scores(i), NEG), vsT_ref[0, i])
    o_s = acc_s * (1.0 / l_s)

    carry = init
    nwc = WINDOW // KEY_CHUNK
    for d in range(nwc + 1):
        c = i - nwc + d
        cc = jnp.maximum(c, 0)
        kc = kw_ref[0, pl.ds(pl.multiple_of(cc * KEY_CHUNK, KEY_CHUNK), KEY_CHUNK), :]
        s = jnp.dot(kc, qc, preferred_element_type=f32)
        if d == 0:
            s = jnp.where(_tile4(jj > tt) & (c >= 0), s, NEG)
        elif d == nwc:
            s = jnp.where(diag, s, NEG)
        else:
            s = jnp.where(c >= 0, s, NEG)
        carry = _softmax_step(carry, s, vwT_ref[0, cc])
    m_w, l_w, acc_w = carry
    o_w = acc_w * (1.0 / l_w)

    gate = jax.nn.sigmoid(gT_ref[0])
    for g in range(A_GROUP):
        cols = slice(g * Q_BLOCK, (g + 1) * Q_BLOCK)
        o = (gate[3 * g:3 * g + 1, :] * o_c[:, cols] + gate[3 * g + 1:3 * g + 2, :] * o_s[:, cols]
             + gate[3 * g + 2:3 * g + 3, :] * o_w[:, cols])
        o_ref[g * HEAD_DIM:(g + 1) * HEAD_DIM, :] = o


def _cmp_prompt_kernel(x_ref, pe_ref, w1_ref, b1_ref, w2T_ref, ck_ref, cvT_ref, bias_ref):
    f32, bf = jnp.float32, jnp.bfloat16
    nseg = ck_ref.shape[1]
    row_w = 2 * A_KV_HEADS * HEAD_DIM
    for c in range(2):
        bias = jnp.dot(pe_ref[c], w1_ref[c], preferred_element_type=f32,
                       precision=lax.Precision.HIGHEST) + b1_ref[c]
        bias_ref[c] = bias
        for k in range(A_KV_HEADS):
            col = (c * A_KV_HEADS + k) * HEAD_DIM
            y0 = jnp.zeros((nseg, CMP_HIDDEN), f32)
            y1 = jnp.zeros((nseg, CMP_HIDDEN), f32)
            for j in range(CMP_STRIDE):
                xj = x_ref[:, j * row_w + col:j * row_w + col + HEAD_DIM].astype(bf)
                w0 = w1_ref[c, j * HEAD_DIM:(j + 1) * HEAD_DIM, :].astype(bf)
                w1 = w1_ref[c, (CMP_STRIDE + j) * HEAD_DIM:(CMP_STRIDE + j + 1) * HEAD_DIM, :].astype(bf)
                y0 = y0 + jnp.dot(xj, w0, preferred_element_type=f32)
                y1 = y1 + jnp.dot(xj, w1, preferred_element_type=f32)
            pre = y0 + pltpu.roll(y1, nseg - 1, 0)
            hid = jnp.maximum(pre + bias, 0.0).astype(bf)
            if c == 0:
                ck_ref[k] = lax.dot_general(hid, w2T_ref[c].astype(bf), (((1,), (1,)), ((), ())),
                                            preferred_element_type=f32)
            else:
                cvT_ref[k] = lax.dot_general(w2T_ref[c].astype(bf), hid, (((1,), (1,)), ((), ())),
                                             preferred_element_type=f32)


def compress_prompt(x_cmp, pe, w1, b1, w2):
    T = x_cmp.shape[0]
    nseg = T // CMP_STRIDE
    return pl.pallas_call(
        _cmp_prompt_kernel,
        out_shape=(jax.ShapeDtypeStruct((A_KV_HEADS, nseg, HEAD_DIM), jnp.float32),
                   jax.ShapeDtypeStruct((A_KV_HEADS, HEAD_DIM, nseg), jnp.float32),
                   jax.ShapeDtypeStruct((2, 1, CMP_HIDDEN), jnp.float32)),
        compiler_params=pltpu.CompilerParams(vmem_limit_bytes=VMEM_LIMIT_BYTES),
        name="compress_prompt",
    )(x_cmp.reshape(nseg, CMP_STRIDE * x_cmp.shape[1]), pe.reshape(2, 1, CMP_BLOCK * HEAD_DIM), w1, b1.reshape(2, 1, CMP_HIDDEN), w2.transpose(0, 2, 1))


def nsa_prompt_pallas(qa, ga, kva, pe, w1, b1, w2):
    bf = jnp.bfloat16
    T = qa.shape[0]
    nqb = T // Q_BLOCK
    nch = T // KEY_CHUNK
    ncp = T // CMP_STRIDE
    kvw = A_KV_HEADS * HEAD_DIM
    ck, cvT, pe_bias = compress_prompt(kva[:, 0:2 * kvw], pe, w1, b1, w2)

    def heads(x):
        return x.reshape(T, A_KV_HEADS, HEAD_DIM).transpose(1, 0, 2)

    def chunksT(x):
        return x.reshape(nch, KEY_CHUNK, A_KV_HEADS, HEAD_DIM).transpose(2, 0, 3, 1).astype(bf)

    pos = np.arange(T)
    key_pos = jnp.broadcast_to(jnp.asarray(_pos_pieces(pos)), (A_KV_HEADS, T, POS_PIECES))
    zpad = jnp.zeros((A_KV_HEADS, T, AUG - HEAD_DIM - POS_PIECES), jnp.float32)
    onehot = jnp.broadcast_to(jnp.asarray((pos[:, None] // SLC_BLOCK == np.arange(SLC_SLOTS)[None, :]).astype(np.float32)),
                              (A_KV_HEADS, T, SLC_SLOTS))
    ks_aug = jnp.concatenate([heads(kva[:, 2 * kvw:3 * kvw]), key_pos, zpad, onehot], axis=-1).astype(bf)
    kw_aug = jnp.concatenate([heads(kva[:, 4 * kvw:5 * kvw]), key_pos, zpad], axis=-1).astype(bf)
    c_end = np.arange(ncp) * CMP_STRIDE + (CMP_BLOCK - 1)
    ck_aug = jnp.concatenate([ck, jnp.broadcast_to(jnp.asarray(_pos_pieces(c_end)), (A_KV_HEADS, ncp, POS_PIECES)),
                              jnp.zeros((A_KV_HEADS, ncp, AUG - HEAD_DIM - POS_PIECES), jnp.float32)], axis=-1).astype(bf)
    vsT = chunksT(kva[:, 3 * kvw:4 * kvw])
    vwT = chunksT(kva[:, 5 * kvw:6 * kvw])
    c_start = c_end - (CMP_BLOCK - 1)
    s_start = np.arange(SLC_SLOTS) * SLC_BLOCK
    ovT = ((c_start[None, :] < s_start[:, None] + SLC_BLOCK) & (c_end[None, :] >= s_start[:, None])
           & (np.arange(ncp)[None, :] < ncp - 1))
    ovT = jnp.asarray(ovT.astype(np.float32)).astype(bf)
    gT = jnp.pad(ga.T.reshape(A_KV_HEADS, 3 * A_GROUP, T), ((0, 0), (0, 16 - 3 * A_GROUP), (0, 0)))

    yT = pl.pallas_call(
        _nsa_prompt_kernel,
        grid=(A_KV_HEADS, nqb),
        in_specs=[
            pl.BlockSpec((A_GROUP * HEAD_DIM, Q_BLOCK), lambda k, i: (k, i)),
            pl.BlockSpec((1, 16, NSA_LANES), lambda k, i: (k, 0, 0)),
            pl.BlockSpec((1, 16, Q_BLOCK), lambda k, i: (k, 0, i)),
            pl.BlockSpec((1, ncp, AUG), lambda k, i: (k, 0, 0)),
            pl.BlockSpec((1, HEAD_DIM, ncp), lambda k, i: (k, 0, 0)),
            pl.BlockSpec((SLC_SLOTS, ncp), lambda k, i: (0, 0)),
            pl.BlockSpec((1, T, AUG + SLC_SLOTS), lambda k, i: (k, 0, 0)),
            pl.BlockSpec((1, nch, HEAD_DIM, KEY_CHUNK), lambda k, i: (k, 0, 0, 0)),
            pl.BlockSpec((1, T, AUG), lambda k, i: (k, 0, 0)),
            pl.BlockSpec((1, nch, HEAD_DIM, KEY_CHUNK), lambda k, i: (k, 0, 0, 0)),
        ],
        out_specs=pl.BlockSpec((A_GROUP * HEAD_DIM, Q_BLOCK), lambda k, i: (k, i)),
        out_shape=jax.ShapeDtypeStruct((A_WIDTH, T), jnp.float32),
        scratch_shapes=[pltpu.VMEM((AUG + SLC_SLOTS, NSA_LANES), bf)],
        compiler_params=_params("parallel", "arbitrary"),
        name="nsa_prompt",
    )(qa.T, _slope_rows(), gT, ck_aug, cvT.astype(bf), ovT, ks_aug, vsT, kw_aug, vwT)
    return yT.T, pe_bias


PAGE_SIZE = 128
N_PAGES = PAST_LEN // PAGE_SIZE
PAGE_SEGS = PAGE_SIZE // CMP_STRIDE
PAST_SEGS = PAST_LEN // CMP_STRIDE
ROW_W = 4 * A_KV_HEADS * HEAD_DIM
KVW = A_KV_HEADS * HEAD_DIM
NS_SAMPLE = PAST_LEN // SLC_BLOCK + 1


def _nt(a, b):
    return lax.dot_general(a, b, (((1,), (1,)), ((), ())), preferred_element_type=jnp.float32)


def _dot3(x, w):
    return sum(jnp.dot(p, w, preferred_element_type=jnp.float32) for p in _split3(x))


def _nsa_sample_kernel(pt_ref, q_ref, g_ref, new_ref, *refs):
    del pt_ref
    pages = refs[:N_PAGES]
    (win_ref, w1_ref, bias_ref, w2bd_ref, bmask_ref, basec_ref, bases_ref, basew_ref, slope_ref,
     grp_ref, ov_ref, exp_ref, o_ref, wout_ref, xc_ref) = refs[N_PAGES:]
    f32, bf = jnp.float32, jnp.bfloat16

    q = q_ref[0] * (HEAD_DIM ** -0.5)
    qbd = jnp.where(bmask_ref[...] > 0.0, jnp.concatenate([q] * A_KV_HEADS, axis=1), 0.0).astype(bf)
    qbd_f = qbd.astype(f32)
    new = new_ref[0]

    def slab(off, width):
        return jnp.concatenate([pg[0, :, off:off + width] for pg in pages], axis=0)

    ckv = []
    for c in range(2):
        for j in range(CMP_STRIDE):
            s = slab(j * ROW_W + c * KVW, KVW).astype(bf)
            for k in range(A_KV_HEADS):
                xc_ref[k * PAST_SEGS:(k + 1) * PAST_SEGS, j * HEAD_DIM:(j + 1) * HEAD_DIM] = (
                    s[:, k * HEAD_DIM:(k + 1) * HEAD_DIM])
        y = jnp.dot(xc_ref[...], w1_ref[c], preferred_element_type=f32)
        pre = y[:, 0:CMP_HIDDEN] + pltpu.roll(y[:, CMP_HIDDEN:2 * CMP_HIDDEN], A_KV_HEADS * PAST_SEGS - 1, 0)
        hid = jnp.maximum(pre + bias_ref[c], 0.0).astype(bf)
        hid = jnp.concatenate([hid[k * PAST_SEGS:(k + 1) * PAST_SEGS] for k in range(A_KV_HEADS)], axis=1)
        ckv.append(jnp.dot(hid, w2bd_ref[c], preferred_element_type=f32).astype(bf))
    ck, cv = ckv

    sc = _nt(qbd, ck) + basec_ref[...]
    ec = jnp.exp(sc - jnp.max(sc, axis=1, keepdims=True))
    pc = ec / jnp.sum(ec, axis=1, keepdims=True)
    o_c = jnp.dot(pc.astype(bf), cv, preferred_element_type=f32)

    pov = _dot3(pc, ov_ref[...])
    p_hi, p_mid, p_lo = _split3(pov)
    imp = sum(jnp.dot(grp_ref[...], p, preferred_element_type=f32) for p in (p_hi, p_mid, p_lo))

    lane = lax.broadcasted_iota(jnp.int32, (A_HEADS, PAST_SEGS), 1)
    score = jnp.where((lane == 0) | (lane == NS_SAMPLE - 1) | (lane == NS_SAMPLE - 2), FORCE,
                      jnp.where(lane < NS_SAMPLE, imp, -jnp.inf))
    rr = lax.broadcasted_iota(jnp.int32, (PAST_SEGS, PAST_SEGS), 0)
    cc = lax.broadcasted_iota(jnp.int32, (PAST_SEGS, PAST_SEGS), 1)
    row_id = lax.broadcasted_iota(jnp.int32, (A_HEADS, PAST_SEGS), 0)
    selrows = jnp.zeros((A_HEADS, PAST_SEGS), f32)
    for k in range(A_KV_HEADS):
        row = jnp.broadcast_to(score[A_GROUP * k:A_GROUP * k + 1, :], (PAST_SEGS, PAST_SEGS))
        col = jnp.max(jnp.where(rr == cc, row, -jnp.inf), axis=1, keepdims=True)
        ahead = (col > row) | ((col == row) & (rr < cc))
        rank = jnp.sum(jnp.where(ahead, 1.0, 0.0), axis=0, keepdims=True)
        sel_k = jnp.where(rank < float(SLC_TOP), 1.0, 0.0)
        selrows = jnp.where(row_id // A_GROUP == k, jnp.broadcast_to(sel_k, (A_HEADS, PAST_SEGS)), selrows)
    selseg = jnp.dot(selrows.astype(bf), exp_ref[...], preferred_element_type=f32)
    maskbias = jnp.where(selseg > 0.5, 0.0, NEG)

    slope = slope_ref[...]
    scores = []
    for j in range(CMP_STRIDE):
        kj = slab(j * ROW_W + 2 * KVW, KVW).astype(bf)
        scores.append(_nt(qbd, kj) + (bases_ref[...] + slope * float(j)) + maskbias)
    s_new = jnp.sum(qbd_f * new[:, 2 * KVW:3 * KVW], axis=1, keepdims=True)
    mx = scores[0]
    for s in scores[1:]:
        mx = jnp.maximum(mx, s)
    m = jnp.maximum(jnp.max(mx, axis=1, keepdims=True), s_new)
    p_new = jnp.exp(s_new - m)
    acc = p_new * new[:, 3 * KVW:4 * KVW]
    psum = jnp.zeros((A_HEADS, PAST_SEGS), f32)
    for j in range(CMP_STRIDE):
        p = jnp.exp(scores[j] - m)
        psum = psum + p
        vj = slab(j * ROW_W + 3 * KVW, KVW).astype(bf)
        acc = acc + jnp.dot(p.astype(bf), vj, preferred_element_type=f32)
    o_s = acc / (jnp.sum(psum, axis=1, keepdims=True) + p_new)

    kw = win_ref[0, :, 0:KVW].astype(bf)
    vw = win_ref[0, :, KVW:2 * KVW].astype(bf)
    sw = _nt(qbd, kw) + basew_ref[...]
    s_neww = jnp.sum(qbd_f * new[:, 4 * KVW:5 * KVW], axis=1, keepdims=True)
    m = jnp.maximum(jnp.max(sw, axis=1, keepdims=True), s_neww)
    pw = jnp.exp(sw - m)
    p_new = jnp.exp(s_neww - m)
    o_w = (jnp.dot(pw.astype(bf), vw, preferred_element_type=f32) + p_new * new[:, 5 * KVW:6 * KVW]) / (
        jnp.sum(pw, axis=1, keepdims=True) + p_new)

    gate = jax.nn.sigmoid(g_ref[0])
    o = gate[:, 0:1] * o_c + gate[:, 1:2] * o_s + gate[:, 2:3] * o_w
    for k in range(A_KV_HEADS):
        o_ref[0, A_GROUP * k:A_GROUP * (k + 1), :] = o[A_GROUP * k:A_GROUP * (k + 1), k * HEAD_DIM:(k + 1) * HEAD_DIM]

    wout_ref[0, 0:WINDOW - 1, :] = win_ref[0, 1:WINDOW, :]
    wout_ref[0, WINDOW - 1:WINDOW, :] = new[:, 4 * KVW:6 * KVW]


def _sample_constants():
    slopes = (2.0 ** (-8.0 * np.arange(1, A_HEADS + 1, dtype=np.float64) / A_HEADS)).astype(np.float32)[:, None]
    heads = np.arange(A_HEADS)[:, None]
    bmask = (np.arange(KVW)[None, :] // HEAD_DIM == heads // A_GROUP).astype(np.float32)
    n = np.arange(PAST_SEGS)[None, :]
    c_end = n * CMP_STRIDE + (CMP_BLOCK - 1)
    basec = np.where(n < PAST_SEGS - 1, slopes * (c_end - PAST_LEN), NEG).astype(np.float32)
    bases = (slopes * (n * CMP_STRIDE - PAST_LEN)).astype(np.float32)
    r = np.arange(WINDOW)[None, :]
    basew = np.where(r >= 1, slopes * (r - WINDOW), NEG).astype(np.float32)
    slope = np.broadcast_to(slopes, (A_HEADS, PAST_SEGS)).astype(np.float32)
    grp = (heads // A_GROUP == heads.T // A_GROUP).astype(np.float32)
    c_start = np.arange(PAST_SEGS) * CMP_STRIDE
    s_start = np.arange(PAST_SEGS) * SLC_BLOCK
    ov = ((c_start[:, None] < s_start[None, :] + SLC_BLOCK) & (c_start[:, None] + CMP_BLOCK - 1 >= s_start[None, :])
          & (np.arange(PAST_SEGS)[:, None] < PAST_SEGS - 1) & (np.arange(PAST_SEGS)[None, :] < NS_SAMPLE))
    expand = (np.arange(PAST_SEGS)[None, :] * CMP_STRIDE // SLC_BLOCK == np.arange(PAST_SEGS)[:, None])
    as_bf = lambda a: jnp.asarray(a.astype(np.float32)).astype(jnp.bfloat16)
    return (jnp.asarray(bmask), jnp.asarray(basec), jnp.asarray(bases), jnp.asarray(basew), jnp.asarray(slope),
            as_bf(grp), as_bf(ov), as_bf(expand))


def nsa_sample_pallas(qa, ga, kva, pool, page_table, win_buf, layer, w1, w2, pe_bias):
    bf = jnp.bfloat16
    B = qa.shape[0]
    npool = pool.shape[1]
    page_table = page_table + layer * npool
    seg_w = CMP_STRIDE * ROW_W
    w1r = w1.reshape(2, CMP_RATIO, CMP_STRIDE * HEAD_DIM, CMP_HIDDEN).transpose(0, 2, 1, 3)
    w1r = w1r.reshape(2, CMP_STRIDE * HEAD_DIM, CMP_RATIO * CMP_HIDDEN).astype(bf)
    w2bd = jnp.einsum('kl,ced->ckeld', jnp.eye(A_KV_HEADS, dtype=w2.dtype), w2)
    w2bd = w2bd.reshape(2, A_KV_HEADS * CMP_HIDDEN, KVW).astype(bf)
    consts = _sample_constants()
    full = lambda shape: pl.BlockSpec(shape, lambda b, pt: (0,) * len(shape))
    page_specs = [pl.BlockSpec((1, PAGE_SEGS, seg_w), functools.partial(lambda p, b, pt: (pt[b, p], 0, 0), p))
                  for p in range(N_PAGES)]
    grid_spec = pltpu.PrefetchScalarGridSpec(
        num_scalar_prefetch=1,
        grid=(B,),
        in_specs=[
            pl.BlockSpec((1, A_HEADS, HEAD_DIM), lambda b, pt: (b, 0, 0)),
            pl.BlockSpec((1, A_HEADS, 3), lambda b, pt: (b, 0, 0)),
            pl.BlockSpec((1, 1, N_KV_BRANCH * KVW), lambda b, pt: (b, 0, 0)),
            *page_specs,
            pl.BlockSpec((1, WINDOW, 2 * KVW), lambda b, pt: (layer * B + b, 0, 0)),
            full(w1r.shape), full((2, 1, CMP_HIDDEN)), full(w2bd.shape),
            *[full(c.shape) for c in consts],
        ],
        out_specs=[
            pl.BlockSpec((1, A_HEADS, HEAD_DIM), lambda b, pt: (b, 0, 0)),
            pl.BlockSpec((1, WINDOW, 2 * KVW), lambda b, pt: (b, 0, 0)),
        ],
        scratch_shapes=[pltpu.VMEM((A_KV_HEADS * PAST_SEGS, CMP_STRIDE * HEAD_DIM), bf)],
    )
    pool3 = pool.reshape(pool.shape[0] * npool, PAGE_SEGS, seg_w)
    o, wout = pl.pallas_call(
        _nsa_sample_kernel,
        grid_spec=grid_spec,
        out_shape=(jax.ShapeDtypeStruct((B, A_HEADS, HEAD_DIM), jnp.float32),
                   jax.ShapeDtypeStruct((B, WINDOW, 2 * KVW), jnp.float32)),
        compiler_params=_params("arbitrary"),
        name="nsa_sample",
    )(page_table, qa.reshape(B, A_HEADS, HEAD_DIM), ga.reshape(B, A_HEADS, 3), kva.reshape(B, 1, N_KV_BRANCH * KVW),
      *([pool3] * N_PAGES), win_buf.reshape(win_buf.shape[0] * B, WINDOW, 2 * KVW), w1r, pe_bias.reshape(2, 1, CMP_HIDDEN), w2bd, *consts)
    return o.reshape(B, A_WIDTH), wout


def _nsa_decode_kernel(pt_ref, q_ref, g_ref, new_ref, *refs):
    del pt_ref
    pages = refs[:N_PAGES]
    (win_ref, w1_ref, bias_ref, w2bd_ref, bmask_ref, basec_ref, bases_ref, basew_ref, slope_ref,
     grp_ref, ov_ref, exp_ref, o_ref, wout_ref, tr_ref, xc_ref) = refs[N_PAGES:]
    f32, bf = jnp.float32, jnp.bfloat16

    q = q_ref[0] * (HEAD_DIM ** -0.5)
    qbd = jnp.where(bmask_ref[...] > 0.0, jnp.concatenate([q] * A_KV_HEADS, axis=1), 0.0).astype(bf)
    qbd_f = qbd.astype(f32)
    new = new_ref[0]

    ckv = []
    for c in range(2):
        for p in range(N_PAGES):
            for kp in range(2):
                r0 = c * KVW + kp * 2 * HEAD_DIM
                t0 = kp * PAST_LEN + p * PAGE_SIZE
                tr_ref[t0:t0 + PAGE_SIZE, :] = pages[p][0, r0:r0 + 2 * HEAD_DIM, :].T
        for j in range(CMP_STRIDE):
            for kp in range(2):
                piece = tr_ref[pl.ds(kp * PAST_LEN + j, PAST_SEGS, stride=CMP_STRIDE), :]
                for kk in range(2):
                    k = 2 * kp + kk
                    xc_ref[k * PAST_SEGS:(k + 1) * PAST_SEGS, j * HEAD_DIM:(j + 1) * HEAD_DIM] = (
                        piece[:, kk * HEAD_DIM:(kk + 1) * HEAD_DIM])
        y = jnp.dot(xc_ref[...].astype(bf), w1_ref[c], preferred_element_type=f32)
        pre = y[:, 0:CMP_HIDDEN] + pltpu.roll(y[:, CMP_HIDDEN:2 * CMP_HIDDEN], A_KV_HEADS * PAST_SEGS - 1, 0)
        hid = jnp.maximum(pre + bias_ref[c], 0.0).astype(bf)
        hid = jnp.concatenate([hid[k * PAST_SEGS:(k + 1) * PAST_SEGS] for k in range(A_KV_HEADS)], axis=1)
        ckv.append(jnp.dot(hid, w2bd_ref[c], preferred_element_type=f32).astype(bf))
    ck, cv = ckv

    sc = _nt(qbd, ck) + basec_ref[...]
    ec = jnp.exp(sc - jnp.max(sc, axis=1, keepdims=True))
    pc = ec / jnp.sum(ec, axis=1, keepdims=True)
    o_c = jnp.dot(pc.astype(bf), cv, preferred_element_type=f32)

    pov = _dot3(pc, ov_ref[...])
    imp = sum(jnp.dot(grp_ref[...], p, preferred_element_type=f32) for p in _split3(pov))

    lane = lax.broadcasted_iota(jnp.int32, (A_HEADS, PAST_SEGS), 1)
    score = jnp.where((lane == 0) | (lane == NS_SAMPLE - 1) | (lane == NS_SAMPLE - 2), FORCE,
                      jnp.where(lane < NS_SAMPLE, imp, -jnp.inf))
    rr = lax.broadcasted_iota(jnp.int32, (PAST_SEGS, PAST_SEGS), 0)
    cc = lax.broadcasted_iota(jnp.int32, (PAST_SEGS, PAST_SEGS), 1)
    row_id = lax.broadcasted_iota(jnp.int32, (A_HEADS, PAST_SEGS), 0)
    selrows = jnp.zeros((A_HEADS, PAST_SEGS), f32)
    for k in range(A_KV_HEADS):
        row = jnp.broadcast_to(score[A_GROUP * k:A_GROUP * k + 1, :], (PAST_SEGS, PAST_SEGS))
        col = jnp.max(jnp.where(rr == cc, row, -jnp.inf), axis=1, keepdims=True)
        ahead = (col > row) | ((col == row) & (rr < cc))
        rank = jnp.sum(jnp.where(ahead, 1.0, 0.0), axis=0, keepdims=True)
        sel_k = jnp.where(rank < float(SLC_TOP), 1.0, 0.0)
        selrows = jnp.where(row_id // A_GROUP == k, jnp.broadcast_to(sel_k, (A_HEADS, PAST_SEGS)), selrows)
    selpos = jnp.dot(selrows.astype(bf), exp_ref[...], preferred_element_type=f32)

    slope = slope_ref[...]
    scores = []
    for p in range(N_PAGES):
        kp_ = pages[p][0, 2 * KVW:3 * KVW, :].astype(bf)
        mask = jnp.where(selpos[:, p * PAGE_SIZE:(p + 1) * PAGE_SIZE] > 0.5, 0.0, NEG)
        scores.append(jnp.dot(qbd, kp_, preferred_element_type=f32) + (bases_ref[...] + slope * float(p * PAGE_SIZE)) + mask)
    s_new = jnp.sum(qbd_f * new[:, 2 * KVW:3 * KVW], axis=1, keepdims=True)
    mx = scores[0]
    for s in scores[1:]:
        mx = jnp.maximum(mx, s)
    m = jnp.maximum(jnp.max(mx, axis=1, keepdims=True), s_new)
    p_new = jnp.exp(s_new - m)
    acc = p_new * new[:, 3 * KVW:4 * KVW]
    psum = jnp.zeros((A_HEADS, PAGE_SIZE), f32)
    for p in range(N_PAGES):
        pr = jnp.exp(scores[p] - m)
        psum = psum + pr
        acc = acc + _nt(pr.astype(bf), pages[p][0, 3 * KVW:4 * KVW, :].astype(bf))
    o_s = acc / (jnp.sum(psum, axis=1, keepdims=True) + p_new)

    sw = jnp.dot(qbd, win_ref[0, 0:KVW, :].astype(bf), preferred_element_type=f32) + basew_ref[...]
    s_neww = jnp.sum(qbd_f * new[:, 4 * KVW:5 * KVW], axis=1, keepdims=True)
    m = jnp.maximum(jnp.max(sw, axis=1, keepdims=True), s_neww)
    pw = jnp.exp(sw - m)
    p_new = jnp.exp(s_neww - m)
    o_w = (_nt(pw.astype(bf), win_ref[0, KVW:2 * KVW, :].astype(bf)) + p_new * new[:, 5 * KVW:6 * KVW]) / (
        jnp.sum(pw, axis=1, keepdims=True) + p_new)

    gate = jax.nn.sigmoid(g_ref[0])
    o = gate[:, 0:1] * o_c + gate[:, 1:2] * o_s + gate[:, 2:3] * o_w
    for k in range(A_KV_HEADS):
        o_ref[0, A_GROUP * k:A_GROUP * (k + 1), :] = o[A_GROUP * k:A_GROUP * (k + 1), k * HEAD_DIM:(k + 1) * HEAD_DIM]

    wr = lax.broadcasted_iota(jnp.int32, (WINDOW, 2 * KVW), 0)
    wc = lax.broadcasted_iota(jnp.int32, (WINDOW, 2 * KVW), 1)
    new_col = jnp.sum(jnp.where(wr == wc, jnp.broadcast_to(new[:, 4 * KVW:6 * KVW], (WINDOW, 2 * KVW)), 0.0),
                      axis=1, keepdims=True)
    wout_ref[0] = jnp.where(wc == WINDOW - 1, new_col, pltpu.roll(win_ref[0], WINDOW - 1, 1))


def _decode_constants():
    slopes = (2.0 ** (-8.0 * np.arange(1, A_HEADS + 1, dtype=np.float64) / A_HEADS)).astype(np.float32)[:, None]
    heads = np.arange(A_HEADS)[:, None]
    bmask = (np.arange(KVW)[None, :] // HEAD_DIM == heads // A_GROUP).astype(np.float32)
    n = np.arange(PAST_SEGS)[None, :]
    c_end = n * CMP_STRIDE + (CMP_BLOCK - 1)
    basec = np.where(n < PAST_SEGS - 1, slopes * (c_end - PAST_LEN), NEG).astype(np.float32)
    bases = (slopes * (np.arange(PAGE_SIZE)[None, :] - PAST_LEN)).astype(np.float32)
    r = np.arange(WINDOW)[None, :]
    basew = np.where(r >= 1, slopes * (r - WINDOW), NEG).astype(np.float32)
    slope = np.broadcast_to(slopes, (A_HEADS, PAGE_SIZE)).astype(np.float32)
    grp = (heads // A_GROUP == heads.T // A_GROUP).astype(np.float32)
    c_start = np.arange(PAST_SEGS) * CMP_STRIDE
    s_start = np.arange(PAST_SEGS) * SLC_BLOCK
    ov = ((c_start[:, None] < s_start[None, :] + SLC_BLOCK) & (c_start[:, None] + CMP_BLOCK - 1 >= s_start[None, :])
          & (np.arange(PAST_SEGS)[:, None] < PAST_SEGS - 1) & (np.arange(PAST_SEGS)[None, :] < NS_SAMPLE))
    expand = (np.arange(PAST_LEN)[None, :] // SLC_BLOCK == np.arange(PAST_SEGS)[:, None])
    as_bf = lambda a: jnp.asarray(a.astype(np.float32)).astype(jnp.bfloat16)
    return (jnp.asarray(bmask), jnp.asarray(basec), jnp.asarray(bases), jnp.asarray(basew), jnp.asarray(slope),
            as_bf(grp), as_bf(ov), as_bf(expand))


def nsa_decode_pallas(qa, ga, kva, pool, page_table, win_buf, layer, w1, w2, pe_bias):
    bf = jnp.bfloat16
    B = qa.shape[0]
    depth, npool = pool.shape[:2]
    page_table = page_table + layer * npool
    poolT = pool.transpose(0, 1, 3, 4, 5, 2).reshape(depth * npool, ROW_W, PAGE_SIZE)
    winT = win_buf.transpose(0, 1, 3, 4, 5, 2).reshape(depth * B, 2 * KVW, WINDOW)
    w1r = w1.reshape(2, CMP_RATIO, CMP_STRIDE * HEAD_DIM, CMP_HIDDEN).transpose(0, 2, 1, 3)
    w1r = w1r.reshape(2, CMP_STRIDE * HEAD_DIM, CMP_RATIO * CMP_HIDDEN).astype(bf)
    w2bd = jnp.einsum('kl,ced->ckeld', jnp.eye(A_KV_HEADS, dtype=w2.dtype), w2)
    w2bd = w2bd.reshape(2, A_KV_HEADS * CMP_HIDDEN, KVW).astype(bf)
    consts = _decode_constants()
    full = lambda shape: pl.BlockSpec(shape, lambda b, pt: (0,) * len(shape))
    page_specs = [pl.BlockSpec((1, ROW_W, PAGE_SIZE), functools.partial(lambda p, b, pt: (pt[b, p], 0, 0), p))
                  for p in range(N_PAGES)]
    grid_spec = pltpu.PrefetchScalarGridSpec(
        num_scalar_prefetch=1,
        grid=(B,),
        in_specs=[
            pl.BlockSpec((1, A_HEADS, HEAD_DIM), lambda b, pt: (b, 0, 0)),
            pl.BlockSpec((1, A_HEADS, 3), lambda b, pt: (b, 0, 0)),
            pl.BlockSpec((1, 1, N_KV_BRANCH * KVW), lambda b, pt: (b, 0, 0)),
            *page_specs,
            pl.BlockSpec((1, 2 * KVW, WINDOW), lambda b, pt: (layer * B + b, 0, 0)),
            full(w1r.shape), full((2, 1, CMP_HIDDEN)), full(w2bd.shape),
            *[full(c.shape) for c in consts],
        ],
        out_specs=[
            pl.BlockSpec((1, A_HEADS, HEAD_DIM), lambda b, pt: (b, 0, 0)),
            pl.BlockSpec((1, 2 * KVW, WINDOW), lambda b, pt: (b, 0, 0)),
        ],
        scratch_shapes=[pltpu.VMEM((2 * PAST_LEN, 2 * HEAD_DIM), jnp.float32),
                        pltpu.VMEM((A_KV_HEADS * PAST_SEGS, CMP_STRIDE * HEAD_DIM), jnp.float32)],
    )
    o, wout = pl.pallas_call(
        _nsa_decode_kernel,
        grid_spec=grid_spec,
        out_shape=(jax.ShapeDtypeStruct((B, A_HEADS, HEAD_DIM), jnp.float32),
                   jax.ShapeDtypeStruct((B, 2 * KVW, WINDOW), jnp.float32)),
        compiler_params=_params("arbitrary"),
        name="nsa_decode",
    )(page_table, qa.reshape(B, A_HEADS, HEAD_DIM), ga.reshape(B, A_HEADS, 3), kva.reshape(B, 1, N_KV_BRANCH * KVW),
      *([poolT] * N_PAGES), winT, w1r, pe_bias.reshape(2, 1, CMP_HIDDEN), w2bd, *consts)
    return o.reshape(B, A_WIDTH), wout


MLSTM_L = 128
VAUG = 2 * V_DIM


def _mlstm_prompt_kernel(q_ref, kT_ref, v_ref, o_ref, fsrc_ref, ifT_ref, y_ref, c_ref, m_ref, *, f_off):
    f32, bf = jnp.float32, jnp.bfloat16
    L = MLSTM_L
    f_ref = fsrc_ref.at[:, f_off:f_off + B_HEADS]
    iT_ref = ifT_ref.at[0:B_HEADS, :]
    fT_ref = ifT_ref.at[B_HEADS:2 * B_HEADS, :]

    @pl.when(pl.program_id(0) == 0)
    def _():
        c_ref[...] = jnp.zeros_like(c_ref)
        m_ref[...] = jnp.zeros_like(m_ref)

    rr = lax.broadcasted_iota(jnp.int32, (L, L), 0)
    cc = lax.broadcasted_iota(jnp.int32, (L, L), 1)
    lower = rr >= cc
    tril = jnp.where(lower, 1.0, 0.0).astype(bf)
    triu = jnp.where(rr <= cc, 1.0, 0.0).astype(bf)
    b_col = sum(jnp.dot(tril, p, preferred_element_type=f32) for p in _split3(jax.nn.log_sigmoid(f_ref[...])))
    b_row = _dot3(jax.nn.log_sigmoid(fT_ref[...]), triu)
    a_row = iT_ref[...] - b_row
    ones_col = jnp.where(lax.broadcasted_iota(jnp.int32, (L, V_DIM), 1) == 0, 1.0, 0.0)

    for h in range(B_HEADS):
        m_prev = m_ref[h, 0:1, 0:1]
        a = a_row[h:h + 1, :]
        amat = jnp.where(lower, jnp.broadcast_to(a, (L, L)), -jnp.inf)
        big_m = jnp.maximum(m_prev, jnp.max(amat, axis=1, keepdims=True))
        dmat = jnp.exp(amat - big_m)
        inter = jnp.exp(m_prev - big_m)
        q = q_ref[:, h * QK_DIM:(h + 1) * QK_DIM].astype(bf)
        kT = kT_ref[h * QK_DIM:(h + 1) * QK_DIM, :] * (QK_DIM ** -0.5)
        vaug = jnp.concatenate([v_ref[:, h * V_DIM:(h + 1) * V_DIM], ones_col], axis=1).astype(bf)
        s = jnp.dot(q, kT.astype(bf), preferred_element_type=f32) * dmat
        r = (jnp.dot(s.astype(bf), vaug, preferred_element_type=f32)
             + inter * jnp.dot(q, c_ref[h].astype(bf), preferred_element_type=f32))
        m_new = b_col[:, h:h + 1] + big_m
        den = jnp.maximum(jnp.abs(r[:, V_DIM:V_DIM + 1]), jnp.exp(-m_new))
        y_ref[:, h * V_DIM:(h + 1) * V_DIM] = (jax.nn.sigmoid(o_ref[:, h * V_DIM:(h + 1) * V_DIM])
                                               * (r[:, 0:V_DIM] / den))
        m_end = big_m[L - 1:L, :]
        w_end = jnp.exp(a - m_end)
        c_ref[h] = inter[L - 1:L, :] * c_ref[h] + jnp.dot((kT * w_end).astype(bf), vaug, preferred_element_type=f32)
        m_ref[h] = jnp.broadcast_to(m_new[L - 1:L, :], m_ref.shape[1:])


def mlstm_prompt_pallas(q_src, kT, v_src, o_src, f_src, ifT, T):
    L = MLSTM_L
    (q_arr, q_cb), (v_arr, v_cb), (o_arr, o_cb), (f_arr, f_cb, f_off) = q_src, v_src, o_src, f_src
    y, caug, m = pl.pallas_call(
        functools.partial(_mlstm_prompt_kernel, f_off=f_off),
        grid=(T // L,),
        in_specs=[
            pl.BlockSpec((L, B_HEADS * QK_DIM), lambda c: (c, q_cb)),
            pl.BlockSpec((B_HEADS * QK_DIM, L), lambda c: (0, c)),
            pl.BlockSpec((L, B_WIDTH), lambda c: (c, v_cb)),
            pl.BlockSpec((L, B_WIDTH), lambda c: (c, o_cb)),
            pl.BlockSpec((L, f_arr.shape[1] if f_arr.shape[1] < 128 else 128), lambda c: (c, f_cb)),
            pl.BlockSpec((2 * B_HEADS, L), lambda c: (0, c)),
        ],
        out_specs=[
            pl.BlockSpec((L, B_WIDTH), lambda c: (c, 0)),
            pl.BlockSpec((B_HEADS, QK_DIM, VAUG), lambda c: (0, 0, 0)),
            pl.BlockSpec((B_HEADS, 8, 128), lambda c: (0, 0, 0)),
        ],
        out_shape=(jax.ShapeDtypeStruct((T, B_WIDTH), jnp.float32),
                   jax.ShapeDtypeStruct((B_HEADS, QK_DIM, VAUG), jnp.float32),
                   jax.ShapeDtypeStruct((B_HEADS, 8, 128), jnp.float32)),
        compiler_params=_params("arbitrary"),
        name="mlstm_prompt",
    )(q_arr, kT, v_arr, o_arr, f_arr, ifT)
    return y, caug[:, :, 0:V_DIM], caug[:, :, V_DIM], m[:, 0, 0]


def _mlstm_sample_kernel(q_ref, k_ref, qT_ref, kT_ref, v_ref, o_ref, i_ref, f_ref, m_ref, n_ref, c_ref,
                         y_ref, cn_ref, nn_ref, mn_ref):
    bb = q_ref.shape[0]
    scale = QK_DIM ** -0.5
    for b in range(bb):
        logf = jax.nn.log_sigmoid(f_ref[b])
        m_new = jnp.maximum(logf + m_ref[b], i_ref[b])
        d_all = jnp.exp(i_ref[b] - m_new)
        inter_all = jnp.exp(logf + m_ref[b] - m_new)
        floor_all = jnp.exp(-m_new)
        mn_ref[b] = m_new
        qk_all = jnp.sum(q_ref[b] * k_ref[b], axis=1, keepdims=True) * scale
        qn_all = jnp.sum(q_ref[b] * n_ref[b], axis=1, keepdims=True)
        for h in range(B_HEADS):
            d = d_all[:, h:h + 1]
            inter = inter_all[:, h:h + 1]
            s = qk_all[h:h + 1, :] * d
            c = c_ref[b, h]
            v = v_ref[b, h:h + 1, :]
            qc = qT_ref[b, :, h:h + 1]
            kc = kT_ref[b, :, h:h + 1] * scale
            num = inter * jnp.sum(qc * c, axis=0, keepdims=True) + s * v
            den = inter * qn_all[h:h + 1, :] + s
            hout = num / jnp.maximum(jnp.abs(den), floor_all[:, h:h + 1])
            y_ref[b, h:h + 1, :] = jax.nn.sigmoid(o_ref[b, h:h + 1, :]) * hout
            cn_ref[b, h] = inter * c + d * (kc * v)
            nn_ref[b, h:h + 1, :] = inter * n_ref[b, h:h + 1, :] + d * (k_ref[b, h:h + 1, :] * scale)


def mlstm_sample_pallas(qb, kb, vb, ib, fb, ob, state_C, state_n, state_m, layer, *, bb=8):
    B = qb.shape[0]
    nb = B // bb
    q3 = qb.reshape(B, B_HEADS, QK_DIM)
    k3 = kb.reshape(B, B_HEADS, QK_DIM)
    row8 = lambda x: x.reshape(-1, 1, B_HEADS)
    lay = layer * nb
    y, cn, nn, mn = pl.pallas_call(
        _mlstm_sample_kernel,
        grid=(nb,),
        in_specs=[
            pl.BlockSpec((bb, B_HEADS, QK_DIM), lambda i: (i, 0, 0)),
            pl.BlockSpec((bb, B_HEADS, QK_DIM), lambda i: (i, 0, 0)),
            pl.BlockSpec((bb, QK_DIM, B_HEADS), lambda i: (i, 0, 0)),
            pl.BlockSpec((bb, QK_DIM, B_HEADS), lambda i: (i, 0, 0)),
            pl.BlockSpec((bb, B_HEADS, V_DIM), lambda i: (i, 0, 0)),
            pl.BlockSpec((bb, B_HEADS, V_DIM), lambda i: (i, 0, 0)),
            pl.BlockSpec((bb, 1, B_HEADS), lambda i: (i, 0, 0)),
            pl.BlockSpec((bb, 1, B_HEADS), lambda i: (i, 0, 0)),
            pl.BlockSpec((bb, 1, B_HEADS), lambda i: (lay + i, 0, 0)),
            pl.BlockSpec((bb, B_HEADS, QK_DIM), lambda i: (lay + i, 0, 0)),
            pl.BlockSpec((bb, B_HEADS, QK_DIM, V_DIM), lambda i: (lay + i, 0, 0, 0)),
        ],
        out_specs=[
            pl.BlockSpec((bb, B_HEADS, V_DIM), lambda i: (i, 0, 0)),
            pl.BlockSpec((bb, B_HEADS, QK_DIM, V_DIM), lambda i: (i, 0, 0, 0)),
            pl.BlockSpec((bb, B_HEADS, QK_DIM), lambda i: (i, 0, 0)),
            pl.BlockSpec((bb, 1, B_HEADS), lambda i: (i, 0, 0)),
        ],
        out_shape=(jax.ShapeDtypeStruct((B, B_HEADS, V_DIM), jnp.float32),
                   jax.ShapeDtypeStruct((B, B_HEADS, QK_DIM, V_DIM), jnp.float32),
                   jax.ShapeDtypeStruct((B, B_HEADS, QK_DIM), jnp.float32),
                   jax.ShapeDtypeStruct((B, 1, B_HEADS), jnp.float32)),
        compiler_params=_params("arbitrary"),
        name="mlstm_sample",
    )(q3, k3, q3.transpose(0, 2, 1), k3.transpose(0, 2, 1), vb.reshape(B, B_HEADS, V_DIM),
      ob.reshape(B, B_HEADS, V_DIM), row8(ib), row8(fb), row8(state_m),
      state_n.reshape(-1, B_HEADS, QK_DIM), state_C.reshape(-1, B_HEADS, QK_DIM, V_DIM))
    return y.reshape(B, B_WIDTH), cn, nn, mn.reshape(B, B_HEADS)


(G_QA, G_KVA, G_GA, G_QB, G_KB, G_VB, G_IB, G_FB, G_OB, G_GM) = range(10)
WIDE_ORDER = (G_QA, G_VB, G_OB, G_KVA, G_QB, G_KB, G_GM)
NARROW_ORDER = (G_GA, G_IB, G_FB)
WIDE_OFF = dict(zip(WIDE_ORDER, np.cumsum((0,) + tuple(SPLITS[g] for g in WIDE_ORDER))[:-1].tolist()))
NARROW_OFF = dict(zip(NARROW_ORDER, np.cumsum((0,) + tuple(SPLITS[g] for g in NARROW_ORDER))[:-1].tolist()))
NARROW_W = 128


def project_in(h, w_in, b_in):
    w_t = w_in.T

    def gather(order, pad_to):
        rows = jnp.concatenate([w_t[BOUNDS[g]:BOUNDS[g + 1]] for g in order], axis=0)
        bias = jnp.concatenate([b_in[BOUNDS[g]:BOUNDS[g + 1]] for g in order], axis=0)
        pad = pad_to - rows.shape[0]
        return jnp.pad(rows, ((0, pad), (0, 0))).astype(jnp.bfloat16), jnp.pad(bias, (0, pad))

    n_wide = sum(SPLITS[g] for g in WIDE_ORDER)
    wide = matmul_act(h, *gather(WIDE_ORDER, n_wide), nt=True)
    narrow = matmul_act(h, *gather(NARROW_ORDER, NARROW_W), nt=True, tn=NARROW_W)
    return wide, narrow


def kernel(x_prompt, x_sample, cache_nsa_kv, cache_win_kv, state_C, state_n, state_m, page_table,
           norm_g, w_in, b_in, cmp_pe, cmp_w1, cmp_b1, cmp_w2, w_up_a, w_up_b, w_out, w_mlp1, w_mlp2):
    Bp, Tp = x_prompt.shape[:2]
    Bs, Ts = x_sample.shape[:2]
    assert Bp == 1 and Ts == 1 and Tp >= WINDOW, (x_prompt.shape, x_sample.shape)
    x = jnp.concatenate([x_prompt.reshape(Tp, D_MODEL), x_sample.reshape(Bs, D_MODEL)], axis=0)
    kv_p, kv_s, win_p, win_s = [], [], [], []
    C_p, C_s, n_p, n_s, m_p, m_s = [], [], [], [], [], []
    kv4 = 4 * KVW
    for l in range(DEPTH):
        wide, narrow = project_in(rmsnorm_cast(x, norm_g[l, 0]), w_in[l], b_in[l])
        col = lambda g, rows: wide[rows, WIDE_OFF[g]:WIDE_OFF[g] + SPLITS[g]]
        ncol = lambda g, rows: narrow[rows, NARROW_OFF[g]:NARROW_OFF[g] + SPLITS[g]]
        pr, sr = slice(0, Tp), slice(Tp, Tp + Bs)

        kva = col(G_KVA, pr)
        ya_p, pe_bias = nsa_prompt_pallas(col(G_QA, pr), ncol(G_GA, pr), kva, cmp_pe[l], cmp_w1[l], cmp_b1[l], cmp_w2[l])
        yb_p, C, n, m = mlstm_prompt_pallas(
            (wide, WIDE_OFF[G_QB] // (B_HEADS * QK_DIM)), col(G_KB, pr).T, (wide, WIDE_OFF[G_VB] // B_WIDTH),
            (wide, WIDE_OFF[G_OB] // B_WIDTH), (narrow, 0, NARROW_OFF[G_FB]),
            narrow[pr, NARROW_OFF[G_IB]:NARROW_OFF[G_IB] + 2 * B_HEADS].T, Tp)
        kv_p.append(kva[:, 0:kv4].reshape(Bp, Tp, 4, A_KV_HEADS, HEAD_DIM))
        win_p.append(kva[Tp - WINDOW:, kv4:].reshape(Bp, WINDOW, 2, A_KV_HEADS, HEAD_DIM))
        C_p.append(C[None])
        n_p.append(n[None])
        m_p.append(m[None])

        kva = col(G_KVA, sr)
        ya_s, win_new = nsa_decode_pallas(col(G_QA, sr), ncol(G_GA, sr), kva, cache_nsa_kv, page_table, cache_win_kv, l,
                                          cmp_w1[l], cmp_w2[l], pe_bias)
        yb_s, C, n, m = mlstm_sample_pallas(col(G_QB, sr), col(G_KB, sr), col(G_VB, sr), ncol(G_IB, sr), ncol(G_FB, sr),
                                            col(G_OB, sr), state_C, state_n, state_m, l)
        kv_s.append(kva[:, 0:kv4].reshape(Bs, Ts, 4, A_KV_HEADS, HEAD_DIM))
        win_s.append(win_new.reshape(Bs, 2, A_KV_HEADS, HEAD_DIM, WINDOW).transpose(0, 4, 1, 2, 3))
        C_s.append(C)
        n_s.append(n)
        m_s.append(m)

        ya = jnp.concatenate([ya_p, ya_s], axis=0)
        yb = jnp.concatenate([yb_p, yb_s], axis=0)
        mix = mix_matmul(ya, yb, w_up_a[l], w_up_b[l], (wide, WIDE_OFF[G_GM]))
        x = matmul_norm_res(mix, w_out[l], x, norm_g[l, 1])
        hid = matmul_act(rmsnorm_cast(x, norm_g[l, 2]), w_mlp1[l], jnp.zeros((D_FF,), jnp.float32), nt=False,
                         act="relu2", out_dtype=jnp.bfloat16)
        x = matmul_norm_res(hid, w_mlp2[l], x, norm_g[l, 3])

    return (x[:Tp].reshape(Bp, Tp, D_MODEL), x[Tp:].reshape(Bs, Ts, D_MODEL),
            jnp.stack(kv_p), jnp.stack(kv_s), jnp.stack(win_p), jnp.stack(win_s),
            jnp.stack(C_p), jnp.stack(C_s), jnp.stack(n_p), jnp.stack(n_s), jnp.stack(m_p), jnp.stack(m_s))
```

```python
import functools
import math

import jax
import jax.numpy as jnp
import numpy as np
from jax import lax
from jax.experimental import pallas as pl
from jax.experimental.pallas import tpu as pltpu

D_MODEL = 2048
DEPTH = 2
PAST_LEN = 2048
A_HEADS = 16
A_KV_HEADS = 4
A_GROUP = A_HEADS // A_KV_HEADS
HEAD_DIM = 64
A_WIDTH = A_HEADS * HEAD_DIM
CMP_BLOCK = 32
CMP_STRIDE = 16
CMP_RATIO = CMP_BLOCK // CMP_STRIDE
CMP_HIDDEN = 128
SLC_BLOCK = 64
SLC_TOP = 16
WINDOW = 512
Q_BLOCK = 128
N_KV_BRANCH = 6
B_HEADS = 8
QK_DIM = 64
V_DIM = 128
B_WIDTH = B_HEADS * V_DIM
MLSTM_CHUNK = 64
D_FF = 4 * D_MODEL
EPS = 1e-6
FORCE = 1e4
NEG = -1e30

SPLITS = (A_WIDTH, N_KV_BRANCH * A_KV_HEADS * HEAD_DIM, 3 * A_HEADS,
          B_HEADS * QK_DIM, B_HEADS * QK_DIM, B_WIDTH, B_HEADS, B_HEADS, B_WIDTH, 2 * D_MODEL)
BOUNDS = tuple(int(b) for b in np.cumsum((0,) + SPLITS))

VMEM_LIMIT_BYTES = 56 * 1024 * 1024


def _params(*sem):
    return pltpu.CompilerParams(dimension_semantics=sem, vmem_limit_bytes=VMEM_LIMIT_BYTES)


def _row_tile(m, want):
    for t in range(min(want, m), 15, -1):
        if m % t == 0 and t % 16 == 0:
            return t
    return m


def _rmsnorm_kernel(x_ref, g_ref, o_ref):
    x = x_ref[...]
    r = lax.rsqrt(jnp.mean(x * x, axis=-1, keepdims=True) + EPS)
    o_ref[...] = (x * r * g_ref[...]).astype(o_ref.dtype)


def rmsnorm_cast(x, g, *, tm=640):
    M, K = x.shape
    tm = _row_tile(M, tm)
    return pl.pallas_call(
        _rmsnorm_kernel,
        grid=(M // tm,),
        in_specs=[pl.BlockSpec((tm, K), lambda i: (i, 0)), pl.BlockSpec((1, K), lambda i: (0, 0))],
        out_specs=pl.BlockSpec((tm, K), lambda i: (i, 0)),
        out_shape=jax.ShapeDtypeStruct((M, K), jnp.bfloat16),
        compiler_params=_params("parallel"),
        name="rmsnorm_cast",
    )(x, g.reshape(1, K))


def _matmul_act_kernel(h_ref, w_ref, b_ref, o_ref, *, nt, act):
    w = w_ref[...].astype(jnp.bfloat16)
    z = (_nt(h_ref[...], w) if nt else jnp.dot(h_ref[...], w, preferred_element_type=jnp.float32)) + b_ref[...]
    if act == "relu2":
        z = jnp.square(jnp.maximum(z, 0.0))
    o_ref[...] = z.astype(o_ref.dtype)


def matmul_act(h, w, b, *, nt, act=None, out_dtype=jnp.float32, tm=1664, tn=512):
    M, K = h.shape
    N = w.shape[0] if nt else w.shape[1]
    tm = _row_tile(M, tm)
    tn = min(tn, N)
    assert N % tn == 0, (N, tn)
    w_spec = pl.BlockSpec((tn, K), lambda i, j: (j, 0)) if nt else pl.BlockSpec((K, tn), lambda i, j: (0, j))
    return pl.pallas_call(
        functools.partial(_matmul_act_kernel, nt=nt, act=act),
        grid=(M // tm, N // tn),
        in_specs=[pl.BlockSpec((tm, K), lambda i, j: (i, 0)), w_spec, pl.BlockSpec((1, tn), lambda i, j: (0, j))],
        out_specs=pl.BlockSpec((tm, tn), lambda i, j: (i, j)),
        out_shape=jax.ShapeDtypeStruct((M, N), out_dtype),
        compiler_params=_params("parallel", "arbitrary"),
        name="matmul_act",
    )(h, w, b.reshape(1, N))


def _mix_kernel(ya_ref, yb_ref, wa_ref, wb_ref, ga_ref, gb_ref, o_ref):
    bf = jnp.bfloat16
    a = jnp.dot(ya_ref[...].astype(bf), wa_ref[...].astype(bf), preferred_element_type=jnp.float32)
    b = jnp.dot(yb_ref[...].astype(bf), wb_ref[...].astype(bf), preferred_element_type=jnp.float32)
    o_ref[...] = (jax.nn.sigmoid(ga_ref[...]) * a + jax.nn.sigmoid(gb_ref[...]) * b).astype(o_ref.dtype)


def mix_matmul(ya, yb, wa, wb, gm_src, *, tm=832, tn=512):
    M, Ka = ya.shape
    Kb = yb.shape[1]
    N = wa.shape[1]
    gm, off = gm_src
    tm = _row_tile(M, tm)
    nb = N // tn
    assert off % tn == 0 and N % tn == 0, (off, N, tn)
    ja, jb = off // tn, off // tn + nb
    return pl.pallas_call(
        _mix_kernel,
        grid=(M // tm, nb),
        in_specs=[
            pl.BlockSpec((tm, Ka), lambda i, j: (i, 0)),
            pl.BlockSpec((tm, Kb), lambda i, j: (i, 0)),
            pl.BlockSpec((Ka, tn), lambda i, j: (0, j)),
            pl.BlockSpec((Kb, tn), lambda i, j: (0, j)),
            pl.BlockSpec((tm, tn), lambda i, j: (i, ja + j)),
            pl.BlockSpec((tm, tn), lambda i, j: (i, jb + j)),
        ],
        out_specs=pl.BlockSpec((tm, tn), lambda i, j: (i, j)),
        out_shape=jax.ShapeDtypeStruct((M, N), jnp.bfloat16),
        compiler_params=_params("parallel", "arbitrary"),
        name="mix_matmul",
    )(ya, yb, wa, wb, gm, gm)


def _matmul_norm_res_kernel(a_ref, w_ref, x_ref, g_ref, o_ref, acc_ref):
    k = pl.program_id(1)

    @pl.when(k == 0)
    def _():
        acc_ref[...] = jnp.zeros_like(acc_ref)

    acc_ref[...] += jnp.dot(a_ref[...], w_ref[...].astype(jnp.bfloat16), preferred_element_type=jnp.float32)

    @pl.when(k == pl.num_programs(1) - 1)
    def _():
        y = acc_ref[...]
        r = lax.rsqrt(jnp.mean(y * y, axis=-1, keepdims=True) + EPS)
        o_ref[...] = x_ref[...] + y * r * g_ref[...]


def matmul_norm_res(a, w, x, g, *, tm=640, tk=512):
    M, K = a.shape
    N = w.shape[1]
    tm = _row_tile(M, tm)
    tk = min(tk, K)
    return pl.pallas_call(
        _matmul_norm_res_kernel,
        grid=(M // tm, K // tk),
        in_specs=[
            pl.BlockSpec((tm, tk), lambda i, k: (i, k)),
            pl.BlockSpec((tk, N), lambda i, k: (k, 0)),
            pl.BlockSpec((tm, N), lambda i, k: (i, 0)),
            pl.BlockSpec((1, N), lambda i, k: (0, 0)),
        ],
        out_specs=pl.BlockSpec((tm, N), lambda i, k: (i, 0)),
        out_shape=jax.ShapeDtypeStruct((M, N), jnp.float32),
        scratch_shapes=[pltpu.VMEM((tm, N), jnp.float32)],
        compiler_params=_params("parallel", "arbitrary"),
        name="matmul_norm_res",
    )(a, w, x, g.reshape(1, N))


NSA_LANES = A_GROUP * Q_BLOCK
KEY_CHUNK = 512
WIN_CHUNK = 128
WIN_SPAN = WINDOW + Q_BLOCK
SLC_SLOTS = 128
POS_PIECES = 6
AUG = 128
MASK_BIG = 30000.0


def _pos_pieces(pos):
    pos = np.asarray(pos, np.int64)
    hi = (pos // 64) * 64
    lo = pos % 64
    return np.stack([hi, lo] * 3, axis=-1).astype(np.float32)


def _slope_rows():
    slopes = 2.0 ** (-8.0 * np.arange(1, A_HEADS + 1, dtype=np.float64) / A_HEADS)
    s = jnp.asarray(slopes.astype(np.float32))
    hi = s.astype(jnp.bfloat16)
    r1 = s - hi.astype(jnp.float32)
    mid = r1.astype(jnp.bfloat16)
    lo = (r1 - mid.astype(jnp.float32)).astype(jnp.bfloat16)
    rows = jnp.stack([hi, hi, mid, mid, lo, lo], axis=0)
    rows = rows.reshape(POS_PIECES, A_KV_HEADS, A_GROUP).transpose(1, 0, 2)
    rows = jnp.repeat(rows, Q_BLOCK, axis=-1)
    return jnp.pad(rows, ((0, 0), (0, 16 - POS_PIECES), (0, 0)))


def _split3(x):
    hi = x.astype(jnp.bfloat16)
    r1 = x - hi.astype(jnp.float32)
    mid = r1.astype(jnp.bfloat16)
    lo = (r1 - mid.astype(jnp.float32)).astype(jnp.bfloat16)
    return hi, mid, lo


def _tile4(x):
    return jnp.concatenate([x] * A_GROUP, axis=1)


def _softmax_step(carry, s, vT):
    m, l, acc = carry
    m_new = jnp.maximum(m, jnp.max(s, axis=0, keepdims=True))
    alpha = jnp.exp(m - m_new)
    p = jnp.exp(s - m_new)
    l = alpha * l + jnp.sum(p, axis=0, keepdims=True)
    acc = alpha * acc + jnp.dot(vT, p.astype(jnp.bfloat16), preferred_element_type=jnp.float32)
    return m_new, l, acc


def _nsa_prompt_kernel(qT_ref, slope_ref, gT_ref, ck_ref, cvT_ref, ovT_ref, ks_ref, vsT_ref,
                       kw_ref, vwT_ref, o_ref, qa_ref, sel_ref):
    f32, bf = jnp.float32, jnp.bfloat16
    i = pl.program_id(1)
    s0 = i * Q_BLOCK
    ncp = ck_ref.shape[1]

    qT = qT_ref[...] * (HEAD_DIM ** -0.5)
    qcat = jnp.concatenate([qT[g * HEAD_DIM:(g + 1) * HEAD_DIM, :] for g in range(A_GROUP)], axis=1)
    qa_ref[0:HEAD_DIM, :] = qcat.astype(bf)
    qa_ref[HEAD_DIM:HEAD_DIM + 16, :] = slope_ref[0]
    qa_ref[HEAD_DIM + 16:AUG, :] = jnp.zeros((AUG - HEAD_DIM - 16, NSA_LANES), bf)
    qc = qa_ref[0:AUG, :]


    sc = jnp.dot(ck_ref[0], qc, preferred_element_type=f32)
    n_idx = lax.broadcasted_iota(jnp.int32, (ncp, Q_BLOCK), 0)
    t_idx = s0 + lax.broadcasted_iota(jnp.int32, (ncp, Q_BLOCK), 1)
    valid = _tile4(n_idx * CMP_STRIDE + (CMP_BLOCK - 1) <= t_idx)
    sc = jnp.where(valid, sc, NEG)
    ec = jnp.where(valid, jnp.exp(sc - jnp.max(sc, axis=0, keepdims=True)), 0.0)
    lc = jnp.sum(ec, axis=0, keepdims=True)
    pc = ec * jnp.where(lc > 0.0, 1.0 / lc, 0.0)
    o_c = jnp.dot(cvT_ref[0], pc.astype(bf), preferred_element_type=f32)

    pg = pc[:, 0:Q_BLOCK]
    for g in range(1, A_GROUP):
        pg = pg + pc[:, g * Q_BLOCK:(g + 1) * Q_BLOCK]
    imp = sum(jnp.dot(ovT_ref[...], piece, preferred_element_type=f32) for piece in _split3(pg))

    blk = lax.broadcasted_iota(jnp.int32, (SLC_SLOTS, Q_BLOCK), 0)
    cur = (s0 + lax.broadcasted_iota(jnp.int32, (SLC_SLOTS, Q_BLOCK), 1)) // SLC_BLOCK
    forced = (blk == 0) | (blk == cur) | (blk == cur - 1)
    causal = blk <= cur
    score = jnp.where(forced, FORCE, jnp.where(causal, imp, -FORCE))
    blk_f = blk.astype(f32)
    sel = jnp.zeros((SLC_SLOTS, Q_BLOCK), f32)
    for _ in range(SLC_TOP):
        top = jnp.max(score, axis=0, keepdims=True)
        first = jnp.min(jnp.where(score == top, blk_f, float(SLC_SLOTS)), axis=0, keepdims=True)
        pick = blk_f == first
        sel = jnp.where(pick, 1.0, sel)
        score = jnp.where(pick, -jnp.inf, score)
    selneg = jnp.where((sel > 0.0) & causal, 0.0, -MASK_BIG)
    qa_ref[AUG:AUG + SLC_SLOTS, :] = _tile4(selneg).astype(bf)

    def sel_scores(c):
        kc = ks_ref[0, pl.ds(pl.multiple_of(c * KEY_CHUNK, KEY_CHUNK), KEY_CHUNK), :]
        return jnp.dot(kc, qa_ref[...], preferred_element_type=f32)

    init = (jnp.full((1, NSA_LANES), NEG, f32), jnp.zeros((1, NSA_LANES), f32),
            jnp.zeros((HEAD_DIM, NSA_LANES), f32))
    c_last = (i * Q_BLOCK) // KEY_CHUNK
    bpc = KEY_CHUNK // SLC_BLOCK
    sel_ref[...] = jnp.where((sel > 0.0) & causal, 1.0, 0.0)

    def sel_chunk(c, cr):
        picked = jnp.max(sel_ref[pl.ds(pl.multiple_of(c * bpc, bpc), bpc), :]) > 0.0
        return lax.cond(picked, lambda x: _softmax_step(x, sel_scores(c), vsT_ref[0, c]), lambda x: x, cr)

    carry = lax.fori_loop(0, c_last, sel_chunk, init)
    j_last = c_last * KEY_CHUNK + lax.broadcasted_iota(jnp.int32, (KEY_CHUNK, Q_BLOCK), 0)
    t_last = s0 + lax.broadcasted_iota(jnp.int32, (KEY_CHUNK, Q_BLOCK), 1)
    m_s, l_s, acc_s = _softmax_step(carry, jnp.where(_tile4(j_last <= t_last), sel_scores(c_last), NEG),
                                    vsT_ref[0, c_last])
    o_s = acc_s * (1.0 / l_s)

    w0 = jnp.maximum(i - WINDOW // WIN_CHUNK, 0)
    kw = kw_ref[0, pl.ds(pl.multiple_of(w0 * WIN_CHUNK, WIN_CHUNK), WIN_SPAN), :]
    sw = jnp.dot(kw, qc, preferred_element_type=f32)
    jw = w0 * WIN_CHUNK + lax.broadcasted_iota(jnp.int32, (WIN_SPAN, Q_BLOCK), 0)
    tw = s0 + lax.broadcasted_iota(jnp.int32, (WIN_SPAN, Q_BLOCK), 1)
    sw = jnp.where(_tile4((jw <= tw) & (jw > tw - WINDOW)), sw, NEG)
    pw = jnp.exp(sw - jnp.max(sw, axis=0, keepdims=True))
    l_w = jnp.sum(pw, axis=0, keepdims=True)
    pw = pw.astype(bf)
    acc_w = jnp.zeros((HEAD_DIM, NSA_LANES), f32)
    for d in range(WIN_SPAN // WIN_CHUNK):
        acc_w = acc_w + jnp.dot(vwT_ref[0, w0 + d], pw[d * WIN_CHUNK:(d + 1) * WIN_CHUNK, :], preferred_element_type=f32)
    o_w = acc_w * (1.0 / l_w)

    gate = jax.nn.sigmoid(gT_ref[0])
    for g in range(A_GROUP):
        cols = slice(g * Q_BLOCK, (g + 1) * Q_BLOCK)
        o = (gate[3 * g:3 * g + 1, :] * o_c[:, cols] + gate[3 * g + 1:3 * g + 2, :] * o_s[:, cols]
             + gate[3 * g + 2:3 * g + 3, :] * o_w[:, cols])
        o_ref[g * HEAD_DIM:(g + 1) * HEAD_DIM, :] = o


def _cmp_prompt_kernel(x_ref, pe_ref, w1_ref, b1_ref, w2T_ref, ck_ref, cvT_ref, bias_ref):
    f32, bf = jnp.float32, jnp.bfloat16
    nseg = ck_ref.shape[1]
    row_w = 2 * A_KV_HEADS * HEAD_DIM
    for c in range(2):
        bias = jnp.dot(pe_ref[c], w1_ref[c], preferred_element_type=f32,
                       precision=lax.Precision.HIGHEST) + b1_ref[c]
        bias_ref[c] = bias
        for k in range(A_KV_HEADS):
            col = (c * A_KV_HEADS + k) * HEAD_DIM
            y0 = jnp.zeros((nseg, CMP_HIDDEN), f32)
            y1 = jnp.zeros((nseg, CMP_HIDDEN), f32)
            for j in range(CMP_STRIDE):
                xj = x_ref[:, j * row_w + col:j * row_w + col + HEAD_DIM].astype(bf)
                w0 = w1_ref[c, j * HEAD_DIM:(j + 1) * HEAD_DIM, :].astype(bf)
                w1 = w1_ref[c, (CMP_STRIDE + j) * HEAD_DIM:(CMP_STRIDE + j + 1) * HEAD_DIM, :].astype(bf)
                y0 = y0 + jnp.dot(xj, w0, preferred_element_type=f32)
                y1 = y1 + jnp.dot(xj, w1, preferred_element_type=f32)
            pre = y0 + pltpu.roll(y1, nseg - 1, 0)
            hid = jnp.maximum(pre + bias, 0.0).astype(bf)
            if c == 0:
                ck_ref[k] = lax.dot_general(hid, w2T_ref[c].astype(bf), (((1,), (1,)), ((), ())),
                                            preferred_element_type=f32)
            else:
                cvT_ref[k] = lax.dot_general(w2T_ref[c].astype(bf), hid, (((1,), (1,)), ((), ())),
                                             preferred_element_type=f32)


def compress_prompt(x_cmp, pe, w1, b1, w2):
    T = x_cmp.shape[0]
    nseg = T // CMP_STRIDE
    return pl.pallas_call(
        _cmp_prompt_kernel,
        out_shape=(jax.ShapeDtypeStruct((A_KV_HEADS, nseg, HEAD_DIM), jnp.float32),
                   jax.ShapeDtypeStruct((A_KV_HEADS, HEAD_DIM, nseg), jnp.float32),
                   jax.ShapeDtypeStruct((2, 1, CMP_HIDDEN), jnp.float32)),
        compiler_params=pltpu.CompilerParams(vmem_limit_bytes=VMEM_LIMIT_BYTES),
        name="compress_prompt",
    )(x_cmp.reshape(nseg, CMP_STRIDE * x_cmp.shape[1]), pe.reshape(2, 1, CMP_BLOCK * HEAD_DIM), w1, b1.reshape(2, 1, CMP_HIDDEN), w2.transpose(0, 2, 1))


def nsa_prompt_pallas(qa, ga, kva, pe, w1, b1, w2):
    bf = jnp.bfloat16
    T = qa.shape[0]
    nqb = T // Q_BLOCK
    assert T % KEY_CHUNK == 0 and T >= WIN_SPAN, T
    ncp = T // CMP_STRIDE
    kvw = A_KV_HEADS * HEAD_DIM
    ck, cvT, pe_bias = compress_prompt(kva[:, 0:2 * kvw], pe, w1, b1, w2)

    def heads(x):
        return x.reshape(T, A_KV_HEADS, HEAD_DIM).transpose(1, 0, 2)

    def chunksT(x, chunk):
        return x.reshape(T // chunk, chunk, A_KV_HEADS, HEAD_DIM).transpose(2, 0, 3, 1).astype(bf)

    pos = np.arange(T)
    key_pos = jnp.broadcast_to(jnp.asarray(_pos_pieces(pos)), (A_KV_HEADS, T, POS_PIECES))
    zpad = jnp.zeros((A_KV_HEADS, T, AUG - HEAD_DIM - POS_PIECES), jnp.float32)
    onehot = jnp.broadcast_to(jnp.asarray((pos[:, None] // SLC_BLOCK == np.arange(SLC_SLOTS)[None, :]).astype(np.float32)),
                              (A_KV_HEADS, T, SLC_SLOTS))
    ks_aug = jnp.concatenate([heads(kva[:, 2 * kvw:3 * kvw]), key_pos, zpad, onehot], axis=-1).astype(bf)
    kw_aug = jnp.concatenate([heads(kva[:, 4 * kvw:5 * kvw]), key_pos, zpad], axis=-1).astype(bf)
    c_end = np.arange(ncp) * CMP_STRIDE + (CMP_BLOCK - 1)
    ck_aug = jnp.concatenate([ck, jnp.broadcast_to(jnp.asarray(_pos_pieces(c_end)), (A_KV_HEADS, ncp, POS_PIECES)),
                              jnp.zeros((A_KV_HEADS, ncp, AUG - HEAD_DIM - POS_PIECES), jnp.float32)], axis=-1).astype(bf)
    vsT = chunksT(kva[:, 3 * kvw:4 * kvw], KEY_CHUNK)
    vwT = chunksT(kva[:, 5 * kvw:6 * kvw], WIN_CHUNK)
    c_start = c_end - (CMP_BLOCK - 1)
    s_start = np.arange(SLC_SLOTS) * SLC_BLOCK
    ovT = ((c_start[None, :] < s_start[:, None] + SLC_BLOCK) & (c_end[None, :] >= s_start[:, None])
           & (np.arange(ncp)[None, :] < ncp - 1))
    ovT = jnp.asarray(ovT.astype(np.float32)).astype(bf)
    gT = jnp.pad(ga.T.reshape(A_KV_HEADS, 3 * A_GROUP, T), ((0, 0), (0, 16 - 3 * A_GROUP), (0, 0)))

    yT = pl.pallas_call(
        _nsa_prompt_kernel,
        grid=(A_KV_HEADS, nqb),
        in_specs=[
            pl.BlockSpec((A_GROUP * HEAD_DIM, Q_BLOCK), lambda k, i: (k, i)),
            pl.BlockSpec((1, 16, NSA_LANES), lambda k, i: (k, 0, 0)),
            pl.BlockSpec((1, 16, Q_BLOCK), lambda k, i: (k, 0, i)),
            pl.BlockSpec((1, ncp, AUG), lambda k, i: (k, 0, 0)),
            pl.BlockSpec((1, HEAD_DIM, ncp), lambda k, i: (k, 0, 0)),
            pl.BlockSpec((SLC_SLOTS, ncp), lambda k, i: (0, 0)),
            pl.BlockSpec((1, T, AUG + SLC_SLOTS), lambda k, i: (k, 0, 0)),
            pl.BlockSpec((1, T // KEY_CHUNK, HEAD_DIM, KEY_CHUNK), lambda k, i: (k, 0, 0, 0)),
            pl.BlockSpec((1, T, AUG), lambda k, i: (k, 0, 0)),
            pl.BlockSpec((1, T // WIN_CHUNK, HEAD_DIM, WIN_CHUNK), lambda k, i: (k, 0, 0, 0)),
        ],
        out_specs=pl.BlockSpec((A_GROUP * HEAD_DIM, Q_BLOCK), lambda k, i: (k, i)),
        out_shape=jax.ShapeDtypeStruct((A_WIDTH, T), jnp.float32),
        scratch_shapes=[pltpu.VMEM((AUG + SLC_SLOTS, NSA_LANES), bf), pltpu.VMEM((SLC_SLOTS, Q_BLOCK), jnp.float32)],
        compiler_params=_params("parallel", "arbitrary"),
        name="nsa_prompt",
    )(qa.T, _slope_rows(), gT, ck_aug, cvT.astype(bf), ovT, ks_aug, vsT, kw_aug, vwT)
    return yT.T, pe_bias


PAGE_SIZE = 128
N_PAGES = PAST_LEN // PAGE_SIZE
PAGE_SEGS = PAGE_SIZE // CMP_STRIDE
PAST_SEGS = PAST_LEN // CMP_STRIDE
ROW_W = 4 * A_KV_HEADS * HEAD_DIM
KVW = A_KV_HEADS * HEAD_DIM
NS_SAMPLE = PAST_LEN // SLC_BLOCK + 1


def _nt(a, b):
    return lax.dot_general(a, b, (((1,), (1,)), ((), ())), preferred_element_type=jnp.float32)


def _dot3(x, w):
    return sum(jnp.dot(p, w, preferred_element_type=jnp.float32) for p in _split3(x))


def _nsa_sample_kernel(pt_ref, q_ref, g_ref, new_ref, *refs):
    del pt_ref
    pages = refs[:N_PAGES]
    (win_ref, w1_ref, bias_ref, w2bd_ref, bmask_ref, basec_ref, bases_ref, basew_ref, slope_ref,
     grp_ref, ov_ref, exp_ref, o_ref, wout_ref, xc_ref) = refs[N_PAGES:]
    f32, bf = jnp.float32, jnp.bfloat16

    q = q_ref[0] * (HEAD_DIM ** -0.5)
    qbd = jnp.where(bmask_ref[...] > 0.0, jnp.concatenate([q] * A_KV_HEADS, axis=1), 0.0).astype(bf)
    qbd_f = qbd.astype(f32)
    new = new_ref[0]

    def slab(off, width):
        return jnp.concatenate([pg[0, :, off:off + width] for pg in pages], axis=0)

    ckv = []
    for c in range(2):
        for j in range(CMP_STRIDE):
            s = slab(j * ROW_W + c * KVW, KVW).astype(bf)
            for k in range(A_KV_HEADS):
                xc_ref[k * PAST_SEGS:(k + 1) * PAST_SEGS, j * HEAD_DIM:(j + 1) * HEAD_DIM] = (
                    s[:, k * HEAD_DIM:(k + 1) * HEAD_DIM])
        y = jnp.dot(xc_ref[...], w1_ref[c], preferred_element_type=f32)
        pre = y[:, 0:CMP_HIDDEN] + pltpu.roll(y[:, CMP_HIDDEN:2 * CMP_HIDDEN], A_KV_HEADS * PAST_SEGS - 1, 0)
        hid = jnp.maximum(pre + bias_ref[c], 0.0).astype(bf)
        hid = jnp.concatenate([hid[k * PAST_SEGS:(k + 1) * PAST_SEGS] for k in range(A_KV_HEADS)], axis=1)
        ckv.append(jnp.dot(hid, w2bd_ref[c], preferred_element_type=f32).astype(bf))
    ck, cv = ckv

    sc = _nt(qbd, ck) + basec_ref[...]
    ec = jnp.exp(sc - jnp.max(sc, axis=1, keepdims=True))
    pc = ec / jnp.sum(ec, axis=1, keepdims=True)
    o_c = jnp.dot(pc.astype(bf), cv, preferred_element_type=f32)

    pov = _dot3(pc, ov_ref[...])
    p_hi, p_mid, p_lo = _split3(pov)
    imp = sum(jnp.dot(grp_ref[...], p, preferred_element_type=f32) for p in (p_hi, p_mid, p_lo))

    lane = lax.broadcasted_iota(jnp.int32, (A_HEADS, PAST_SEGS), 1)
    score = jnp.where((lane == 0) | (lane == NS_SAMPLE - 1) | (lane == NS_SAMPLE - 2), FORCE,
                      jnp.where(lane < NS_SAMPLE, imp, -jnp.inf))
    rr = lax.broadcasted_iota(jnp.int32, (PAST_SEGS, PAST_SEGS), 0)
    cc = lax.broadcasted_iota(jnp.int32, (PAST_SEGS, PAST_SEGS), 1)
    row_id = lax.broadcasted_iota(jnp.int32, (A_HEADS, PAST_SEGS), 0)
    selrows = jnp.zeros((A_HEADS, PAST_SEGS), f32)
    for k in range(A_KV_HEADS):
        row = jnp.broadcast_to(score[A_GROUP * k:A_GROUP * k + 1, :], (PAST_SEGS, PAST_SEGS))
        col = jnp.max(jnp.where(rr == cc, row, -jnp.inf), axis=1, keepdims=True)
        ahead = (col > row) | ((col == row) & (rr < cc))
        rank = jnp.sum(jnp.where(ahead, 1.0, 0.0), axis=0, keepdims=True)
        sel_k = jnp.where(rank < float(SLC_TOP), 1.0, 0.0)
        selrows = jnp.where(row_id // A_GROUP == k, jnp.broadcast_to(sel_k, (A_HEADS, PAST_SEGS)), selrows)
    selseg = jnp.dot(selrows.astype(bf), exp_ref[...], preferred_element_type=f32)
    maskbias = jnp.where(selseg > 0.5, 0.0, NEG)

    slope = slope_ref[...]
    scores = []
    for j in range(CMP_STRIDE):
        kj = slab(j * ROW_W + 2 * KVW, KVW).astype(bf)
        scores.append(_nt(qbd, kj) + (bases_ref[...] + slope * float(j)) + maskbias)
    s_new = jnp.sum(qbd_f * new[:, 2 * KVW:3 * KVW], axis=1, keepdims=True)
    mx = scores[0]
    for s in scores[1:]:
        mx = jnp.maximum(mx, s)
    m = jnp.maximum(jnp.max(mx, axis=1, keepdims=True), s_new)
    p_new = jnp.exp(s_new - m)
    acc = p_new * new[:, 3 * KVW:4 * KVW]
    psum = jnp.zeros((A_HEADS, PAST_SEGS), f32)
    for j in range(CMP_STRIDE):
        p = jnp.exp(scores[j] - m)
        psum = psum + p
        vj = slab(j * ROW_W + 3 * KVW, KVW).astype(bf)
        acc = acc + jnp.dot(p.astype(bf), vj, preferred_element_type=f32)
    o_s = acc / (jnp.sum(psum, axis=1, keepdims=True) + p_new)

    kw = win_ref[0, :, 0:KVW].astype(bf)
    vw = win_ref[0, :, KVW:2 * KVW].astype(bf)
    sw = _nt(qbd, kw) + basew_ref[...]
    s_neww = jnp.sum(qbd_f * new[:, 4 * KVW:5 * KVW], axis=1, keepdims=True)
    m = jnp.maximum(jnp.max(sw, axis=1, keepdims=True), s_neww)
    pw = jnp.exp(sw - m)
    p_new = jnp.exp(s_neww - m)
    o_w = (jnp.dot(pw.astype(bf), vw, preferred_element_type=f32) + p_new * new[:, 5 * KVW:6 * KVW]) / (
        jnp.sum(pw, axis=1, keepdims=True) + p_new)

    gate = jax.nn.sigmoid(g_ref[0])
    o = gate[:, 0:1] * o_c + gate[:, 1:2] * o_s + gate[:, 2:3] * o_w
    for k in range(A_KV_HEADS):
        o_ref[0, A_GROUP * k:A_GROUP * (k + 1), :] = o[A_GROUP * k:A_GROUP * (k + 1), k * HEAD_DIM:(k + 1) * HEAD_DIM]

    wout_ref[0, 0:WINDOW - 1, :] = win_ref[0, 1:WINDOW, :]
    wout_ref[0, WINDOW - 1:WINDOW, :] = new[:, 4 * KVW:6 * KVW]


def _sample_constants():
    slopes = (2.0 ** (-8.0 * np.arange(1, A_HEADS + 1, dtype=np.float64) / A_HEADS)).astype(np.float32)[:, None]
    heads = np.arange(A_HEADS)[:, None]
    bmask = (np.arange(KVW)[None, :] // HEAD_DIM == heads // A_GROUP).astype(np.float32)
    n = np.arange(PAST_SEGS)[None, :]
    c_end = n * CMP_STRIDE + (CMP_BLOCK - 1)
    basec = np.where(n < PAST_SEGS - 1, slopes * (c_end - PAST_LEN), NEG).astype(np.float32)
    bases = (slopes * (n * CMP_STRIDE - PAST_LEN)).astype(np.float32)
    r = np.arange(WINDOW)[None, :]
    basew = np.where(r >= 1, slopes * (r - WINDOW), NEG).astype(np.float32)
    slope = np.broadcast_to(slopes, (A_HEADS, PAST_SEGS)).astype(np.float32)
    grp = (heads // A_GROUP == heads.T // A_GROUP).astype(np.float32)
    c_start = np.arange(PAST_SEGS) * CMP_STRIDE
    s_start = np.arange(PAST_SEGS) * SLC_BLOCK
    ov = ((c_start[:, None] < s_start[None, :] + SLC_BLOCK) & (c_start[:, None] + CMP_BLOCK - 1 >= s_start[None, :])
          & (np.arange(PAST_SEGS)[:, None] < PAST_SEGS - 1) & (np.arange(PAST_SEGS)[None, :] < NS_SAMPLE))
    expand = (np.arange(PAST_SEGS)[None, :] * CMP_STRIDE // SLC_BLOCK == np.arange(PAST_SEGS)[:, None])
    as_bf = lambda a: jnp.asarray(a.astype(np.float32)).astype(jnp.bfloat16)
    return (jnp.asarray(bmask), jnp.asarray(basec), jnp.asarray(bases), jnp.asarray(basew), jnp.asarray(slope),
            as_bf(grp), as_bf(ov), as_bf(expand))


def nsa_sample_pallas(qa, ga, kva, pool, page_table, win_buf, layer, w1, w2, pe_bias):
    bf = jnp.bfloat16
    B = qa.shape[0]
    npool = pool.shape[1]
    page_table = page_table + layer * npool
    seg_w = CMP_STRIDE * ROW_W
    w1r = w1.reshape(2, CMP_RATIO, CMP_STRIDE * HEAD_DIM, CMP_HIDDEN).transpose(0, 2, 1, 3)
    w1r = w1r.reshape(2, CMP_STRIDE * HEAD_DIM, CMP_RATIO * CMP_HIDDEN).astype(bf)
    w2bd = jnp.einsum('kl,ced->ckeld', jnp.eye(A_KV_HEADS, dtype=w2.dtype), w2)
    w2bd = w2bd.reshape(2, A_KV_HEADS * CMP_HIDDEN, KVW).astype(bf)
    consts = _sample_constants()
    full = lambda shape: pl.BlockSpec(shape, lambda b, pt: (0,) * len(shape))
    page_specs = [pl.BlockSpec((1, PAGE_SEGS, seg_w), functools.partial(lambda p, b, pt: (pt[b, p], 0, 0), p))
                  for p in range(N_PAGES)]
    grid_spec = pltpu.PrefetchScalarGridSpec(
        num_scalar_prefetch=1,
        grid=(B,),
        in_specs=[
            pl.BlockSpec((1, A_HEADS, HEAD_DIM), lambda b, pt: (b, 0, 0)),
            pl.BlockSpec((1, A_HEADS, 3), lambda b, pt: (b, 0, 0)),
            pl.BlockSpec((1, 1, N_KV_BRANCH * KVW), lambda b, pt: (b, 0, 0)),
            *page_specs,
            pl.BlockSpec((1, WINDOW, 2 * KVW), lambda b, pt: (layer * B + b, 0, 0)),
            full(w1r.shape), full((2, 1, CMP_HIDDEN)), full(w2bd.shape),
            *[full(c.shape) for c in consts],
        ],
        out_specs=[
            pl.BlockSpec((1, A_HEADS, HEAD_DIM), lambda b, pt: (b, 0, 0)),
            pl.BlockSpec((1, WINDOW, 2 * KVW), lambda b, pt: (b, 0, 0)),
        ],
        scratch_shapes=[pltpu.VMEM((A_KV_HEADS * PAST_SEGS, CMP_STRIDE * HEAD_DIM), bf)],
    )
    pool3 = pool.reshape(pool.shape[0] * npool, PAGE_SEGS, seg_w)
    o, wout = pl.pallas_call(
        _nsa_sample_kernel,
        grid_spec=grid_spec,
        out_shape=(jax.ShapeDtypeStruct((B, A_HEADS, HEAD_DIM), jnp.float32),
                   jax.ShapeDtypeStruct((B, WINDOW, 2 * KVW), jnp.float32)),
        compiler_params=_params("arbitrary"),
        name="nsa_sample",
    )(page_table, qa.reshape(B, A_HEADS, HEAD_DIM), ga.reshape(B, A_HEADS, 3), kva.reshape(B, 1, N_KV_BRANCH * KVW),
      *([pool3] * N_PAGES), win_buf.reshape(win_buf.shape[0] * B, WINDOW, 2 * KVW), w1r, pe_bias.reshape(2, 1, CMP_HIDDEN), w2bd, *consts)
    return o.reshape(B, A_WIDTH), wout


def _nsa_decode_kernel(pt_ref, q_ref, g_ref, new_ref, *refs):
    del pt_ref
    pages = refs[:N_PAGES]
    (win_ref, w1_ref, bias_ref, w2bd_ref, bmask_ref, basec_ref, bases_ref, basew_ref, slope_ref,
     grp_ref, ov_ref, exp_ref, o_ref, wout_ref, tr_ref, xc_ref) = refs[N_PAGES:]
    f32, bf = jnp.float32, jnp.bfloat16

    q = q_ref[0] * (HEAD_DIM ** -0.5)
    qbd = jnp.where(bmask_ref[...] > 0.0, jnp.concatenate([q] * A_KV_HEADS, axis=1), 0.0).astype(bf)
    qbd_f = qbd.astype(f32)
    new = new_ref[0]

    ckv = []
    for c in range(2):
        for p in range(N_PAGES):
            for kp in range(2):
                r0 = c * KVW + kp * 2 * HEAD_DIM
                t0 = kp * PAST_LEN + p * PAGE_SIZE
                tr_ref[t0:t0 + PAGE_SIZE, :] = pages[p][0, r0:r0 + 2 * HEAD_DIM, :].T
        for j in range(CMP_STRIDE):
            for kp in range(2):
                piece = tr_ref[pl.ds(kp * PAST_LEN + j, PAST_SEGS, stride=CMP_STRIDE), :]
                for kk in range(2):
                    k = 2 * kp + kk
                    xc_ref[k * PAST_SEGS:(k + 1) * PAST_SEGS, j * HEAD_DIM:(j + 1) * HEAD_DIM] = (
                        piece[:, kk * HEAD_DIM:(kk + 1) * HEAD_DIM])
        y = jnp.dot(xc_ref[...].astype(bf), w1_ref[c], preferred_element_type=f32)
        pre = y[:, 0:CMP_HIDDEN] + pltpu.roll(y[:, CMP_HIDDEN:2 * CMP_HIDDEN], A_KV_HEADS * PAST_SEGS - 1, 0)
        hid = jnp.maximum(pre + bias_ref[c], 0.0).astype(bf)
        hid = jnp.concatenate([hid[k * PAST_SEGS:(k + 1) * PAST_SEGS] for k in range(A_KV_HEADS)], axis=1)
        ckv.append(jnp.dot(hid, w2bd_ref[c], preferred_element_type=f32).astype(bf))
    ck, cv = ckv

    sc = _nt(qbd, ck) + basec_ref[...]
    ec = jnp.exp(sc - jnp.max(sc, axis=1, keepdims=True))
    pc = ec / jnp.sum(ec, axis=1, keepdims=True)
    o_c = jnp.dot(pc.astype(bf), cv, preferred_element_type=f32)

    pov = _dot3(pc, ov_ref[...])
    imp = sum(jnp.dot(grp_ref[...], p, preferred_element_type=f32) for p in _split3(pov))

    lane = lax.broadcasted_iota(jnp.int32, (A_HEADS, PAST_SEGS), 1)
    score = jnp.where((lane == 0) | (lane == NS_SAMPLE - 1) | (lane == NS_SAMPLE - 2), FORCE,
                      jnp.where(lane < NS_SAMPLE, imp, -jnp.inf))
    rr = lax.broadcasted_iota(jnp.int32, (PAST_SEGS, PAST_SEGS), 0)
    cc = lax.broadcasted_iota(jnp.int32, (PAST_SEGS, PAST_SEGS), 1)
    row_id = lax.broadcasted_iota(jnp.int32, (A_HEADS, PAST_SEGS), 0)
    selrows = jnp.zeros((A_HEADS, PAST_SEGS), f32)
    for k in range(A_KV_HEADS):
        row = jnp.broadcast_to(score[A_GROUP * k:A_GROUP * k + 1, :], (PAST_SEGS, PAST_SEGS))
        col = jnp.max(jnp.where(rr == cc, row, -jnp.inf), axis=1, keepdims=True)
        ahead = (col > row) | ((col == row) & (rr < cc))
        rank = jnp.sum(jnp.where(ahead, 1.0, 0.0), axis=0, keepdims=True)
        sel_k = jnp.where(rank < float(SLC_TOP), 1.0, 0.0)
        selrows = jnp.where(row_id // A_GROUP == k, jnp.broadcast_to(sel_k, (A_HEADS, PAST_SEGS)), selrows)
    selpos = jnp.dot(selrows.astype(bf), exp_ref[...], preferred_element_type=f32)

    slope = slope_ref[...]
    scores = []
    for p in range(N_PAGES):
        kp_ = pages[p][0, 2 * KVW:3 * KVW, :].astype(bf)
        mask = jnp.where(selpos[:, p * PAGE_SIZE:(p + 1) * PAGE_SIZE] > 0.5, 0.0, NEG)
        scores.append(jnp.dot(qbd, kp_, preferred_element_type=f32) + (bases_ref[...] + slope * float(p * PAGE_SIZE)) + mask)
    s_new = jnp.sum(qbd_f * new[:, 2 * KVW:3 * KVW], axis=1, keepdims=True)
    mx = scores[0]
    for s in scores[1:]:
        mx = jnp.maximum(mx, s)
    m = jnp.maximum(jnp.max(mx, axis=1, keepdims=True), s_new)
    p_new = jnp.exp(s_new - m)
    acc = p_new * new[:, 3 * KVW:4 * KVW]
    psum = jnp.zeros((A_HEADS, PAGE_SIZE), f32)
    for p in range(N_PAGES):
        pr = jnp.exp(scores[p] - m)
        psum = psum + pr
        acc = acc + _nt(pr.astype(bf), pages[p][0, 3 * KVW:4 * KVW, :].astype(bf))
    o_s = acc / (jnp.sum(psum, axis=1, keepdims=True) + p_new)

    sw = jnp.dot(qbd, win_ref[0, 0:KVW, :].astype(bf), preferred_element_type=f32) + basew_ref[...]
    s_neww = jnp.sum(qbd_f * new[:, 4 * KVW:5 * KVW], axis=1, keepdims=True)
    m = jnp.maximum(jnp.max(sw, axis=1, keepdims=True), s_neww)
    pw = jnp.exp(sw - m)
    p_new = jnp.exp(s_neww - m)
    o_w = (_nt(pw.astype(bf), win_ref[0, KVW:2 * KVW, :].astype(bf)) + p_new * new[:, 5 * KVW:6 * KVW]) / (
        jnp.sum(pw, axis=1, keepdims=True) + p_new)

    gate = jax.nn.sigmoid(g_ref[0])
    o = gate[:, 0:1] * o_c + gate[:, 1:2] * o_s + gate[:, 2:3] * o_w
    for k in range(A_KV_HEADS):
        o_ref[0, A_GROUP * k:A_GROUP * (k + 1), :] = o[A_GROUP * k:A_GROUP * (k + 1), k * HEAD_DIM:(k + 1) * HEAD_DIM]

    wr = lax.broadcasted_iota(jnp.int32, (WINDOW, 2 * KVW), 0)
    wc = lax.broadcasted_iota(jnp.int32, (WINDOW, 2 * KVW), 1)
    new_col = jnp.sum(jnp.where(wr == wc, jnp.broadcast_to(new[:, 4 * KVW:6 * KVW], (WINDOW, 2 * KVW)), 0.0),
                      axis=1, keepdims=True)
    wout_ref[0] = jnp.where(wc == WINDOW - 1, new_col, pltpu.roll(win_ref[0], WINDOW - 1, 1))


def _decode_constants():
    slopes = (2.0 ** (-8.0 * np.arange(1, A_HEADS + 1, dtype=np.float64) / A_HEADS)).astype(np.float32)[:, None]
    heads = np.arange(A_HEADS)[:, None]
    bmask = (np.arange(KVW)[None, :] // HEAD_DIM == heads // A_GROUP).astype(np.float32)
    n = np.arange(PAST_SEGS)[None, :]
    c_end = n * CMP_STRIDE + (CMP_BLOCK - 1)
    basec = np.where(n < PAST_SEGS - 1, slopes * (c_end - PAST_LEN), NEG).astype(np.float32)
    bases = (slopes * (np.arange(PAGE_SIZE)[None, :] - PAST_LEN)).astype(np.float32)
    r = np.arange(WINDOW)[None, :]
    basew = np.where(r >= 1, slopes * (r - WINDOW), NEG).astype(np.float32)
    slope = np.broadcast_to(slopes, (A_HEADS, PAGE_SIZE)).astype(np.float32)
    grp = (heads // A_GROUP == heads.T // A_GROUP).astype(np.float32)
    c_start = np.arange(PAST_SEGS) * CMP_STRIDE
    s_start = np.arange(PAST_SEGS) * SLC_BLOCK
    ov = ((c_start[:, None] < s_start[None, :] + SLC_BLOCK) & (c_start[:, None] + CMP_BLOCK - 1 >= s_start[None, :])
          & (np.arange(PAST_SEGS)[:, None] < PAST_SEGS - 1) & (np.arange(PAST_SEGS)[None, :] < NS_SAMPLE))
    expand = (np.arange(PAST_LEN)[None, :] // SLC_BLOCK == np.arange(PAST_SEGS)[:, None])
    as_bf = lambda a: jnp.asarray(a.astype(np.float32)).astype(jnp.bfloat16)
    return (jnp.asarray(bmask), jnp.asarray(basec), jnp.asarray(bases), jnp.asarray(basew), jnp.asarray(slope),
            as_bf(grp), as_bf(ov), as_bf(expand))


def nsa_decode_pallas(qa, ga, kva, pool, page_table, win_buf, layer, w1, w2, pe_bias):
    bf = jnp.bfloat16
    B = qa.shape[0]
    depth, npool = pool.shape[:2]
    page_table = page_table + layer * npool
    poolT = pool.transpose(0, 1, 3, 4, 5, 2).reshape(depth * npool, ROW_W, PAGE_SIZE)
    winT = win_buf.transpose(0, 1, 3, 4, 5, 2).reshape(depth * B, 2 * KVW, WINDOW)
    w1r = w1.reshape(2, CMP_RATIO, CMP_STRIDE * HEAD_DIM, CMP_HIDDEN).transpose(0, 2, 1, 3)
    w1r = w1r.reshape(2, CMP_STRIDE * HEAD_DIM, CMP_RATIO * CMP_HIDDEN).astype(bf)
    w2bd = jnp.einsum('kl,ced->ckeld', jnp.eye(A_KV_HEADS, dtype=w2.dtype), w2)
    w2bd = w2bd.reshape(2, A_KV_HEADS * CMP_HIDDEN, KVW).astype(bf)
    consts = _decode_constants()
    full = lambda shape: pl.BlockSpec(shape, lambda b, pt: (0,) * len(shape))
    page_specs = [pl.BlockSpec((1, ROW_W, PAGE_SIZE), functools.partial(lambda p, b, pt: (pt[b, p], 0, 0), p))
                  for p in range(N_PAGES)]
    grid_spec = pltpu.PrefetchScalarGridSpec(
        num_scalar_prefetch=1,
        grid=(B,),
        in_specs=[
            pl.BlockSpec((1, A_HEADS, HEAD_DIM), lambda b, pt: (b, 0, 0)),
            pl.BlockSpec((1, A_HEADS, 3), lambda b, pt: (b, 0, 0)),
            pl.BlockSpec((1, 1, N_KV_BRANCH * KVW), lambda b, pt: (b, 0, 0)),
            *page_specs,
            pl.BlockSpec((1, 2 * KVW, WINDOW), lambda b, pt: (layer * B + b, 0, 0)),
            full(w1r.shape), full((2, 1, CMP_HIDDEN)), full(w2bd.shape),
            *[full(c.shape) for c in consts],
        ],
        out_specs=[
            pl.BlockSpec((1, A_HEADS, HEAD_DIM), lambda b, pt: (b, 0, 0)),
            pl.BlockSpec((1, 2 * KVW, WINDOW), lambda b, pt: (b, 0, 0)),
        ],
        scratch_shapes=[pltpu.VMEM((2 * PAST_LEN, 2 * HEAD_DIM), jnp.float32),
                        pltpu.VMEM((A_KV_HEADS * PAST_SEGS, CMP_STRIDE * HEAD_DIM), jnp.float32)],
    )
    o, wout = pl.pallas_call(
        _nsa_decode_kernel,
        grid_spec=grid_spec,
        out_shape=(jax.ShapeDtypeStruct((B, A_HEADS, HEAD_DIM), jnp.float32),
                   jax.ShapeDtypeStruct((B, 2 * KVW, WINDOW), jnp.float32)),
        compiler_params=_params("arbitrary"),
        name="nsa_decode",
    )(page_table, qa.reshape(B, A_HEADS, HEAD_DIM), ga.reshape(B, A_HEADS, 3), kva.reshape(B, 1, N_KV_BRANCH * KVW),
      *([poolT] * N_PAGES), winT, w1r, pe_bias.reshape(2, 1, CMP_HIDDEN), w2bd, *consts)
    return o.reshape(B, A_WIDTH), wout


MLSTM_L = 128
VAUG = 2 * V_DIM


def _mlstm_prompt_kernel(q_ref, kT_ref, v_ref, o_ref, fsrc_ref, ifT_ref, y_ref, c_ref, m_ref, *, f_off):
    f32, bf = jnp.float32, jnp.bfloat16
    L = MLSTM_L
    f_ref = fsrc_ref.at[:, f_off:f_off + B_HEADS]
    iT_ref = ifT_ref.at[0:B_HEADS, :]
    fT_ref = ifT_ref.at[B_HEADS:2 * B_HEADS, :]

    @pl.when(pl.program_id(0) == 0)
    def _():
        c_ref[...] = jnp.zeros_like(c_ref)
        m_ref[...] = jnp.zeros_like(m_ref)

    rr = lax.broadcasted_iota(jnp.int32, (L, L), 0)
    cc = lax.broadcasted_iota(jnp.int32, (L, L), 1)
    lower = rr >= cc
    tril = jnp.where(lower, 1.0, 0.0).astype(bf)
    triu = jnp.where(rr <= cc, 1.0, 0.0).astype(bf)
    b_col = sum(jnp.dot(tril, p, preferred_element_type=f32) for p in _split3(jax.nn.log_sigmoid(f_ref[...])))
    b_row = _dot3(jax.nn.log_sigmoid(fT_ref[...]), triu)
    a_row = iT_ref[...] - b_row
    ones_col = jnp.where(lax.broadcasted_iota(jnp.int32, (L, V_DIM), 1) == 0, 1.0, 0.0)

    for h in range(B_HEADS):
        m_prev = m_ref[h, 0:1, 0:1]
        a = a_row[h:h + 1, :]
        amat = jnp.where(lower, jnp.broadcast_to(a, (L, L)), -jnp.inf)
        big_m = jnp.maximum(m_prev, jnp.max(amat, axis=1, keepdims=True))
        dmat = jnp.exp(amat - big_m)
        inter = jnp.exp(m_prev - big_m)
        q = q_ref[:, h * QK_DIM:(h + 1) * QK_DIM].astype(bf)
        kT = kT_ref[h * QK_DIM:(h + 1) * QK_DIM, :] * (QK_DIM ** -0.5)
        vaug = jnp.concatenate([v_ref[:, h * V_DIM:(h + 1) * V_DIM], ones_col], axis=1).astype(bf)
        s = jnp.dot(q, kT.astype(bf), preferred_element_type=f32) * dmat
        r = (jnp.dot(s.astype(bf), vaug, preferred_element_type=f32)
             + inter * jnp.dot(q, c_ref[h].astype(bf), preferred_element_type=f32))
        m_new = b_col[:, h:h + 1] + big_m
        den = jnp.maximum(jnp.abs(r[:, V_DIM:V_DIM + 1]), jnp.exp(-m_new))
        y_ref[:, h * V_DIM:(h + 1) * V_DIM] = (jax.nn.sigmoid(o_ref[:, h * V_DIM:(h + 1) * V_DIM])
                                               * (r[:, 0:V_DIM] / den))
        m_end = big_m[L - 1:L, :]
        w_end = jnp.exp(a - m_end)
        c_ref[h] = inter[L - 1:L, :] * c_ref[h] + jnp.dot((kT * w_end).astype(bf), vaug, preferred_element_type=f32)
        m_ref[h] = jnp.broadcast_to(m_new[L - 1:L, :], m_ref.shape[1:])


def mlstm_prompt_pallas(q_src, kT, v_src, o_src, f_src, ifT, T):
    L = MLSTM_L
    (q_arr, q_cb), (v_arr, v_cb), (o_arr, o_cb), (f_arr, f_cb, f_off) = q_src, v_src, o_src, f_src
    y, caug, m = pl.pallas_call(
        functools.partial(_mlstm_prompt_kernel, f_off=f_off),
        grid=(T // L,),
        in_specs=[
            pl.BlockSpec((L, B_HEADS * QK_DIM), lambda c: (c, q_cb)),
            pl.BlockSpec((B_HEADS * QK_DIM, L), lambda c: (0, c)),
            pl.BlockSpec((L, B_WIDTH), lambda c: (c, v_cb)),
            pl.BlockSpec((L, B_WIDTH), lambda c: (c, o_cb)),
            pl.BlockSpec((L, f_arr.shape[1] if f_arr.shape[1] < 128 else 128), lambda c: (c, f_cb)),
            pl.BlockSpec((2 * B_HEADS, L), lambda c: (0, c)),
        ],
        out_specs=[
            pl.BlockSpec((L, B_WIDTH), lambda c: (c, 0)),
            pl.BlockSpec((B_HEADS, QK_DIM, VAUG), lambda c: (0, 0, 0)),
            pl.BlockSpec((B_HEADS, 8, 128), lambda c: (0, 0, 0)),
        ],
        out_shape=(jax.ShapeDtypeStruct((T, B_WIDTH), jnp.float32),
                   jax.ShapeDtypeStruct((B_HEADS, QK_DIM, VAUG), jnp.float32),
                   jax.ShapeDtypeStruct((B_HEADS, 8, 128), jnp.float32)),
        compiler_params=_params("arbitrary"),
        name="mlstm_prompt",
    )(q_arr, kT, v_arr, o_arr, f_arr, ifT)
    return y, caug[:, :, 0:V_DIM], caug[:, :, V_DIM], m[:, 0, 0]


def _mlstm_sample_kernel(q_ref, k_ref, qT_ref, kT_ref, v_ref, o_ref, i_ref, f_ref, m_ref, n_ref, c_ref,
                         y_ref, cn_ref, nn_ref, mn_ref):
    bb = q_ref.shape[0]
    scale = QK_DIM ** -0.5
    for b in range(bb):
        logf = jax.nn.log_sigmoid(f_ref[b])
        m_new = jnp.maximum(logf + m_ref[b], i_ref[b])
        d_all = jnp.exp(i_ref[b] - m_new)
        inter_all = jnp.exp(logf + m_ref[b] - m_new)
        floor_all = jnp.exp(-m_new)
        mn_ref[b] = m_new
        qk_all = jnp.sum(q_ref[b] * k_ref[b], axis=1, keepdims=True) * scale
        qn_all = jnp.sum(q_ref[b] * n_ref[b], axis=1, keepdims=True)
        for h in range(B_HEADS):
            d = d_all[:, h:h + 1]
            inter = inter_all[:, h:h + 1]
            s = qk_all[h:h + 1, :] * d
            c = c_ref[b, h]
            v = v_ref[b, h:h + 1, :]
            qc = qT_ref[b, :, h:h + 1]
            kc = kT_ref[b, :, h:h + 1] * scale
            num = inter * jnp.sum(qc * c, axis=0, keepdims=True) + s * v
            den = inter * qn_all[h:h + 1, :] + s
            hout = num / jnp.maximum(jnp.abs(den), floor_all[:, h:h + 1])
            y_ref[b, h:h + 1, :] = jax.nn.sigmoid(o_ref[b, h:h + 1, :]) * hout
            cn_ref[b, h] = inter * c + d * (kc * v)
            nn_ref[b, h:h + 1, :] = inter * n_ref[b, h:h + 1, :] + d * (k_ref[b, h:h + 1, :] * scale)


def mlstm_sample_pallas(qb, kb, vb, ib, fb, ob, state_C, state_n, state_m, layer, *, bb=8):
    B = qb.shape[0]
    nb = B // bb
    q3 = qb.reshape(B, B_HEADS, QK_DIM)
    k3 = kb.reshape(B, B_HEADS, QK_DIM)
    row8 = lambda x: x.reshape(-1, 1, B_HEADS)
    lay = layer * nb
    y, cn, nn, mn = pl.pallas_call(
        _mlstm_sample_kernel,
        grid=(nb,),
        in_specs=[
            pl.BlockSpec((bb, B_HEADS, QK_DIM), lambda i: (i, 0, 0)),
            pl.BlockSpec((bb, B_HEADS, QK_DIM), lambda i: (i, 0, 0)),
            pl.BlockSpec((bb, QK_DIM, B_HEADS), lambda i: (i, 0, 0)),
            pl.BlockSpec((bb, QK_DIM, B_HEADS), lambda i: (i, 0, 0)),
            pl.BlockSpec((bb, B_HEADS, V_DIM), lambda i: (i, 0, 0)),
            pl.BlockSpec((bb, B_HEADS, V_DIM), lambda i: (i, 0, 0)),
            pl.BlockSpec((bb, 1, B_HEADS), lambda i: (i, 0, 0)),
            pl.BlockSpec((bb, 1, B_HEADS), lambda i: (i, 0, 0)),
            pl.BlockSpec((bb, 1, B_HEADS), lambda i: (lay + i, 0, 0)),
            pl.BlockSpec((bb, B_HEADS, QK_DIM), lambda i: (lay + i, 0, 0)),
            pl.BlockSpec((bb, B_HEADS, QK_DIM, V_DIM), lambda i: (lay + i, 0, 0, 0)),
        ],
        out_specs=[
            pl.BlockSpec((bb, B_HEADS, V_DIM), lambda i: (i, 0, 0)),
            pl.BlockSpec((bb, B_HEADS, QK_DIM, V_DIM), lambda i: (i, 0, 0, 0)),
            pl.BlockSpec((bb, B_HEADS, QK_DIM), lambda i: (i, 0, 0)),
            pl.BlockSpec((bb, 1, B_HEADS), lambda i: (i, 0, 0)),
        ],
        out_shape=(jax.ShapeDtypeStruct((B, B_HEADS, V_DIM), jnp.float32),
                   jax.ShapeDtypeStruct((B, B_HEADS, QK_DIM, V_DIM), jnp.float32),
                   jax.ShapeDtypeStruct((B, B_HEADS, QK_DIM), jnp.float32),
                   jax.ShapeDtypeStruct((B, 1, B_HEADS), jnp.float32)),
        compiler_params=_params("arbitrary"),
        name="mlstm_sample",
    )(q3, k3, q3.transpose(0, 2, 1), k3.transpose(0, 2, 1), vb.reshape(B, B_HEADS, V_DIM),
      ob.reshape(B, B_HEADS, V_DIM), row8(ib), row8(fb), row8(state_m),
      state_n.reshape(-1, B_HEADS, QK_DIM), state_C.reshape(-1, B_HEADS, QK_DIM, V_DIM))
    return y.reshape(B, B_WIDTH), cn, nn, mn.reshape(B, B_HEADS)


(G_QA, G_KVA, G_GA, G_QB, G_KB, G_VB, G_IB, G_FB, G_OB, G_GM) = range(10)
WIDE_ORDER = (G_QA, G_VB, G_OB, G_KVA, G_QB, G_KB, G_GM)
NARROW_ORDER = (G_GA, G_IB, G_FB)
WIDE_OFF = dict(zip(WIDE_ORDER, np.cumsum((0,) + tuple(SPLITS[g] for g in WIDE_ORDER))[:-1].tolist()))
NARROW_OFF = dict(zip(NARROW_ORDER, np.cumsum((0,) + tuple(SPLITS[g] for g in NARROW_ORDER))[:-1].tolist()))
NARROW_W = 128


def project_in(h, w_in, b_in):
    w_t = w_in.T

    def gather(order, pad_to):
        rows = jnp.concatenate([w_t[BOUNDS[g]:BOUNDS[g + 1]] for g in order], axis=0)
        bias = jnp.concatenate([b_in[BOUNDS[g]:BOUNDS[g + 1]] for g in order], axis=0)
        pad = pad_to - rows.shape[0]
        return jnp.pad(rows, ((0, pad), (0, 0))).astype(jnp.bfloat16), jnp.pad(bias, (0, pad))

    n_wide = sum(SPLITS[g] for g in WIDE_ORDER)
    wide = matmul_act(h, *gather(WIDE_ORDER, n_wide), nt=True)
    narrow = matmul_act(h, *gather(NARROW_ORDER, NARROW_W), nt=True, tn=NARROW_W)
    return wide, narrow


def kernel(x_prompt, x_sample, cache_nsa_kv, cache_win_kv, state_C, state_n, state_m, page_table,
           norm_g, w_in, b_in, cmp_pe, cmp_w1, cmp_b1, cmp_w2, w_up_a, w_up_b, w_out, w_mlp1, w_mlp2):
    Bp, Tp = x_prompt.shape[:2]
    Bs, Ts = x_sample.shape[:2]
    assert Bp == 1 and Ts == 1 and Tp >= WINDOW, (x_prompt.shape, x_sample.shape)
    x = jnp.concatenate([x_prompt.reshape(Tp, D_MODEL), x_sample.reshape(Bs, D_MODEL)], axis=0)
    kv_p, kv_s, win_p, win_s = [], [], [], []
    C_p, C_s, n_p, n_s, m_p, m_s = [], [], [], [], [], []
    kv4 = 4 * KVW
    for l in range(DEPTH):
        wide, narrow = project_in(rmsnorm_cast(x, norm_g[l, 0]), w_in[l], b_in[l])
        col = lambda g, rows: wide[rows, WIDE_OFF[g]:WIDE_OFF[g] + SPLITS[g]]
        ncol = lambda g, rows: narrow[rows, NARROW_OFF[g]:NARROW_OFF[g] + SPLITS[g]]
        pr, sr = slice(0, Tp), slice(Tp, Tp + Bs)

        kva = col(G_KVA, pr)
        ya_p, pe_bias = nsa_prompt_pallas(col(G_QA, pr), ncol(G_GA, pr), kva, cmp_pe[l], cmp_w1[l], cmp_b1[l], cmp_w2[l])
        yb_p, C, n, m = mlstm_prompt_pallas(
            (wide, WIDE_OFF[G_QB] // (B_HEADS * QK_DIM)), col(G_KB, pr).T, (wide, WIDE_OFF[G_VB] // B_WIDTH),
            (wide, WIDE_OFF[G_OB] // B_WIDTH), (narrow, 0, NARROW_OFF[G_FB]),
            narrow[pr, NARROW_OFF[G_IB]:NARROW_OFF[G_IB] + 2 * B_HEADS].T, Tp)
        kv_p.append(kva[:, 0:kv4].reshape(Bp, Tp, 4, A_KV_HEADS, HEAD_DIM))
        win_p.append(kva[Tp - WINDOW:, kv4:].reshape(Bp, WINDOW, 2, A_KV_HEADS, HEAD_DIM))
        C_p.append(C[None])
        n_p.append(n[None])
        m_p.append(m[None])

        kva = col(G_KVA, sr)
        ya_s, win_new = nsa_decode_pallas(col(G_QA, sr), ncol(G_GA, sr), kva, cache_nsa_kv, page_table, cache_win_kv, l,
                                          cmp_w1[l], cmp_w2[l], pe_bias)
        yb_s, C, n, m = mlstm_sample_pallas(col(G_QB, sr), col(G_KB, sr), col(G_VB, sr), ncol(G_IB, sr), ncol(G_FB, sr),
                                            col(G_OB, sr), state_C, state_n, state_m, l)
        kv_s.append(kva[:, 0:kv4].reshape(Bs, Ts, 4, A_KV_HEADS, HEAD_DIM))
        win_s.append(win_new.reshape(Bs, 2, A_KV_HEADS, HEAD_DIM, WINDOW).transpose(0, 4, 1, 2, 3))
        C_s.append(C)
        n_s.append(n)
        m_s.append(m)

        ya = jnp.concatenate([ya_p, ya_s], axis=0)
        yb = jnp.concatenate([yb_p, yb_s], axis=0)
        mix = mix_matmul(ya, yb, w_up_a[l], w_up_b[l], (wide, WIDE_OFF[G_GM]))
        x = matmul_norm_res(mix, w_out[l], x, norm_g[l, 1])
        hid = matmul_act(rmsnorm_cast(x, norm_g[l, 2]), w_mlp1[l], jnp.zeros((D_FF,), jnp.float32), nt=False,
                         act="relu2", out_dtype=jnp.bfloat16)
        x = matmul_norm_res(hid, w_mlp2[l], x, norm_g[l, 3])

    return (x[:Tp].reshape(Bp, Tp, D_MODEL), x[Tp:].reshape(Bs, Ts, D_MODEL),
            jnp.stack(kv_p), jnp.stack(kv_s), jnp.stack(win_p), jnp.stack(win_s),
            jnp.stack(C_p), jnp.stack(C_s), jnp.stack(n_p), jnp.stack(n_s), jnp.stack(m_p), jnp.stack(m_s))
```

```python
import functools
import math

import jax
import jax.numpy as jnp
import numpy as np
from jax import lax
from jax.experimental import pallas as pl
from jax.experimental.pallas import tpu as pltpu

D_MODEL = 2048
DEPTH = 2
PAST_LEN = 2048
A_HEADS = 16
A_KV_HEADS = 4
A_GROUP = A_HEADS // A_KV_HEADS
HEAD_DIM = 64
A_WIDTH = A_HEADS * HEAD_DIM
CMP_BLOCK = 32
CMP_STRIDE = 16
CMP_RATIO = CMP_BLOCK // CMP_STRIDE
CMP_HIDDEN = 128
SLC_BLOCK = 64
SLC_TOP = 16
WINDOW = 512
Q_BLOCK = 128
N_KV_BRANCH = 6
B_HEADS = 8
QK_DIM = 64
V_DIM = 128
B_WIDTH = B_HEADS * V_DIM
MLSTM_CHUNK = 64
D_FF = 4 * D_MODEL
EPS = 1e-6
FORCE = 1e4
NEG = -1e30

SPLITS = (A_WIDTH, N_KV_BRANCH * A_KV_HEADS * HEAD_DIM, 3 * A_HEADS,
          B_HEADS * QK_DIM, B_HEADS * QK_DIM, B_WIDTH, B_HEADS, B_HEADS, B_WIDTH, 2 * D_MODEL)
BOUNDS = tuple(int(b) for b in np.cumsum((0,) + SPLITS))

VMEM_LIMIT_BYTES = 56 * 1024 * 1024


def _params(*sem):
    return pltpu.CompilerParams(dimension_semantics=sem, vmem_limit_bytes=VMEM_LIMIT_BYTES)


def _row_tile(m, want):
    for t in range(min(want, m), 15, -1):
        if m % t == 0 and t % 16 == 0:
            return t
    return m


def _rmsnorm_kernel(x_ref, g_ref, o_ref):
    x = x_ref[...]
    r = lax.rsqrt(jnp.mean(x * x, axis=-1, keepdims=True) + EPS)
    o_ref[...] = (x * r * g_ref[...]).astype(o_ref.dtype)


def rmsnorm_cast(x, g, *, tm=640):
    M, K = x.shape
    tm = _row_tile(M, tm)
    return pl.pallas_call(
        _rmsnorm_kernel,
        grid=(M // tm,),
        in_specs=[pl.BlockSpec((tm, K), lambda i: (i, 0)), pl.BlockSpec((1, K), lambda i: (0, 0))],
        out_specs=pl.BlockSpec((tm, K), lambda i: (i, 0)),
        out_shape=jax.ShapeDtypeStruct((M, K), jnp.bfloat16),
        compiler_params=_params("parallel"),
        name="rmsnorm_cast",
    )(x, g.reshape(1, K))


def _matmul_act_kernel(h_ref, w_ref, b_ref, o_ref, *, nt, act):
    w = w_ref[...].astype(jnp.bfloat16)
    z = (_nt(h_ref[...], w) if nt else jnp.dot(h_ref[...], w, preferred_element_type=jnp.float32)) + b_ref[...]
    if act == "relu2":
        z = jnp.square(jnp.maximum(z, 0.0))
    o_ref[...] = z.astype(o_ref.dtype)


def matmul_act(h, w, b, *, nt, act=None, out_dtype=jnp.float32, tm=1664, tn=512):
    M, K = h.shape
    N = w.shape[0] if nt else w.shape[1]
    tm = _row_tile(M, tm)
    tn = min(tn, N)
    assert N % tn == 0, (N, tn)
    w_spec = pl.BlockSpec((tn, K), lambda i, j: (j, 0)) if nt else pl.BlockSpec((K, tn), lambda i, j: (0, j))
    return pl.pallas_call(
        functools.partial(_matmul_act_kernel, nt=nt, act=act),
        grid=(M // tm, N // tn),
        in_specs=[pl.BlockSpec((tm, K), lambda i, j: (i, 0)), w_spec, pl.BlockSpec((1, tn), lambda i, j: (0, j))],
        out_specs=pl.BlockSpec((tm, tn), lambda i, j: (i, j)),
        out_shape=jax.ShapeDtypeStruct((M, N), out_dtype),
        compiler_params=_params("parallel", "arbitrary"),
        name="matmul_act",
    )(h, w, b.reshape(1, N))


def _mix_kernel(ya_ref, yb_ref, wa_ref, wb_ref, ga_ref, gb_ref, o_ref):
    bf = jnp.bfloat16
    a = jnp.dot(ya_ref[...].astype(bf), wa_ref[...].astype(bf), preferred_element_type=jnp.float32)
    b = jnp.dot(yb_ref[...].astype(bf), wb_ref[...].astype(bf), preferred_element_type=jnp.float32)
    o_ref[...] = (jax.nn.sigmoid(ga_ref[...]) * a + jax.nn.sigmoid(gb_ref[...]) * b).astype(o_ref.dtype)


def _mix_into_kernel(ya_ref, yb_ref, wa_ref, wb_ref, ga_ref, gb_ref, prev_ref, o_ref):
    del prev_ref
    _mix_kernel(ya_ref, yb_ref, wa_ref, wb_ref, ga_ref, gb_ref, o_ref)


def mix_matmul(ya, yb, wa, wb, gm_src, *, row0, rows_total, into=None, tm=1024, tn=512):
    M, Ka = ya.shape
    Kb = yb.shape[1]
    N = wa.shape[1]
    gm, off = gm_src
    tm = _row_tile(M, tm)
    nb = N // tn
    assert off % tn == 0 and N % tn == 0 and row0 % tm == 0, (off, N, tn, row0, tm)
    ja, jb, i0 = off // tn, off // tn + nb, row0 // tm
    in_specs = [
        pl.BlockSpec((tm, Ka), lambda i, j: (i, 0)),
        pl.BlockSpec((tm, Kb), lambda i, j: (i, 0)),
        pl.BlockSpec((Ka, tn), lambda i, j: (0, j)),
        pl.BlockSpec((Kb, tn), lambda i, j: (0, j)),
        pl.BlockSpec((tm, tn), lambda i, j: (i0 + i, ja + j)),
        pl.BlockSpec((tm, tn), lambda i, j: (i0 + i, jb + j)),
    ]
    args = (ya, yb, wa, wb, gm, gm)
    aliases = {}
    if into is not None:
        in_specs.append(pl.BlockSpec(memory_space=pl.ANY))
        args += (into,)
        aliases = {len(args) - 1: 0}
    return pl.pallas_call(
        _mix_kernel if into is None else _mix_into_kernel,
        grid=(M // tm, nb),
        in_specs=in_specs,
        out_specs=pl.BlockSpec((tm, tn), lambda i, j: (i0 + i, j)),
        out_shape=jax.ShapeDtypeStruct((rows_total, N), jnp.bfloat16),
        input_output_aliases=aliases,
        compiler_params=_params("parallel", "arbitrary"),
        name="mix_matmul",
    )(*args)


def _matmul_norm_res_kernel(a_ref, w_ref, x_ref, g_ref, o_ref, acc_ref):
    k = pl.program_id(1)

    @pl.when(k == 0)
    def _():
        acc_ref[...] = jnp.zeros_like(acc_ref)

    a = a_ref[...]
    for n0 in range(0, acc_ref.shape[1], NORM_RES_COLS):
        cols = slice(n0, n0 + NORM_RES_COLS)
        acc_ref[:, cols] += jnp.dot(a, w_ref[:, cols].astype(jnp.bfloat16), preferred_element_type=jnp.float32)

    @pl.when(k == pl.num_programs(1) - 1)
    def _():
        y = acc_ref[...]
        r = lax.rsqrt(jnp.mean(y * y, axis=-1, keepdims=True) + EPS)
        o_ref[...] = x_ref[...] + y * r * g_ref[...]


NORM_RES_COLS = 512


def matmul_norm_res(a, w, x, g, *, tm=640, tk=1024):
    M, K = a.shape
    N = w.shape[1]
    tm = _row_tile(M, tm)
    tk = min(tk, K)
    return pl.pallas_call(
        _matmul_norm_res_kernel,
        grid=(M // tm, K // tk),
        in_specs=[
            pl.BlockSpec((tm, tk), lambda i, k: (i, k)),
            pl.BlockSpec((tk, N), lambda i, k: (k, 0)),
            pl.BlockSpec((tm, N), lambda i, k: (i, 0)),
            pl.BlockSpec((1, N), lambda i, k: (0, 0)),
        ],
        out_specs=pl.BlockSpec((tm, N), lambda i, k: (i, 0)),
        out_shape=jax.ShapeDtypeStruct((M, N), jnp.float32),
        scratch_shapes=[pltpu.VMEM((tm, N), jnp.float32)],
        compiler_params=_params("parallel", "arbitrary"),
        name="matmul_norm_res",
    )(a, w, x, g.reshape(1, N))


NSA_LANES = A_GROUP * Q_BLOCK
KEY_CHUNK = 512
WIN_CHUNK = 128
WIN_SPAN = WINDOW + Q_BLOCK
SLC_SLOTS = 128
POS_PIECES = 6
AUG = 128
MASK_BIG = 30000.0


def _pos_pieces(pos):
    pos = np.asarray(pos, np.int64)
    hi = (pos // 64) * 64
    lo = pos % 64
    return np.stack([hi, lo] * 3, axis=-1).astype(np.float32)


def _slope_rows():
    slopes = 2.0 ** (-8.0 * np.arange(1, A_HEADS + 1, dtype=np.float64) / A_HEADS)
    s = jnp.asarray(slopes.astype(np.float32))
    hi = s.astype(jnp.bfloat16)
    r1 = s - hi.astype(jnp.float32)
    mid = r1.astype(jnp.bfloat16)
    lo = (r1 - mid.astype(jnp.float32)).astype(jnp.bfloat16)
    rows = jnp.stack([hi, hi, mid, mid, lo, lo], axis=0)
    rows = rows.reshape(POS_PIECES, A_KV_HEADS, A_GROUP).transpose(1, 0, 2)
    rows = jnp.repeat(rows, Q_BLOCK, axis=-1)
    return jnp.pad(rows, ((0, 0), (0, 16 - POS_PIECES), (0, 0)))


def _split3(x):
    hi = x.astype(jnp.bfloat16)
    r1 = x - hi.astype(jnp.float32)
    mid = r1.astype(jnp.bfloat16)
    lo = (r1 - mid.astype(jnp.float32)).astype(jnp.bfloat16)
    return hi, mid, lo


def _tile4(x):
    return jnp.concatenate([x] * A_GROUP, axis=1)


def _softmax_step(carry, s, vT):
    m, l, acc = carry
    m_new = jnp.maximum(m, jnp.max(s, axis=0, keepdims=True))
    alpha = jnp.exp(m - m_new)
    p = jnp.exp(s - m_new)
    l = alpha * l + jnp.sum(p, axis=0, keepdims=True)
    acc = alpha * acc + jnp.dot(vT, p.astype(jnp.bfloat16), preferred_element_type=jnp.float32)
    return m_new, l, acc


def _nsa_prompt_kernel(qT_ref, slope_ref, gT_ref, ck_ref, cvT_ref, ovT_ref, ks_ref, vsT_ref,
                       kw_ref, vwT_ref, o_ref, qa_ref, sel_ref):
    f32, bf = jnp.float32, jnp.bfloat16
    i = pl.program_id(1)
    s0 = i * Q_BLOCK
    ncp = ck_ref.shape[1]

    qT = qT_ref[...] * (HEAD_DIM ** -0.5)
    qcat = jnp.concatenate([qT[g * HEAD_DIM:(g + 1) * HEAD_DIM, :] for g in range(A_GROUP)], axis=1)
    qa_ref[0:HEAD_DIM, :] = qcat.astype(bf)
    qa_ref[HEAD_DIM:HEAD_DIM + 16, :] = slope_ref[0]
    qa_ref[HEAD_DIM + 16:AUG, :] = jnp.zeros((AUG - HEAD_DIM - 16, NSA_LANES), bf)
    qc = qa_ref[0:AUG, :]


    sc = jnp.dot(ck_ref[0], qc, preferred_element_type=f32)
    n_idx = lax.broadcasted_iota(jnp.int32, (ncp, Q_BLOCK), 0)
    t_idx = s0 + lax.broadcasted_iota(jnp.int32, (ncp, Q_BLOCK), 1)
    valid = _tile4(n_idx * CMP_STRIDE + (CMP_BLOCK - 1) <= t_idx)
    sc = jnp.where(valid, sc, NEG)
    ec = jnp.where(valid, jnp.exp(sc - jnp.max(sc, axis=0, keepdims=True)), 0.0)
    lc = jnp.sum(ec, axis=0, keepdims=True)
    pc = ec * jnp.where(lc > 0.0, 1.0 / lc, 0.0)
    o_c = jnp.dot(cvT_ref[0], pc.astype(bf), preferred_element_type=f32)

    pg = pc[:, 0:Q_BLOCK]
    for g in range(1, A_GROUP):
        pg = pg + pc[:, g * Q_BLOCK:(g + 1) * Q_BLOCK]
    imp = sum(jnp.dot(ovT_ref[...], piece, preferred_element_type=f32) for piece in _split3(pg))

    blk = lax.broadcasted_iota(jnp.int32, (SLC_SLOTS, Q_BLOCK), 0)
    cur = (s0 + lax.broadcasted_iota(jnp.int32, (SLC_SLOTS, Q_BLOCK), 1)) // SLC_BLOCK
    forced = (blk == 0) | (blk == cur) | (blk == cur - 1)
    causal = blk <= cur
    score = jnp.where(forced, FORCE, jnp.where(causal, imp, -FORCE))
    blk_f = blk.astype(f32)
    sel = jnp.zeros((SLC_SLOTS, Q_BLOCK), f32)
    for _ in range(SLC_TOP):
        top = jnp.max(score, axis=0, keepdims=True)
        first = jnp.min(jnp.where(score == top, blk_f, float(SLC_SLOTS)), axis=0, keepdims=True)
        pick = blk_f == first
        sel = jnp.where(pick, 1.0, sel)
        score = jnp.where(pick, -jnp.inf, score)
    selneg = jnp.where((sel > 0.0) & causal, 0.0, -MASK_BIG)
    qa_ref[AUG:AUG + SLC_SLOTS, :] = _tile4(selneg).astype(bf)

    def sel_scores(c):
        kc = ks_ref[0, pl.ds(pl.multiple_of(c * KEY_CHUNK, KEY_CHUNK), KEY_CHUNK), :]
        return jnp.dot(kc, qa_ref[...], preferred_element_type=f32)

    init = (jnp.full((1, NSA_LANES), NEG, f32), jnp.zeros((1, NSA_LANES), f32),
            jnp.zeros((HEAD_DIM, NSA_LANES), f32))
    c_last = (i * Q_BLOCK) // KEY_CHUNK
    bpc = KEY_CHUNK // SLC_BLOCK
    sel_ref[...] = jnp.where((sel > 0.0) & causal, 1.0, 0.0)

    def sel_chunk(c, cr):
        picked = jnp.max(sel_ref[pl.ds(pl.multiple_of(c * bpc, bpc), bpc), :]) > 0.0
        return lax.cond(picked, lambda x: _softmax_step(x, sel_scores(c), vsT_ref[0, c]), lambda x: x, cr)

    carry = lax.fori_loop(0, c_last, sel_chunk, init)
    j_last = c_last * KEY_CHUNK + lax.broadcasted_iota(jnp.int32, (KEY_CHUNK, Q_BLOCK), 0)
    t_last = s0 + lax.broadcasted_iota(jnp.int32, (KEY_CHUNK, Q_BLOCK), 1)
    m_s, l_s, acc_s = _softmax_step(carry, jnp.where(_tile4(j_last <= t_last), sel_scores(c_last), NEG),
                                    vsT_ref[0, c_last])
    o_s = acc_s * (1.0 / l_s)

    w0 = jnp.maximum(i - WINDOW // WIN_CHUNK, 0)
    kw = kw_ref[0, pl.ds(pl.multiple_of(w0 * WIN_CHUNK, WIN_CHUNK), WIN_SPAN), :]
    sw = jnp.dot(kw, qc, preferred_element_type=f32)
    jw = w0 * WIN_CHUNK + lax.broadcasted_iota(jnp.int32, (WIN_SPAN, Q_BLOCK), 0)
    tw = s0 + lax.broadcasted_iota(jnp.int32, (WIN_SPAN, Q_BLOCK), 1)
    sw = jnp.where(_tile4((jw <= tw) & (jw > tw - WINDOW)), sw, NEG)
    pw = jnp.exp(sw - jnp.max(sw, axis=0, keepdims=True))
    l_w = jnp.sum(pw, axis=0, keepdims=True)
    pw = pw.astype(bf)
    acc_w = jnp.zeros((HEAD_DIM, NSA_LANES), f32)
    for d in range(WIN_SPAN // WIN_CHUNK):
        acc_w = acc_w + jnp.dot(vwT_ref[0, w0 + d], pw[d * WIN_CHUNK:(d + 1) * WIN_CHUNK, :], preferred_element_type=f32)
    o_w = acc_w * (1.0 / l_w)

    gate = jax.nn.sigmoid(gT_ref[0])
    for g in range(A_GROUP):
        cols = slice(g * Q_BLOCK, (g + 1) * Q_BLOCK)
        o = (gate[3 * g:3 * g + 1, :] * o_c[:, cols] + gate[3 * g + 1:3 * g + 2, :] * o_s[:, cols]
             + gate[3 * g + 2:3 * g + 3, :] * o_w[:, cols])
        o_ref[g * HEAD_DIM:(g + 1) * HEAD_DIM, :] = o


def _cmp_prompt_kernel(x_ref, pe_ref, w1_ref, b1_ref, w2T_ref, ck_ref, cvT_ref, bias_ref):
    f32, bf = jnp.float32, jnp.bfloat16
    nseg = ck_ref.shape[1]
    row_w = 2 * A_KV_HEADS * HEAD_DIM
    for c in range(2):
        bias = jnp.dot(pe_ref[c], w1_ref[c], preferred_element_type=f32,
                       precision=lax.Precision.HIGHEST) + b1_ref[c]
        bias_ref[c] = bias
        for k in range(A_KV_HEADS):
            col = (c * A_KV_HEADS + k) * HEAD_DIM
            y0 = jnp.zeros((nseg, CMP_HIDDEN), f32)
            y1 = jnp.zeros((nseg, CMP_HIDDEN), f32)
            for j in range(CMP_STRIDE):
                xj = x_ref[:, j * row_w + col:j * row_w + col + HEAD_DIM].astype(bf)
                w0 = w1_ref[c, j * HEAD_DIM:(j + 1) * HEAD_DIM, :].astype(bf)
                w1 = w1_ref[c, (CMP_STRIDE + j) * HEAD_DIM:(CMP_STRIDE + j + 1) * HEAD_DIM, :].astype(bf)
                y0 = y0 + jnp.dot(xj, w0, preferred_element_type=f32)
                y1 = y1 + jnp.dot(xj, w1, preferred_element_type=f32)
            pre = y0 + pltpu.roll(y1, nseg - 1, 0)
            hid = jnp.maximum(pre + bias, 0.0).astype(bf)
            if c == 0:
                ck_ref[k] = lax.dot_general(hid, w2T_ref[c].astype(bf), (((1,), (1,)), ((), ())),
                                            preferred_element_type=f32)
            else:
                cvT_ref[k] = lax.dot_general(w2T_ref[c].astype(bf), hid, (((1,), (1,)), ((), ())),
                                             preferred_element_type=f32)


def compress_prompt(x_cmp, pe, w1, b1, w2):
    T = x_cmp.shape[0]
    nseg = T // CMP_STRIDE
    return pl.pallas_call(
        _cmp_prompt_kernel,
        out_shape=(jax.ShapeDtypeStruct((A_KV_HEADS, nseg, HEAD_DIM), jnp.float32),
                   jax.ShapeDtypeStruct((A_KV_HEADS, HEAD_DIM, nseg), jnp.float32),
                   jax.ShapeDtypeStruct((2, 1, CMP_HIDDEN), jnp.float32)),
        compiler_params=pltpu.CompilerParams(vmem_limit_bytes=VMEM_LIMIT_BYTES),
        name="compress_prompt",
    )(x_cmp.reshape(nseg, CMP_STRIDE * x_cmp.shape[1]), pe.reshape(2, 1, CMP_BLOCK * HEAD_DIM), w1, b1.reshape(2, 1, CMP_HIDDEN), w2.transpose(0, 2, 1))


def nsa_prompt_pallas(qa, ga, kva, pe, w1, b1, w2):
    bf = jnp.bfloat16
    T = qa.shape[0]
    nqb = T // Q_BLOCK
    assert T % KEY_CHUNK == 0 and T >= WIN_SPAN, T
    ncp = T // CMP_STRIDE
    kvw = A_KV_HEADS * HEAD_DIM
    ck, cvT, pe_bias = compress_prompt(kva[:, 0:2 * kvw], pe, w1, b1, w2)

    def heads(x):
        return x.reshape(T, A_KV_HEADS, HEAD_DIM).transpose(1, 0, 2)

    def chunksT(x, chunk):
        return x.reshape(T // chunk, chunk, A_KV_HEADS, HEAD_DIM).transpose(2, 0, 3, 1).astype(bf)

    pos = np.arange(T)
    key_pos = jnp.broadcast_to(jnp.asarray(_pos_pieces(pos)), (A_KV_HEADS, T, POS_PIECES))
    zpad = jnp.zeros((A_KV_HEADS, T, AUG - HEAD_DIM - POS_PIECES), jnp.float32)
    onehot = jnp.broadcast_to(jnp.asarray((pos[:, None] // SLC_BLOCK == np.arange(SLC_SLOTS)[None, :]).astype(np.float32)),
                              (A_KV_HEADS, T, SLC_SLOTS))
    ks_aug = jnp.concatenate([heads(kva[:, 2 * kvw:3 * kvw]), key_pos, zpad, onehot], axis=-1).astype(bf)
    kw_aug = jnp.concatenate([heads(kva[:, 4 * kvw:5 * kvw]), key_pos, zpad], axis=-1).astype(bf)
    c_end = np.arange(ncp) * CMP_STRIDE + (CMP_BLOCK - 1)
    ck_aug = jnp.concatenate([ck, jnp.broadcast_to(jnp.asarray(_pos_pieces(c_end)), (A_KV_HEADS, ncp, POS_PIECES)),
                              jnp.zeros((A_KV_HEADS, ncp, AUG - HEAD_DIM - POS_PIECES), jnp.float32)], axis=-1).astype(bf)
    vsT = chunksT(kva[:, 3 * kvw:4 * kvw], KEY_CHUNK)
    vwT = chunksT(kva[:, 5 * kvw:6 * kvw], WIN_CHUNK)
    c_start = c_end - (CMP_BLOCK - 1)
    s_start = np.arange(SLC_SLOTS) * SLC_BLOCK
    ovT = ((c_start[None, :] < s_start[:, None] + SLC_BLOCK) & (c_end[None, :] >= s_start[:, None])
           & (np.arange(ncp)[None, :] < ncp - 1))
    ovT = jnp.asarray(ovT.astype(np.float32)).astype(bf)
    gT = jnp.pad(ga.T.reshape(A_KV_HEADS, 3 * A_GROUP, T), ((0, 0), (0, 16 - 3 * A_GROUP), (0, 0)))

    yT = pl.pallas_call(
        _nsa_prompt_kernel,
        grid=(A_KV_HEADS, nqb),
        in_specs=[
            pl.BlockSpec((A_GROUP * HEAD_DIM, Q_BLOCK), lambda k, i: (k, i)),
            pl.BlockSpec((1, 16, NSA_LANES), lambda k, i: (k, 0, 0)),
            pl.BlockSpec((1, 16, Q_BLOCK), lambda k, i: (k, 0, i)),
            pl.BlockSpec((1, ncp, AUG), lambda k, i: (k, 0, 0)),
            pl.BlockSpec((1, HEAD_DIM, ncp), lambda k, i: (k, 0, 0)),
            pl.BlockSpec((SLC_SLOTS, ncp), lambda k, i: (0, 0)),
            pl.BlockSpec((1, T, AUG + SLC_SLOTS), lambda k, i: (k, 0, 0)),
            pl.BlockSpec((1, T // KEY_CHUNK, HEAD_DIM, KEY_CHUNK), lambda k, i: (k, 0, 0, 0)),
            pl.BlockSpec((1, T, AUG), lambda k, i: (k, 0, 0)),
            pl.BlockSpec((1, T // WIN_CHUNK, HEAD_DIM, WIN_CHUNK), lambda k, i: (k, 0, 0, 0)),
        ],
        out_specs=pl.BlockSpec((A_GROUP * HEAD_DIM, Q_BLOCK), lambda k, i: (k, i)),
        out_shape=jax.ShapeDtypeStruct((A_WIDTH, T), jnp.float32),
        scratch_shapes=[pltpu.VMEM((AUG + SLC_SLOTS, NSA_LANES), bf), pltpu.VMEM((SLC_SLOTS, Q_BLOCK), jnp.float32)],
        compiler_params=_params("parallel", "arbitrary"),
        name="nsa_prompt",
    )(qa.T, _slope_rows(), gT, ck_aug, cvT.astype(bf), ovT, ks_aug, vsT, kw_aug, vwT)
    return yT.T, pe_bias


PAGE_SIZE = 128
N_PAGES = PAST_LEN // PAGE_SIZE
PAGE_SEGS = PAGE_SIZE // CMP_STRIDE
PAST_SEGS = PAST_LEN // CMP_STRIDE
ROW_W = 4 * A_KV_HEADS * HEAD_DIM
KVW = A_KV_HEADS * HEAD_DIM
NS_SAMPLE = PAST_LEN // SLC_BLOCK + 1


def _nt(a, b):
    return lax.dot_general(a, b, (((1,), (1,)), ((), ())), preferred_element_type=jnp.float32)


def _dot3(x, w):
    return sum(jnp.dot(p, w, preferred_element_type=jnp.float32) for p in _split3(x))


def _nsa_sample_kernel(pt_ref, q_ref, g_ref, new_ref, *refs):
    del pt_ref
    pages = refs[:N_PAGES]
    (win_ref, w1_ref, bias_ref, w2bd_ref, bmask_ref, basec_ref, bases_ref, basew_ref, slope_ref,
     grp_ref, ov_ref, exp_ref, o_ref, wout_ref, xc_ref) = refs[N_PAGES:]
    f32, bf = jnp.float32, jnp.bfloat16

    q = q_ref[0] * (HEAD_DIM ** -0.5)
    qbd = jnp.where(bmask_ref[...] > 0.0, jnp.concatenate([q] * A_KV_HEADS, axis=1), 0.0).astype(bf)
    qbd_f = qbd.astype(f32)
    new = new_ref[0]

    def slab(off, width):
        return jnp.concatenate([pg[0, :, off:off + width] for pg in pages], axis=0)

    ckv = []
    for c in range(2):
        for j in range(CMP_STRIDE):
            s = slab(j * ROW_W + c * KVW, KVW).astype(bf)
            for k in range(A_KV_HEADS):
                xc_ref[k * PAST_SEGS:(k + 1) * PAST_SEGS, j * HEAD_DIM:(j + 1) * HEAD_DIM] = (
                    s[:, k * HEAD_DIM:(k + 1) * HEAD_DIM])
        y = jnp.dot(xc_ref[...], w1_ref[c], preferred_element_type=f32)
        pre = y[:, 0:CMP_HIDDEN] + pltpu.roll(y[:, CMP_HIDDEN:2 * CMP_HIDDEN], A_KV_HEADS * PAST_SEGS - 1, 0)
        hid = jnp.maximum(pre + bias_ref[c], 0.0).astype(bf)
        hid = jnp.concatenate([hid[k * PAST_SEGS:(k + 1) * PAST_SEGS] for k in range(A_KV_HEADS)], axis=1)
        ckv.append(jnp.dot(hid, w2bd_ref[c], preferred_element_type=f32).astype(bf))
    ck, cv = ckv

    sc = _nt(qbd, ck) + basec_ref[...]
    ec = jnp.exp(sc - jnp.max(sc, axis=1, keepdims=True))
    pc = ec / jnp.sum(ec, axis=1, keepdims=True)
    o_c = jnp.dot(pc.astype(bf), cv, preferred_element_type=f32)

    pov = _dot3(pc, ov_ref[...])
    p_hi, p_mid, p_lo = _split3(pov)
    imp = sum(jnp.dot(grp_ref[...], p, preferred_element_type=f32) for p in (p_hi, p_mid, p_lo))

    lane = lax.broadcasted_iota(jnp.int32, (A_HEADS, PAST_SEGS), 1)
    score = jnp.where((lane == 0) | (lane == NS_SAMPLE - 1) | (lane == NS_SAMPLE - 2), FORCE,
                      jnp.where(lane < NS_SAMPLE, imp, -jnp.inf))
    rr = lax.broadcasted_iota(jnp.int32, (PAST_SEGS, PAST_SEGS), 0)
    cc = lax.broadcasted_iota(jnp.int32, (PAST_SEGS, PAST_SEGS), 1)
    row_id = lax.broadcasted_iota(jnp.int32, (A_HEADS, PAST_SEGS), 0)
    selrows = jnp.zeros((A_HEADS, PAST_SEGS), f32)
    for k in range(A_KV_HEADS):
        row = jnp.broadcast_to(score[A_GROUP * k:A_GROUP * k + 1, :], (PAST_SEGS, PAST_SEGS))
        col = jnp.max(jnp.where(rr == cc, row, -jnp.inf), axis=1, keepdims=True)
        ahead = (col > row) | ((col == row) & (rr < cc))
        rank = jnp.sum(jnp.where(ahead, 1.0, 0.0), axis=0, keepdims=True)
        sel_k = jnp.where(rank < float(SLC_TOP), 1.0, 0.0)
        selrows = jnp.where(row_id // A_GROUP == k, jnp.broadcast_to(sel_k, (A_HEADS, PAST_SEGS)), selrows)
    selseg = jnp.dot(selrows.astype(bf), exp_ref[...], preferred_element_type=f32)
    maskbias = jnp.where(selseg > 0.5, 0.0, NEG)

    slope = slope_ref[...]
    scores = []
    for j in range(CMP_STRIDE):
        kj = slab(j * ROW_W + 2 * KVW, KVW).astype(bf)
        scores.append(_nt(qbd, kj) + (bases_ref[...] + slope * float(j)) + maskbias)
    s_new = jnp.sum(qbd_f * new[:, 2 * KVW:3 * KVW], axis=1, keepdims=True)
    mx = scores[0]
    for s in scores[1:]:
        mx = jnp.maximum(mx, s)
    m = jnp.maximum(jnp.max(mx, axis=1, keepdims=True), s_new)
    p_new = jnp.exp(s_new - m)
    acc = p_new * new[:, 3 * KVW:4 * KVW]
    psum = jnp.zeros((A_HEADS, PAST_SEGS), f32)
    for j in range(CMP_STRIDE):
        p = jnp.exp(scores[j] - m)
        psum = psum + p
        vj = slab(j * ROW_W + 3 * KVW, KVW).astype(bf)
        acc = acc + jnp.dot(p.astype(bf), vj, preferred_element_type=f32)
    o_s = acc / (jnp.sum(psum, axis=1, keepdims=True) + p_new)

    kw = win_ref[0, :, 0:KVW].astype(bf)
    vw = win_ref[0, :, KVW:2 * KVW].astype(bf)
    sw = _nt(qbd, kw) + basew_ref[...]
    s_neww = jnp.sum(qbd_f * new[:, 4 * KVW:5 * KVW], axis=1, keepdims=True)
    m = jnp.maximum(jnp.max(sw, axis=1, keepdims=True), s_neww)
    pw = jnp.exp(sw - m)
    p_new = jnp.exp(s_neww - m)
    o_w = (jnp.dot(pw.astype(bf), vw, preferred_element_type=f32) + p_new * new[:, 5 * KVW:6 * KVW]) / (
        jnp.sum(pw, axis=1, keepdims=True) + p_new)

    gate = jax.nn.sigmoid(g_ref[0])
    o = gate[:, 0:1] * o_c + gate[:, 1:2] * o_s + gate[:, 2:3] * o_w
    for k in range(A_KV_HEADS):
        o_ref[0, A_GROUP * k:A_GROUP * (k + 1), :] = o[A_GROUP * k:A_GROUP * (k + 1), k * HEAD_DIM:(k + 1) * HEAD_DIM]

    wout_ref[0, 0:WINDOW - 1, :] = win_ref[0, 1:WINDOW, :]
    wout_ref[0, WINDOW - 1:WINDOW, :] = new[:, 4 * KVW:6 * KVW]


def _sample_constants():
    slopes = (2.0 ** (-8.0 * np.arange(1, A_HEADS + 1, dtype=np.float64) / A_HEADS)).astype(np.float32)[:, None]
    heads = np.arange(A_HEADS)[:, None]
    bmask = (np.arange(KVW)[None, :] // HEAD_DIM == heads // A_GROUP).astype(np.float32)
    n = np.arange(PAST_SEGS)[None, :]
    c_end = n * CMP_STRIDE + (CMP_BLOCK - 1)
    basec = np.where(n < PAST_SEGS - 1, slopes * (c_end - PAST_LEN), NEG).astype(np.float32)
    bases = (slopes * (n * CMP_STRIDE - PAST_LEN)).astype(np.float32)
    r = np.arange(WINDOW)[None, :]
    basew = np.where(r >= 1, slopes * (r - WINDOW), NEG).astype(np.float32)
    slope = np.broadcast_to(slopes, (A_HEADS, PAST_SEGS)).astype(np.float32)
    grp = (heads // A_GROUP == heads.T // A_GROUP).astype(np.float32)
    c_start = np.arange(PAST_SEGS) * CMP_STRIDE
    s_start = np.arange(PAST_SEGS) * SLC_BLOCK
    ov = ((c_start[:, None] < s_start[None, :] + SLC_BLOCK) & (c_start[:, None] + CMP_BLOCK - 1 >= s_start[None, :])
          & (np.arange(PAST_SEGS)[:, None] < PAST_SEGS - 1) & (np.arange(PAST_SEGS)[None, :] < NS_SAMPLE))
    expand = (np.arange(PAST_SEGS)[None, :] * CMP_STRIDE // SLC_BLOCK == np.arange(PAST_SEGS)[:, None])
    as_bf = lambda a: jnp.asarray(a.astype(np.float32)).astype(jnp.bfloat16)
    return (jnp.asarray(bmask), jnp.asarray(basec), jnp.asarray(bases), jnp.asarray(basew), jnp.asarray(slope),
            as_bf(grp), as_bf(ov), as_bf(expand))


def nsa_sample_pallas(qa, ga, kva, pool, page_table, win_buf, layer, w1, w2, pe_bias):
    bf = jnp.bfloat16
    B = qa.shape[0]
    npool = pool.shape[1]
    page_table = page_table + layer * npool
    seg_w = CMP_STRIDE * ROW_W
    w1r = w1.reshape(2, CMP_RATIO, CMP_STRIDE * HEAD_DIM, CMP_HIDDEN).transpose(0, 2, 1, 3)
    w1r = w1r.reshape(2, CMP_STRIDE * HEAD_DIM, CMP_RATIO * CMP_HIDDEN).astype(bf)
    w2bd = jnp.einsum('kl,ced->ckeld', jnp.eye(A_KV_HEADS, dtype=w2.dtype), w2)
    w2bd = w2bd.reshape(2, A_KV_HEADS * CMP_HIDDEN, KVW).astype(bf)
    consts = _sample_constants()
    full = lambda shape: pl.BlockSpec(shape, lambda b, pt: (0,) * len(shape))
    page_specs = [pl.BlockSpec((1, PAGE_SEGS, seg_w), functools.partial(lambda p, b, pt: (pt[b, p], 0, 0), p))
                  for p in range(N_PAGES)]
    grid_spec = pltpu.PrefetchScalarGridSpec(
        num_scalar_prefetch=1,
        grid=(B,),
        in_specs=[
            pl.BlockSpec((1, A_HEADS, HEAD_DIM), lambda b, pt: (b, 0, 0)),
            pl.BlockSpec((1, A_HEADS, 3), lambda b, pt: (b, 0, 0)),
            pl.BlockSpec((1, 1, N_KV_BRANCH * KVW), lambda b, pt: (b, 0, 0)),
            *page_specs,
            pl.BlockSpec((1, WINDOW, 2 * KVW), lambda b, pt: (layer * B + b, 0, 0)),
            full(w1r.shape), full((2, 1, CMP_HIDDEN)), full(w2bd.shape),
            *[full(c.shape) for c in consts],
        ],
        out_specs=[
            pl.BlockSpec((1, A_HEADS, HEAD_DIM), lambda b, pt: (b, 0, 0)),
            pl.BlockSpec((1, WINDOW, 2 * KVW), lambda b, pt: (b, 0, 0)),
        ],
        scratch_shapes=[pltpu.VMEM((A_KV_HEADS * PAST_SEGS, CMP_STRIDE * HEAD_DIM), bf)],
    )
    pool3 = pool.reshape(pool.shape[0] * npool, PAGE_SEGS, seg_w)
    o, wout = pl.pallas_call(
        _nsa_sample_kernel,
        grid_spec=grid_spec,
        out_shape=(jax.ShapeDtypeStruct((B, A_HEADS, HEAD_DIM), jnp.float32),
                   jax.ShapeDtypeStruct((B, WINDOW, 2 * KVW), jnp.float32)),
        compiler_params=_params("arbitrary"),
        name="nsa_sample",
    )(page_table, qa.reshape(B, A_HEADS, HEAD_DIM), ga.reshape(B, A_HEADS, 3), kva.reshape(B, 1, N_KV_BRANCH * KVW),
      *([pool3] * N_PAGES), win_buf.reshape(win_buf.shape[0] * B, WINDOW, 2 * KVW), w1r, pe_bias.reshape(2, 1, CMP_HIDDEN), w2bd, *consts)
    return o.reshape(B, A_WIDTH), wout


SEG_PITCH = 24


def _nsa_decode_kernel(pt_ref, q_ref, g_ref, new_ref, *refs, has_into):
    del pt_ref
    pages = refs[:N_PAGES]
    rest = refs[N_PAGES:]
    (win_ref, w1_ref, bias_ref, w2bd_ref, bmask_ref, basec_ref, bases_ref, basew_ref, slope_ref,
     grp_ref, ov_ref, exp_ref) = rest[:12]
    o_ref, wout_ref, tr_ref, xc_ref = rest[12 + int(has_into):]
    f32, bf = jnp.float32, jnp.bfloat16

    q = q_ref[0] * (HEAD_DIM ** -0.5)
    qbd = jnp.where(bmask_ref[...] > 0.0, jnp.concatenate([q] * A_KV_HEADS, axis=1), 0.0).astype(bf)
    qbd_f = qbd.astype(f32)
    new = new_ref[0]

    ckv = []
    for c in range(2):
        for p in range(N_PAGES):
            for kp in range(2):
                r0 = c * KVW + kp * 2 * HEAD_DIM
                t = pages[p][0, r0:r0 + 2 * HEAD_DIM, :].T
                for s in range(PAGE_SEGS):
                    t0 = (kp * PAST_SEGS + p * PAGE_SEGS + s) * SEG_PITCH
                    tr_ref[t0:t0 + CMP_STRIDE, :] = t[s * CMP_STRIDE:(s + 1) * CMP_STRIDE, :]
        for j in range(CMP_STRIDE):
            for kp in range(2):
                piece = tr_ref[pl.ds(kp * PAST_SEGS * SEG_PITCH + j, PAST_SEGS, stride=SEG_PITCH), :]
                for kk in range(2):
                    k = 2 * kp + kk
                    xc_ref[k * PAST_SEGS:(k + 1) * PAST_SEGS, j * HEAD_DIM:(j + 1) * HEAD_DIM] = (
                        piece[:, kk * HEAD_DIM:(kk + 1) * HEAD_DIM])
        y = jnp.dot(xc_ref[...].astype(bf), w1_ref[c], preferred_element_type=f32)
        pre = y[:, 0:CMP_HIDDEN] + pltpu.roll(y[:, CMP_HIDDEN:2 * CMP_HIDDEN], A_KV_HEADS * PAST_SEGS - 1, 0)
        hid = jnp.maximum(pre + bias_ref[c], 0.0).astype(bf)
        hid = jnp.concatenate([hid[k * PAST_SEGS:(k + 1) * PAST_SEGS] for k in range(A_KV_HEADS)], axis=1)
        ckv.append(jnp.dot(hid, w2bd_ref[c], preferred_element_type=f32).astype(bf))
    ck, cv = ckv

    sc = _nt(qbd, ck) + basec_ref[...]
    ec = jnp.exp(sc - jnp.max(sc, axis=1, keepdims=True))
    pc = ec / jnp.sum(ec, axis=1, keepdims=True)
    o_c = jnp.dot(pc.astype(bf), cv, preferred_element_type=f32)

    pov = _dot3(pc, ov_ref[...])
    imp = sum(jnp.dot(grp_ref[...], p, preferred_element_type=f32) for p in _split3(pov))

    lane = lax.broadcasted_iota(jnp.int32, (A_HEADS, PAST_SEGS), 1)
    score = jnp.where((lane == 0) | (lane == NS_SAMPLE - 1) | (lane == NS_SAMPLE - 2), FORCE,
                      jnp.where(lane < NS_SAMPLE, imp, -jnp.inf))
    rr = lax.broadcasted_iota(jnp.int32, (PAST_SEGS, PAST_SEGS), 0)
    cc = lax.broadcasted_iota(jnp.int32, (PAST_SEGS, PAST_SEGS), 1)
    row_id = lax.broadcasted_iota(jnp.int32, (A_HEADS, PAST_SEGS), 0)
    selrows = jnp.zeros((A_HEADS, PAST_SEGS), f32)
    for k in range(A_KV_HEADS):
        row = jnp.broadcast_to(score[A_GROUP * k:A_GROUP * k + 1, :], (PAST_SEGS, PAST_SEGS))
        col = jnp.max(jnp.where(rr == cc, row, -jnp.inf), axis=1, keepdims=True)
        ahead = (col > row) | ((col == row) & (rr < cc))
        rank = jnp.sum(jnp.where(ahead, 1.0, 0.0), axis=0, keepdims=True)
        sel_k = jnp.where(rank < float(SLC_TOP), 1.0, 0.0)
        selrows = jnp.where(row_id // A_GROUP == k, jnp.broadcast_to(sel_k, (A_HEADS, PAST_SEGS)), selrows)
    selpos = jnp.dot(selrows.astype(bf), exp_ref[...], preferred_element_type=f32)

    slope = slope_ref[...]
    scores = []
    for p in range(N_PAGES):
        kp_ = pages[p][0, 2 * KVW:3 * KVW, :].astype(bf)
        mask = jnp.where(selpos[:, p * PAGE_SIZE:(p + 1) * PAGE_SIZE] > 0.5, 0.0, NEG)
        scores.append(jnp.dot(qbd, kp_, preferred_element_type=f32) + (bases_ref[...] + slope * float(p * PAGE_SIZE)) + mask)
    s_new = jnp.sum(qbd_f * new[:, 2 * KVW:3 * KVW], axis=1, keepdims=True)
    mx = scores[0]
    for s in scores[1:]:
        mx = jnp.maximum(mx, s)
    m = jnp.maximum(jnp.max(mx, axis=1, keepdims=True), s_new)
    p_new = jnp.exp(s_new - m)
    acc = p_new * new[:, 3 * KVW:4 * KVW]
    psum = jnp.zeros((A_HEADS, PAGE_SIZE), f32)
    for p in range(N_PAGES):
        pr = jnp.exp(scores[p] - m)
        psum = psum + pr
        acc = acc + _nt(pr.astype(bf), pages[p][0, 3 * KVW:4 * KVW, :].astype(bf))
    o_s = acc / (jnp.sum(psum, axis=1, keepdims=True) + p_new)

    sw = jnp.dot(qbd, win_ref[0, 0:KVW, :].astype(bf), preferred_element_type=f32) + basew_ref[...]
    s_neww = jnp.sum(qbd_f * new[:, 4 * KVW:5 * KVW], axis=1, keepdims=True)
    m = jnp.maximum(jnp.max(sw, axis=1, keepdims=True), s_neww)
    pw = jnp.exp(sw - m)
    p_new = jnp.exp(s_neww - m)
    o_w = (_nt(pw.astype(bf), win_ref[0, KVW:2 * KVW, :].astype(bf)) + p_new * new[:, 5 * KVW:6 * KVW]) / (
        jnp.sum(pw, axis=1, keepdims=True) + p_new)

    gate = jax.nn.sigmoid(g_ref[0])
    o = gate[:, 0:1] * o_c + gate[:, 1:2] * o_s + gate[:, 2:3] * o_w
    for k in range(A_KV_HEADS):
        o_ref[0, A_GROUP * k:A_GROUP * (k + 1), :] = o[A_GROUP * k:A_GROUP * (k + 1), k * HEAD_DIM:(k + 1) * HEAD_DIM]

    wr = lax.broadcasted_iota(jnp.int32, (WINDOW, 2 * KVW), 0)
    wc = lax.broadcasted_iota(jnp.int32, (WINDOW, 2 * KVW), 1)
    new_col = jnp.sum(jnp.where(wr == wc, jnp.broadcast_to(new[:, 4 * KVW:6 * KVW], (WINDOW, 2 * KVW)), 0.0),
                      axis=1, keepdims=True)
    wout_ref[0] = jnp.where(wc == WINDOW - 1, new_col, pltpu.roll(win_ref[0], WINDOW - 1, 1))


def _decode_constants():
    slopes = (2.0 ** (-8.0 * np.arange(1, A_HEADS + 1, dtype=np.float64) / A_HEADS)).astype(np.float32)[:, None]
    heads = np.arange(A_HEADS)[:, None]
    bmask = (np.arange(KVW)[None, :] // HEAD_DIM == heads // A_GROUP).astype(np.float32)
    n = np.arange(PAST_SEGS)[None, :]
    c_end = n * CMP_STRIDE + (CMP_BLOCK - 1)
    basec = np.where(n < PAST_SEGS - 1, slopes * (c_end - PAST_LEN), NEG).astype(np.float32)
    bases = (slopes * (np.arange(PAGE_SIZE)[None, :] - PAST_LEN)).astype(np.float32)
    r = np.arange(WINDOW)[None, :]
    basew = np.where(r >= 1, slopes * (r - WINDOW), NEG).astype(np.float32)
    slope = np.broadcast_to(slopes, (A_HEADS, PAGE_SIZE)).astype(np.float32)
    grp = (heads // A_GROUP == heads.T // A_GROUP).astype(np.float32)
    c_start = np.arange(PAST_SEGS) * CMP_STRIDE
    s_start = np.arange(PAST_SEGS) * SLC_BLOCK
    ov = ((c_start[:, None] < s_start[None, :] + SLC_BLOCK) & (c_start[:, None] + CMP_BLOCK - 1 >= s_start[None, :])
          & (np.arange(PAST_SEGS)[:, None] < PAST_SEGS - 1) & (np.arange(PAST_SEGS)[None, :] < NS_SAMPLE))
    expand = (np.arange(PAST_LEN)[None, :] // SLC_BLOCK == np.arange(PAST_SEGS)[:, None])
    as_bf = lambda a: jnp.asarray(a.astype(np.float32)).astype(jnp.bfloat16)
    return (jnp.asarray(bmask), jnp.asarray(basec), jnp.asarray(bases), jnp.asarray(basew), jnp.asarray(slope),
            as_bf(grp), as_bf(ov), as_bf(expand))


def nsa_decode_pallas(qa, ga, kva, pool, page_table, win_buf, layer, w1, w2, pe_bias, win_into=None):
    bf = jnp.bfloat16
    B = qa.shape[0]
    depth, npool = pool.shape[:2]
    page_table = page_table + layer * npool
    poolT = pool.transpose(0, 1, 3, 4, 5, 2).reshape(depth * npool, ROW_W, PAGE_SIZE)
    winT = win_buf.transpose(0, 1, 3, 4, 5, 2).reshape(depth * B, 2 * KVW, WINDOW)
    w1r = w1.reshape(2, CMP_RATIO, CMP_STRIDE * HEAD_DIM, CMP_HIDDEN).transpose(0, 2, 1, 3)
    w1r = w1r.reshape(2, CMP_STRIDE * HEAD_DIM, CMP_RATIO * CMP_HIDDEN).astype(bf)
    w2bd = jnp.einsum('kl,ced->ckeld', jnp.eye(A_KV_HEADS, dtype=w2.dtype), w2)
    w2bd = w2bd.reshape(2, A_KV_HEADS * CMP_HIDDEN, KVW).astype(bf)
    consts = _decode_constants()
    full = lambda shape: pl.BlockSpec(shape, lambda b, pt: (0,) * len(shape))
    page_specs = [pl.BlockSpec((1, ROW_W, PAGE_SIZE), functools.partial(lambda p, b, pt: (pt[b, p], 0, 0), p))
                  for p in range(N_PAGES)]
    grid_spec = pltpu.PrefetchScalarGridSpec(
        num_scalar_prefetch=1,
        grid=(B,),
        in_specs=[
            pl.BlockSpec((1, A_HEADS, HEAD_DIM), lambda b, pt: (b, 0, 0)),
            pl.BlockSpec((1, A_HEADS, 3), lambda b, pt: (b, 0, 0)),
            pl.BlockSpec((1, 1, N_KV_BRANCH * KVW), lambda b, pt: (b, 0, 0)),
            *page_specs,
            pl.BlockSpec((1, 2 * KVW, WINDOW), lambda b, pt: (layer * B + b, 0, 0)),
            full(w1r.shape), full((2, 1, CMP_HIDDEN)), full(w2bd.shape),
            *[full(c.shape) for c in consts],
            *([] if win_into is None else [pl.BlockSpec(memory_space=pl.ANY)]),
        ],
        out_specs=[
            pl.BlockSpec((1, A_HEADS, HEAD_DIM), lambda b, pt: (b, 0, 0)),
            pl.BlockSpec((1, 2 * KVW, WINDOW), lambda b, pt: (layer * B + b, 0, 0)),
        ],
        scratch_shapes=[pltpu.VMEM((2 * PAST_SEGS * SEG_PITCH, 2 * HEAD_DIM), jnp.float32),
                        pltpu.VMEM((A_KV_HEADS * PAST_SEGS, CMP_STRIDE * HEAD_DIM), jnp.float32)],
    )
    args = (page_table, qa.reshape(B, A_HEADS, HEAD_DIM), ga.reshape(B, A_HEADS, 3), kva.reshape(B, 1, N_KV_BRANCH * KVW),
            *([poolT] * N_PAGES), winT, w1r, pe_bias.reshape(2, 1, CMP_HIDDEN), w2bd, *consts)
    aliases = {}
    if win_into is not None:
        args += (win_into,)
        aliases = {len(args) - 1: 1}
    o, wout = pl.pallas_call(
        functools.partial(_nsa_decode_kernel, has_into=win_into is not None),
        grid_spec=grid_spec,
        out_shape=(jax.ShapeDtypeStruct((B, A_HEADS, HEAD_DIM), jnp.float32),
                   jax.ShapeDtypeStruct((depth * B, 2 * KVW, WINDOW), jnp.float32)),
        input_output_aliases=aliases,
        compiler_params=_params("arbitrary"),
        name="nsa_decode",
    )(*args)
    return o.reshape(B, A_WIDTH), wout


MLSTM_L = 128
VAUG = 2 * V_DIM


def _mlstm_prompt_kernel(q_ref, kT_ref, v_ref, o_ref, fsrc_ref, ifT_ref, y_ref, c_ref, m_ref, *, f_off):
    f32, bf = jnp.float32, jnp.bfloat16
    L = MLSTM_L
    f_ref = fsrc_ref.at[:, f_off:f_off + B_HEADS]
    iT_ref = ifT_ref.at[0:B_HEADS, :]
    fT_ref = ifT_ref.at[B_HEADS:2 * B_HEADS, :]

    @pl.when(pl.program_id(0) == 0)
    def _():
        c_ref[...] = jnp.zeros_like(c_ref)
        m_ref[...] = jnp.zeros_like(m_ref)

    rr = lax.broadcasted_iota(jnp.int32, (L, L), 0)
    cc = lax.broadcasted_iota(jnp.int32, (L, L), 1)
    lower = rr >= cc
    tril = jnp.where(lower, 1.0, 0.0).astype(bf)
    triu = jnp.where(rr <= cc, 1.0, 0.0).astype(bf)
    b_col = sum(jnp.dot(tril, p, preferred_element_type=f32) for p in _split3(jax.nn.log_sigmoid(f_ref[...])))
    b_row = _dot3(jax.nn.log_sigmoid(fT_ref[...]), triu)
    a_row = iT_ref[...] - b_row
    ones_col = jnp.where(lax.broadcasted_iota(jnp.int32, (L, V_DIM), 1) == 0, 1.0, 0.0)

    for h in range(B_HEADS):
        m_prev = m_ref[h, 0:1, 0:1]
        a = a_row[h:h + 1, :]
        amat = jnp.where(lower, jnp.broadcast_to(a, (L, L)), -jnp.inf)
        big_m = jnp.maximum(m_prev, jnp.max(amat, axis=1, keepdims=True))
        dmat = jnp.exp(amat - big_m)
        inter = jnp.exp(m_prev - big_m)
        q = q_ref[:, h * QK_DIM:(h + 1) * QK_DIM].astype(bf)
        kT = kT_ref[h * QK_DIM:(h + 1) * QK_DIM, :] * (QK_DIM ** -0.5)
        vaug = jnp.concatenate([v_ref[:, h * V_DIM:(h + 1) * V_DIM], ones_col], axis=1).astype(bf)
        s = jnp.dot(q, kT.astype(bf), preferred_element_type=f32) * dmat
        r = (jnp.dot(s.astype(bf), vaug, preferred_element_type=f32)
             + inter * jnp.dot(q, c_ref[h].astype(bf), preferred_element_type=f32))
        m_new = b_col[:, h:h + 1] + big_m
        den = jnp.maximum(jnp.abs(r[:, V_DIM:V_DIM + 1]), jnp.exp(-m_new))
        y_ref[:, h * V_DIM:(h + 1) * V_DIM] = (jax.nn.sigmoid(o_ref[:, h * V_DIM:(h + 1) * V_DIM])
                                               * (r[:, 0:V_DIM] / den))
        m_end = big_m[L - 1:L, :]
        w_end = jnp.exp(a - m_end)
        kw_end = kT * w_end
        c_new = inter[L - 1:L, :] * c_ref[h] + jnp.dot(kw_end.astype(bf), vaug, preferred_element_type=f32)
        n_new = inter[L - 1:L, :] * c_ref[h, :, V_DIM:V_DIM + 1] + jnp.sum(kw_end, axis=1, keepdims=True)
        c_ref[h] = jnp.where(lax.broadcasted_iota(jnp.int32, (QK_DIM, VAUG), 1) == V_DIM, n_new, c_new)
        m_ref[h] = jnp.broadcast_to(m_new[L - 1:L, :], m_ref.shape[1:])


def mlstm_prompt_pallas(q_src, kT, v_src, o_src, f_src, ifT, T):
    L = MLSTM_L
    (q_arr, q_cb), (v_arr, v_cb), (o_arr, o_cb), (f_arr, f_cb, f_off) = q_src, v_src, o_src, f_src
    y, caug, m = pl.pallas_call(
        functools.partial(_mlstm_prompt_kernel, f_off=f_off),
        grid=(T // L,),
        in_specs=[
            pl.BlockSpec((L, B_HEADS * QK_DIM), lambda c: (c, q_cb)),
            pl.BlockSpec((B_HEADS * QK_DIM, L), lambda c: (0, c)),
            pl.BlockSpec((L, B_WIDTH), lambda c: (c, v_cb)),
            pl.BlockSpec((L, B_WIDTH), lambda c: (c, o_cb)),
            pl.BlockSpec((L, f_arr.shape[1] if f_arr.shape[1] < 128 else 128), lambda c: (c, f_cb)),
            pl.BlockSpec((2 * B_HEADS, L), lambda c: (0, c)),
        ],
        out_specs=[
            pl.BlockSpec((L, B_WIDTH), lambda c: (c, 0)),
            pl.BlockSpec((B_HEADS, QK_DIM, VAUG), lambda c: (0, 0, 0)),
            pl.BlockSpec((B_HEADS, 8, 128), lambda c: (0, 0, 0)),
        ],
        out_shape=(jax.ShapeDtypeStruct((T, B_WIDTH), jnp.float32),
                   jax.ShapeDtypeStruct((B_HEADS, QK_DIM, VAUG), jnp.float32),
                   jax.ShapeDtypeStruct((B_HEADS, 8, 128), jnp.float32)),
        compiler_params=_params("arbitrary"),
        name="mlstm_prompt",
    )(q_arr, kT, v_arr, o_arr, f_arr, ifT)
    return y, caug[:, :, 0:V_DIM], caug[:, :, V_DIM], m[:, 0, 0]


def _mlstm_sample_kernel(q_ref, k_ref, qT_ref, kT_ref, v_ref, o_ref, i_ref, f_ref, m_ref, n_ref, c_ref,
                         y_ref, cn_ref, nn_ref, mn_ref):
    bb = q_ref.shape[0]
    scale = QK_DIM ** -0.5
    for b in range(bb):
        logf = jax.nn.log_sigmoid(f_ref[b])
        m_new = jnp.maximum(logf + m_ref[b], i_ref[b])
        d_all = jnp.exp(i_ref[b] - m_new)
        inter_all = jnp.exp(logf + m_ref[b] - m_new)
        floor_all = jnp.exp(-m_new)
        mn_ref[b] = m_new
        qk_all = jnp.sum(q_ref[b] * k_ref[b], axis=1, keepdims=True) * scale
        qn_all = jnp.sum(q_ref[b] * n_ref[b], axis=1, keepdims=True)
        for h in range(B_HEADS):
            d = d_all[:, h:h + 1]
            inter = inter_all[:, h:h + 1]
            s = qk_all[h:h + 1, :] * d
            c = c_ref[b, h]
            v = v_ref[b, h:h + 1, :]
            qc = qT_ref[b, :, h:h + 1]
            kc = kT_ref[b, :, h:h + 1] * scale
            num = inter * jnp.sum(qc * c, axis=0, keepdims=True) + s * v
            den = inter * qn_all[h:h + 1, :] + s
            hout = num / jnp.maximum(jnp.abs(den), floor_all[:, h:h + 1])
            y_ref[b, h:h + 1, :] = jax.nn.sigmoid(o_ref[b, h:h + 1, :]) * hout
            cn_ref[b, h] = inter * c + d * (kc * v)
            nn_ref[b, h:h + 1, :] = inter * n_ref[b, h:h + 1, :] + d * (k_ref[b, h:h + 1, :] * scale)


def mlstm_sample_pallas(qb, kb, vb, ib, fb, ob, state_C, state_n, state_m, layer, *, bb=8):
    B = qb.shape[0]
    nb = B // bb
    q3 = qb.reshape(B, B_HEADS, QK_DIM)
    k3 = kb.reshape(B, B_HEADS, QK_DIM)
    row8 = lambda x: x.reshape(-1, 1, B_HEADS)
    lay = layer * nb
    y, cn, nn, mn = pl.pallas_call(
        _mlstm_sample_kernel,
        grid=(nb,),
        in_specs=[
            pl.BlockSpec((bb, B_HEADS, QK_DIM), lambda i: (i, 0, 0)),
            pl.BlockSpec((bb, B_HEADS, QK_DIM), lambda i: (i, 0, 0)),
            pl.BlockSpec((bb, QK_DIM, B_HEADS), lambda i: (i, 0, 0)),
            pl.BlockSpec((bb, QK_DIM, B_HEADS), lambda i: (i, 0, 0)),
            pl.BlockSpec((bb, B_HEADS, V_DIM), lambda i: (i, 0, 0)),
            pl.BlockSpec((bb, B_HEADS, V_DIM), lambda i: (i, 0, 0)),
            pl.BlockSpec((bb, 1, B_HEADS), lambda i: (i, 0, 0)),
            pl.BlockSpec((bb, 1, B_HEADS), lambda i: (i, 0, 0)),
            pl.BlockSpec((bb, 1, B_HEADS), lambda i: (lay + i, 0, 0)),
            pl.BlockSpec((bb, B_HEADS, QK_DIM), lambda i: (lay + i, 0, 0)),
            pl.BlockSpec((bb, B_HEADS, QK_DIM, V_DIM), lambda i: (lay + i, 0, 0, 0)),
        ],
        out_specs=[
            pl.BlockSpec((bb, B_HEADS, V_DIM), lambda i: (i, 0, 0)),
            pl.BlockSpec((bb, B_HEADS, QK_DIM, V_DIM), lambda i: (i, 0, 0, 0)),
            pl.BlockSpec((bb, B_HEADS, QK_DIM), lambda i: (i, 0, 0)),
            pl.BlockSpec((bb, 1, B_HEADS), lambda i: (i, 0, 0)),
        ],
        out_shape=(jax.ShapeDtypeStruct((B, B_HEADS, V_DIM), jnp.float32),
                   jax.ShapeDtypeStruct((B, B_HEADS, QK_DIM, V_DIM), jnp.float32),
                   jax.ShapeDtypeStruct((B, B_HEADS, QK_DIM), jnp.float32),
                   jax.ShapeDtypeStruct((B, 1, B_HEADS), jnp.float32)),
        compiler_params=_params("arbitrary"),
        name="mlstm_sample",
    )(q3, k3, q3.transpose(0, 2, 1), k3.transpose(0, 2, 1), vb.reshape(B, B_HEADS, V_DIM),
      ob.reshape(B, B_HEADS, V_DIM), row8(ib), row8(fb), row8(state_m),
      state_n.reshape(-1, B_HEADS, QK_DIM), state_C.reshape(-1, B_HEADS, QK_DIM, V_DIM))
    return y.reshape(B, B_WIDTH), cn, nn, mn.reshape(B, B_HEADS)


(G_QA, G_KVA, G_GA, G_QB, G_KB, G_VB, G_IB, G_FB, G_OB, G_GM) = range(10)
WIDE_ORDER = (G_QA, G_VB, G_OB, G_KVA, G_QB, G_KB, G_GM)
NARROW_ORDER = (G_GA, G_IB, G_FB)
WIDE_OFF = dict(zip(WIDE_ORDER, np.cumsum((0,) + tuple(SPLITS[g] for g in WIDE_ORDER))[:-1].tolist()))
NARROW_OFF = dict(zip(NARROW_ORDER, np.cumsum((0,) + tuple(SPLITS[g] for g in NARROW_ORDER))[:-1].tolist()))
NARROW_W = 128


def project_in(h, w_in, b_in):
    w_t = w_in.T

    def gather(order, pad_to):
        rows = jnp.concatenate([w_t[BOUNDS[g]:BOUNDS[g + 1]] for g in order], axis=0)
        bias = jnp.concatenate([b_in[BOUNDS[g]:BOUNDS[g + 1]] for g in order], axis=0)
        pad = pad_to - rows.shape[0]
        return jnp.pad(rows, ((0, pad), (0, 0))).astype(jnp.bfloat16), jnp.pad(bias, (0, pad))

    n_wide = sum(SPLITS[g] for g in WIDE_ORDER)
    wide = matmul_act(h, *gather(WIDE_ORDER, n_wide), nt=True)
    narrow = matmul_act(h, *gather(NARROW_ORDER, NARROW_W), nt=True, tn=NARROW_W)
    return wide, narrow


def kernel(x_prompt, x_sample, cache_nsa_kv, cache_win_kv, state_C, state_n, state_m, page_table,
           norm_g, w_in, b_in, cmp_pe, cmp_w1, cmp_b1, cmp_w2, w_up_a, w_up_b, w_out, w_mlp1, w_mlp2):
    Bp, Tp = x_prompt.shape[:2]
    Bs, Ts = x_sample.shape[:2]
    assert Bp == 1 and Ts == 1 and Tp >= WINDOW, (x_prompt.shape, x_sample.shape)
    x = jnp.concatenate([x_prompt.reshape(Tp, D_MODEL), x_sample.reshape(Bs, D_MODEL)], axis=0)
    kv_p, kv_s, win_p, win_all = [], [], [], None
    C_p, C_s, n_p, n_s, m_p, m_s = [], [], [], [], [], []
    kv4 = 4 * KVW
    for l in range(DEPTH):
        wide, narrow = project_in(rmsnorm_cast(x, norm_g[l, 0]), w_in[l], b_in[l])
        col = lambda g, rows: wide[rows, WIDE_OFF[g]:WIDE_OFF[g] + SPLITS[g]]
        ncol = lambda g, rows: narrow[rows, NARROW_OFF[g]:NARROW_OFF[g] + SPLITS[g]]
        pr, sr = slice(0, Tp), slice(Tp, Tp + Bs)

        kva = col(G_KVA, pr)
        ya_p, pe_bias = nsa_prompt_pallas(col(G_QA, pr), ncol(G_GA, pr), kva, cmp_pe[l], cmp_w1[l], cmp_b1[l], cmp_w2[l])
        yb_p, C, n, m = mlstm_prompt_pallas(
            (wide, WIDE_OFF[G_QB] // (B_HEADS * QK_DIM)), col(G_KB, pr).T, (wide, WIDE_OFF[G_VB] // B_WIDTH),
            (wide, WIDE_OFF[G_OB] // B_WIDTH), (narrow, 0, NARROW_OFF[G_FB]),
            narrow[pr, NARROW_OFF[G_IB]:NARROW_OFF[G_IB] + 2 * B_HEADS].T, Tp)
        kv_p.append(kva[:, 0:kv4].reshape(Bp, Tp, 4, A_KV_HEADS, HEAD_DIM))
        win_p.append(kva[Tp - WINDOW:, kv4:].reshape(Bp, WINDOW, 2, A_KV_HEADS, HEAD_DIM))
        C_p.append(C[None])
        n_p.append(n[None])
        m_p.append(m[None])

        kva = col(G_KVA, sr)
        ya_s, win_all = nsa_decode_pallas(col(G_QA, sr), ncol(G_GA, sr), kva, cache_nsa_kv, page_table, cache_win_kv, l,
                                          cmp_w1[l], cmp_w2[l], pe_bias, win_into=win_all)
        yb_s, C, n, m = mlstm_sample_pallas(col(G_QB, sr), col(G_KB, sr), col(G_VB, sr), ncol(G_IB, sr), ncol(G_FB, sr),
                                            col(G_OB, sr), state_C, state_n, state_m, l)
        kv_s.append(kva[:, 0:kv4].reshape(Bs, Ts, 4, A_KV_HEADS, HEAD_DIM))
        C_s.append(C)
        n_s.append(n)
        m_s.append(m)

        gm_src = (wide, WIDE_OFF[G_GM])
        mix = mix_matmul(ya_p, yb_p, w_up_a[l], w_up_b[l], gm_src, row0=0, rows_total=Tp + Bs)
        mix = mix_matmul(ya_s, yb_s, w_up_a[l], w_up_b[l], gm_src, row0=Tp, rows_total=Tp + Bs, into=mix)
        x = matmul_norm_res(mix, w_out[l].astype(jnp.bfloat16), x, norm_g[l, 1])
        hid = matmul_act(rmsnorm_cast(x, norm_g[l, 2]), w_mlp1[l], jnp.zeros((D_FF,), jnp.float32), nt=False,
                         act="relu2", out_dtype=jnp.bfloat16)
        x = matmul_norm_res(hid, w_mlp2[l].astype(jnp.bfloat16), x, norm_g[l, 3])

    win_s = win_all.reshape(DEPTH, Bs, 2, A_KV_HEADS, HEAD_DIM, WINDOW).transpose(0, 1, 5, 2, 3, 4)
    return (x[:Tp].reshape(Bp, Tp, D_MODEL), x[Tp:].reshape(Bs, Ts, D_MODEL),
            jnp.stack(kv_p), jnp.stack(kv_s), jnp.stack(win_p), win_s,
            jnp.stack(C_p), jnp.stack(C_s), jnp.stack(n_p), jnp.stack(n_s), jnp.stack(m_p), jnp.stack(m_s))
```

```python
import functools
import math

import jax
import jax.numpy as jnp
import numpy as np
from jax import lax
from jax.experimental import pallas as pl
from jax.experimental.pallas import tpu as pltpu

D_MODEL = 2048
DEPTH = 2
PAST_LEN = 2048
A_HEADS = 16
A_KV_HEADS = 4
A_GROUP = A_HEADS // A_KV_HEADS
HEAD_DIM = 64
A_WIDTH = A_HEADS * HEAD_DIM
CMP_BLOCK = 32
CMP_STRIDE = 16
CMP_RATIO = CMP_BLOCK // CMP_STRIDE
CMP_HIDDEN = 128
SLC_BLOCK = 64
SLC_TOP = 16
WINDOW = 512
Q_BLOCK = 128
N_KV_BRANCH = 6
B_HEADS = 8
QK_DIM = 64
V_DIM = 128
B_WIDTH = B_HEADS * V_DIM
MLSTM_CHUNK = 64
D_FF = 4 * D_MODEL
EPS = 1e-6
FORCE = 1e4
NEG = -1e30

SPLITS = (A_WIDTH, N_KV_BRANCH * A_KV_HEADS * HEAD_DIM, 3 * A_HEADS,
          B_HEADS * QK_DIM, B_HEADS * QK_DIM, B_WIDTH, B_HEADS, B_HEADS, B_WIDTH, 2 * D_MODEL)
BOUNDS = tuple(int(b) for b in np.cumsum((0,) + SPLITS))

VMEM_LIMIT_BYTES = 56 * 1024 * 1024


def _params(*sem):
    return pltpu.CompilerParams(dimension_semantics=sem, vmem_limit_bytes=VMEM_LIMIT_BYTES)


def _row_tile(m, want):
    for t in range(min(want, m), 15, -1):
        if m % t == 0 and t % 16 == 0:
            return t
    return m


def _rmsnorm_kernel(x_ref, g_ref, o_ref):
    x = x_ref[...]
    r = lax.rsqrt(jnp.mean(x * x, axis=-1, keepdims=True) + EPS)
    o_ref[...] = (x * r * g_ref[...]).astype(o_ref.dtype)


def rmsnorm_cast(x, g, *, tm=640):
    M, K = x.shape
    tm = _row_tile(M, tm)
    return pl.pallas_call(
        _rmsnorm_kernel,
        grid=(M // tm,),
        in_specs=[pl.BlockSpec((tm, K), lambda i: (i, 0)), pl.BlockSpec((1, K), lambda i: (0, 0))],
        out_specs=pl.BlockSpec((tm, K), lambda i: (i, 0)),
        out_shape=jax.ShapeDtypeStruct((M, K), jnp.bfloat16),
        compiler_params=_params("parallel"),
        name="rmsnorm_cast",
    )(x, g.reshape(1, K))


def _matmul_act_kernel(h_ref, w_ref, b_ref, o_ref, *, nt, act):
    w = w_ref[...].astype(jnp.bfloat16)
    z = (_nt(h_ref[...], w) if nt else jnp.dot(h_ref[...], w, preferred_element_type=jnp.float32)) + b_ref[...]
    if act == "relu2":
        z = jnp.square(jnp.maximum(z, 0.0))
    o_ref[...] = z.astype(o_ref.dtype)


def matmul_act(h, w, b, *, nt, act=None, out_dtype=jnp.float32, tm=1664, tn=512):
    M, K = h.shape
    N = w.shape[0] if nt else w.shape[1]
    tm = _row_tile(M, tm)
    tn = min(tn, N)
    assert N % tn == 0, (N, tn)
    w_spec = pl.BlockSpec((tn, K), lambda i, j: (j, 0)) if nt else pl.BlockSpec((K, tn), lambda i, j: (0, j))
    return pl.pallas_call(
        functools.partial(_matmul_act_kernel, nt=nt, act=act),
        grid=(M // tm, N // tn),
        in_specs=[pl.BlockSpec((tm, K), lambda i, j: (i, 0)), w_spec, pl.BlockSpec((1, tn), lambda i, j: (0, j))],
        out_specs=pl.BlockSpec((tm, tn), lambda i, j: (i, j)),
        out_shape=jax.ShapeDtypeStruct((M, N), out_dtype),
        compiler_params=_params("parallel", "arbitrary"),
        name="matmul_act",
    )(h, w, b.reshape(1, N))


def _mix_kernel(ya_ref, yb_ref, wa_ref, wb_ref, ga_ref, gb_ref, o_ref):
    bf = jnp.bfloat16
    a = jnp.dot(ya_ref[...].astype(bf), wa_ref[...].astype(bf), preferred_element_type=jnp.float32)
    b = jnp.dot(yb_ref[...].astype(bf), wb_ref[...].astype(bf), preferred_element_type=jnp.float32)
    o_ref[...] = (jax.nn.sigmoid(ga_ref[...]) * a + jax.nn.sigmoid(gb_ref[...]) * b).astype(o_ref.dtype)


def _mix_into_kernel(ya_ref, yb_ref, wa_ref, wb_ref, ga_ref, gb_ref, prev_ref, o_ref):
    del prev_ref
    _mix_kernel(ya_ref, yb_ref, wa_ref, wb_ref, ga_ref, gb_ref, o_ref)


def mix_matmul(ya, yb, wa, wb, gm_src, *, row0, rows_total, into=None, tm=1024, tn=512):
    M, Ka = ya.shape
    Kb = yb.shape[1]
    N = wa.shape[1]
    gm, off = gm_src
    tm = _row_tile(M, tm)
    nb = N // tn
    assert off % tn == 0 and N % tn == 0 and row0 % tm == 0, (off, N, tn, row0, tm)
    ja, jb, i0 = off // tn, off // tn + nb, row0 // tm
    in_specs = [
        pl.BlockSpec((tm, Ka), lambda i, j: (i, 0)),
        pl.BlockSpec((tm, Kb), lambda i, j: (i, 0)),
        pl.BlockSpec((Ka, tn), lambda i, j: (0, j)),
        pl.BlockSpec((Kb, tn), lambda i, j: (0, j)),
        pl.BlockSpec((tm, tn), lambda i, j: (i0 + i, ja + j)),
        pl.BlockSpec((tm, tn), lambda i, j: (i0 + i, jb + j)),
    ]
    args = (ya, yb, wa, wb, gm, gm)
    aliases = {}
    if into is not None:
        in_specs.append(pl.BlockSpec(memory_space=pl.ANY))
        args += (into,)
        aliases = {len(args) - 1: 0}
    return pl.pallas_call(
        _mix_kernel if into is None else _mix_into_kernel,
        grid=(M // tm, nb),
        in_specs=in_specs,
        out_specs=pl.BlockSpec((tm, tn), lambda i, j: (i0 + i, j)),
        out_shape=jax.ShapeDtypeStruct((rows_total, N), jnp.bfloat16),
        input_output_aliases=aliases,
        compiler_params=_params("parallel", "arbitrary"),
        name="mix_matmul",
    )(*args)


def _matmul_norm_res_kernel(a_ref, w_ref, x_ref, g_ref, o_ref, acc_ref):
    k = pl.program_id(1)

    @pl.when(k == 0)
    def _():
        acc_ref[...] = jnp.zeros_like(acc_ref)

    a = a_ref[...]
    for n0 in range(0, acc_ref.shape[1], NORM_RES_COLS):
        cols = slice(n0, n0 + NORM_RES_COLS)
        acc_ref[:, cols] += jnp.dot(a, w_ref[:, cols].astype(jnp.bfloat16), preferred_element_type=jnp.float32)

    @pl.when(k == pl.num_programs(1) - 1)
    def _():
        y = acc_ref[...]
        r = lax.rsqrt(jnp.mean(y * y, axis=-1, keepdims=True) + EPS)
        o_ref[...] = x_ref[...] + y * r * g_ref[...]


NORM_RES_COLS = 512


def matmul_norm_res(a, w, x, g, *, tm=640, tk=1024):
    M, K = a.shape
    N = w.shape[1]
    tm = _row_tile(M, tm)
    tk = min(tk, K)
    return pl.pallas_call(
        _matmul_norm_res_kernel,
        grid=(M // tm, K // tk),
        in_specs=[
            pl.BlockSpec((tm, tk), lambda i, k: (i, k)),
            pl.BlockSpec((tk, N), lambda i, k: (k, 0)),
            pl.BlockSpec((tm, N), lambda i, k: (i, 0)),
            pl.BlockSpec((1, N), lambda i, k: (0, 0)),
        ],
        out_specs=pl.BlockSpec((tm, N), lambda i, k: (i, 0)),
        out_shape=jax.ShapeDtypeStruct((M, N), jnp.float32),
        scratch_shapes=[pltpu.VMEM((tm, N), jnp.float32)],
        compiler_params=_params("parallel", "arbitrary"),
        name="matmul_norm_res",
    )(a, w, x, g.reshape(1, N))


NSA_LANES = A_GROUP * Q_BLOCK
KEY_CHUNK = 512
WIN_CHUNK = 128
WIN_SPAN = WINDOW + Q_BLOCK
SLC_SLOTS = 128
POS_PIECES = 6
AUG = 128
MASK_BIG = 30000.0


def _pos_pieces(pos):
    pos = np.asarray(pos, np.int64)
    hi = (pos // 64) * 64
    lo = pos % 64
    return np.stack([hi, lo] * 3, axis=-1).astype(np.float32)


def _slope_rows():
    slopes = 2.0 ** (-8.0 * np.arange(1, A_HEADS + 1, dtype=np.float64) / A_HEADS)
    s = jnp.asarray(slopes.astype(np.float32))
    hi = s.astype(jnp.bfloat16)
    r1 = s - hi.astype(jnp.float32)
    mid = r1.astype(jnp.bfloat16)
    lo = (r1 - mid.astype(jnp.float32)).astype(jnp.bfloat16)
    rows = jnp.stack([hi, hi, mid, mid, lo, lo], axis=0)
    rows = rows.reshape(POS_PIECES, A_KV_HEADS, A_GROUP).transpose(1, 0, 2)
    rows = jnp.repeat(rows, Q_BLOCK, axis=-1)
    return jnp.pad(rows, ((0, 0), (0, 16 - POS_PIECES), (0, 0)))


def _split3(x):
    hi = x.astype(jnp.bfloat16)
    r1 = x - hi.astype(jnp.float32)
    mid = r1.astype(jnp.bfloat16)
    lo = (r1 - mid.astype(jnp.float32)).astype(jnp.bfloat16)
    return hi, mid, lo


def _tile4(x):
    return jnp.concatenate([x] * A_GROUP, axis=1)


def _softmax_step(carry, s, vT):
    m, l, acc = carry
    m_new = jnp.maximum(m, jnp.max(s, axis=0, keepdims=True))
    alpha = jnp.exp(m - m_new)
    p = jnp.exp(s - m_new)
    l = alpha * l + jnp.sum(p, axis=0, keepdims=True)
    acc = alpha * acc + jnp.dot(vT, p.astype(jnp.bfloat16), preferred_element_type=jnp.float32)
    return m_new, l, acc


def _nsa_prompt_kernel(q_ref, slope_ref, gT_ref, ck_ref, cvT_ref, ovT_ref, ks_ref, vsT_ref,
                       kw_ref, vwT_ref, o_ref, qa_ref, sel_ref):
    f32, bf = jnp.float32, jnp.bfloat16
    i = pl.program_id(1)
    s0 = i * Q_BLOCK
    ncp = ck_ref.shape[1]

    qT = q_ref[...].T * (HEAD_DIM ** -0.5)
    qcat = jnp.concatenate([qT[g * HEAD_DIM:(g + 1) * HEAD_DIM, :] for g in range(A_GROUP)], axis=1)
    qa_ref[0:HEAD_DIM, :] = qcat.astype(bf)
    qa_ref[HEAD_DIM:HEAD_DIM + 16, :] = slope_ref[0]
    qa_ref[HEAD_DIM + 16:AUG, :] = jnp.zeros((AUG - HEAD_DIM - 16, NSA_LANES), bf)
    qc = qa_ref[0:AUG, :]


    sc = jnp.dot(ck_ref[0], qc, preferred_element_type=f32)
    n_idx = lax.broadcasted_iota(jnp.int32, (ncp, Q_BLOCK), 0)
    t_idx = s0 + lax.broadcasted_iota(jnp.int32, (ncp, Q_BLOCK), 1)
    valid = _tile4(n_idx * CMP_STRIDE + (CMP_BLOCK - 1) <= t_idx)
    sc = jnp.where(valid, sc, NEG)
    ec = jnp.where(valid, jnp.exp(sc - jnp.max(sc, axis=0, keepdims=True)), 0.0)
    lc = jnp.sum(ec, axis=0, keepdims=True)
    pc = ec * jnp.where(lc > 0.0, 1.0 / lc, 0.0)
    o_c = jnp.dot(cvT_ref[0], pc.astype(bf), preferred_element_type=f32)

    pg = pc[:, 0:Q_BLOCK]
    for g in range(1, A_GROUP):
        pg = pg + pc[:, g * Q_BLOCK:(g + 1) * Q_BLOCK]
    imp = sum(jnp.dot(ovT_ref[...], piece, preferred_element_type=f32) for piece in _split3(pg))

    blk = lax.broadcasted_iota(jnp.int32, (SLC_SLOTS, Q_BLOCK), 0)
    cur = (s0 + lax.broadcasted_iota(jnp.int32, (SLC_SLOTS, Q_BLOCK), 1)) // SLC_BLOCK
    forced = (blk == 0) | (blk == cur) | (blk == cur - 1)
    causal = blk <= cur
    score = jnp.where(forced, FORCE, jnp.where(causal, imp, -FORCE))
    blk_f = blk.astype(f32)
    sel = jnp.zeros((SLC_SLOTS, Q_BLOCK), f32)
    for _ in range(SLC_TOP):
        top = jnp.max(score, axis=0, keepdims=True)
        first = jnp.min(jnp.where(score == top, blk_f, float(SLC_SLOTS)), axis=0, keepdims=True)
        pick = blk_f == first
        sel = jnp.where(pick, 1.0, sel)
        score = jnp.where(pick, -jnp.inf, score)
    selneg = jnp.where((sel > 0.0) & causal, 0.0, -MASK_BIG)
    qa_ref[AUG:AUG + SLC_SLOTS, :] = _tile4(selneg).astype(bf)

    def sel_scores(c):
        kc = ks_ref[0, pl.ds(pl.multiple_of(c * KEY_CHUNK, KEY_CHUNK), KEY_CHUNK), :]
        return jnp.dot(kc, qa_ref[...], preferred_element_type=f32)

    init = (jnp.full((1, NSA_LANES), NEG, f32), jnp.zeros((1, NSA_LANES), f32),
            jnp.zeros((HEAD_DIM, NSA_LANES), f32))
    c_last = (i * Q_BLOCK) // KEY_CHUNK
    bpc = KEY_CHUNK // SLC_BLOCK
    sel_ref[...] = jnp.where((sel > 0.0) & causal, 1.0, 0.0)

    def sel_chunk(c, cr):
        picked = jnp.max(sel_ref[pl.ds(pl.multiple_of(c * bpc, bpc), bpc), :]) > 0.0
        return lax.cond(picked, lambda x: _softmax_step(x, sel_scores(c), vsT_ref[0, c]), lambda x: x, cr)

    carry = lax.fori_loop(0, c_last, sel_chunk, init)
    j_last = c_last * KEY_CHUNK + lax.broadcasted_iota(jnp.int32, (KEY_CHUNK, Q_BLOCK), 0)
    t_last = s0 + lax.broadcasted_iota(jnp.int32, (KEY_CHUNK, Q_BLOCK), 1)
    m_s, l_s, acc_s = _softmax_step(carry, jnp.where(_tile4(j_last <= t_last), sel_scores(c_last), NEG),
                                    vsT_ref[0, c_last])
    o_s = acc_s * (1.0 / l_s)

    w0 = jnp.maximum(i - WINDOW // WIN_CHUNK, 0)
    kw = kw_ref[0, pl.ds(pl.multiple_of(w0 * WIN_CHUNK, WIN_CHUNK), WIN_SPAN), :]
    sw = jnp.dot(kw, qc, preferred_element_type=f32)
    jw = w0 * WIN_CHUNK + lax.broadcasted_iota(jnp.int32, (WIN_SPAN, Q_BLOCK), 0)
    tw = s0 + lax.broadcasted_iota(jnp.int32, (WIN_SPAN, Q_BLOCK), 1)
    sw = jnp.where(_tile4((jw <= tw) & (jw > tw - WINDOW)), sw, NEG)
    pw = jnp.exp(sw - jnp.max(sw, axis=0, keepdims=True))
    l_w = jnp.sum(pw, axis=0, keepdims=True)
    pw = pw.astype(bf)
    acc_w = jnp.zeros((HEAD_DIM, NSA_LANES), f32)
    for d in range(WIN_SPAN // WIN_CHUNK):
        acc_w = acc_w + jnp.dot(vwT_ref[0, w0 + d], pw[d * WIN_CHUNK:(d + 1) * WIN_CHUNK, :], preferred_element_type=f32)
    o_w = acc_w * (1.0 / l_w)

    gate = jax.nn.sigmoid(gT_ref[0])
    outs = []
    for g in range(A_GROUP):
        cols = slice(g * Q_BLOCK, (g + 1) * Q_BLOCK)
        outs.append(gate[3 * g:3 * g + 1, :] * o_c[:, cols] + gate[3 * g + 1:3 * g + 2, :] * o_s[:, cols]
                    + gate[3 * g + 2:3 * g + 3, :] * o_w[:, cols])
    o_ref[...] = jnp.concatenate(outs, axis=0).T


def _cmp_prompt_kernel(x_ref, pe_ref, w1_ref, b1_ref, w2T_ref, ck_ref, cvT_ref, bias_ref):
    f32, bf = jnp.float32, jnp.bfloat16
    nseg = ck_ref.shape[1]
    row_w = 2 * A_KV_HEADS * HEAD_DIM
    for c in range(2):
        bias = jnp.dot(pe_ref[c], w1_ref[c], preferred_element_type=f32,
                       precision=lax.Precision.HIGHEST) + b1_ref[c]
        bias_ref[c] = bias
        for k in range(A_KV_HEADS):
            col = (c * A_KV_HEADS + k) * HEAD_DIM
            y0 = jnp.zeros((nseg, CMP_HIDDEN), f32)
            y1 = jnp.zeros((nseg, CMP_HIDDEN), f32)
            for j in range(CMP_STRIDE):
                xj = x_ref[:, j * row_w + col:j * row_w + col + HEAD_DIM].astype(bf)
                w0 = w1_ref[c, j * HEAD_DIM:(j + 1) * HEAD_DIM, :].astype(bf)
                w1 = w1_ref[c, (CMP_STRIDE + j) * HEAD_DIM:(CMP_STRIDE + j + 1) * HEAD_DIM, :].astype(bf)
                y0 = y0 + jnp.dot(xj, w0, preferred_element_type=f32)
                y1 = y1 + jnp.dot(xj, w1, preferred_element_type=f32)
            pre = y0 + pltpu.roll(y1, nseg - 1, 0)
            hid = jnp.maximum(pre + bias, 0.0).astype(bf)
            if c == 0:
                ck_ref[k] = lax.dot_general(hid, w2T_ref[c].astype(bf), (((1,), (1,)), ((), ())),
                                            preferred_element_type=f32)
            else:
                cvT_ref[k] = lax.dot_general(w2T_ref[c].astype(bf), hid, (((1,), (1,)), ((), ())),
                                             preferred_element_type=f32)


def compress_prompt(x_cmp, pe, w1, b1, w2):
    T = x_cmp.shape[0]
    nseg = T // CMP_STRIDE
    return pl.pallas_call(
        _cmp_prompt_kernel,
        out_shape=(jax.ShapeDtypeStruct((A_KV_HEADS, nseg, HEAD_DIM), jnp.float32),
                   jax.ShapeDtypeStruct((A_KV_HEADS, HEAD_DIM, nseg), jnp.float32),
                   jax.ShapeDtypeStruct((2, 1, CMP_HIDDEN), jnp.float32)),
        compiler_params=pltpu.CompilerParams(vmem_limit_bytes=VMEM_LIMIT_BYTES),
        name="compress_prompt",
    )(x_cmp.reshape(nseg, CMP_STRIDE * x_cmp.shape[1]), pe.reshape(2, 1, CMP_BLOCK * HEAD_DIM), w1, b1.reshape(2, 1, CMP_HIDDEN), w2.transpose(0, 2, 1))


def nsa_prompt_pallas(q_src, ga, kva, pe, w1, b1, w2):
    bf = jnp.bfloat16
    q_arr, q_off = q_src
    q_cb = q_off // (A_GROUP * HEAD_DIM)
    T = kva.shape[0]
    nqb = T // Q_BLOCK
    assert T % KEY_CHUNK == 0 and T >= WIN_SPAN, T
    ncp = T // CMP_STRIDE
    kvw = A_KV_HEADS * HEAD_DIM
    ck, cvT, pe_bias = compress_prompt(kva[:, 0:2 * kvw], pe, w1, b1, w2)

    def heads(x):
        return x.reshape(T, A_KV_HEADS, HEAD_DIM).transpose(1, 0, 2)

    def chunksT(x, chunk):
        return x.reshape(T // chunk, chunk, A_KV_HEADS, HEAD_DIM).transpose(2, 0, 3, 1).astype(bf)

    pos = np.arange(T)
    key_pos = jnp.broadcast_to(jnp.asarray(_pos_pieces(pos)), (A_KV_HEADS, T, POS_PIECES))
    zpad = jnp.zeros((A_KV_HEADS, T, AUG - HEAD_DIM - POS_PIECES), jnp.float32)
    onehot = jnp.broadcast_to(jnp.asarray((pos[:, None] // SLC_BLOCK == np.arange(SLC_SLOTS)[None, :]).astype(np.float32)),
                              (A_KV_HEADS, T, SLC_SLOTS))
    ks_aug = jnp.concatenate([heads(kva[:, 2 * kvw:3 * kvw]), key_pos, zpad, onehot], axis=-1).astype(bf)
    kw_aug = jnp.concatenate([heads(kva[:, 4 * kvw:5 * kvw]), key_pos, zpad], axis=-1).astype(bf)
    c_end = np.arange(ncp) * CMP_STRIDE + (CMP_BLOCK - 1)
    ck_aug = jnp.concatenate([ck, jnp.broadcast_to(jnp.asarray(_pos_pieces(c_end)), (A_KV_HEADS, ncp, POS_PIECES)),
                              jnp.zeros((A_KV_HEADS, ncp, AUG - HEAD_DIM - POS_PIECES), jnp.float32)], axis=-1).astype(bf)
    vsT = chunksT(kva[:, 3 * kvw:4 * kvw], KEY_CHUNK)
    vwT = chunksT(kva[:, 5 * kvw:6 * kvw], WIN_CHUNK)
    c_start = c_end - (CMP_BLOCK - 1)
    s_start = np.arange(SLC_SLOTS) * SLC_BLOCK
    ovT = ((c_start[None, :] < s_start[:, None] + SLC_BLOCK) & (c_end[None, :] >= s_start[:, None])
           & (np.arange(ncp)[None, :] < ncp - 1))
    ovT = jnp.asarray(ovT.astype(np.float32)).astype(bf)
    gT = jnp.pad(ga.T.reshape(A_KV_HEADS, 3 * A_GROUP, T), ((0, 0), (0, 16 - 3 * A_GROUP), (0, 0)))

    y = pl.pallas_call(
        _nsa_prompt_kernel,
        grid=(A_KV_HEADS, nqb),
        in_specs=[
            pl.BlockSpec((Q_BLOCK, A_GROUP * HEAD_DIM), lambda k, i: (i, q_cb + k)),
            pl.BlockSpec((1, 16, NSA_LANES), lambda k, i: (k, 0, 0)),
            pl.BlockSpec((1, 16, Q_BLOCK), lambda k, i: (k, 0, i)),
            pl.BlockSpec((1, ncp, AUG), lambda k, i: (k, 0, 0)),
            pl.BlockSpec((1, HEAD_DIM, ncp), lambda k, i: (k, 0, 0)),
            pl.BlockSpec((SLC_SLOTS, ncp), lambda k, i: (0, 0)),
            pl.BlockSpec((1, T, AUG + SLC_SLOTS), lambda k, i: (k, 0, 0)),
            pl.BlockSpec((1, T // KEY_CHUNK, HEAD_DIM, KEY_CHUNK), lambda k, i: (k, 0, 0, 0)),
            pl.BlockSpec((1, T, AUG), lambda k, i: (k, 0, 0)),
            pl.BlockSpec((1, T // WIN_CHUNK, HEAD_DIM, WIN_CHUNK), lambda k, i: (k, 0, 0, 0)),
        ],
        out_specs=pl.BlockSpec((Q_BLOCK, A_GROUP * HEAD_DIM), lambda k, i: (i, k)),
        out_shape=jax.ShapeDtypeStruct((T, A_WIDTH), jnp.float32),
        scratch_shapes=[pltpu.VMEM((AUG + SLC_SLOTS, NSA_LANES), bf), pltpu.VMEM((SLC_SLOTS, Q_BLOCK), jnp.float32)],
        compiler_params=_params("parallel", "arbitrary"),
        name="nsa_prompt",
    )(q_arr, _slope_rows(), gT, ck_aug, cvT.astype(bf), ovT, ks_aug, vsT, kw_aug, vwT)
    return y, pe_bias


PAGE_SIZE = 128
N_PAGES = PAST_LEN // PAGE_SIZE
PAGE_SEGS = PAGE_SIZE // CMP_STRIDE
PAST_SEGS = PAST_LEN // CMP_STRIDE
ROW_W = 4 * A_KV_HEADS * HEAD_DIM
KVW = A_KV_HEADS * HEAD_DIM
NS_SAMPLE = PAST_LEN // SLC_BLOCK + 1


def _nt(a, b):
    return lax.dot_general(a, b, (((1,), (1,)), ((), ())), preferred_element_type=jnp.float32)


def _dot3(x, w):
    return sum(jnp.dot(p, w, preferred_element_type=jnp.float32) for p in _split3(x))


def _nsa_sample_kernel(pt_ref, q_ref, g_ref, new_ref, *refs):
    del pt_ref
    pages = refs[:N_PAGES]
    (win_ref, w1_ref, bias_ref, w2bd_ref, bmask_ref, basec_ref, bases_ref, basew_ref, slope_ref,
     grp_ref, ov_ref, exp_ref, o_ref, wout_ref, xc_ref) = refs[N_PAGES:]
    f32, bf = jnp.float32, jnp.bfloat16

    q = q_ref[0] * (HEAD_DIM ** -0.5)
    qbd = jnp.where(bmask_ref[...] > 0.0, jnp.concatenate([q] * A_KV_HEADS, axis=1), 0.0).astype(bf)
    qbd_f = qbd.astype(f32)
    new = new_ref[0]

    def slab(off, width):
        return jnp.concatenate([pg[0, :, off:off + width] for pg in pages], axis=0)

    ckv = []
    for c in range(2):
        for j in range(CMP_STRIDE):
            s = slab(j * ROW_W + c * KVW, KVW).astype(bf)
            for k in range(A_KV_HEADS):
                xc_ref[k * PAST_SEGS:(k + 1) * PAST_SEGS, j * HEAD_DIM:(j + 1) * HEAD_DIM] = (
                    s[:, k * HEAD_DIM:(k + 1) * HEAD_DIM])
        y = jnp.dot(xc_ref[...], w1_ref[c], preferred_element_type=f32)
        pre = y[:, 0:CMP_HIDDEN] + pltpu.roll(y[:, CMP_HIDDEN:2 * CMP_HIDDEN], A_KV_HEADS * PAST_SEGS - 1, 0)
        hid = jnp.maximum(pre + bias_ref[c], 0.0).astype(bf)
        hid = jnp.concatenate([hid[k * PAST_SEGS:(k + 1) * PAST_SEGS] for k in range(A_KV_HEADS)], axis=1)
        ckv.append(jnp.dot(hid, w2bd_ref[c], preferred_element_type=f32).astype(bf))
    ck, cv = ckv

    sc = _nt(qbd, ck) + basec_ref[...]
    ec = jnp.exp(sc - jnp.max(sc, axis=1, keepdims=True))
    pc = ec / jnp.sum(ec, axis=1, keepdims=True)
    o_c = jnp.dot(pc.astype(bf), cv, preferred_element_type=f32)

    pov = _dot3(pc, ov_ref[...])
    p_hi, p_mid, p_lo = _split3(pov)
    imp = sum(jnp.dot(grp_ref[...], p, preferred_element_type=f32) for p in (p_hi, p_mid, p_lo))

    lane = lax.broadcasted_iota(jnp.int32, (A_HEADS, PAST_SEGS), 1)
    score = jnp.where((lane == 0) | (lane == NS_SAMPLE - 1) | (lane == NS_SAMPLE - 2), FORCE,
                      jnp.where(lane < NS_SAMPLE, imp, -jnp.inf))
    rr = lax.broadcasted_iota(jnp.int32, (PAST_SEGS, PAST_SEGS), 0)
    cc = lax.broadcasted_iota(jnp.int32, (PAST_SEGS, PAST_SEGS), 1)
    row_id = lax.broadcasted_iota(jnp.int32, (A_HEADS, PAST_SEGS), 0)
    selrows = jnp.zeros((A_HEADS, PAST_SEGS), f32)
    for k in range(A_KV_HEADS):
        row = jnp.broadcast_to(score[A_GROUP * k:A_GROUP * k + 1, :], (PAST_SEGS, PAST_SEGS))
        col = jnp.max(jnp.where(rr == cc, row, -jnp.inf), axis=1, keepdims=True)
        ahead = (col > row) | ((col == row) & (rr < cc))
        rank = jnp.sum(jnp.where(ahead, 1.0, 0.0), axis=0, keepdims=True)
        sel_k = jnp.where(rank < float(SLC_TOP), 1.0, 0.0)
        selrows = jnp.where(row_id // A_GROUP == k, jnp.broadcast_to(sel_k, (A_HEADS, PAST_SEGS)), selrows)
    selseg = jnp.dot(selrows.astype(bf), exp_ref[...], preferred_element_type=f32)
    maskbias = jnp.where(selseg > 0.5, 0.0, NEG)

    slope = slope_ref[...]
    scores = []
    for j in range(CMP_STRIDE):
        kj = slab(j * ROW_W + 2 * KVW, KVW).astype(bf)
        scores.append(_nt(qbd, kj) + (bases_ref[...] + slope * float(j)) + maskbias)
    s_new = jnp.sum(qbd_f * new[:, 2 * KVW:3 * KVW], axis=1, keepdims=True)
    mx = scores[0]
    for s in scores[1:]:
        mx = jnp.maximum(mx, s)
    m = jnp.maximum(jnp.max(mx, axis=1, keepdims=True), s_new)
    p_new = jnp.exp(s_new - m)
    acc = p_new * new[:, 3 * KVW:4 * KVW]
    psum = jnp.zeros((A_HEADS, PAST_SEGS), f32)
    for j in range(CMP_STRIDE):
        p = jnp.exp(scores[j] - m)
        psum = psum + p
        vj = slab(j * ROW_W + 3 * KVW, KVW).astype(bf)
        acc = acc + jnp.dot(p.astype(bf), vj, preferred_element_type=f32)
    o_s = acc / (jnp.sum(psum, axis=1, keepdims=True) + p_new)

    kw = win_ref[0, :, 0:KVW].astype(bf)
    vw = win_ref[0, :, KVW:2 * KVW].astype(bf)
    sw = _nt(qbd, kw) + basew_ref[...]
    s_neww = jnp.sum(qbd_f * new[:, 4 * KVW:5 * KVW], axis=1, keepdims=True)
    m = jnp.maximum(jnp.max(sw, axis=1, keepdims=True), s_neww)
    pw = jnp.exp(sw - m)
    p_new = jnp.exp(s_neww - m)
    o_w = (jnp.dot(pw.astype(bf), vw, preferred_element_type=f32) + p_new * new[:, 5 * KVW:6 * KVW]) / (
        jnp.sum(pw, axis=1, keepdims=True) + p_new)

    gate = jax.nn.sigmoid(g_ref[0])
    o = gate[:, 0:1] * o_c + gate[:, 1:2] * o_s + gate[:, 2:3] * o_w
    for k in range(A_KV_HEADS):
        o_ref[0, A_GROUP * k:A_GROUP * (k + 1), :] = o[A_GROUP * k:A_GROUP * (k + 1), k * HEAD_DIM:(k + 1) * HEAD_DIM]

    wout_ref[0, 0:WINDOW - 1, :] = win_ref[0, 1:WINDOW, :]
    wout_ref[0, WINDOW - 1:WINDOW, :] = new[:, 4 * KVW:6 * KVW]


def _sample_constants():
    slopes = (2.0 ** (-8.0 * np.arange(1, A_HEADS + 1, dtype=np.float64) / A_HEADS)).astype(np.float32)[:, None]
    heads = np.arange(A_HEADS)[:, None]
    bmask = (np.arange(KVW)[None, :] // HEAD_DIM == heads // A_GROUP).astype(np.float32)
    n = np.arange(PAST_SEGS)[None, :]
    c_end = n * CMP_STRIDE + (CMP_BLOCK - 1)
    basec = np.where(n < PAST_SEGS - 1, slopes * (c_end - PAST_LEN), NEG).astype(np.float32)
    bases = (slopes * (n * CMP_STRIDE - PAST_LEN)).astype(np.float32)
    r = np.arange(WINDOW)[None, :]
    basew = np.where(r >= 1, slopes * (r - WINDOW), NEG).astype(np.float32)
    slope = np.broadcast_to(slopes, (A_HEADS, PAST_SEGS)).astype(np.float32)
    grp = (heads // A_GROUP == heads.T // A_GROUP).astype(np.float32)
    c_start = np.arange(PAST_SEGS) * CMP_STRIDE
    s_start = np.arange(PAST_SEGS) * SLC_BLOCK
    ov = ((c_start[:, None] < s_start[None, :] + SLC_BLOCK) & (c_start[:, None] + CMP_BLOCK - 1 >= s_start[None, :])
          & (np.arange(PAST_SEGS)[:, None] < PAST_SEGS - 1) & (np.arange(PAST_SEGS)[None, :] < NS_SAMPLE))
    expand = (np.arange(PAST_SEGS)[None, :] * CMP_STRIDE // SLC_BLOCK == np.arange(PAST_SEGS)[:, None])
    as_bf = lambda a: jnp.asarray(a.astype(np.float32)).astype(jnp.bfloat16)
    return (jnp.asarray(bmask), jnp.asarray(basec), jnp.asarray(bases), jnp.asarray(basew), jnp.asarray(slope),
            as_bf(grp), as_bf(ov), as_bf(expand))


def nsa_sample_pallas(qa, ga, kva, pool, page_table, win_buf, layer, w1, w2, pe_bias):
    bf = jnp.bfloat16
    B = qa.shape[0]
    npool = pool.shape[1]
    page_table = page_table + layer * npool
    seg_w = CMP_STRIDE * ROW_W
    w1r = w1.reshape(2, CMP_RATIO, CMP_STRIDE * HEAD_DIM, CMP_HIDDEN).transpose(0, 2, 1, 3)
    w1r = w1r.reshape(2, CMP_STRIDE * HEAD_DIM, CMP_RATIO * CMP_HIDDEN).astype(bf)
    w2bd = jnp.einsum('kl,ced->ckeld', jnp.eye(A_KV_HEADS, dtype=w2.dtype), w2)
    w2bd = w2bd.reshape(2, A_KV_HEADS * CMP_HIDDEN, KVW).astype(bf)
    consts = _sample_constants()
    full = lambda shape: pl.BlockSpec(shape, lambda b, pt: (0,) * len(shape))
    page_specs = [pl.BlockSpec((1, PAGE_SEGS, seg_w), functools.partial(lambda p, b, pt: (pt[b, p], 0, 0), p))
                  for p in range(N_PAGES)]
    grid_spec = pltpu.PrefetchScalarGridSpec(
        num_scalar_prefetch=1,
        grid=(B,),
        in_specs=[
            pl.BlockSpec((1, A_HEADS, HEAD_DIM), lambda b, pt: (b, 0, 0)),
            pl.BlockSpec((1, A_HEADS, 3), lambda b, pt: (b, 0, 0)),
            pl.BlockSpec((1, 1, N_KV_BRANCH * KVW), lambda b, pt: (b, 0, 0)),
            *page_specs,
            pl.BlockSpec((1, WINDOW, 2 * KVW), lambda b, pt: (layer * B + b, 0, 0)),
            full(w1r.shape), full((2, 1, CMP_HIDDEN)), full(w2bd.shape),
            *[full(c.shape) for c in consts],
        ],
        out_specs=[
            pl.BlockSpec((1, A_HEADS, HEAD_DIM), lambda b, pt: (b, 0, 0)),
            pl.BlockSpec((1, WINDOW, 2 * KVW), lambda b, pt: (b, 0, 0)),
        ],
        scratch_shapes=[pltpu.VMEM((A_KV_HEADS * PAST_SEGS, CMP_STRIDE * HEAD_DIM), bf)],
    )
    pool3 = pool.reshape(pool.shape[0] * npool, PAGE_SEGS, seg_w)
    o, wout = pl.pallas_call(
        _nsa_sample_kernel,
        grid_spec=grid_spec,
        out_shape=(jax.ShapeDtypeStruct((B, A_HEADS, HEAD_DIM), jnp.float32),
                   jax.ShapeDtypeStruct((B, WINDOW, 2 * KVW), jnp.float32)),
        compiler_params=_params("arbitrary"),
        name="nsa_sample",
    )(page_table, qa.reshape(B, A_HEADS, HEAD_DIM), ga.reshape(B, A_HEADS, 3), kva.reshape(B, 1, N_KV_BRANCH * KVW),
      *([pool3] * N_PAGES), win_buf.reshape(win_buf.shape[0] * B, WINDOW, 2 * KVW), w1r, pe_bias.reshape(2, 1, CMP_HIDDEN), w2bd, *consts)
    return o.reshape(B, A_WIDTH), wout


SEG_PITCH = 24


def _nsa_decode_kernel(pt_ref, q_ref, g_ref, new_ref, *refs, has_into):
    del pt_ref
    pages = refs[:N_PAGES]
    rest = refs[N_PAGES:]
    (win_ref, w1_ref, bias_ref, w2bd_ref, bmask_ref, basec_ref, bases_ref, basew_ref, slope_ref,
     grp_ref, ov_ref, exp_ref) = rest[:12]
    o_ref, wout_ref, tr_ref, xc_ref = rest[12 + int(has_into):]
    f32, bf = jnp.float32, jnp.bfloat16

    q = q_ref[0] * (HEAD_DIM ** -0.5)
    qbd = jnp.where(bmask_ref[...] > 0.0, jnp.concatenate([q] * A_KV_HEADS, axis=1), 0.0).astype(bf)
    qbd_f = qbd.astype(f32)
    new = new_ref[0]

    ckv = []
    for c in range(2):
        for p in range(N_PAGES):
            for kp in range(2):
                r0 = c * KVW + kp * 2 * HEAD_DIM
                t = pages[p][0, r0:r0 + 2 * HEAD_DIM, :].T
                for s in range(PAGE_SEGS):
                    t0 = (kp * PAST_SEGS + p * PAGE_SEGS + s) * SEG_PITCH
                    tr_ref[t0:t0 + CMP_STRIDE, :] = t[s * CMP_STRIDE:(s + 1) * CMP_STRIDE, :]
        for j in range(CMP_STRIDE):
            for kp in range(2):
                piece = tr_ref[pl.ds(kp * PAST_SEGS * SEG_PITCH + j, PAST_SEGS, stride=SEG_PITCH), :]
                for kk in range(2):
                    k = 2 * kp + kk
                    xc_ref[k * PAST_SEGS:(k + 1) * PAST_SEGS, j * HEAD_DIM:(j + 1) * HEAD_DIM] = (
                        piece[:, kk * HEAD_DIM:(kk + 1) * HEAD_DIM])
        y = jnp.dot(xc_ref[...].astype(bf), w1_ref[c], preferred_element_type=f32)
        pre = y[:, 0:CMP_HIDDEN] + pltpu.roll(y[:, CMP_HIDDEN:2 * CMP_HIDDEN], A_KV_HEADS * PAST_SEGS - 1, 0)
        hid = jnp.maximum(pre + bias_ref[c], 0.0).astype(bf)
        hid = jnp.concatenate([hid[k * PAST_SEGS:(k + 1) * PAST_SEGS] for k in range(A_KV_HEADS)], axis=1)
        ckv.append(jnp.dot(hid, w2bd_ref[c], preferred_element_type=f32).astype(bf))
    ck, cv = ckv

    sc = _nt(qbd, ck) + basec_ref[...]
    ec = jnp.exp(sc - jnp.max(sc, axis=1, keepdims=True))
    pc = ec / jnp.sum(ec, axis=1, keepdims=True)
    o_c = jnp.dot(pc.astype(bf), cv, preferred_element_type=f32)

    pov = _dot3(pc, ov_ref[...])
    imp = sum(jnp.dot(grp_ref[...], p, preferred_element_type=f32) for p in _split3(pov))

    lane = lax.broadcasted_iota(jnp.int32, (A_HEADS, PAST_SEGS), 1)
    score = jnp.where((lane == 0) | (lane == NS_SAMPLE - 1) | (lane == NS_SAMPLE - 2), FORCE,
                      jnp.where(lane < NS_SAMPLE, imp, -jnp.inf))
    rr = lax.broadcasted_iota(jnp.int32, (PAST_SEGS, PAST_SEGS), 0)
    cc = lax.broadcasted_iota(jnp.int32, (PAST_SEGS, PAST_SEGS), 1)
    row_id = lax.broadcasted_iota(jnp.int32, (A_HEADS, PAST_SEGS), 0)
    selrows = jnp.zeros((A_HEADS, PAST_SEGS), f32)
    for k in range(A_KV_HEADS):
        row = jnp.broadcast_to(score[A_GROUP * k:A_GROUP * k + 1, :], (PAST_SEGS, PAST_SEGS))
        col = jnp.max(jnp.where(rr == cc, row, -jnp.inf), axis=1, keepdims=True)
        ahead = (col > row) | ((col == row) & (rr < cc))
        rank = jnp.sum(jnp.where(ahead, 1.0, 0.0), axis=0, keepdims=True)
        sel_k = jnp.where(rank < float(SLC_TOP), 1.0, 0.0)
        selrows = jnp.where(row_id // A_GROUP == k, jnp.broadcast_to(sel_k, (A_HEADS, PAST_SEGS)), selrows)
    selpos = jnp.dot(selrows.astype(bf), exp_ref[...], preferred_element_type=f32)

    slope = slope_ref[...]
    scores = []
    for p in range(N_PAGES):
        kp_ = pages[p][0, 2 * KVW:3 * KVW, :].astype(bf)
        mask = jnp.where(selpos[:, p * PAGE_SIZE:(p + 1) * PAGE_SIZE] > 0.5, 0.0, NEG)
        scores.append(jnp.dot(qbd, kp_, preferred_element_type=f32) + (bases_ref[...] + slope * float(p * PAGE_SIZE)) + mask)
    s_new = jnp.sum(qbd_f * new[:, 2 * KVW:3 * KVW], axis=1, keepdims=True)
    mx = scores[0]
    for s in scores[1:]:
        mx = jnp.maximum(mx, s)
    m = jnp.maximum(jnp.max(mx, axis=1, keepdims=True), s_new)
    p_new = jnp.exp(s_new - m)
    acc = p_new * new[:, 3 * KVW:4 * KVW]
    psum = jnp.zeros((A_HEADS, PAGE_SIZE), f32)
    for p in range(N_PAGES):
        pr = jnp.exp(scores[p] - m)
        psum = psum + pr
        acc = acc + _nt(pr.astype(bf), pages[p][0, 3 * KVW:4 * KVW, :].astype(bf))
    o_s = acc / (jnp.sum(psum, axis=1, keepdims=True) + p_new)

    sw = jnp.dot(qbd, win_ref[0, 0:KVW, :].astype(bf), preferred_element_type=f32) + basew_ref[...]
    s_neww = jnp.sum(qbd_f * new[:, 4 * KVW:5 * KVW], axis=1, keepdims=True)
    m = jnp.maximum(jnp.max(sw, axis=1, keepdims=True), s_neww)
    pw = jnp.exp(sw - m)
    p_new = jnp.exp(s_neww - m)
    o_w = (_nt(pw.astype(bf), win_ref[0, KVW:2 * KVW, :].astype(bf)) + p_new * new[:, 5 * KVW:6 * KVW]) / (
        jnp.sum(pw, axis=1, keepdims=True) + p_new)

    gate = jax.nn.sigmoid(g_ref[0])
    o = gate[:, 0:1] * o_c + gate[:, 1:2] * o_s + gate[:, 2:3] * o_w
    for k in range(A_KV_HEADS):
        o_ref[0, A_GROUP * k:A_GROUP * (k + 1), :] = o[A_GROUP * k:A_GROUP * (k + 1), k * HEAD_DIM:(k + 1) * HEAD_DIM]

    wr = lax.broadcasted_iota(jnp.int32, (WINDOW, 2 * KVW), 0)
    wc = lax.broadcasted_iota(jnp.int32, (WINDOW, 2 * KVW), 1)
    new_col = jnp.sum(jnp.where(wr == wc, jnp.broadcast_to(new[:, 4 * KVW:6 * KVW], (WINDOW, 2 * KVW)), 0.0),
                      axis=1, keepdims=True)
    wout_ref[0] = jnp.where(wc == WINDOW - 1, new_col, pltpu.roll(win_ref[0], WINDOW - 1, 1))


def _decode_constants():
    slopes = (2.0 ** (-8.0 * np.arange(1, A_HEADS + 1, dtype=np.float64) / A_HEADS)).astype(np.float32)[:, None]
    heads = np.arange(A_HEADS)[:, None]
    bmask = (np.arange(KVW)[None, :] // HEAD_DIM == heads // A_GROUP).astype(np.float32)
    n = np.arange(PAST_SEGS)[None, :]
    c_end = n * CMP_STRIDE + (CMP_BLOCK - 1)
    basec = np.where(n < PAST_SEGS - 1, slopes * (c_end - PAST_LEN), NEG).astype(np.float32)
    bases = (slopes * (np.arange(PAGE_SIZE)[None, :] - PAST_LEN)).astype(np.float32)
    r = np.arange(WINDOW)[None, :]
    basew = np.where(r >= 1, slopes * (r - WINDOW), NEG).astype(np.float32)
    slope = np.broadcast_to(slopes, (A_HEADS, PAGE_SIZE)).astype(np.float32)
    grp = (heads // A_GROUP == heads.T // A_GROUP).astype(np.float32)
    c_start = np.arange(PAST_SEGS) * CMP_STRIDE
    s_start = np.arange(PAST_SEGS) * SLC_BLOCK
    ov = ((c_start[:, None] < s_start[None, :] + SLC_BLOCK) & (c_start[:, None] + CMP_BLOCK - 1 >= s_start[None, :])
          & (np.arange(PAST_SEGS)[:, None] < PAST_SEGS - 1) & (np.arange(PAST_SEGS)[None, :] < NS_SAMPLE))
    expand = (np.arange(PAST_LEN)[None, :] // SLC_BLOCK == np.arange(PAST_SEGS)[:, None])
    as_bf = lambda a: jnp.asarray(a.astype(np.float32)).astype(jnp.bfloat16)
    return (jnp.asarray(bmask), jnp.asarray(basec), jnp.asarray(bases), jnp.asarray(basew), jnp.asarray(slope),
            as_bf(grp), as_bf(ov), as_bf(expand))


def nsa_decode_pallas(qa, ga, kva, pool, page_table, win_buf, layer, w1, w2, pe_bias, win_into=None):
    bf = jnp.bfloat16
    B = qa.shape[0]
    depth, npool = pool.shape[:2]
    page_table = page_table + layer * npool
    poolT = pool.transpose(0, 1, 3, 4, 5, 2).reshape(depth * npool, ROW_W, PAGE_SIZE)
    winT = win_buf.transpose(0, 1, 3, 4, 5, 2).reshape(depth * B, 2 * KVW, WINDOW)
    w1r = w1.reshape(2, CMP_RATIO, CMP_STRIDE * HEAD_DIM, CMP_HIDDEN).transpose(0, 2, 1, 3)
    w1r = w1r.reshape(2, CMP_STRIDE * HEAD_DIM, CMP_RATIO * CMP_HIDDEN).astype(bf)
    w2bd = jnp.einsum('kl,ced->ckeld', jnp.eye(A_KV_HEADS, dtype=w2.dtype), w2)
    w2bd = w2bd.reshape(2, A_KV_HEADS * CMP_HIDDEN, KVW).astype(bf)
    consts = _decode_constants()
    full = lambda shape: pl.BlockSpec(shape, lambda b, pt: (0,) * len(shape))
    page_specs = [pl.BlockSpec((1, ROW_W, PAGE_SIZE), functools.partial(lambda p, b, pt: (pt[b, p], 0, 0), p))
                  for p in range(N_PAGES)]
    grid_spec = pltpu.PrefetchScalarGridSpec(
        num_scalar_prefetch=1,
        grid=(B,),
        in_specs=[
            pl.BlockSpec((1, A_HEADS, HEAD_DIM), lambda b, pt: (b, 0, 0)),
            pl.BlockSpec((1, A_HEADS, 3), lambda b, pt: (b, 0, 0)),
            pl.BlockSpec((1, 1, N_KV_BRANCH * KVW), lambda b, pt: (b, 0, 0)),
            *page_specs,
            pl.BlockSpec((1, 2 * KVW, WINDOW), lambda b, pt: (layer * B + b, 0, 0)),
            full(w1r.shape), full((2, 1, CMP_HIDDEN)), full(w2bd.shape),
            *[full(c.shape) for c in consts],
            *([] if win_into is None else [pl.BlockSpec(memory_space=pl.ANY)]),
        ],
        out_specs=[
            pl.BlockSpec((1, A_HEADS, HEAD_DIM), lambda b, pt: (b, 0, 0)),
            pl.BlockSpec((1, 2 * KVW, WINDOW), lambda b, pt: (layer * B + b, 0, 0)),
        ],
        scratch_shapes=[pltpu.VMEM((2 * PAST_SEGS * SEG_PITCH, 2 * HEAD_DIM), jnp.float32),
                        pltpu.VMEM((A_KV_HEADS * PAST_SEGS, CMP_STRIDE * HEAD_DIM), jnp.float32)],
    )
    args = (page_table, qa.reshape(B, A_HEADS, HEAD_DIM), ga.reshape(B, A_HEADS, 3), kva.reshape(B, 1, N_KV_BRANCH * KVW),
            *([poolT] * N_PAGES), winT, w1r, pe_bias.reshape(2, 1, CMP_HIDDEN), w2bd, *consts)
    aliases = {}
    if win_into is not None:
        args += (win_into,)
        aliases = {len(args) - 1: 1}
    o, wout = pl.pallas_call(
        functools.partial(_nsa_decode_kernel, has_into=win_into is not None),
        grid_spec=grid_spec,
        out_shape=(jax.ShapeDtypeStruct((B, A_HEADS, HEAD_DIM), jnp.float32),
                   jax.ShapeDtypeStruct((depth * B, 2 * KVW, WINDOW), jnp.float32)),
        input_output_aliases=aliases,
        compiler_params=_params("arbitrary"),
        name="nsa_decode",
    )(*args)
    return o.reshape(B, A_WIDTH), wout


MLSTM_L = 128
VAUG = 2 * V_DIM


def _mlstm_prompt_kernel(q_ref, k_ref, v_ref, o_ref, fsrc_ref, ifT_ref, y_ref, c_ref, m_ref, *, f_off):
    f32, bf = jnp.float32, jnp.bfloat16
    L = MLSTM_L
    f_ref = fsrc_ref.at[:, f_off:f_off + B_HEADS]
    iT_ref = ifT_ref.at[0:B_HEADS, :]
    fT_ref = ifT_ref.at[B_HEADS:2 * B_HEADS, :]

    @pl.when(pl.program_id(0) == 0)
    def _():
        c_ref[...] = jnp.zeros_like(c_ref)
        m_ref[...] = jnp.zeros_like(m_ref)

    kT_all = k_ref[...].T * (QK_DIM ** -0.5)
    rr = lax.broadcasted_iota(jnp.int32, (L, L), 0)
    cc = lax.broadcasted_iota(jnp.int32, (L, L), 1)
    lower = rr >= cc
    tril = jnp.where(lower, 1.0, 0.0).astype(bf)
    triu = jnp.where(rr <= cc, 1.0, 0.0).astype(bf)
    b_col = sum(jnp.dot(tril, p, preferred_element_type=f32) for p in _split3(jax.nn.log_sigmoid(f_ref[...])))
    b_row = _dot3(jax.nn.log_sigmoid(fT_ref[...]), triu)
    a_row = iT_ref[...] - b_row
    ones_col = jnp.where(lax.broadcasted_iota(jnp.int32, (L, V_DIM), 1) == 0, 1.0, 0.0)

    for h in range(B_HEADS):
        m_prev = m_ref[h, 0:1, 0:1]
        a = a_row[h:h + 1, :]
        amat = jnp.where(lower, jnp.broadcast_to(a, (L, L)), -jnp.inf)
        big_m = jnp.maximum(m_prev, jnp.max(amat, axis=1, keepdims=True))
        dmat = jnp.exp(amat - big_m)
        inter = jnp.exp(m_prev - big_m)
        q = q_ref[:, h * QK_DIM:(h + 1) * QK_DIM].astype(bf)
        kT = kT_all[h * QK_DIM:(h + 1) * QK_DIM, :]
        vaug = jnp.concatenate([v_ref[:, h * V_DIM:(h + 1) * V_DIM], ones_col], axis=1).astype(bf)
        s = jnp.dot(q, kT.astype(bf), preferred_element_type=f32) * dmat
        r = (jnp.dot(s.astype(bf), vaug, preferred_element_type=f32)
             + inter * jnp.dot(q, c_ref[h].astype(bf), preferred_element_type=f32))
        m_new = b_col[:, h:h + 1] + big_m
        den = jnp.maximum(jnp.abs(r[:, V_DIM:V_DIM + 1]), jnp.exp(-m_new))
        y_ref[:, h * V_DIM:(h + 1) * V_DIM] = (jax.nn.sigmoid(o_ref[:, h * V_DIM:(h + 1) * V_DIM])
                                               * (r[:, 0:V_DIM] / den))
        m_end = big_m[L - 1:L, :]
        w_end = jnp.exp(a - m_end)
        kw_end = kT * w_end
        c_new = inter[L - 1:L, :] * c_ref[h] + jnp.dot(kw_end.astype(bf), vaug, preferred_element_type=f32)
        n_new = inter[L - 1:L, :] * c_ref[h, :, V_DIM:V_DIM + 1] + jnp.sum(kw_end, axis=1, keepdims=True)
        c_ref[h] = jnp.where(lax.broadcasted_iota(jnp.int32, (QK_DIM, VAUG), 1) == V_DIM, n_new, c_new)
        m_ref[h] = jnp.broadcast_to(m_new[L - 1:L, :], m_ref.shape[1:])


def mlstm_prompt_pallas(q_src, k_src, v_src, o_src, f_src, ifT, T):
    L = MLSTM_L
    (q_arr, q_cb), (k_arr, k_cb), (v_arr, v_cb), (o_arr, o_cb), (f_arr, f_cb, f_off) = q_src, k_src, v_src, o_src, f_src
    y, caug, m = pl.pallas_call(
        functools.partial(_mlstm_prompt_kernel, f_off=f_off),
        grid=(T // L,),
        in_specs=[
            pl.BlockSpec((L, B_HEADS * QK_DIM), lambda c: (c, q_cb)),
            pl.BlockSpec((L, B_HEADS * QK_DIM), lambda c: (c, k_cb)),
            pl.BlockSpec((L, B_WIDTH), lambda c: (c, v_cb)),
            pl.BlockSpec((L, B_WIDTH), lambda c: (c, o_cb)),
            pl.BlockSpec((L, f_arr.shape[1] if f_arr.shape[1] < 128 else 128), lambda c: (c, f_cb)),
            pl.BlockSpec((2 * B_HEADS, L), lambda c: (0, c)),
        ],
        out_specs=[
            pl.BlockSpec((L, B_WIDTH), lambda c: (c, 0)),
            pl.BlockSpec((B_HEADS, QK_DIM, VAUG), lambda c: (0, 0, 0)),
            pl.BlockSpec((B_HEADS, 8, 128), lambda c: (0, 0, 0)),
        ],
        out_shape=(jax.ShapeDtypeStruct((T, B_WIDTH), jnp.float32),
                   jax.ShapeDtypeStruct((B_HEADS, QK_DIM, VAUG), jnp.float32),
                   jax.ShapeDtypeStruct((B_HEADS, 8, 128), jnp.float32)),
        compiler_params=_params("arbitrary"),
        name="mlstm_prompt",
    )(q_arr, k_arr, v_arr, o_arr, f_arr, ifT)
    return y, caug[:, :, 0:V_DIM], caug[:, :, V_DIM], m[:, 0, 0]


def _mlstm_sample_kernel(q_ref, k_ref, qT_ref, kT_ref, v_ref, o_ref, i_ref, f_ref, m_ref, n_ref, c_ref,
                         y_ref, cn_ref, nn_ref, mn_ref):
    bb = q_ref.shape[0]
    scale = QK_DIM ** -0.5
    for b in range(bb):
        logf = jax.nn.log_sigmoid(f_ref[b])
        m_new = jnp.maximum(logf + m_ref[b], i_ref[b])
        d_all = jnp.exp(i_ref[b] - m_new)
        inter_all = jnp.exp(logf + m_ref[b] - m_new)
        floor_all = jnp.exp(-m_new)
        mn_ref[b] = m_new
        qk_all = jnp.sum(q_ref[b] * k_ref[b], axis=1, keepdims=True) * scale
        qn_all = jnp.sum(q_ref[b] * n_ref[b], axis=1, keepdims=True)
        for h in range(B_HEADS):
            d = d_all[:, h:h + 1]
            inter = inter_all[:, h:h + 1]
            s = qk_all[h:h + 1, :] * d
            c = c_ref[b, h]
            v = v_ref[b, h:h + 1, :]
            qc = qT_ref[b, :, h:h + 1]
            kc = kT_ref[b, :, h:h + 1] * scale
            num = inter * jnp.sum(qc * c, axis=0, keepdims=True) + s * v
            den = inter * qn_all[h:h + 1, :] + s
            hout = num / jnp.maximum(jnp.abs(den), floor_all[:, h:h + 1])
            y_ref[b, h:h + 1, :] = jax.nn.sigmoid(o_ref[b, h:h + 1, :]) * hout
            cn_ref[b, h] = inter * c + d * (kc * v)
            nn_ref[b, h:h + 1, :] = inter * n_ref[b, h:h + 1, :] + d * (k_ref[b, h:h + 1, :] * scale)


def mlstm_sample_pallas(qb, kb, vb, ib, fb, ob, state_C, state_n, state_m, layer, *, bb=8):
    B = qb.shape[0]
    nb = B // bb
    q3 = qb.reshape(B, B_HEADS, QK_DIM)
    k3 = kb.reshape(B, B_HEADS, QK_DIM)
    row8 = lambda x: x.reshape(-1, 1, B_HEADS)
    lay = layer * nb
    y, cn, nn, mn = pl.pallas_call(
        _mlstm_sample_kernel,
        grid=(nb,),
        in_specs=[
            pl.BlockSpec((bb, B_HEADS, QK_DIM), lambda i: (i, 0, 0)),
            pl.BlockSpec((bb, B_HEADS, QK_DIM), lambda i: (i, 0, 0)),
            pl.BlockSpec((bb, QK_DIM, B_HEADS), lambda i: (i, 0, 0)),
            pl.BlockSpec((bb, QK_DIM, B_HEADS), lambda i: (i, 0, 0)),
            pl.BlockSpec((bb, B_HEADS, V_DIM), lambda i: (i, 0, 0)),
            pl.BlockSpec((bb, B_HEADS, V_DIM), lambda i: (i, 0, 0)),
            pl.BlockSpec((bb, 1, B_HEADS), lambda i: (i, 0, 0)),
            pl.BlockSpec((bb, 1, B_HEADS), lambda i: (i, 0, 0)),
            pl.BlockSpec((bb, 1, B_HEADS), lambda i: (lay + i, 0, 0)),
            pl.BlockSpec((bb, B_HEADS, QK_DIM), lambda i: (lay + i, 0, 0)),
            pl.BlockSpec((bb, B_HEADS, QK_DIM, V_DIM), lambda i: (lay + i, 0, 0, 0)),
        ],
        out_specs=[
            pl.BlockSpec((bb, B_HEADS, V_DIM), lambda i: (i, 0, 0)),
            pl.BlockSpec((bb, B_HEADS, QK_DIM, V_DIM), lambda i: (i, 0, 0, 0)),
            pl.BlockSpec((bb, B_HEADS, QK_DIM), lambda i: (i, 0, 0)),
            pl.BlockSpec((bb, 1, B_HEADS), lambda i: (i, 0, 0)),
        ],
        out_shape=(jax.ShapeDtypeStruct((B, B_HEADS, V_DIM), jnp.float32),
                   jax.ShapeDtypeStruct((B, B_HEADS, QK_DIM, V_DIM), jnp.float32),
                   jax.ShapeDtypeStruct((B, B_HEADS, QK_DIM), jnp.float32),
                   jax.ShapeDtypeStruct((B, 1, B_HEADS), jnp.float32)),
        compiler_params=_params("arbitrary"),
        name="mlstm_sample",
    )(q3, k3, q3.transpose(0, 2, 1), k3.transpose(0, 2, 1), vb.reshape(B, B_HEADS, V_DIM),
      ob.reshape(B, B_HEADS, V_DIM), row8(ib), row8(fb), row8(state_m),
      state_n.reshape(-1, B_HEADS, QK_DIM), state_C.reshape(-1, B_HEADS, QK_DIM, V_DIM))
    return y.reshape(B, B_WIDTH), cn, nn, mn.reshape(B, B_HEADS)


(G_QA, G_KVA, G_GA, G_QB, G_KB, G_VB, G_IB, G_FB, G_OB, G_GM) = range(10)
WIDE_ORDER = (G_QA, G_VB, G_OB, G_KVA, G_QB, G_KB, G_GM)
NARROW_ORDER = (G_GA, G_IB, G_FB)
WIDE_OFF = dict(zip(WIDE_ORDER, np.cumsum((0,) + tuple(SPLITS[g] for g in WIDE_ORDER))[:-1].tolist()))
NARROW_OFF = dict(zip(NARROW_ORDER, np.cumsum((0,) + tuple(SPLITS[g] for g in NARROW_ORDER))[:-1].tolist()))
NARROW_W = 128


def project_in(h, w_in, b_in):
    w_t = w_in.T

    def gather(order, pad_to):
        rows = jnp.concatenate([w_t[BOUNDS[g]:BOUNDS[g + 1]] for g in order], axis=0)
        bias = jnp.concatenate([b_in[BOUNDS[g]:BOUNDS[g + 1]] for g in order], axis=0)
        pad = pad_to - rows.shape[0]
        return jnp.pad(rows, ((0, pad), (0, 0))).astype(jnp.bfloat16), jnp.pad(bias, (0, pad))

    n_wide = sum(SPLITS[g] for g in WIDE_ORDER)
    wide = matmul_act(h, *gather(WIDE_ORDER, n_wide), nt=True)
    narrow = matmul_act(h, *gather(NARROW_ORDER, NARROW_W), nt=True, tn=NARROW_W)
    return wide, narrow


def kernel(x_prompt, x_sample, cache_nsa_kv, cache_win_kv, state_C, state_n, state_m, page_table,
           norm_g, w_in, b_in, cmp_pe, cmp_w1, cmp_b1, cmp_w2, w_up_a, w_up_b, w_out, w_mlp1, w_mlp2):
    Bp, Tp = x_prompt.shape[:2]
    Bs, Ts = x_sample.shape[:2]
    assert Bp == 1 and Ts == 1 and Tp >= WINDOW, (x_prompt.shape, x_sample.shape)
    x = jnp.concatenate([x_prompt.reshape(Tp, D_MODEL), x_sample.reshape(Bs, D_MODEL)], axis=0)
    kv_p, kv_s, win_p, win_all = [], [], [], None
    C_p, C_s, n_p, n_s, m_p, m_s = [], [], [], [], [], []
    kv4 = 4 * KVW
    for l in range(DEPTH):
        wide, narrow = project_in(rmsnorm_cast(x, norm_g[l, 0]), w_in[l], b_in[l])
        col = lambda g, rows: wide[rows, WIDE_OFF[g]:WIDE_OFF[g] + SPLITS[g]]
        ncol = lambda g, rows: narrow[rows, NARROW_OFF[g]:NARROW_OFF[g] + SPLITS[g]]
        pr, sr = slice(0, Tp), slice(Tp, Tp + Bs)

        kva = col(G_KVA, pr)
        ya_p, pe_bias = nsa_prompt_pallas((wide, WIDE_OFF[G_QA]), ncol(G_GA, pr), kva, cmp_pe[l], cmp_w1[l], cmp_b1[l],
                                          cmp_w2[l])
        yb_p, C, n, m = mlstm_prompt_pallas(
            (wide, WIDE_OFF[G_QB] // (B_HEADS * QK_DIM)), (wide, WIDE_OFF[G_KB] // (B_HEADS * QK_DIM)),
            (wide, WIDE_OFF[G_VB] // B_WIDTH),
            (wide, WIDE_OFF[G_OB] // B_WIDTH), (narrow, 0, NARROW_OFF[G_FB]),
            narrow[pr, NARROW_OFF[G_IB]:NARROW_OFF[G_IB] + 2 * B_HEADS].T, Tp)
        kv_p.append(kva[:, 0:kv4].reshape(Bp, Tp, 4, A_KV_HEADS, HEAD_DIM))
        win_p.append(kva[Tp - WINDOW:, kv4:].reshape(Bp, WINDOW, 2, A_KV_HEADS, HEAD_DIM))
        C_p.append(C[None])
        n_p.append(n[None])
        m_p.append(m[None])

        kva = col(G_KVA, sr)
        ya_s, win_all = nsa_decode_pallas(col(G_QA, sr), ncol(G_GA, sr), kva, cache_nsa_kv, page_table, cache_win_kv, l,
                                          cmp_w1[l], cmp_w2[l], pe_bias, win_into=win_all)
        yb_s, C, n, m = mlstm_sample_pallas(col(G_QB, sr), col(G_KB, sr), col(G_VB, sr), ncol(G_IB, sr), ncol(G_FB, sr),
                                            col(G_OB, sr), state_C, state_n, state_m, l)
        kv_s.append(kva[:, 0:kv4].reshape(Bs, Ts, 4, A_KV_HEADS, HEAD_DIM))
        C_s.append(C)
        n_s.append(n)
        m_s.append(m)

        gm_src = (wide, WIDE_OFF[G_GM])
        mix = mix_matmul(ya_p, yb_p, w_up_a[l], w_up_b[l], gm_src, row0=0, rows_total=Tp + Bs)
        mix = mix_matmul(ya_s, yb_s, w_up_a[l], w_up_b[l], gm_src, row0=Tp, rows_total=Tp + Bs, into=mix)
        x = matmul_norm_res(mix, w_out[l].astype(jnp.bfloat16), x, norm_g[l, 1])
        hid = matmul_act(rmsnorm_cast(x, norm_g[l, 2]), w_mlp1[l], jnp.zeros((D_FF,), jnp.float32), nt=False,
                         act="relu2", out_dtype=jnp.bfloat16)
        x = matmul_norm_res(hid, w_mlp2[l].astype(jnp.bfloat16), x, norm_g[l, 3])

    win_s = win_all.reshape(DEPTH, Bs, 2, A_KV_HEADS, HEAD_DIM, WINDOW).transpose(0, 1, 5, 2, 3, 4)
    return (x[:Tp].reshape(Bp, Tp, D_MODEL), x[Tp:].reshape(Bs, Ts, D_MODEL),
            jnp.stack(kv_p), jnp.stack(kv_s), jnp.stack(win_p), win_s,
            jnp.stack(C_p), jnp.stack(C_s), jnp.stack(n_p), jnp.stack(n_s), jnp.stack(m_p), jnp.stack(m_s))
```

```python
import functools
import math

import jax
import jax.numpy as jnp
import numpy as np
from jax import lax
from jax.experimental import pallas as pl
from jax.experimental.pallas import tpu as pltpu

D_MODEL = 2048
DEPTH = 2
PAST_LEN = 2048
A_HEADS = 16
A_KV_HEADS = 4
A_GROUP = A_HEADS // A_KV_HEADS
HEAD_DIM = 64
A_WIDTH = A_HEADS * HEAD_DIM
CMP_BLOCK = 32
CMP_STRIDE = 16
CMP_RATIO = CMP_BLOCK // CMP_STRIDE
CMP_HIDDEN = 128
SLC_BLOCK = 64
SLC_TOP = 16
WINDOW = 512
Q_BLOCK = 128
N_KV_BRANCH = 6
B_HEADS = 8
QK_DIM = 64
V_DIM = 128
B_WIDTH = B_HEADS * V_DIM
MLSTM_CHUNK = 64
D_FF = 4 * D_MODEL
EPS = 1e-6
FORCE = 1e4
NEG = -1e30

SPLITS = (A_WIDTH, N_KV_BRANCH * A_KV_HEADS * HEAD_DIM, 3 * A_HEADS,
          B_HEADS * QK_DIM, B_HEADS * QK_DIM, B_WIDTH, B_HEADS, B_HEADS, B_WIDTH, 2 * D_MODEL)
BOUNDS = tuple(int(b) for b in np.cumsum((0,) + SPLITS))

VMEM_LIMIT_BYTES = 56 * 1024 * 1024


def _params(*sem):
    return pltpu.CompilerParams(dimension_semantics=sem, vmem_limit_bytes=VMEM_LIMIT_BYTES)


def _row_tile(m, want):
    for t in range(min(want, m), 15, -1):
        if m % t == 0 and t % 16 == 0:
            return t
    return m


def _rmsnorm_kernel(x_ref, g_ref, o_ref):
    x = x_ref[...]
    r = lax.rsqrt(jnp.mean(x * x, axis=-1, keepdims=True) + EPS)
    o_ref[...] = (x * r * g_ref[...]).astype(o_ref.dtype)


def rmsnorm_cast(x, g, *, tm=640):
    M, K = x.shape
    tm = _row_tile(M, tm)
    return pl.pallas_call(
        _rmsnorm_kernel,
        grid=(M // tm,),
        in_specs=[pl.BlockSpec((tm, K), lambda i: (i, 0)), pl.BlockSpec((1, K), lambda i: (0, 0))],
        out_specs=pl.BlockSpec((tm, K), lambda i: (i, 0)),
        out_shape=jax.ShapeDtypeStruct((M, K), jnp.bfloat16),
        compiler_params=_params("parallel"),
        name="rmsnorm_cast",
    )(x, g.reshape(1, K))


def _matmul_act_kernel(h_ref, w_ref, b_ref, o_ref, *, nt, act):
    w = w_ref[...].astype(jnp.bfloat16)
    z = (_nt(h_ref[...], w) if nt else jnp.dot(h_ref[...], w, preferred_element_type=jnp.float32)) + b_ref[...]
    if act == "relu2":
        z = jnp.square(jnp.maximum(z, 0.0))
    o_ref[...] = z.astype(o_ref.dtype)


def matmul_act(h, w, b, *, nt, act=None, out_dtype=jnp.float32, tm=1664, tn=512):
    M, K = h.shape
    N = w.shape[0] if nt else w.shape[1]
    tm = _row_tile(M, tm)
    tn = min(tn, N)
    assert N % tn == 0, (N, tn)
    w_spec = pl.BlockSpec((tn, K), lambda i, j: (j, 0)) if nt else pl.BlockSpec((K, tn), lambda i, j: (0, j))
    return pl.pallas_call(
        functools.partial(_matmul_act_kernel, nt=nt, act=act),
        grid=(M // tm, N // tn),
        in_specs=[pl.BlockSpec((tm, K), lambda i, j: (i, 0)), w_spec, pl.BlockSpec((1, tn), lambda i, j: (0, j))],
        out_specs=pl.BlockSpec((tm, tn), lambda i, j: (i, j)),
        out_shape=jax.ShapeDtypeStruct((M, N), out_dtype),
        compiler_params=_params("parallel", "arbitrary"),
        name="matmul_act",
    )(h, w, b.reshape(1, N))


def _mix_kernel(ya_ref, yb_ref, wa_ref, wb_ref, ga_ref, gb_ref, o_ref):
    bf = jnp.bfloat16
    a = jnp.dot(ya_ref[...].astype(bf), wa_ref[...].astype(bf), preferred_element_type=jnp.float32)
    b = jnp.dot(yb_ref[...].astype(bf), wb_ref[...].astype(bf), preferred_element_type=jnp.float32)
    o_ref[...] = (jax.nn.sigmoid(ga_ref[...]) * a + jax.nn.sigmoid(gb_ref[...]) * b).astype(o_ref.dtype)


def _mix_into_kernel(ya_ref, yb_ref, wa_ref, wb_ref, ga_ref, gb_ref, prev_ref, o_ref):
    del prev_ref
    _mix_kernel(ya_ref, yb_ref, wa_ref, wb_ref, ga_ref, gb_ref, o_ref)


def mix_matmul(ya, yb, wa, wb, gm_src, *, row0, rows_total, into=None, tm=1024, tn=512):
    M, Ka = ya.shape
    Kb = yb.shape[1]
    N = wa.shape[1]
    gm, off = gm_src
    tm = _row_tile(M, tm)
    nb = N // tn
    assert off % tn == 0 and N % tn == 0 and row0 % tm == 0, (off, N, tn, row0, tm)
    ja, jb, i0 = off // tn, off // tn + nb, row0 // tm
    in_specs = [
        pl.BlockSpec((tm, Ka), lambda i, j: (i, 0)),
        pl.BlockSpec((tm, Kb), lambda i, j: (i, 0)),
        pl.BlockSpec((Ka, tn), lambda i, j: (0, j)),
        pl.BlockSpec((Kb, tn), lambda i, j: (0, j)),
        pl.BlockSpec((tm, tn), lambda i, j: (i0 + i, ja + j)),
        pl.BlockSpec((tm, tn), lambda i, j: (i0 + i, jb + j)),
    ]
    args = (ya, yb, wa, wb, gm, gm)
    aliases = {}
    if into is not None:
        in_specs.append(pl.BlockSpec(memory_space=pl.ANY))
        args += (into,)
        aliases = {len(args) - 1: 0}
    return pl.pallas_call(
        _mix_kernel if into is None else _mix_into_kernel,
        grid=(M // tm, nb),
        in_specs=in_specs,
        out_specs=pl.BlockSpec((tm, tn), lambda i, j: (i0 + i, j)),
        out_shape=jax.ShapeDtypeStruct((rows_total, N), jnp.bfloat16),
        input_output_aliases=aliases,
        compiler_params=_params("parallel", "arbitrary"),
        name="mix_matmul",
    )(*args)


def _matmul_norm_res_kernel(a_ref, w_ref, x_ref, g_ref, o_ref, acc_ref):
    k = pl.program_id(1)

    @pl.when(k == 0)
    def _():
        acc_ref[...] = jnp.zeros_like(acc_ref)

    a = a_ref[...]
    for n0 in range(0, acc_ref.shape[1], NORM_RES_COLS):
        cols = slice(n0, n0 + NORM_RES_COLS)
        acc_ref[:, cols] += jnp.dot(a, w_ref[:, cols].astype(jnp.bfloat16), preferred_element_type=jnp.float32)

    @pl.when(k == pl.num_programs(1) - 1)
    def _():
        y = acc_ref[...]
        r = lax.rsqrt(jnp.mean(y * y, axis=-1, keepdims=True) + EPS)
        o_ref[...] = x_ref[...] + y * r * g_ref[...]


NORM_RES_COLS = 512


def matmul_norm_res(a, w, x, g, *, tm=640, tk=1024):
    M, K = a.shape
    N = w.shape[1]
    tm = _row_tile(M, tm)
    tk = min(tk, K)
    return pl.pallas_call(
        _matmul_norm_res_kernel,
        grid=(M // tm, K // tk),
        in_specs=[
            pl.BlockSpec((tm, tk), lambda i, k: (i, k)),
            pl.BlockSpec((tk, N), lambda i, k: (k, 0)),
            pl.BlockSpec((tm, N), lambda i, k: (i, 0)),
            pl.BlockSpec((1, N), lambda i, k: (0, 0)),
        ],
        out_specs=pl.BlockSpec((tm, N), lambda i, k: (i, 0)),
        out_shape=jax.ShapeDtypeStruct((M, N), jnp.float32),
        scratch_shapes=[pltpu.VMEM((tm, N), jnp.float32)],
        compiler_params=_params("parallel", "arbitrary"),
        name="matmul_norm_res",
    )(a, w, x, g.reshape(1, N))


NSA_LANES = A_GROUP * Q_BLOCK
KEY_CHUNK = 512
WIN_CHUNK = 128
WIN_SPAN = WINDOW + Q_BLOCK
SLC_SLOTS = 128
POS_PIECES = 6
AUG = 128
MASK_BIG = 30000.0


def _pos_pieces(pos):
    pos = np.asarray(pos, np.int64)
    hi = (pos // 64) * 64
    lo = pos % 64
    return np.stack([hi, lo] * 3, axis=-1).astype(np.float32)


def _slope_rows():
    slopes = 2.0 ** (-8.0 * np.arange(1, A_HEADS + 1, dtype=np.float64) / A_HEADS)
    s = jnp.asarray(slopes.astype(np.float32))
    hi = s.astype(jnp.bfloat16)
    r1 = s - hi.astype(jnp.float32)
    mid = r1.astype(jnp.bfloat16)
    lo = (r1 - mid.astype(jnp.float32)).astype(jnp.bfloat16)
    rows = jnp.stack([hi, hi, mid, mid, lo, lo], axis=0)
    rows = rows.reshape(POS_PIECES, A_KV_HEADS, A_GROUP).transpose(1, 0, 2)
    rows = jnp.repeat(rows, Q_BLOCK, axis=-1)
    return jnp.pad(rows, ((0, 0), (0, 16 - POS_PIECES), (0, 0)))


def _split3(x):
    hi = x.astype(jnp.bfloat16)
    r1 = x - hi.astype(jnp.float32)
    mid = r1.astype(jnp.bfloat16)
    lo = (r1 - mid.astype(jnp.float32)).astype(jnp.bfloat16)
    return hi, mid, lo


def _tile4(x):
    return jnp.concatenate([x] * A_GROUP, axis=1)


def _softmax_step(carry, s, vT):
    m, l, acc = carry
    m_new = jnp.maximum(m, jnp.max(s, axis=0, keepdims=True))
    alpha = jnp.exp(m - m_new)
    p = jnp.exp(s - m_new)
    l = alpha * l + jnp.sum(p, axis=0, keepdims=True)
    acc = alpha * acc + jnp.dot(vT, p.astype(jnp.bfloat16), preferred_element_type=jnp.float32)
    return m_new, l, acc


def _nsa_prompt_kernel(q_ref, slope_ref, gT_ref, ck_ref, cvT_ref, ovT_ref, ks_ref, vsT_ref,
                       kw_ref, vwT_ref, o_ref, qa_ref, flag_ref):
    f32, bf = jnp.float32, jnp.bfloat16
    i = pl.program_id(1)
    s0 = i * Q_BLOCK
    ncp = ck_ref.shape[1]

    qT = q_ref[...].T * (HEAD_DIM ** -0.5)
    qcat = jnp.concatenate([qT[g * HEAD_DIM:(g + 1) * HEAD_DIM, :] for g in range(A_GROUP)], axis=1)
    qa_ref[0:HEAD_DIM, :] = qcat.astype(bf)
    qa_ref[HEAD_DIM:HEAD_DIM + 16, :] = slope_ref[0]
    qa_ref[HEAD_DIM + 16:AUG, :] = jnp.zeros((AUG - HEAD_DIM - 16, NSA_LANES), bf)
    qc = qa_ref[0:AUG, :]


    def compressed_and_select(nc, ns):
        sc = jnp.dot(ck_ref[0, 0:nc, :], qc, preferred_element_type=f32)
        n_idx = lax.broadcasted_iota(jnp.int32, (nc, Q_BLOCK), 0)
        t_idx = s0 + lax.broadcasted_iota(jnp.int32, (nc, Q_BLOCK), 1)
        valid = _tile4(n_idx * CMP_STRIDE + (CMP_BLOCK - 1) <= t_idx)
        sc = jnp.where(valid, sc, NEG)
        ec = jnp.where(valid, jnp.exp(sc - jnp.max(sc, axis=0, keepdims=True)), 0.0)
        lc = jnp.sum(ec, axis=0, keepdims=True)
        pc = ec * jnp.where(lc > 0.0, 1.0 / lc, 0.0)
        o_c = jnp.dot(cvT_ref[0, :, 0:nc], pc.astype(bf), preferred_element_type=f32)

        pg = pc[:, 0:Q_BLOCK]
        for g in range(1, A_GROUP):
            pg = pg + pc[:, g * Q_BLOCK:(g + 1) * Q_BLOCK]
        imp = sum(jnp.dot(ovT_ref[0:ns, 0:nc], piece, preferred_element_type=f32) for piece in _split3(pg))

        blk = lax.broadcasted_iota(jnp.int32, (ns, Q_BLOCK), 0)
        cur = (s0 + lax.broadcasted_iota(jnp.int32, (ns, Q_BLOCK), 1)) // SLC_BLOCK
        forced = (blk == 0) | (blk == cur) | (blk == cur - 1)
        causal = blk <= cur
        score = jnp.where(forced, FORCE, jnp.where(causal, imp, -FORCE))
        blk_f = blk.astype(f32)
        sel = jnp.zeros((ns, Q_BLOCK), f32)
        for _ in range(SLC_TOP):
            top = jnp.max(score, axis=0, keepdims=True)
            first = jnp.min(jnp.where(score == top, blk_f, float(SLC_SLOTS)), axis=0, keepdims=True)
            pick = blk_f == first
            sel = jnp.where(pick, 1.0, sel)
            score = jnp.where(pick, -jnp.inf, score)
        picked = jnp.where((sel > 0.0) & causal, 1.0, 0.0)
        if ns < SLC_SLOTS:
            picked = jnp.concatenate([picked, jnp.zeros((SLC_SLOTS - ns, Q_BLOCK), f32)], axis=0)
        return o_c, picked

    nqb = ks_ref.shape[1] // Q_BLOCK
    if ncp % 256 == 0 and nqb % 8 == 0 and nqb < SLC_SLOTS:
        o_c, picked = lax.cond(i < nqb // 2, lambda: compressed_and_select(ncp // 2, nqb),
                               lambda: compressed_and_select(ncp, SLC_SLOTS))
    else:
        o_c, picked = compressed_and_select(ncp, SLC_SLOTS)
    qa_ref[AUG:AUG + SLC_SLOTS, :] = _tile4(jnp.where(picked > 0.0, 0.0, -MASK_BIG)).astype(bf)

    def sel_scores(c):
        kc = ks_ref[0, pl.ds(pl.multiple_of(c * KEY_CHUNK, KEY_CHUNK), KEY_CHUNK), :]
        return jnp.dot(kc, qa_ref[...], preferred_element_type=f32)

    init = (jnp.full((1, NSA_LANES), NEG, f32), jnp.zeros((1, NSA_LANES), f32),
            jnp.zeros((HEAD_DIM, NSA_LANES), f32))
    c_last = (i * Q_BLOCK) // KEY_CHUNK
    bpc = KEY_CHUNK // SLC_BLOCK
    for c in range(SLC_SLOTS // bpc):
        flag_ref[c] = (jnp.max(picked[c * bpc:(c + 1) * bpc, :]) > 0.0).astype(jnp.int32)

    def sel_chunk(c, cr):
        return lax.cond(flag_ref[c] > 0, lambda x: _softmax_step(x, sel_scores(c), vsT_ref[0, c]), lambda x: x, cr)

    carry = lax.fori_loop(0, c_last, sel_chunk, init)
    j_last = c_last * KEY_CHUNK + lax.broadcasted_iota(jnp.int32, (KEY_CHUNK, Q_BLOCK), 0)
    t_last = s0 + lax.broadcasted_iota(jnp.int32, (KEY_CHUNK, Q_BLOCK), 1)
    m_s, l_s, acc_s = _softmax_step(carry, jnp.where(_tile4(j_last <= t_last), sel_scores(c_last), NEG),
                                    vsT_ref[0, c_last])
    o_s = acc_s * (1.0 / l_s)

    w0 = jnp.maximum(i - WINDOW // WIN_CHUNK, 0)
    kw = kw_ref[0, pl.ds(pl.multiple_of(w0 * WIN_CHUNK, WIN_CHUNK), WIN_SPAN), :]
    sw = jnp.dot(kw, qc, preferred_element_type=f32)
    jw = w0 * WIN_CHUNK + lax.broadcasted_iota(jnp.int32, (WIN_SPAN, Q_BLOCK), 0)
    tw = s0 + lax.broadcasted_iota(jnp.int32, (WIN_SPAN, Q_BLOCK), 1)
    sw = jnp.where(_tile4((jw <= tw) & (jw > tw - WINDOW)), sw, NEG)
    pw = jnp.exp(sw - jnp.max(sw, axis=0, keepdims=True))
    l_w = jnp.sum(pw, axis=0, keepdims=True)
    pw = pw.astype(bf)
    acc_w = jnp.zeros((HEAD_DIM, NSA_LANES), f32)
    for d in range(WIN_SPAN // WIN_CHUNK):
        acc_w = acc_w + jnp.dot(vwT_ref[0, w0 + d], pw[d * WIN_CHUNK:(d + 1) * WIN_CHUNK, :], preferred_element_type=f32)
    o_w = acc_w * (1.0 / l_w)

    gate = jax.nn.sigmoid(gT_ref[0])
    outs = []
    for g in range(A_GROUP):
        cols = slice(g * Q_BLOCK, (g + 1) * Q_BLOCK)
        outs.append(gate[3 * g:3 * g + 1, :] * o_c[:, cols] + gate[3 * g + 1:3 * g + 2, :] * o_s[:, cols]
                    + gate[3 * g + 2:3 * g + 3, :] * o_w[:, cols])
    o_ref[...] = jnp.concatenate(outs, axis=0).T


def _cmp_prompt_kernel(x_ref, pe_ref, w1_ref, b1_ref, w2T_ref, ck_ref, cvT_ref, bias_ref):
    f32, bf = jnp.float32, jnp.bfloat16
    nseg = ck_ref.shape[1]
    row_w = 2 * A_KV_HEADS * HEAD_DIM
    for c in range(2):
        bias = jnp.dot(pe_ref[c], w1_ref[c], preferred_element_type=f32,
                       precision=lax.Precision.HIGHEST) + b1_ref[c]
        bias_ref[c] = bias
        for k in range(A_KV_HEADS):
            col = (c * A_KV_HEADS + k) * HEAD_DIM
            y0 = jnp.zeros((nseg, CMP_HIDDEN), f32)
            y1 = jnp.zeros((nseg, CMP_HIDDEN), f32)
            for j in range(CMP_STRIDE):
                xj = x_ref[:, j * row_w + col:j * row_w + col + HEAD_DIM].astype(bf)
                w0 = w1_ref[c, j * HEAD_DIM:(j + 1) * HEAD_DIM, :].astype(bf)
                w1 = w1_ref[c, (CMP_STRIDE + j) * HEAD_DIM:(CMP_STRIDE + j + 1) * HEAD_DIM, :].astype(bf)
                y0 = y0 + jnp.dot(xj, w0, preferred_element_type=f32)
                y1 = y1 + jnp.dot(xj, w1, preferred_element_type=f32)
            pre = y0 + pltpu.roll(y1, nseg - 1, 0)
            hid = jnp.maximum(pre + bias, 0.0).astype(bf)
            if c == 0:
                ck_ref[k] = lax.dot_general(hid, w2T_ref[c].astype(bf), (((1,), (1,)), ((), ())),
                                            preferred_element_type=f32)
            else:
                cvT_ref[k] = lax.dot_general(w2T_ref[c].astype(bf), hid, (((1,), (1,)), ((), ())),
                                             preferred_element_type=f32)


def compress_prompt(x_cmp, pe, w1, b1, w2):
    T = x_cmp.shape[0]
    nseg = T // CMP_STRIDE
    return pl.pallas_call(
        _cmp_prompt_kernel,
        out_shape=(jax.ShapeDtypeStruct((A_KV_HEADS, nseg, HEAD_DIM), jnp.float32),
                   jax.ShapeDtypeStruct((A_KV_HEADS, HEAD_DIM, nseg), jnp.float32),
                   jax.ShapeDtypeStruct((2, 1, CMP_HIDDEN), jnp.float32)),
        compiler_params=pltpu.CompilerParams(vmem_limit_bytes=VMEM_LIMIT_BYTES),
        name="compress_prompt",
    )(x_cmp.reshape(nseg, CMP_STRIDE * x_cmp.shape[1]), pe.reshape(2, 1, CMP_BLOCK * HEAD_DIM), w1, b1.reshape(2, 1, CMP_HIDDEN), w2.transpose(0, 2, 1))


def nsa_prompt_pallas(q_src, ga, kva, pe, w1, b1, w2):
    bf = jnp.bfloat16
    q_arr, q_off = q_src
    q_cb = q_off // (A_GROUP * HEAD_DIM)
    T = kva.shape[0]
    nqb = T // Q_BLOCK
    assert T % KEY_CHUNK == 0 and T >= WIN_SPAN, T
    ncp = T // CMP_STRIDE
    kvw = A_KV_HEADS * HEAD_DIM
    ck, cvT, pe_bias = compress_prompt(kva[:, 0:2 * kvw], pe, w1, b1, w2)

    def heads(x):
        return x.reshape(T, A_KV_HEADS, HEAD_DIM).transpose(1, 0, 2)

    def chunksT(x, chunk):
        return x.reshape(T // chunk, chunk, A_KV_HEADS, HEAD_DIM).transpose(2, 0, 3, 1).astype(bf)

    pos = np.arange(T)
    key_pos = jnp.broadcast_to(jnp.asarray(_pos_pieces(pos)), (A_KV_HEADS, T, POS_PIECES))
    zpad = jnp.zeros((A_KV_HEADS, T, AUG - HEAD_DIM - POS_PIECES), jnp.float32)
    onehot = jnp.broadcast_to(jnp.asarray((pos[:, None] // SLC_BLOCK == np.arange(SLC_SLOTS)[None, :]).astype(np.float32)),
                              (A_KV_HEADS, T, SLC_SLOTS))
    ks_aug = jnp.concatenate([heads(kva[:, 2 * kvw:3 * kvw]), key_pos, zpad, onehot], axis=-1).astype(bf)
    kw_aug = jnp.concatenate([heads(kva[:, 4 * kvw:5 * kvw]), key_pos, zpad], axis=-1).astype(bf)
    c_end = np.arange(ncp) * CMP_STRIDE + (CMP_BLOCK - 1)
    ck_aug = jnp.concatenate([ck, jnp.broadcast_to(jnp.asarray(_pos_pieces(c_end)), (A_KV_HEADS, ncp, POS_PIECES)),
                              jnp.zeros((A_KV_HEADS, ncp, AUG - HEAD_DIM - POS_PIECES), jnp.float32)], axis=-1).astype(bf)
    vsT = chunksT(kva[:, 3 * kvw:4 * kvw], KEY_CHUNK)
    vwT = chunksT(kva[:, 5 * kvw:6 * kvw], WIN_CHUNK)
    c_start = c_end - (CMP_BLOCK - 1)
    s_start = np.arange(SLC_SLOTS) * SLC_BLOCK
    ovT = ((c_start[None, :] < s_start[:, None] + SLC_BLOCK) & (c_end[None, :] >= s_start[:, None])
           & (np.arange(ncp)[None, :] < ncp - 1))
    ovT = jnp.asarray(ovT.astype(np.float32)).astype(bf)
    gT = jnp.pad(ga.T.reshape(A_KV_HEADS, 3 * A_GROUP, T), ((0, 0), (0, 16 - 3 * A_GROUP), (0, 0)))

    y = pl.pallas_call(
        _nsa_prompt_kernel,
        grid=(A_KV_HEADS, nqb),
        in_specs=[
            pl.BlockSpec((Q_BLOCK, A_GROUP * HEAD_DIM), lambda k, i: (i, q_cb + k)),
            pl.BlockSpec((1, 16, NSA_LANES), lambda k, i: (k, 0, 0)),
            pl.BlockSpec((1, 16, Q_BLOCK), lambda k, i: (k, 0, i)),
            pl.BlockSpec((1, ncp, AUG), lambda k, i: (k, 0, 0)),
            pl.BlockSpec((1, HEAD_DIM, ncp), lambda k, i: (k, 0, 0)),
            pl.BlockSpec((SLC_SLOTS, ncp), lambda k, i: (0, 0)),
            pl.BlockSpec((1, T, AUG + SLC_SLOTS), lambda k, i: (k, 0, 0)),
            pl.BlockSpec((1, T // KEY_CHUNK, HEAD_DIM, KEY_CHUNK), lambda k, i: (k, 0, 0, 0)),
            pl.BlockSpec((1, T, AUG), lambda k, i: (k, 0, 0)),
            pl.BlockSpec((1, T // WIN_CHUNK, HEAD_DIM, WIN_CHUNK), lambda k, i: (k, 0, 0, 0)),
        ],
        out_specs=pl.BlockSpec((Q_BLOCK, A_GROUP * HEAD_DIM), lambda k, i: (i, k)),
        out_shape=jax.ShapeDtypeStruct((T, A_WIDTH), jnp.float32),
        scratch_shapes=[pltpu.VMEM((AUG + SLC_SLOTS, NSA_LANES), bf),
                        pltpu.SMEM((SLC_SLOTS * SLC_BLOCK // KEY_CHUNK,), jnp.int32)],
        compiler_params=_params("parallel", "arbitrary"),
        name="nsa_prompt",
    )(q_arr, _slope_rows(), gT, ck_aug, cvT.astype(bf), ovT, ks_aug, vsT, kw_aug, vwT)
    return y, pe_bias


PAGE_SIZE = 128
N_PAGES = PAST_LEN // PAGE_SIZE
PAGE_SEGS = PAGE_SIZE // CMP_STRIDE
PAST_SEGS = PAST_LEN // CMP_STRIDE
ROW_W = 4 * A_KV_HEADS * HEAD_DIM
KVW = A_KV_HEADS * HEAD_DIM
NS_SAMPLE = PAST_LEN // SLC_BLOCK + 1


def _nt(a, b):
    return lax.dot_general(a, b, (((1,), (1,)), ((), ())), preferred_element_type=jnp.float32)


def _dot3(x, w):
    return sum(jnp.dot(p, w, preferred_element_type=jnp.float32) for p in _split3(x))


def _nsa_sample_kernel(pt_ref, q_ref, g_ref, new_ref, *refs):
    del pt_ref
    pages = refs[:N_PAGES]
    (win_ref, w1_ref, bias_ref, w2bd_ref, bmask_ref, basec_ref, bases_ref, basew_ref, slope_ref,
     grp_ref, ov_ref, exp_ref, o_ref, wout_ref, xc_ref) = refs[N_PAGES:]
    f32, bf = jnp.float32, jnp.bfloat16

    q = q_ref[0] * (HEAD_DIM ** -0.5)
    qbd = jnp.where(bmask_ref[...] > 0.0, jnp.concatenate([q] * A_KV_HEADS, axis=1), 0.0).astype(bf)
    qbd_f = qbd.astype(f32)
    new = new_ref[0]

    def slab(off, width):
        return jnp.concatenate([pg[0, :, off:off + width] for pg in pages], axis=0)

    ckv = []
    for c in range(2):
        for j in range(CMP_STRIDE):
            s = slab(j * ROW_W + c * KVW, KVW).astype(bf)
            for k in range(A_KV_HEADS):
                xc_ref[k * PAST_SEGS:(k + 1) * PAST_SEGS, j * HEAD_DIM:(j + 1) * HEAD_DIM] = (
                    s[:, k * HEAD_DIM:(k + 1) * HEAD_DIM])
        y = jnp.dot(xc_ref[...], w1_ref[c], preferred_element_type=f32)
        pre = y[:, 0:CMP_HIDDEN] + pltpu.roll(y[:, CMP_HIDDEN:2 * CMP_HIDDEN], A_KV_HEADS * PAST_SEGS - 1, 0)
        hid = jnp.maximum(pre + bias_ref[c], 0.0).astype(bf)
        hid = jnp.concatenate([hid[k * PAST_SEGS:(k + 1) * PAST_SEGS] for k in range(A_KV_HEADS)], axis=1)
        ckv.append(jnp.dot(hid, w2bd_ref[c], preferred_element_type=f32).astype(bf))
    ck, cv = ckv

    sc = _nt(qbd, ck) + basec_ref[...]
    ec = jnp.exp(sc - jnp.max(sc, axis=1, keepdims=True))
    pc = ec / jnp.sum(ec, axis=1, keepdims=True)
    o_c = jnp.dot(pc.astype(bf), cv, preferred_element_type=f32)

    pov = _dot3(pc, ov_ref[...])
    p_hi, p_mid, p_lo = _split3(pov)
    imp = sum(jnp.dot(grp_ref[...], p, preferred_element_type=f32) for p in (p_hi, p_mid, p_lo))

    lane = lax.broadcasted_iota(jnp.int32, (A_HEADS, PAST_SEGS), 1)
    score = jnp.where((lane == 0) | (lane == NS_SAMPLE - 1) | (lane == NS_SAMPLE - 2), FORCE,
                      jnp.where(lane < NS_SAMPLE, imp, -jnp.inf))
    rr = lax.broadcasted_iota(jnp.int32, (PAST_SEGS, PAST_SEGS), 0)
    cc = lax.broadcasted_iota(jnp.int32, (PAST_SEGS, PAST_SEGS), 1)
    row_id = lax.broadcasted_iota(jnp.int32, (A_HEADS, PAST_SEGS), 0)
    selrows = jnp.zeros((A_HEADS, PAST_SEGS), f32)
    for k in range(A_KV_HEADS):
        row = jnp.broadcast_to(score[A_GROUP * k:A_GROUP * k + 1, :], (PAST_SEGS, PAST_SEGS))
        col = jnp.max(jnp.where(rr == cc, row, -jnp.inf), axis=1, keepdims=True)
        ahead = (col > row) | ((col == row) & (rr < cc))
        rank = jnp.sum(jnp.where(ahead, 1.0, 0.0), axis=0, keepdims=True)
        sel_k = jnp.where(rank < float(SLC_TOP), 1.0, 0.0)
        selrows = jnp.where(row_id // A_GROUP == k, jnp.broadcast_to(sel_k, (A_HEADS, PAST_SEGS)), selrows)
    selseg = jnp.dot(selrows.astype(bf), exp_ref[...], preferred_element_type=f32)
    maskbias = jnp.where(selseg > 0.5, 0.0, NEG)

    slope = slope_ref[...]
    scores = []
    for j in range(CMP_STRIDE):
        kj = slab(j * ROW_W + 2 * KVW, KVW).astype(bf)
        scores.append(_nt(qbd, kj) + (bases_ref[...] + slope * float(j)) + maskbias)
    s_new = jnp.sum(qbd_f * new[:, 2 * KVW:3 * KVW], axis=1, keepdims=True)
    mx = scores[0]
    for s in scores[1:]:
        mx = jnp.maximum(mx, s)
    m = jnp.maximum(jnp.max(mx, axis=1, keepdims=True), s_new)
    p_new = jnp.exp(s_new - m)
    acc = p_new * new[:, 3 * KVW:4 * KVW]
    psum = jnp.zeros((A_HEADS, PAST_SEGS), f32)
    for j in range(CMP_STRIDE):
        p = jnp.exp(scores[j] - m)
        psum = psum + p
        vj = slab(j * ROW_W + 3 * KVW, KVW).astype(bf)
        acc = acc + jnp.dot(p.astype(bf), vj, preferred_element_type=f32)
    o_s = acc / (jnp.sum(psum, axis=1, keepdims=True) + p_new)

    kw = win_ref[0, :, 0:KVW].astype(bf)
    vw = win_ref[0, :, KVW:2 * KVW].astype(bf)
    sw = _nt(qbd, kw) + basew_ref[...]
    s_neww = jnp.sum(qbd_f * new[:, 4 * KVW:5 * KVW], axis=1, keepdims=True)
    m = jnp.maximum(jnp.max(sw, axis=1, keepdims=True), s_neww)
    pw = jnp.exp(sw - m)
    p_new = jnp.exp(s_neww - m)
    o_w = (jnp.dot(pw.astype(bf), vw, preferred_element_type=f32) + p_new * new[:, 5 * KVW:6 * KVW]) / (
        jnp.sum(pw, axis=1, keepdims=True) + p_new)

    gate = jax.nn.sigmoid(g_ref[0])
    o = gate[:, 0:1] * o_c + gate[:, 1:2] * o_s + gate[:, 2:3] * o_w
    for k in range(A_KV_HEADS):
        o_ref[0, A_GROUP * k:A_GROUP * (k + 1), :] = o[A_GROUP * k:A_GROUP * (k + 1), k * HEAD_DIM:(k + 1) * HEAD_DIM]

    wout_ref[0, 0:WINDOW - 1, :] = win_ref[0, 1:WINDOW, :]
    wout_ref[0, WINDOW - 1:WINDOW, :] = new[:, 4 * KVW:6 * KVW]


def _sample_constants():
    slopes = (2.0 ** (-8.0 * np.arange(1, A_HEADS + 1, dtype=np.float64) / A_HEADS)).astype(np.float32)[:, None]
    heads = np.arange(A_HEADS)[:, None]
    bmask = (np.arange(KVW)[None, :] // HEAD_DIM == heads // A_GROUP).astype(np.float32)
    n = np.arange(PAST_SEGS)[None, :]
    c_end = n * CMP_STRIDE + (CMP_BLOCK - 1)
    basec = np.where(n < PAST_SEGS - 1, slopes * (c_end - PAST_LEN), NEG).astype(np.float32)
    bases = (slopes * (n * CMP_STRIDE - PAST_LEN)).astype(np.float32)
    r = np.arange(WINDOW)[None, :]
    basew = np.where(r >= 1, slopes * (r - WINDOW), NEG).astype(np.float32)
    slope = np.broadcast_to(slopes, (A_HEADS, PAST_SEGS)).astype(np.float32)
    grp = (heads // A_GROUP == heads.T // A_GROUP).astype(np.float32)
    c_start = np.arange(PAST_SEGS) * CMP_STRIDE
    s_start = np.arange(PAST_SEGS) * SLC_BLOCK
    ov = ((c_start[:, None] < s_start[None, :] + SLC_BLOCK) & (c_start[:, None] + CMP_BLOCK - 1 >= s_start[None, :])
          & (np.arange(PAST_SEGS)[:, None] < PAST_SEGS - 1) & (np.arange(PAST_SEGS)[None, :] < NS_SAMPLE))
    expand = (np.arange(PAST_SEGS)[None, :] * CMP_STRIDE // SLC_BLOCK == np.arange(PAST_SEGS)[:, None])
    as_bf = lambda a: jnp.asarray(a.astype(np.float32)).astype(jnp.bfloat16)
    return (jnp.asarray(bmask), jnp.asarray(basec), jnp.asarray(bases), jnp.asarray(basew), jnp.asarray(slope),
            as_bf(grp), as_bf(ov), as_bf(expand))


def nsa_sample_pallas(qa, ga, kva, pool, page_table, win_buf, layer, w1, w2, pe_bias):
    bf = jnp.bfloat16
    B = qa.shape[0]
    npool = pool.shape[1]
    page_table = page_table + layer * npool
    seg_w = CMP_STRIDE * ROW_W
    w1r = w1.reshape(2, CMP_RATIO, CMP_STRIDE * HEAD_DIM, CMP_HIDDEN).transpose(0, 2, 1, 3)
    w1r = w1r.reshape(2, CMP_STRIDE * HEAD_DIM, CMP_RATIO * CMP_HIDDEN).astype(bf)
    w2bd = jnp.einsum('kl,ced->ckeld', jnp.eye(A_KV_HEADS, dtype=w2.dtype), w2)
    w2bd = w2bd.reshape(2, A_KV_HEADS * CMP_HIDDEN, KVW).astype(bf)
    consts = _sample_constants()
    full = lambda shape: pl.BlockSpec(shape, lambda b, pt: (0,) * len(shape))
    page_specs = [pl.BlockSpec((1, PAGE_SEGS, seg_w), functools.partial(lambda p, b, pt: (pt[b, p], 0, 0), p))
                  for p in range(N_PAGES)]
    grid_spec = pltpu.PrefetchScalarGridSpec(
        num_scalar_prefetch=1,
        grid=(B,),
        in_specs=[
            pl.BlockSpec((1, A_HEADS, HEAD_DIM), lambda b, pt: (b, 0, 0)),
            pl.BlockSpec((1, A_HEADS, 3), lambda b, pt: (b, 0, 0)),
            pl.BlockSpec((1, 1, N_KV_BRANCH * KVW), lambda b, pt: (b, 0, 0)),
            *page_specs,
            pl.BlockSpec((1, WINDOW, 2 * KVW), lambda b, pt: (layer * B + b, 0, 0)),
            full(w1r.shape), full((2, 1, CMP_HIDDEN)), full(w2bd.shape),
            *[full(c.shape) for c in consts],
        ],
        out_specs=[
            pl.BlockSpec((1, A_HEADS, HEAD_DIM), lambda b, pt: (b, 0, 0)),
            pl.BlockSpec((1, WINDOW, 2 * KVW), lambda b, pt: (b, 0, 0)),
        ],
        scratch_shapes=[pltpu.VMEM((A_KV_HEADS * PAST_SEGS, CMP_STRIDE * HEAD_DIM), bf)],
    )
    pool3 = pool.reshape(pool.shape[0] * npool, PAGE_SEGS, seg_w)
    o, wout = pl.pallas_call(
        _nsa_sample_kernel,
        grid_spec=grid_spec,
        out_shape=(jax.ShapeDtypeStruct((B, A_HEADS, HEAD_DIM), jnp.float32),
                   jax.ShapeDtypeStruct((B, WINDOW, 2 * KVW), jnp.float32)),
        compiler_params=_params("arbitrary"),
        name="nsa_sample",
    )(page_table, qa.reshape(B, A_HEADS, HEAD_DIM), ga.reshape(B, A_HEADS, 3), kva.reshape(B, 1, N_KV_BRANCH * KVW),
      *([pool3] * N_PAGES), win_buf.reshape(win_buf.shape[0] * B, WINDOW, 2 * KVW), w1r, pe_bias.reshape(2, 1, CMP_HIDDEN), w2bd, *consts)
    return o.reshape(B, A_WIDTH), wout


SEG_PITCH = 24


def _nsa_decode_kernel(pt_ref, q_ref, g_ref, new_ref, *refs, has_into):
    del pt_ref
    pages = refs[:N_PAGES]
    rest = refs[N_PAGES:]
    (win_ref, w1_ref, bias_ref, w2bd_ref, bmask_ref, basec_ref, bases_ref, basew_ref, slope_ref,
     grp_ref, ov_ref, exp_ref) = rest[:12]
    o_ref, wout_ref, tr_ref, xc_ref = rest[12 + int(has_into):]
    f32, bf = jnp.float32, jnp.bfloat16

    q = q_ref[0] * (HEAD_DIM ** -0.5)
    qbd = jnp.where(bmask_ref[...] > 0.0, jnp.concatenate([q] * A_KV_HEADS, axis=1), 0.0).astype(bf)
    qbd_f = qbd.astype(f32)
    new = new_ref[0]

    ckv = []
    for c in range(2):
        for p in range(N_PAGES):
            for kp in range(2):
                r0 = c * KVW + kp * 2 * HEAD_DIM
                t = pages[p][0, r0:r0 + 2 * HEAD_DIM, :].T
                for s in range(PAGE_SEGS):
                    t0 = (kp * PAST_SEGS + p * PAGE_SEGS + s) * SEG_PITCH
                    tr_ref[t0:t0 + CMP_STRIDE, :] = t[s * CMP_STRIDE:(s + 1) * CMP_STRIDE, :]
        for j in range(CMP_STRIDE):
            for kp in range(2):
                piece = tr_ref[pl.ds(kp * PAST_SEGS * SEG_PITCH + j, PAST_SEGS, stride=SEG_PITCH), :]
                for kk in range(2):
                    k = 2 * kp + kk
                    xc_ref[k * PAST_SEGS:(k + 1) * PAST_SEGS, j * HEAD_DIM:(j + 1) * HEAD_DIM] = (
                        piece[:, kk * HEAD_DIM:(kk + 1) * HEAD_DIM])
        y = jnp.dot(xc_ref[...].astype(bf), w1_ref[c], preferred_element_type=f32)
        pre = y[:, 0:CMP_HIDDEN] + pltpu.roll(y[:, CMP_HIDDEN:2 * CMP_HIDDEN], A_KV_HEADS * PAST_SEGS - 1, 0)
        hid = jnp.maximum(pre + bias_ref[c], 0.0).astype(bf)
        hid = jnp.concatenate([hid[k * PAST_SEGS:(k + 1) * PAST_SEGS] for k in range(A_KV_HEADS)], axis=1)
        ckv.append(jnp.dot(hid, w2bd_ref[c], preferred_element_type=f32).astype(bf))
    ck, cv = ckv

    sc = _nt(qbd, ck) + basec_ref[...]
    ec = jnp.exp(sc - jnp.max(sc, axis=1, keepdims=True))
    pc = ec / jnp.sum(ec, axis=1, keepdims=True)
    o_c = jnp.dot(pc.astype(bf), cv, preferred_element_type=f32)

    pov = _dot3(pc, ov_ref[...])
    imp = sum(jnp.dot(grp_ref[...], p, preferred_element_type=f32) for p in _split3(pov))

    lane = lax.broadcasted_iota(jnp.int32, (A_HEADS, PAST_SEGS), 1)
    score = jnp.where((lane == 0) | (lane == NS_SAMPLE - 1) | (lane == NS_SAMPLE - 2), FORCE,
                      jnp.where(lane < NS_SAMPLE, imp, -jnp.inf))
    rr = lax.broadcasted_iota(jnp.int32, (PAST_SEGS, PAST_SEGS), 0)
    cc = lax.broadcasted_iota(jnp.int32, (PAST_SEGS, PAST_SEGS), 1)
    row_id = lax.broadcasted_iota(jnp.int32, (A_HEADS, PAST_SEGS), 0)
    selrows = jnp.zeros((A_HEADS, PAST_SEGS), f32)
    for k in range(A_KV_HEADS):
        row = jnp.broadcast_to(score[A_GROUP * k:A_GROUP * k + 1, :], (PAST_SEGS, PAST_SEGS))
        col = jnp.max(jnp.where(rr == cc, row, -jnp.inf), axis=1, keepdims=True)
        ahead = (col > row) | ((col == row) & (rr < cc))
        rank = jnp.sum(jnp.where(ahead, 1.0, 0.0), axis=0, keepdims=True)
        sel_k = jnp.where(rank < float(SLC_TOP), 1.0, 0.0)
        selrows = jnp.where(row_id // A_GROUP == k, jnp.broadcast_to(sel_k, (A_HEADS, PAST_SEGS)), selrows)
    selpos = jnp.dot(selrows.astype(bf), exp_ref[...], preferred_element_type=f32)

    slope = slope_ref[...]
    scores = []
    for p in range(N_PAGES):
        kp_ = pages[p][0, 2 * KVW:3 * KVW, :].astype(bf)
        mask = jnp.where(selpos[:, p * PAGE_SIZE:(p + 1) * PAGE_SIZE] > 0.5, 0.0, NEG)
        scores.append(jnp.dot(qbd, kp_, preferred_element_type=f32) + (bases_ref[...] + slope * float(p * PAGE_SIZE)) + mask)
    s_new = jnp.sum(qbd_f * new[:, 2 * KVW:3 * KVW], axis=1, keepdims=True)
    mx = scores[0]
    for s in scores[1:]:
        mx = jnp.maximum(mx, s)
    m = jnp.maximum(jnp.max(mx, axis=1, keepdims=True), s_new)
    p_new = jnp.exp(s_new - m)
    acc = p_new * new[:, 3 * KVW:4 * KVW]
    psum = jnp.zeros((A_HEADS, PAGE_SIZE), f32)
    for p in range(N_PAGES):
        pr = jnp.exp(scores[p] - m)
        psum = psum + pr
        acc = acc + _nt(pr.astype(bf), pages[p][0, 3 * KVW:4 * KVW, :].astype(bf))
    o_s = acc / (jnp.sum(psum, axis=1, keepdims=True) + p_new)

    sw = jnp.dot(qbd, win_ref[0, 0:KVW, :].astype(bf), preferred_element_type=f32) + basew_ref[...]
    s_neww = jnp.sum(qbd_f * new[:, 4 * KVW:5 * KVW], axis=1, keepdims=True)
    m = jnp.maximum(jnp.max(sw, axis=1, keepdims=True), s_neww)
    pw = jnp.exp(sw - m)
    p_new = jnp.exp(s_neww - m)
    o_w = (_nt(pw.astype(bf), win_ref[0, KVW:2 * KVW, :].astype(bf)) + p_new * new[:, 5 * KVW:6 * KVW]) / (
        jnp.sum(pw, axis=1, keepdims=True) + p_new)

    gate = jax.nn.sigmoid(g_ref[0])
    o = gate[:, 0:1] * o_c + gate[:, 1:2] * o_s + gate[:, 2:3] * o_w
    for k in range(A_KV_HEADS):
        o_ref[0, A_GROUP * k:A_GROUP * (k + 1), :] = o[A_GROUP * k:A_GROUP * (k + 1), k * HEAD_DIM:(k + 1) * HEAD_DIM]

    wr = lax.broadcasted_iota(jnp.int32, (WINDOW, 2 * KVW), 0)
    wc = lax.broadcasted_iota(jnp.int32, (WINDOW, 2 * KVW), 1)
    new_col = jnp.sum(jnp.where(wr == wc, jnp.broadcast_to(new[:, 4 * KVW:6 * KVW], (WINDOW, 2 * KVW)), 0.0),
                      axis=1, keepdims=True)
    wout_ref[0] = jnp.where(wc == WINDOW - 1, new_col, pltpu.roll(win_ref[0], WINDOW - 1, 1))


def _decode_constants():
    slopes = (2.0 ** (-8.0 * np.arange(1, A_HEADS + 1, dtype=np.float64) / A_HEADS)).astype(np.float32)[:, None]
    heads = np.arange(A_HEADS)[:, None]
    bmask = (np.arange(KVW)[None, :] // HEAD_DIM == heads // A_GROUP).astype(np.float32)
    n = np.arange(PAST_SEGS)[None, :]
    c_end = n * CMP_STRIDE + (CMP_BLOCK - 1)
    basec = np.where(n < PAST_SEGS - 1, slopes * (c_end - PAST_LEN), NEG).astype(np.float32)
    bases = (slopes * (np.arange(PAGE_SIZE)[None, :] - PAST_LEN)).astype(np.float32)
    r = np.arange(WINDOW)[None, :]
    basew = np.where(r >= 1, slopes * (r - WINDOW), NEG).astype(np.float32)
    slope = np.broadcast_to(slopes, (A_HEADS, PAGE_SIZE)).astype(np.float32)
    grp = (heads // A_GROUP == heads.T // A_GROUP).astype(np.float32)
    c_start = np.arange(PAST_SEGS) * CMP_STRIDE
    s_start = np.arange(PAST_SEGS) * SLC_BLOCK
    ov = ((c_start[:, None] < s_start[None, :] + SLC_BLOCK) & (c_start[:, None] + CMP_BLOCK - 1 >= s_start[None, :])
          & (np.arange(PAST_SEGS)[:, None] < PAST_SEGS - 1) & (np.arange(PAST_SEGS)[None, :] < NS_SAMPLE))
    expand = (np.arange(PAST_LEN)[None, :] // SLC_BLOCK == np.arange(PAST_SEGS)[:, None])
    as_bf = lambda a: jnp.asarray(a.astype(np.float32)).astype(jnp.bfloat16)
    return (jnp.asarray(bmask), jnp.asarray(basec), jnp.asarray(bases), jnp.asarray(basew), jnp.asarray(slope),
            as_bf(grp), as_bf(ov), as_bf(expand))


def nsa_decode_pallas(qa, ga, kva, pool, page_table, win_buf, layer, w1, w2, pe_bias, win_into=None):
    bf = jnp.bfloat16
    B = qa.shape[0]
    depth, npool = pool.shape[:2]
    page_table = page_table + layer * npool
    poolT = pool.transpose(0, 1, 3, 4, 5, 2).reshape(depth * npool, ROW_W, PAGE_SIZE)
    winT = win_buf.transpose(0, 1, 3, 4, 5, 2).reshape(depth * B, 2 * KVW, WINDOW)
    w1r = w1.reshape(2, CMP_RATIO, CMP_STRIDE * HEAD_DIM, CMP_HIDDEN).transpose(0, 2, 1, 3)
    w1r = w1r.reshape(2, CMP_STRIDE * HEAD_DIM, CMP_RATIO * CMP_HIDDEN).astype(bf)
    w2bd = jnp.einsum('kl,ced->ckeld', jnp.eye(A_KV_HEADS, dtype=w2.dtype), w2)
    w2bd = w2bd.reshape(2, A_KV_HEADS * CMP_HIDDEN, KVW).astype(bf)
    consts = _decode_constants()
    full = lambda shape: pl.BlockSpec(shape, lambda b, pt: (0,) * len(shape))
    page_specs = [pl.BlockSpec((1, ROW_W, PAGE_SIZE), functools.partial(lambda p, b, pt: (pt[b, p], 0, 0), p))
                  for p in range(N_PAGES)]
    grid_spec = pltpu.PrefetchScalarGridSpec(
        num_scalar_prefetch=1,
        grid=(B,),
        in_specs=[
            pl.BlockSpec((1, A_HEADS, HEAD_DIM), lambda b, pt: (b, 0, 0)),
            pl.BlockSpec((1, A_HEADS, 3), lambda b, pt: (b, 0, 0)),
            pl.BlockSpec((1, 1, N_KV_BRANCH * KVW), lambda b, pt: (b, 0, 0)),
            *page_specs,
            pl.BlockSpec((1, 2 * KVW, WINDOW), lambda b, pt: (layer * B + b, 0, 0)),
            full(w1r.shape), full((2, 1, CMP_HIDDEN)), full(w2bd.shape),
            *[full(c.shape) for c in consts],
            *([] if win_into is None else [pl.BlockSpec(memory_space=pl.ANY)]),
        ],
        out_specs=[
            pl.BlockSpec((1, A_HEADS, HEAD_DIM), lambda b, pt: (b, 0, 0)),
            pl.BlockSpec((1, 2 * KVW, WINDOW), lambda b, pt: (layer * B + b, 0, 0)),
        ],
        scratch_shapes=[pltpu.VMEM((2 * PAST_SEGS * SEG_PITCH, 2 * HEAD_DIM), jnp.float32),
                        pltpu.VMEM((A_KV_HEADS * PAST_SEGS, CMP_STRIDE * HEAD_DIM), jnp.float32)],
    )
    args = (page_table, qa.reshape(B, A_HEADS, HEAD_DIM), ga.reshape(B, A_HEADS, 3), kva.reshape(B, 1, N_KV_BRANCH * KVW),
            *([poolT] * N_PAGES), winT, w1r, pe_bias.reshape(2, 1, CMP_HIDDEN), w2bd, *consts)
    aliases = {}
    if win_into is not None:
        args += (win_into,)
        aliases = {len(args) - 1: 1}
    o, wout = pl.pallas_call(
        functools.partial(_nsa_decode_kernel, has_into=win_into is not None),
        grid_spec=grid_spec,
        out_shape=(jax.ShapeDtypeStruct((B, A_HEADS, HEAD_DIM), jnp.float32),
                   jax.ShapeDtypeStruct((depth * B, 2 * KVW, WINDOW), jnp.float32)),
        input_output_aliases=aliases,
        compiler_params=_params("arbitrary"),
        name="nsa_decode",
    )(*args)
    return o.reshape(B, A_WIDTH), wout


MLSTM_L = 128
VAUG = 2 * V_DIM


def _mlstm_prompt_kernel(q_ref, k_ref, v_ref, o_ref, fsrc_ref, ifT_ref, y_ref, c_ref, m_ref, *, f_off):
    f32, bf = jnp.float32, jnp.bfloat16
    L = MLSTM_L
    f_ref = fsrc_ref.at[:, f_off:f_off + B_HEADS]
    iT_ref = ifT_ref.at[0:B_HEADS, :]
    fT_ref = ifT_ref.at[B_HEADS:2 * B_HEADS, :]

    @pl.when(pl.program_id(0) == 0)
    def _():
        c_ref[...] = jnp.zeros_like(c_ref)
        m_ref[...] = jnp.zeros_like(m_ref)

    kT_all = k_ref[...].T * (QK_DIM ** -0.5)
    rr = lax.broadcasted_iota(jnp.int32, (L, L), 0)
    cc = lax.broadcasted_iota(jnp.int32, (L, L), 1)
    lower = rr >= cc
    tril = jnp.where(lower, 1.0, 0.0).astype(bf)
    triu = jnp.where(rr <= cc, 1.0, 0.0).astype(bf)
    b_col = sum(jnp.dot(tril, p, preferred_element_type=f32) for p in _split3(jax.nn.log_sigmoid(f_ref[...])))
    b_row = _dot3(jax.nn.log_sigmoid(fT_ref[...]), triu)
    a_row = iT_ref[...] - b_row
    ones_col = jnp.where(lax.broadcasted_iota(jnp.int32, (L, V_DIM), 1) == 0, 1.0, 0.0)

    for h in range(B_HEADS):
        m_prev = m_ref[h, 0:1, 0:1]
        a = a_row[h:h + 1, :]
        amat = jnp.where(lower, jnp.broadcast_to(a, (L, L)), -jnp.inf)
        big_m = jnp.maximum(m_prev, jnp.max(amat, axis=1, keepdims=True))
        dmat = jnp.exp(amat - big_m)
        inter = jnp.exp(m_prev - big_m)
        q = q_ref[:, h * QK_DIM:(h + 1) * QK_DIM].astype(bf)
        kT = kT_all[h * QK_DIM:(h + 1) * QK_DIM, :]
        vaug = jnp.concatenate([v_ref[:, h * V_DIM:(h + 1) * V_DIM], ones_col], axis=1).astype(bf)
        s = jnp.dot(q, kT.astype(bf), preferred_element_type=f32) * dmat
        r = (jnp.dot(s.astype(bf), vaug, preferred_element_type=f32)
             + inter * jnp.dot(q, c_ref[h].astype(bf), preferred_element_type=f32))
        m_new = b_col[:, h:h + 1] + big_m
        den = jnp.maximum(jnp.abs(r[:, V_DIM:V_DIM + 1]), jnp.exp(-m_new))
        y_ref[:, h * V_DIM:(h + 1) * V_DIM] = (jax.nn.sigmoid(o_ref[:, h * V_DIM:(h + 1) * V_DIM])
                                               * (r[:, 0:V_DIM] / den))
        m_end = big_m[L - 1:L, :]
        w_end = jnp.exp(a - m_end)
        kw_end = kT * w_end
        c_new = inter[L - 1:L, :] * c_ref[h] + jnp.dot(kw_end.astype(bf), vaug, preferred_element_type=f32)
        n_new = inter[L - 1:L, :] * c_ref[h, :, V_DIM:V_DIM + 1] + jnp.sum(kw_end, axis=1, keepdims=True)
        c_ref[h] = jnp.where(lax.broadcasted_iota(jnp.int32, (QK_DIM, VAUG), 1) == V_DIM, n_new, c_new)
        m_ref[h] = jnp.broadcast_to(m_new[L - 1:L, :], m_ref.shape[1:])


def mlstm_prompt_pallas(q_src, k_src, v_src, o_src, f_src, ifT, T):
    L = MLSTM_L
    (q_arr, q_cb), (k_arr, k_cb), (v_arr, v_cb), (o_arr, o_cb), (f_arr, f_cb, f_off) = q_src, k_src, v_src, o_src, f_src
    y, caug, m = pl.pallas_call(
        functools.partial(_mlstm_prompt_kernel, f_off=f_off),
        grid=(T // L,),
        in_specs=[
            pl.BlockSpec((L, B_HEADS * QK_DIM), lambda c: (c, q_cb)),
            pl.BlockSpec((L, B_HEADS * QK_DIM), lambda c: (c, k_cb)),
            pl.BlockSpec((L, B_WIDTH), lambda c: (c, v_cb)),
            pl.BlockSpec((L, B_WIDTH), lambda c: (c, o_cb)),
            pl.BlockSpec((L, f_arr.shape[1] if f_arr.shape[1] < 128 else 128), lambda c: (c, f_cb)),
            pl.BlockSpec((2 * B_HEADS, L), lambda c: (0, c)),
        ],
        out_specs=[
            pl.BlockSpec((L, B_WIDTH), lambda c: (c, 0)),
            pl.BlockSpec((B_HEADS, QK_DIM, VAUG), lambda c: (0, 0, 0)),
            pl.BlockSpec((B_HEADS, 8, 128), lambda c: (0, 0, 0)),
        ],
        out_shape=(jax.ShapeDtypeStruct((T, B_WIDTH), jnp.float32),
                   jax.ShapeDtypeStruct((B_HEADS, QK_DIM, VAUG), jnp.float32),
                   jax.ShapeDtypeStruct((B_HEADS, 8, 128), jnp.float32)),
        compiler_params=_params("arbitrary"),
        name="mlstm_prompt",
    )(q_arr, k_arr, v_arr, o_arr, f_arr, ifT)
    return y, caug[:, :, 0:V_DIM], caug[:, :, V_DIM], m[:, 0, 0]


def _mlstm_sample_kernel(q_ref, k_ref, qT_ref, kT_ref, v_ref, o_ref, i_ref, f_ref, m_ref, n_ref, c_ref,
                         y_ref, cn_ref, nn_ref, mn_ref):
    bb = q_ref.shape[0]
    scale = QK_DIM ** -0.5
    for b in range(bb):
        logf = jax.nn.log_sigmoid(f_ref[b])
        m_new = jnp.maximum(logf + m_ref[b], i_ref[b])
        d_all = jnp.exp(i_ref[b] - m_new)
        inter_all = jnp.exp(logf + m_ref[b] - m_new)
        floor_all = jnp.exp(-m_new)
        mn_ref[b] = m_new
        qk_all = jnp.sum(q_ref[b] * k_ref[b], axis=1, keepdims=True) * scale
        qn_all = jnp.sum(q_ref[b] * n_ref[b], axis=1, keepdims=True)
        for h in range(B_HEADS):
            d = d_all[:, h:h + 1]
            inter = inter_all[:, h:h + 1]
            s = qk_all[h:h + 1, :] * d
            c = c_ref[b, h]
            v = v_ref[b, h:h + 1, :]
            qc = qT_ref[b, :, h:h + 1]
            kc = kT_ref[b, :, h:h + 1] * scale
            num = inter * jnp.sum(qc * c, axis=0, keepdims=True) + s * v
            den = inter * qn_all[h:h + 1, :] + s
            hout = num / jnp.maximum(jnp.abs(den), floor_all[:, h:h + 1])
            y_ref[b, h:h + 1, :] = jax.nn.sigmoid(o_ref[b, h:h + 1, :]) * hout
            cn_ref[b, h] = inter * c + d * (kc * v)
            nn_ref[b, h:h + 1, :] = inter * n_ref[b, h:h + 1, :] + d * (k_ref[b, h:h + 1, :] * scale)


def mlstm_sample_pallas(qb, kb, vb, ib, fb, ob, state_C, state_n, state_m, layer, *, bb=8):
    B = qb.shape[0]
    nb = B // bb
    q3 = qb.reshape(B, B_HEADS, QK_DIM)
    k3 = kb.reshape(B, B_HEADS, QK_DIM)
    row8 = lambda x: x.reshape(-1, 1, B_HEADS)
    lay = layer * nb
    y, cn, nn, mn = pl.pallas_call(
        _mlstm_sample_kernel,
        grid=(nb,),
        in_specs=[
            pl.BlockSpec((bb, B_HEADS, QK_DIM), lambda i: (i, 0, 0)),
            pl.BlockSpec((bb, B_HEADS, QK_DIM), lambda i: (i, 0, 0)),
            pl.BlockSpec((bb, QK_DIM, B_HEADS), lambda i: (i, 0, 0)),
            pl.BlockSpec((bb, QK_DIM, B_HEADS), lambda i: (i, 0, 0)),
            pl.BlockSpec((bb, B_HEADS, V_DIM), lambda i: (i, 0, 0)),
            pl.BlockSpec((bb, B_HEADS, V_DIM), lambda i: (i, 0, 0)),
            pl.BlockSpec((bb, 1, B_HEADS), lambda i: (i, 0, 0)),
            pl.BlockSpec((bb, 1, B_HEADS), lambda i: (i, 0, 0)),
            pl.BlockSpec((bb, 1, B_HEADS), lambda i: (lay + i, 0, 0)),
            pl.BlockSpec((bb, B_HEADS, QK_DIM), lambda i: (lay + i, 0, 0)),
            pl.BlockSpec((bb, B_HEADS, QK_DIM, V_DIM), lambda i: (lay + i, 0, 0, 0)),
        ],
        out_specs=[
            pl.BlockSpec((bb, B_HEADS, V_DIM), lambda i: (i, 0, 0)),
            pl.BlockSpec((bb, B_HEADS, QK_DIM, V_DIM), lambda i: (i, 0, 0, 0)),
            pl.BlockSpec((bb, B_HEADS, QK_DIM), lambda i: (i, 0, 0)),
            pl.BlockSpec((bb, 1, B_HEADS), lambda i: (i, 0, 0)),
        ],
        out_shape=(jax.ShapeDtypeStruct((B, B_HEADS, V_DIM), jnp.float32),
                   jax.ShapeDtypeStruct((B, B_HEADS, QK_DIM, V_DIM), jnp.float32),
                   jax.ShapeDtypeStruct((B, B_HEADS, QK_DIM), jnp.float32),
                   jax.ShapeDtypeStruct((B, 1, B_HEADS), jnp.float32)),
        compiler_params=_params("arbitrary"),
        name="mlstm_sample",
    )(q3, k3, q3.transpose(0, 2, 1), k3.transpose(0, 2, 1), vb.reshape(B, B_HEADS, V_DIM),
      ob.reshape(B, B_HEADS, V_DIM), row8(ib), row8(fb), row8(state_m),
      state_n.reshape(-1, B_HEADS, QK_DIM), state_C.reshape(-1, B_HEADS, QK_DIM, V_DIM))
    return y.reshape(B, B_WIDTH), cn, nn, mn.reshape(B, B_HEADS)


(G_QA, G_KVA, G_GA, G_QB, G_KB, G_VB, G_IB, G_FB, G_OB, G_GM) = range(10)
WIDE_ORDER = (G_QA, G_VB, G_OB, G_KVA, G_QB, G_KB, G_GM)
NARROW_ORDER = (G_GA, G_IB, G_FB)
WIDE_OFF = dict(zip(WIDE_ORDER, np.cumsum((0,) + tuple(SPLITS[g] for g in WIDE_ORDER))[:-1].tolist()))
NARROW_OFF = dict(zip(NARROW_ORDER, np.cumsum((0,) + tuple(SPLITS[g] for g in NARROW_ORDER))[:-1].tolist()))
NARROW_W = 128


def project_in(h, w_in, b_in):
    w_t = w_in.T

    def gather(order, pad_to):
        rows = jnp.concatenate([w_t[BOUNDS[g]:BOUNDS[g + 1]] for g in order], axis=0)
        bias = jnp.concatenate([b_in[BOUNDS[g]:BOUNDS[g + 1]] for g in order], axis=0)
        pad = pad_to - rows.shape[0]
        return jnp.pad(rows, ((0, pad), (0, 0))), jnp.pad(bias, (0, pad))

    n_wide = sum(SPLITS[g] for g in WIDE_ORDER)
    wide = matmul_act(h, *gather(WIDE_ORDER, n_wide), nt=True)
    narrow = matmul_act(h, *gather(NARROW_ORDER, NARROW_W), nt=True, tn=NARROW_W)
    return wide, narrow


def kernel(x_prompt, x_sample, cache_nsa_kv, cache_win_kv, state_C, state_n, state_m, page_table,
           norm_g, w_in, b_in, cmp_pe, cmp_w1, cmp_b1, cmp_w2, w_up_a, w_up_b, w_out, w_mlp1, w_mlp2):
    Bp, Tp = x_prompt.shape[:2]
    Bs, Ts = x_sample.shape[:2]
    assert Bp == 1 and Ts == 1 and Tp >= WINDOW, (x_prompt.shape, x_sample.shape)
    x = jnp.concatenate([x_prompt.reshape(Tp, D_MODEL), x_sample.reshape(Bs, D_MODEL)], axis=0)
    kv_p, kv_s, win_p, win_all = [], [], [], None
    C_p, C_s, n_p, n_s, m_p, m_s = [], [], [], [], [], []
    kv4 = 4 * KVW
    for l in range(DEPTH):
        wide, narrow = project_in(rmsnorm_cast(x, norm_g[l, 0]), w_in[l], b_in[l])
        col = lambda g, rows: wide[rows, WIDE_OFF[g]:WIDE_OFF[g] + SPLITS[g]]
        ncol = lambda g, rows: narrow[rows, NARROW_OFF[g]:NARROW_OFF[g] + SPLITS[g]]
        pr, sr = slice(0, Tp), slice(Tp, Tp + Bs)

        kva = col(G_KVA, pr)
        ya_p, pe_bias = nsa_prompt_pallas((wide, WIDE_OFF[G_QA]), ncol(G_GA, pr), kva, cmp_pe[l], cmp_w1[l], cmp_b1[l],
                                          cmp_w2[l])
        yb_p, C, n, m = mlstm_prompt_pallas(
            (wide, WIDE_OFF[G_QB] // (B_HEADS * QK_DIM)), (wide, WIDE_OFF[G_KB] // (B_HEADS * QK_DIM)),
            (wide, WIDE_OFF[G_VB] // B_WIDTH),
            (wide, WIDE_OFF[G_OB] // B_WIDTH), (narrow, 0, NARROW_OFF[G_FB]),
            narrow[pr, NARROW_OFF[G_IB]:NARROW_OFF[G_IB] + 2 * B_HEADS].T, Tp)
        kv_p.append(kva[:, 0:kv4].reshape(Bp, Tp, 4, A_KV_HEADS, HEAD_DIM))
        win_p.append(kva[Tp - WINDOW:, kv4:].reshape(Bp, WINDOW, 2, A_KV_HEADS, HEAD_DIM))
        C_p.append(C[None])
        n_p.append(n[None])
        m_p.append(m[None])

        kva = col(G_KVA, sr)
        ya_s, win_all = nsa_decode_pallas(col(G_QA, sr), ncol(G_GA, sr), kva, cache_nsa_kv, page_table, cache_win_kv, l,
                                          cmp_w1[l], cmp_w2[l], pe_bias, win_into=win_all)
        yb_s, C, n, m = mlstm_sample_pallas(col(G_QB, sr), col(G_KB, sr), col(G_VB, sr), ncol(G_IB, sr), ncol(G_FB, sr),
                                            col(G_OB, sr), state_C, state_n, state_m, l)
        kv_s.append(kva[:, 0:kv4].reshape(Bs, Ts, 4, A_KV_HEADS, HEAD_DIM))
        C_s.append(C)
        n_s.append(n)
        m_s.append(m)

        gm_src = (wide, WIDE_OFF[G_GM])
        mix = mix_matmul(ya_p, yb_p, w_up_a[l], w_up_b[l], gm_src, row0=0, rows_total=Tp + Bs)
        mix = mix_matmul(ya_s, yb_s, w_up_a[l], w_up_b[l], gm_src, row0=Tp, rows_total=Tp + Bs, into=mix)
        x = matmul_norm_res(mix, w_out[l].astype(jnp.bfloat16), x, norm_g[l, 1])
        hid = matmul_act(rmsnorm_cast(x, norm_g[l, 2]), w_mlp1[l], jnp.zeros((D_FF,), jnp.float32), nt=False,
                         act="relu2", out_dtype=jnp.bfloat16)
        x = matmul_norm_res(hid, w_mlp2[l].astype(jnp.bfloat16), x, norm_g[l, 3])

    win_s = win_all.reshape(DEPTH, Bs, 2, A_KV_HEADS, HEAD_DIM, WINDOW).transpose(0, 1, 5, 2, 3, 4)
    return (x[:Tp].reshape(Bp, Tp, D_MODEL), x[Tp:].reshape(Bs, Ts, D_MODEL),
            jnp.stack(kv_p), jnp.stack(kv_s), jnp.stack(win_p), win_s,
            jnp.stack(C_p), jnp.stack(C_s), jnp.stack(n_p), jnp.stack(n_s), jnp.stack(m_p), jnp.stack(m_s))
```

```python
import functools
import math

import jax
import jax.numpy as jnp
import numpy as np
from jax import lax
from jax.experimental import pallas as pl
from jax.experimental.pallas import tpu as pltpu

D_MODEL = 2048
DEPTH = 2
PAST_LEN = 2048
A_HEADS = 16
A_KV_HEADS = 4
A_GROUP = A_HEADS // A_KV_HEADS
HEAD_DIM = 64
A_WIDTH = A_HEADS * HEAD_DIM
CMP_BLOCK = 32
CMP_STRIDE = 16
CMP_RATIO = CMP_BLOCK // CMP_STRIDE
CMP_HIDDEN = 128
SLC_BLOCK = 64
SLC_TOP = 16
WINDOW = 512
Q_BLOCK = 128
N_KV_BRANCH = 6
B_HEADS = 8
QK_DIM = 64
V_DIM = 128
B_WIDTH = B_HEADS * V_DIM
MLSTM_CHUNK = 64
D_FF = 4 * D_MODEL
EPS = 1e-6
FORCE = 1e4
NEG = -1e30

SPLITS = (A_WIDTH, N_KV_BRANCH * A_KV_HEADS * HEAD_DIM, 3 * A_HEADS,
          B_HEADS * QK_DIM, B_HEADS * QK_DIM, B_WIDTH, B_HEADS, B_HEADS, B_WIDTH, 2 * D_MODEL)
BOUNDS = tuple(int(b) for b in np.cumsum((0,) + SPLITS))

VMEM_LIMIT_BYTES = 56 * 1024 * 1024


def _params(*sem):
    return pltpu.CompilerParams(dimension_semantics=sem, vmem_limit_bytes=VMEM_LIMIT_BYTES)


def _row_tile(m, want):
    for t in range(min(want, m), 15, -1):
        if m % t == 0 and t % 16 == 0:
            return t
    return m


def _rmsnorm_kernel(x_ref, g_ref, o_ref):
    x = x_ref[...]
    r = lax.rsqrt(jnp.mean(x * x, axis=-1, keepdims=True) + EPS)
    o_ref[...] = (x * r * g_ref[...]).astype(o_ref.dtype)


def rmsnorm_cast(x, g, *, tm=640):
    M, K = x.shape
    tm = _row_tile(M, tm)
    return pl.pallas_call(
        _rmsnorm_kernel,
        grid=(M // tm,),
        in_specs=[pl.BlockSpec((tm, K), lambda i: (i, 0)), pl.BlockSpec((1, K), lambda i: (0, 0))],
        out_specs=pl.BlockSpec((tm, K), lambda i: (i, 0)),
        out_shape=jax.ShapeDtypeStruct((M, K), jnp.bfloat16),
        compiler_params=_params("parallel"),
        name="rmsnorm_cast",
    )(x, g.reshape(1, K))


def _matmul_act_kernel(h_ref, w_ref, b_ref, o_ref, *, nt, act):
    w = w_ref[...].astype(jnp.bfloat16)
    z = (_nt(h_ref[...], w) if nt else jnp.dot(h_ref[...], w, preferred_element_type=jnp.float32)) + b_ref[...]
    if act == "relu2":
        z = jnp.square(jnp.maximum(z, 0.0))
    o_ref[...] = z.astype(o_ref.dtype)


def _layer_spec(w, layer, block, index):
    if w.ndim == 2:
        return pl.BlockSpec(block, index)
    return pl.BlockSpec((None,) + block, lambda *g: (layer,) + index(*g))


def matmul_act(h, w, b, *, nt, act=None, out_dtype=jnp.float32, tm=1664, tn=512, layer=None):
    M, K = h.shape
    N = w.shape[-2] if nt else w.shape[-1]
    tm = _row_tile(M, tm)
    tn = min(tn, N)
    assert N % tn == 0, (N, tn)
    w_spec = (_layer_spec(w, layer, (tn, K), lambda i, j: (j, 0)) if nt
              else _layer_spec(w, layer, (K, tn), lambda i, j: (0, j)))
    return pl.pallas_call(
        functools.partial(_matmul_act_kernel, nt=nt, act=act),
        grid=(M // tm, N // tn),
        in_specs=[pl.BlockSpec((tm, K), lambda i, j: (i, 0)), w_spec, pl.BlockSpec((1, tn), lambda i, j: (0, j))],
        out_specs=pl.BlockSpec((tm, tn), lambda i, j: (i, j)),
        out_shape=jax.ShapeDtypeStruct((M, N), out_dtype),
        compiler_params=_params("parallel", "arbitrary"),
        name="matmul_act",
    )(h, w, b.reshape(1, N))


def _mix_kernel(ya_ref, yb_ref, wa_ref, wb_ref, ga_ref, gb_ref, o_ref):
    bf = jnp.bfloat16
    a = jnp.dot(ya_ref[...].astype(bf), wa_ref[...].astype(bf), preferred_element_type=jnp.float32)
    b = jnp.dot(yb_ref[...].astype(bf), wb_ref[...].astype(bf), preferred_element_type=jnp.float32)
    o_ref[...] = (jax.nn.sigmoid(ga_ref[...]) * a + jax.nn.sigmoid(gb_ref[...]) * b).astype(o_ref.dtype)


def _mix_into_kernel(ya_ref, yb_ref, wa_ref, wb_ref, ga_ref, gb_ref, prev_ref, o_ref):
    del prev_ref
    _mix_kernel(ya_ref, yb_ref, wa_ref, wb_ref, ga_ref, gb_ref, o_ref)


def mix_matmul(ya, yb, wa, wb, gm_src, *, row0, rows_total, into=None, tm=1024, tn=512, layer=None):
    M, Ka = ya.shape
    Kb = yb.shape[1]
    N = wa.shape[-1]
    gm, off = gm_src
    tm = _row_tile(M, tm)
    nb = N // tn
    assert off % tn == 0 and N % tn == 0 and row0 % tm == 0, (off, N, tn, row0, tm)
    ja, jb, i0 = off // tn, off // tn + nb, row0 // tm
    in_specs = [
        pl.BlockSpec((tm, Ka), lambda i, j: (i, 0)),
        pl.BlockSpec((tm, Kb), lambda i, j: (i, 0)),
        _layer_spec(wa, layer, (Ka, tn), lambda i, j: (0, j)),
        _layer_spec(wb, layer, (Kb, tn), lambda i, j: (0, j)),
        pl.BlockSpec((tm, tn), lambda i, j: (i0 + i, ja + j)),
        pl.BlockSpec((tm, tn), lambda i, j: (i0 + i, jb + j)),
    ]
    args = (ya, yb, wa, wb, gm, gm)
    aliases = {}
    if into is not None:
        in_specs.append(pl.BlockSpec(memory_space=pl.ANY))
        args += (into,)
        aliases = {len(args) - 1: 0}
    return pl.pallas_call(
        _mix_kernel if into is None else _mix_into_kernel,
        grid=(M // tm, nb),
        in_specs=in_specs,
        out_specs=pl.BlockSpec((tm, tn), lambda i, j: (i0 + i, j)),
        out_shape=jax.ShapeDtypeStruct((rows_total, N), jnp.bfloat16),
        input_output_aliases=aliases,
        compiler_params=_params("parallel", "arbitrary"),
        name="mix_matmul",
    )(*args)


def _matmul_norm_res_kernel(a_ref, w_ref, x_ref, g_ref, o_ref, acc_ref):
    k = pl.program_id(1)

    @pl.when(k == 0)
    def _():
        acc_ref[...] = jnp.zeros_like(acc_ref)

    a = a_ref[...]
    for n0 in range(0, acc_ref.shape[1], NORM_RES_COLS):
        cols = slice(n0, n0 + NORM_RES_COLS)
        acc_ref[:, cols] += jnp.dot(a, w_ref[:, cols].astype(jnp.bfloat16), preferred_element_type=jnp.float32)

    @pl.when(k == pl.num_programs(1) - 1)
    def _():
        y = acc_ref[...]
        r = lax.rsqrt(jnp.mean(y * y, axis=-1, keepdims=True) + EPS)
        o_ref[...] = x_ref[...] + y * r * g_ref[...]


NORM_RES_COLS = 512


def matmul_norm_res(a, w, x, g, *, tm=640, tk=1024, layer=None):
    M, K = a.shape
    N = w.shape[-1]
    tm = _row_tile(M, tm)
    tk = min(tk, K)
    return pl.pallas_call(
        _matmul_norm_res_kernel,
        grid=(M // tm, K // tk),
        in_specs=[
            pl.BlockSpec((tm, tk), lambda i, k: (i, k)),
            _layer_spec(w, layer, (tk, N), lambda i, k: (k, 0)),
            pl.BlockSpec((tm, N), lambda i, k: (i, 0)),
            pl.BlockSpec((1, N), lambda i, k: (0, 0)),
        ],
        out_specs=pl.BlockSpec((tm, N), lambda i, k: (i, 0)),
        out_shape=jax.ShapeDtypeStruct((M, N), jnp.float32),
        scratch_shapes=[pltpu.VMEM((tm, N), jnp.float32)],
        compiler_params=_params("parallel", "arbitrary"),
        name="matmul_norm_res",
    )(a, w, x, g.reshape(1, N))


NSA_LANES = A_GROUP * Q_BLOCK
KEY_CHUNK = 512
WIN_CHUNK = 128
WIN_SPAN = WINDOW + Q_BLOCK
SLC_SLOTS = 128
POS_PIECES = 6
AUG = 128
MASK_BIG = 30000.0


def _pos_pieces(pos):
    pos = np.asarray(pos, np.int64)
    hi = (pos // 64) * 64
    lo = pos % 64
    return np.stack([hi, lo] * 3, axis=-1).astype(np.float32)


def _slope_rows():
    slopes = 2.0 ** (-8.0 * np.arange(1, A_HEADS + 1, dtype=np.float64) / A_HEADS)
    s = jnp.asarray(slopes.astype(np.float32))
    hi = s.astype(jnp.bfloat16)
    r1 = s - hi.astype(jnp.float32)
    mid = r1.astype(jnp.bfloat16)
    lo = (r1 - mid.astype(jnp.float32)).astype(jnp.bfloat16)
    rows = jnp.stack([hi, hi, mid, mid, lo, lo], axis=0)
    rows = rows.reshape(POS_PIECES, A_KV_HEADS, A_GROUP).transpose(1, 0, 2)
    rows = jnp.repeat(rows, Q_BLOCK, axis=-1)
    return jnp.pad(rows, ((0, 0), (0, 16 - POS_PIECES), (0, 0)))


def _split3(x):
    hi = x.astype(jnp.bfloat16)
    r1 = x - hi.astype(jnp.float32)
    mid = r1.astype(jnp.bfloat16)
    lo = (r1 - mid.astype(jnp.float32)).astype(jnp.bfloat16)
    return hi, mid, lo


def _tile4(x):
    return jnp.concatenate([x] * A_GROUP, axis=1)


def _softmax_step(carry, s, vT):
    m, l, acc = carry
    m_new = jnp.maximum(m, jnp.max(s, axis=0, keepdims=True))
    alpha = jnp.exp(m - m_new)
    p = jnp.exp(s - m_new)
    l = alpha * l + jnp.sum(p, axis=0, keepdims=True)
    acc = alpha * acc + jnp.dot(vT, p.astype(jnp.bfloat16), preferred_element_type=jnp.float32)
    return m_new, l, acc


def _nsa_prompt_kernel(q_ref, slope_ref, gT_ref, ck_ref, cvT_ref, ovT_ref, ks_ref, vsT_ref,
                       kw_ref, vwT_ref, o_ref, qa_ref, flag_ref):
    f32, bf = jnp.float32, jnp.bfloat16
    i = pl.program_id(1)
    s0 = i * Q_BLOCK
    ncp = ck_ref.shape[1]

    qT = q_ref[...].T * (HEAD_DIM ** -0.5)
    qcat = jnp.concatenate([qT[g * HEAD_DIM:(g + 1) * HEAD_DIM, :] for g in range(A_GROUP)], axis=1)
    qa_ref[0:HEAD_DIM, :] = qcat.astype(bf)
    qa_ref[HEAD_DIM:HEAD_DIM + 16, :] = slope_ref[0]
    qa_ref[HEAD_DIM + 16:AUG, :] = jnp.zeros((AUG - HEAD_DIM - 16, NSA_LANES), bf)
    qc = qa_ref[0:AUG, :]


    def compressed_and_select(nc, ns):
        sc = jnp.dot(ck_ref[0, 0:nc, :], qc, preferred_element_type=f32)
        n_idx = lax.broadcasted_iota(jnp.int32, (nc, Q_BLOCK), 0)
        t_idx = s0 + lax.broadcasted_iota(jnp.int32, (nc, Q_BLOCK), 1)
        valid = _tile4(n_idx * CMP_STRIDE + (CMP_BLOCK - 1) <= t_idx)
        sc = jnp.where(valid, sc, NEG)
        ec = jnp.where(valid, jnp.exp(sc - jnp.max(sc, axis=0, keepdims=True)), 0.0)
        lc = jnp.sum(ec, axis=0, keepdims=True)
        pc = ec * jnp.where(lc > 0.0, 1.0 / lc, 0.0)
        o_c = jnp.dot(cvT_ref[0, :, 0:nc], pc.astype(bf), preferred_element_type=f32)

        pg = pc[:, 0:Q_BLOCK]
        for g in range(1, A_GROUP):
            pg = pg + pc[:, g * Q_BLOCK:(g + 1) * Q_BLOCK]
        imp = sum(jnp.dot(ovT_ref[0:ns, 0:nc], piece, preferred_element_type=f32) for piece in _split3(pg))

        blk = lax.broadcasted_iota(jnp.int32, (ns, Q_BLOCK), 0)
        cur = (s0 + lax.broadcasted_iota(jnp.int32, (ns, Q_BLOCK), 1)) // SLC_BLOCK
        forced = (blk == 0) | (blk == cur) | (blk == cur - 1)
        causal = blk <= cur
        score = jnp.where(forced, FORCE, jnp.where(causal, imp, -FORCE))
        blk_f = blk.astype(f32)
        sel = jnp.zeros((ns, Q_BLOCK), f32)
        for _ in range(SLC_TOP):
            top = jnp.max(score, axis=0, keepdims=True)
            first = jnp.min(jnp.where(score == top, blk_f, float(SLC_SLOTS)), axis=0, keepdims=True)
            pick = blk_f == first
            sel = jnp.where(pick, 1.0, sel)
            score = jnp.where(pick, -jnp.inf, score)
        picked = jnp.where((sel > 0.0) & causal, 1.0, 0.0)
        if ns < SLC_SLOTS:
            picked = jnp.concatenate([picked, jnp.zeros((SLC_SLOTS - ns, Q_BLOCK), f32)], axis=0)
        return o_c, picked

    nqb = ks_ref.shape[1] // Q_BLOCK
    if ncp % 256 == 0 and nqb % 8 == 0 and nqb < SLC_SLOTS:
        o_c, picked = lax.cond(i < nqb // 2, lambda: compressed_and_select(ncp // 2, nqb),
                               lambda: compressed_and_select(ncp, SLC_SLOTS))
    else:
        o_c, picked = compressed_and_select(ncp, SLC_SLOTS)
    qa_ref[AUG:AUG + SLC_SLOTS, :] = _tile4(jnp.where(picked > 0.0, 0.0, -MASK_BIG)).astype(bf)

    def sel_scores(c):
        kc = ks_ref[0, pl.ds(pl.multiple_of(c * KEY_CHUNK, KEY_CHUNK), KEY_CHUNK), :]
        return jnp.dot(kc, qa_ref[...], preferred_element_type=f32)

    init = (jnp.full((1, NSA_LANES), NEG, f32), jnp.zeros((1, NSA_LANES), f32),
            jnp.zeros((HEAD_DIM, NSA_LANES), f32))
    c_last = (i * Q_BLOCK) // KEY_CHUNK
    bpc = KEY_CHUNK // SLC_BLOCK
    for c in range(SLC_SLOTS // bpc):
        flag_ref[c] = (jnp.max(picked[c * bpc:(c + 1) * bpc, :]) > 0.0).astype(jnp.int32)

    def sel_chunk(c, cr):
        return lax.cond(flag_ref[c] > 0, lambda x: _softmax_step(x, sel_scores(c), vsT_ref[0, c]), lambda x: x, cr)

    carry = lax.fori_loop(0, c_last, sel_chunk, init)
    j_last = c_last * KEY_CHUNK + lax.broadcasted_iota(jnp.int32, (KEY_CHUNK, Q_BLOCK), 0)
    t_last = s0 + lax.broadcasted_iota(jnp.int32, (KEY_CHUNK, Q_BLOCK), 1)
    m_s, l_s, acc_s = _softmax_step(carry, jnp.where(_tile4(j_last <= t_last), sel_scores(c_last), NEG),
                                    vsT_ref[0, c_last])
    o_s = acc_s * (1.0 / l_s)

    w0 = jnp.maximum(i - WINDOW // WIN_CHUNK, 0)
    kw = kw_ref[0, pl.ds(pl.multiple_of(w0 * WIN_CHUNK, WIN_CHUNK), WIN_SPAN), :]
    sw = jnp.dot(kw, qc, preferred_element_type=f32)
    jw = w0 * WIN_CHUNK + lax.broadcasted_iota(jnp.int32, (WIN_SPAN, Q_BLOCK), 0)
    tw = s0 + lax.broadcasted_iota(jnp.int32, (WIN_SPAN, Q_BLOCK), 1)
    sw = jnp.where(_tile4((jw <= tw) & (jw > tw - WINDOW)), sw, NEG)
    pw = jnp.exp(sw - jnp.max(sw, axis=0, keepdims=True))
    l_w = jnp.sum(pw, axis=0, keepdims=True)
    pw = pw.astype(bf)
    acc_w = jnp.zeros((HEAD_DIM, NSA_LANES), f32)
    for d in range(WIN_SPAN // WIN_CHUNK):
        acc_w = acc_w + jnp.dot(vwT_ref[0, w0 + d], pw[d * WIN_CHUNK:(d + 1) * WIN_CHUNK, :], preferred_element_type=f32)
    o_w = acc_w * (1.0 / l_w)

    gate = jax.nn.sigmoid(gT_ref[0])
    outs = []
    for g in range(A_GROUP):
        cols = slice(g * Q_BLOCK, (g + 1) * Q_BLOCK)
        outs.append(gate[3 * g:3 * g + 1, :] * o_c[:, cols] + gate[3 * g + 1:3 * g + 2, :] * o_s[:, cols]
                    + gate[3 * g + 2:3 * g + 3, :] * o_w[:, cols])
    o_ref[...] = jnp.concatenate(outs, axis=0).T


def _cmp_prompt_kernel(x_ref, pe_ref, w1_ref, b1_ref, w2T_ref, ck_ref, cvT_ref, bias_ref):
    f32, bf = jnp.float32, jnp.bfloat16
    nseg = ck_ref.shape[1]
    row_w = 2 * A_KV_HEADS * HEAD_DIM
    for c in range(2):
        bias = jnp.dot(pe_ref[c], w1_ref[c], preferred_element_type=f32,
                       precision=lax.Precision.HIGHEST) + b1_ref[c]
        bias_ref[c] = bias
        for k in range(A_KV_HEADS):
            col = (c * A_KV_HEADS + k) * HEAD_DIM
            y0 = jnp.zeros((nseg, CMP_HIDDEN), f32)
            y1 = jnp.zeros((nseg, CMP_HIDDEN), f32)
            for j in range(CMP_STRIDE):
                xj = x_ref[:, j * row_w + col:j * row_w + col + HEAD_DIM].astype(bf)
                w0 = w1_ref[c, j * HEAD_DIM:(j + 1) * HEAD_DIM, :].astype(bf)
                w1 = w1_ref[c, (CMP_STRIDE + j) * HEAD_DIM:(CMP_STRIDE + j + 1) * HEAD_DIM, :].astype(bf)
                y0 = y0 + jnp.dot(xj, w0, preferred_element_type=f32)
                y1 = y1 + jnp.dot(xj, w1, preferred_element_type=f32)
            pre = y0 + pltpu.roll(y1, nseg - 1, 0)
            hid = jnp.maximum(pre + bias, 0.0).astype(bf)
            if c == 0:
                ck_ref[k] = lax.dot_general(hid, w2T_ref[c].astype(bf), (((1,), (1,)), ((), ())),
                                            preferred_element_type=f32)
            else:
                cvT_ref[k] = lax.dot_general(w2T_ref[c].astype(bf), hid, (((1,), (1,)), ((), ())),
                                             preferred_element_type=f32)


def compress_prompt(x_cmp, pe, w1, b1, w2):
    T = x_cmp.shape[0]
    nseg = T // CMP_STRIDE
    return pl.pallas_call(
        _cmp_prompt_kernel,
        out_shape=(jax.ShapeDtypeStruct((A_KV_HEADS, nseg, HEAD_DIM), jnp.float32),
                   jax.ShapeDtypeStruct((A_KV_HEADS, HEAD_DIM, nseg), jnp.float32),
                   jax.ShapeDtypeStruct((2, 1, CMP_HIDDEN), jnp.float32)),
        compiler_params=pltpu.CompilerParams(vmem_limit_bytes=VMEM_LIMIT_BYTES),
        name="compress_prompt",
    )(x_cmp.reshape(nseg, CMP_STRIDE * x_cmp.shape[1]), pe.reshape(2, 1, CMP_BLOCK * HEAD_DIM), w1, b1.reshape(2, 1, CMP_HIDDEN), w2.transpose(0, 2, 1))


def nsa_prompt_pallas(q_src, ga, kva, pe, w1, b1, w2):
    bf = jnp.bfloat16
    q_arr, q_off = q_src
    q_cb = q_off // (A_GROUP * HEAD_DIM)
    T = kva.shape[0]
    nqb = T // Q_BLOCK
    assert T % KEY_CHUNK == 0 and T >= WIN_SPAN, T
    ncp = T // CMP_STRIDE
    kvw = A_KV_HEADS * HEAD_DIM
    ck, cvT, pe_bias = compress_prompt(kva[:, 0:2 * kvw], pe, w1, b1, w2)

    def heads(x):
        return x.reshape(T, A_KV_HEADS, HEAD_DIM).transpose(1, 0, 2)

    def chunksT(x, chunk):
        return x.reshape(T // chunk, chunk, A_KV_HEADS, HEAD_DIM).transpose(2, 0, 3, 1).astype(bf)

    pos = np.arange(T)
    key_pos = jnp.broadcast_to(jnp.asarray(_pos_pieces(pos)), (A_KV_HEADS, T, POS_PIECES))
    zpad = jnp.zeros((A_KV_HEADS, T, AUG - HEAD_DIM - POS_PIECES), jnp.float32)
    onehot = jnp.broadcast_to(jnp.asarray((pos[:, None] // SLC_BLOCK == np.arange(SLC_SLOTS)[None, :]).astype(np.float32)),
                              (A_KV_HEADS, T, SLC_SLOTS))
    ks_aug = jnp.concatenate([heads(kva[:, 2 * kvw:3 * kvw]), key_pos, zpad, onehot], axis=-1).astype(bf)
    kw_aug = jnp.concatenate([heads(kva[:, 4 * kvw:5 * kvw]), key_pos, zpad], axis=-1).astype(bf)
    c_end = np.arange(ncp) * CMP_STRIDE + (CMP_BLOCK - 1)
    ck_aug = jnp.concatenate([ck, jnp.broadcast_to(jnp.asarray(_pos_pieces(c_end)), (A_KV_HEADS, ncp, POS_PIECES)),
                              jnp.zeros((A_KV_HEADS, ncp, AUG - HEAD_DIM - POS_PIECES), jnp.float32)], axis=-1).astype(bf)
    vsT = chunksT(kva[:, 3 * kvw:4 * kvw], KEY_CHUNK)
    vwT = chunksT(kva[:, 5 * kvw:6 * kvw], WIN_CHUNK)
    c_start = c_end - (CMP_BLOCK - 1)
    s_start = np.arange(SLC_SLOTS) * SLC_BLOCK
    ovT = ((c_start[None, :] < s_start[:, None] + SLC_BLOCK) & (c_end[None, :] >= s_start[:, None])
           & (np.arange(ncp)[None, :] < ncp - 1))
    ovT = jnp.asarray(ovT.astype(np.float32)).astype(bf)
    gT = jnp.pad(ga.T.reshape(A_KV_HEADS, 3 * A_GROUP, T), ((0, 0), (0, 16 - 3 * A_GROUP), (0, 0)))

    y = pl.pallas_call(
        _nsa_prompt_kernel,
        grid=(A_KV_HEADS, nqb),
        in_specs=[
            pl.BlockSpec((Q_BLOCK, A_GROUP * HEAD_DIM), lambda k, i: (i, q_cb + k)),
            pl.BlockSpec((1, 16, NSA_LANES), lambda k, i: (k, 0, 0)),
            pl.BlockSpec((1, 16, Q_BLOCK), lambda k, i: (k, 0, i)),
            pl.BlockSpec((1, ncp, AUG), lambda k, i: (k, 0, 0)),
            pl.BlockSpec((1, HEAD_DIM, ncp), lambda k, i: (k, 0, 0)),
            pl.BlockSpec((SLC_SLOTS, ncp), lambda k, i: (0, 0)),
            pl.BlockSpec((1, T, AUG + SLC_SLOTS), lambda k, i: (k, 0, 0)),
            pl.BlockSpec((1, T // KEY_CHUNK, HEAD_DIM, KEY_CHUNK), lambda k, i: (k, 0, 0, 0)),
            pl.BlockSpec((1, T, AUG), lambda k, i: (k, 0, 0)),
            pl.BlockSpec((1, T // WIN_CHUNK, HEAD_DIM, WIN_CHUNK), lambda k, i: (k, 0, 0, 0)),
        ],
        out_specs=pl.BlockSpec((Q_BLOCK, A_GROUP * HEAD_DIM), lambda k, i: (i, k)),
        out_shape=jax.ShapeDtypeStruct((T, A_WIDTH), jnp.float32),
        scratch_shapes=[pltpu.VMEM((AUG + SLC_SLOTS, NSA_LANES), bf),
                        pltpu.SMEM((SLC_SLOTS * SLC_BLOCK // KEY_CHUNK,), jnp.int32)],
        compiler_params=_params("parallel", "arbitrary"),
        name="nsa_prompt",
    )(q_arr, _slope_rows(), gT, ck_aug, cvT.astype(bf), ovT, ks_aug, vsT, kw_aug, vwT)
    return y, pe_bias


PAGE_SIZE = 128
N_PAGES = PAST_LEN // PAGE_SIZE
PAGE_SEGS = PAGE_SIZE // CMP_STRIDE
PAST_SEGS = PAST_LEN // CMP_STRIDE
ROW_W = 4 * A_KV_HEADS * HEAD_DIM
KVW = A_KV_HEADS * HEAD_DIM
NS_SAMPLE = PAST_LEN // SLC_BLOCK + 1


def _nt(a, b):
    return lax.dot_general(a, b, (((1,), (1,)), ((), ())), preferred_element_type=jnp.float32)


def _dot3(x, w):
    return sum(jnp.dot(p, w, preferred_element_type=jnp.float32) for p in _split3(x))


def _nsa_sample_kernel(pt_ref, q_ref, g_ref, new_ref, *refs):
    del pt_ref
    pages = refs[:N_PAGES]
    (win_ref, w1_ref, bias_ref, w2bd_ref, bmask_ref, basec_ref, bases_ref, basew_ref, slope_ref,
     grp_ref, ov_ref, exp_ref, o_ref, wout_ref, xc_ref) = refs[N_PAGES:]
    f32, bf = jnp.float32, jnp.bfloat16

    q = q_ref[0] * (HEAD_DIM ** -0.5)
    qbd = jnp.where(bmask_ref[...] > 0.0, jnp.concatenate([q] * A_KV_HEADS, axis=1), 0.0).astype(bf)
    qbd_f = qbd.astype(f32)
    new = new_ref[0]

    def slab(off, width):
        return jnp.concatenate([pg[0, :, off:off + width] for pg in pages], axis=0)

    ckv = []
    for c in range(2):
        for j in range(CMP_STRIDE):
            s = slab(j * ROW_W + c * KVW, KVW).astype(bf)
            for k in range(A_KV_HEADS):
                xc_ref[k * PAST_SEGS:(k + 1) * PAST_SEGS, j * HEAD_DIM:(j + 1) * HEAD_DIM] = (
                    s[:, k * HEAD_DIM:(k + 1) * HEAD_DIM])
        y = jnp.dot(xc_ref[...], w1_ref[c], preferred_element_type=f32)
        pre = y[:, 0:CMP_HIDDEN] + pltpu.roll(y[:, CMP_HIDDEN:2 * CMP_HIDDEN], A_KV_HEADS * PAST_SEGS - 1, 0)
        hid = jnp.maximum(pre + bias_ref[c], 0.0).astype(bf)
        hid = jnp.concatenate([hid[k * PAST_SEGS:(k + 1) * PAST_SEGS] for k in range(A_KV_HEADS)], axis=1)
        ckv.append(jnp.dot(hid, w2bd_ref[c], preferred_element_type=f32).astype(bf))
    ck, cv = ckv

    sc = _nt(qbd, ck) + basec_ref[...]
    ec = jnp.exp(sc - jnp.max(sc, axis=1, keepdims=True))
    pc = ec / jnp.sum(ec, axis=1, keepdims=True)
    o_c = jnp.dot(pc.astype(bf), cv, preferred_element_type=f32)

    pov = _dot3(pc, ov_ref[...])
    p_hi, p_mid, p_lo = _split3(pov)
    imp = sum(jnp.dot(grp_ref[...], p, preferred_element_type=f32) for p in (p_hi, p_mid, p_lo))

    lane = lax.broadcasted_iota(jnp.int32, (A_HEADS, PAST_SEGS), 1)
    score = jnp.where((lane == 0) | (lane == NS_SAMPLE - 1) | (lane == NS_SAMPLE - 2), FORCE,
                      jnp.where(lane < NS_SAMPLE, imp, -jnp.inf))
    rr = lax.broadcasted_iota(jnp.int32, (PAST_SEGS, PAST_SEGS), 0)
    cc = lax.broadcasted_iota(jnp.int32, (PAST_SEGS, PAST_SEGS), 1)
    row_id = lax.broadcasted_iota(jnp.int32, (A_HEADS, PAST_SEGS), 0)
    selrows = jnp.zeros((A_HEADS, PAST_SEGS), f32)
    for k in range(A_KV_HEADS):
        row = jnp.broadcast_to(score[A_GROUP * k:A_GROUP * k + 1, :], (PAST_SEGS, PAST_SEGS))
        col = jnp.max(jnp.where(rr == cc, row, -jnp.inf), axis=1, keepdims=True)
        ahead = (col > row) | ((col == row) & (rr < cc))
        rank = jnp.sum(jnp.where(ahead, 1.0, 0.0), axis=0, keepdims=True)
        sel_k = jnp.where(rank < float(SLC_TOP), 1.0, 0.0)
        selrows = jnp.where(row_id // A_GROUP == k, jnp.broadcast_to(sel_k, (A_HEADS, PAST_SEGS)), selrows)
    selseg = jnp.dot(selrows.astype(bf), exp_ref[...], preferred_element_type=f32)
    maskbias = jnp.where(selseg > 0.5, 0.0, NEG)

    slope = slope_ref[...]
    scores = []
    for j in range(CMP_STRIDE):
        kj = slab(j * ROW_W + 2 * KVW, KVW).astype(bf)
        scores.append(_nt(qbd, kj) + (bases_ref[...] + slope * float(j)) + maskbias)
    s_new = jnp.sum(qbd_f * new[:, 2 * KVW:3 * KVW], axis=1, keepdims=True)
    mx = scores[0]
    for s in scores[1:]:
        mx = jnp.maximum(mx, s)
    m = jnp.maximum(jnp.max(mx, axis=1, keepdims=True), s_new)
    p_new = jnp.exp(s_new - m)
    acc = p_new * new[:, 3 * KVW:4 * KVW]
    psum = jnp.zeros((A_HEADS, PAST_SEGS), f32)
    for j in range(CMP_STRIDE):
        p = jnp.exp(scores[j] - m)
        psum = psum + p
        vj = slab(j * ROW_W + 3 * KVW, KVW).astype(bf)
        acc = acc + jnp.dot(p.astype(bf), vj, preferred_element_type=f32)
    o_s = acc / (jnp.sum(psum, axis=1, keepdims=True) + p_new)

    kw = win_ref[0, :, 0:KVW].astype(bf)
    vw = win_ref[0, :, KVW:2 * KVW].astype(bf)
    sw = _nt(qbd, kw) + basew_ref[...]
    s_neww = jnp.sum(qbd_f * new[:, 4 * KVW:5 * KVW], axis=1, keepdims=True)
    m = jnp.maximum(jnp.max(sw, axis=1, keepdims=True), s_neww)
    pw = jnp.exp(sw - m)
    p_new = jnp.exp(s_neww - m)
    o_w = (jnp.dot(pw.astype(bf), vw, preferred_element_type=f32) + p_new * new[:, 5 * KVW:6 * KVW]) / (
        jnp.sum(pw, axis=1, keepdims=True) + p_new)

    gate = jax.nn.sigmoid(g_ref[0])
    o = gate[:, 0:1] * o_c + gate[:, 1:2] * o_s + gate[:, 2:3] * o_w
    for k in range(A_KV_HEADS):
        o_ref[0, A_GROUP * k:A_GROUP * (k + 1), :] = o[A_GROUP * k:A_GROUP * (k + 1), k * HEAD_DIM:(k + 1) * HEAD_DIM]

    wout_ref[0, 0:WINDOW - 1, :] = win_ref[0, 1:WINDOW, :]
    wout_ref[0, WINDOW - 1:WINDOW, :] = new[:, 4 * KVW:6 * KVW]


def _sample_constants():
    slopes = (2.0 ** (-8.0 * np.arange(1, A_HEADS + 1, dtype=np.float64) / A_HEADS)).astype(np.float32)[:, None]
    heads = np.arange(A_HEADS)[:, None]
    bmask = (np.arange(KVW)[None, :] // HEAD_DIM == heads // A_GROUP).astype(np.float32)
    n = np.arange(PAST_SEGS)[None, :]
    c_end = n * CMP_STRIDE + (CMP_BLOCK - 1)
    basec = np.where(n < PAST_SEGS - 1, slopes * (c_end - PAST_LEN), NEG).astype(np.float32)
    bases = (slopes * (n * CMP_STRIDE - PAST_LEN)).astype(np.float32)
    r = np.arange(WINDOW)[None, :]
    basew = np.where(r >= 1, slopes * (r - WINDOW), NEG).astype(np.float32)
    slope = np.broadcast_to(slopes, (A_HEADS, PAST_SEGS)).astype(np.float32)
    grp = (heads // A_GROUP == heads.T // A_GROUP).astype(np.float32)
    c_start = np.arange(PAST_SEGS) * CMP_STRIDE
    s_start = np.arange(PAST_SEGS) * SLC_BLOCK
    ov = ((c_start[:, None] < s_start[None, :] + SLC_BLOCK) & (c_start[:, None] + CMP_BLOCK - 1 >= s_start[None, :])
          & (np.arange(PAST_SEGS)[:, None] < PAST_SEGS - 1) & (np.arange(PAST_SEGS)[None, :] < NS_SAMPLE))
    expand = (np.arange(PAST_SEGS)[None, :] * CMP_STRIDE // SLC_BLOCK == np.arange(PAST_SEGS)[:, None])
    as_bf = lambda a: jnp.asarray(a.astype(np.float32)).astype(jnp.bfloat16)
    return (jnp.asarray(bmask), jnp.asarray(basec), jnp.asarray(bases), jnp.asarray(basew), jnp.asarray(slope),
            as_bf(grp), as_bf(ov), as_bf(expand))


def nsa_sample_pallas(qa, ga, kva, pool, page_table, win_buf, layer, w1, w2, pe_bias):
    bf = jnp.bfloat16
    B = qa.shape[0]
    npool = pool.shape[1]
    page_table = page_table + layer * npool
    seg_w = CMP_STRIDE * ROW_W
    w1r = w1.reshape(2, CMP_RATIO, CMP_STRIDE * HEAD_DIM, CMP_HIDDEN).transpose(0, 2, 1, 3)
    w1r = w1r.reshape(2, CMP_STRIDE * HEAD_DIM, CMP_RATIO * CMP_HIDDEN).astype(bf)
    w2bd = jnp.einsum('kl,ced->ckeld', jnp.eye(A_KV_HEADS, dtype=w2.dtype), w2)
    w2bd = w2bd.reshape(2, A_KV_HEADS * CMP_HIDDEN, KVW).astype(bf)
    consts = _sample_constants()
    full = lambda shape: pl.BlockSpec(shape, lambda b, pt: (0,) * len(shape))
    page_specs = [pl.BlockSpec((1, PAGE_SEGS, seg_w), functools.partial(lambda p, b, pt: (pt[b, p], 0, 0), p))
                  for p in range(N_PAGES)]
    grid_spec = pltpu.PrefetchScalarGridSpec(
        num_scalar_prefetch=1,
        grid=(B,),
        in_specs=[
            pl.BlockSpec((1, A_HEADS, HEAD_DIM), lambda b, pt: (b, 0, 0)),
            pl.BlockSpec((1, A_HEADS, 3), lambda b, pt: (b, 0, 0)),
            pl.BlockSpec((1, 1, N_KV_BRANCH * KVW), lambda b, pt: (b, 0, 0)),
            *page_specs,
            pl.BlockSpec((1, WINDOW, 2 * KVW), lambda b, pt: (layer * B + b, 0, 0)),
            full(w1r.shape), full((2, 1, CMP_HIDDEN)), full(w2bd.shape),
            *[full(c.shape) for c in consts],
        ],
        out_specs=[
            pl.BlockSpec((1, A_HEADS, HEAD_DIM), lambda b, pt: (b, 0, 0)),
            pl.BlockSpec((1, WINDOW, 2 * KVW), lambda b, pt: (b, 0, 0)),
        ],
        scratch_shapes=[pltpu.VMEM((A_KV_HEADS * PAST_SEGS, CMP_STRIDE * HEAD_DIM), bf)],
    )
    pool3 = pool.reshape(pool.shape[0] * npool, PAGE_SEGS, seg_w)
    o, wout = pl.pallas_call(
        _nsa_sample_kernel,
        grid_spec=grid_spec,
        out_shape=(jax.ShapeDtypeStruct((B, A_HEADS, HEAD_DIM), jnp.float32),
                   jax.ShapeDtypeStruct((B, WINDOW, 2 * KVW), jnp.float32)),
        compiler_params=_params("arbitrary"),
        name="nsa_sample",
    )(page_table, qa.reshape(B, A_HEADS, HEAD_DIM), ga.reshape(B, A_HEADS, 3), kva.reshape(B, 1, N_KV_BRANCH * KVW),
      *([pool3] * N_PAGES), win_buf.reshape(win_buf.shape[0] * B, WINDOW, 2 * KVW), w1r, pe_bias.reshape(2, 1, CMP_HIDDEN), w2bd, *consts)
    return o.reshape(B, A_WIDTH), wout


SEG_PITCH = 24


def _nsa_decode_kernel(pt_ref, q_ref, g_ref, new_ref, *refs, has_into):
    del pt_ref
    pages = refs[:N_PAGES]
    rest = refs[N_PAGES:]
    (win_ref, w1_ref, bias_ref, w2bd_ref, bmask_ref, basec_ref, bases_ref, basew_ref, slope_ref,
     grp_ref, ov_ref, exp_ref) = rest[:12]
    o_ref, wout_ref, tr_ref, xc_ref = rest[12 + int(has_into):]
    f32, bf = jnp.float32, jnp.bfloat16

    q = q_ref[0] * (HEAD_DIM ** -0.5)
    qbd = jnp.where(bmask_ref[...] > 0.0, jnp.concatenate([q] * A_KV_HEADS, axis=1), 0.0).astype(bf)
    qbd_f = qbd.astype(f32)
    new = new_ref[0]

    ckv = []
    for c in range(2):
        for p in range(N_PAGES):
            for kp in range(2):
                r0 = c * KVW + kp * 2 * HEAD_DIM
                t = pages[p][0, r0:r0 + 2 * HEAD_DIM, :].T
                for s in range(PAGE_SEGS):
                    t0 = (kp * PAST_SEGS + p * PAGE_SEGS + s) * SEG_PITCH
                    tr_ref[t0:t0 + CMP_STRIDE, :] = t[s * CMP_STRIDE:(s + 1) * CMP_STRIDE, :]
        for j in range(CMP_STRIDE):
            for kp in range(2):
                piece = tr_ref[pl.ds(kp * PAST_SEGS * SEG_PITCH + j, PAST_SEGS, stride=SEG_PITCH), :]
                for kk in range(2):
                    k = 2 * kp + kk
                    xc_ref[k * PAST_SEGS:(k + 1) * PAST_SEGS, j * HEAD_DIM:(j + 1) * HEAD_DIM] = (
                        piece[:, kk * HEAD_DIM:(kk + 1) * HEAD_DIM])
        y = jnp.dot(xc_ref[...].astype(bf), w1_ref[c], preferred_element_type=f32)
        pre = y[:, 0:CMP_HIDDEN] + pltpu.roll(y[:, CMP_HIDDEN:2 * CMP_HIDDEN], A_KV_HEADS * PAST_SEGS - 1, 0)
        hid = jnp.maximum(pre + bias_ref[c], 0.0).astype(bf)
        hid = jnp.concatenate([hid[k * PAST_SEGS:(k + 1) * PAST_SEGS] for k in range(A_KV_HEADS)], axis=1)
        ckv.append(jnp.dot(hid, w2bd_ref[c], preferred_element_type=f32).astype(bf))
    ck, cv = ckv

    sc = _nt(qbd, ck) + basec_ref[...]
    ec = jnp.exp(sc - jnp.max(sc, axis=1, keepdims=True))
    pc = ec / jnp.sum(ec, axis=1, keepdims=True)
    o_c = jnp.dot(pc.astype(bf), cv, preferred_element_type=f32)

    pov = _dot3(pc, ov_ref[...])
    imp = sum(jnp.dot(grp_ref[...], p, preferred_element_type=f32) for p in _split3(pov))

    lane = lax.broadcasted_iota(jnp.int32, (A_HEADS, PAST_SEGS), 1)
    score = jnp.where((lane == 0) | (lane == NS_SAMPLE - 1) | (lane == NS_SAMPLE - 2), FORCE,
                      jnp.where(lane < NS_SAMPLE, imp, -jnp.inf))
    rr = lax.broadcasted_iota(jnp.int32, (PAST_SEGS, PAST_SEGS), 0)
    cc = lax.broadcasted_iota(jnp.int32, (PAST_SEGS, PAST_SEGS), 1)
    row_id = lax.broadcasted_iota(jnp.int32, (A_HEADS, PAST_SEGS), 0)
    selrows = jnp.zeros((A_HEADS, PAST_SEGS), f32)
    for k in range(A_KV_HEADS):
        row = jnp.broadcast_to(score[A_GROUP * k:A_GROUP * k + 1, :], (PAST_SEGS, PAST_SEGS))
        col = jnp.max(jnp.where(rr == cc, row, -jnp.inf), axis=1, keepdims=True)
        ahead = (col > row) | ((col == row) & (rr < cc))
        rank = jnp.sum(jnp.where(ahead, 1.0, 0.0), axis=0, keepdims=True)
        sel_k = jnp.where(rank < float(SLC_TOP), 1.0, 0.0)
        selrows = jnp.where(row_id // A_GROUP == k, jnp.broadcast_to(sel_k, (A_HEADS, PAST_SEGS)), selrows)
    selpos = jnp.dot(selrows.astype(bf), exp_ref[...], preferred_element_type=f32)

    slope = slope_ref[...]
    scores = []
    for p in range(N_PAGES):
        kp_ = pages[p][0, 2 * KVW:3 * KVW, :].astype(bf)
        mask = jnp.where(selpos[:, p * PAGE_SIZE:(p + 1) * PAGE_SIZE] > 0.5, 0.0, NEG)
        scores.append(jnp.dot(qbd, kp_, preferred_element_type=f32) + (bases_ref[...] + slope * float(p * PAGE_SIZE)) + mask)
    s_new = jnp.sum(qbd_f * new[:, 2 * KVW:3 * KVW], axis=1, keepdims=True)
    mx = scores[0]
    for s in scores[1:]:
        mx = jnp.maximum(mx, s)
    m = jnp.maximum(jnp.max(mx, axis=1, keepdims=True), s_new)
    p_new = jnp.exp(s_new - m)
    acc = p_new * new[:, 3 * KVW:4 * KVW]
    psum = jnp.zeros((A_HEADS, PAGE_SIZE), f32)
    for p in range(N_PAGES):
        pr = jnp.exp(scores[p] - m)
        psum = psum + pr
        acc = acc + _nt(pr.astype(bf), pages[p][0, 3 * KVW:4 * KVW, :].astype(bf))
    o_s = acc / (jnp.sum(psum, axis=1, keepdims=True) + p_new)

    sw = jnp.dot(qbd, win_ref[0, 0:KVW, :].astype(bf), preferred_element_type=f32) + basew_ref[...]
    s_neww = jnp.sum(qbd_f * new[:, 4 * KVW:5 * KVW], axis=1, keepdims=True)
    m = jnp.maximum(jnp.max(sw, axis=1, keepdims=True), s_neww)
    pw = jnp.exp(sw - m)
    p_new = jnp.exp(s_neww - m)
    o_w = (_nt(pw.astype(bf), win_ref[0, KVW:2 * KVW, :].astype(bf)) + p_new * new[:, 5 * KVW:6 * KVW]) / (
        jnp.sum(pw, axis=1, keepdims=True) + p_new)

    gate = jax.nn.sigmoid(g_ref[0])
    o = gate[:, 0:1] * o_c + gate[:, 1:2] * o_s + gate[:, 2:3] * o_w
    for k in range(A_KV_HEADS):
        o_ref[0, A_GROUP * k:A_GROUP * (k + 1), :] = o[A_GROUP * k:A_GROUP * (k + 1), k * HEAD_DIM:(k + 1) * HEAD_DIM]

    wr = lax.broadcasted_iota(jnp.int32, (WINDOW, 2 * KVW), 0)
    wc = lax.broadcasted_iota(jnp.int32, (WINDOW, 2 * KVW), 1)
    new_col = jnp.sum(jnp.where(wr == wc, jnp.broadcast_to(new[:, 4 * KVW:6 * KVW], (WINDOW, 2 * KVW)), 0.0),
                      axis=1, keepdims=True)
    wout_ref[0] = jnp.where(wc == WINDOW - 1, new_col, pltpu.roll(win_ref[0], WINDOW - 1, 1))


def _decode_constants():
    slopes = (2.0 ** (-8.0 * np.arange(1, A_HEADS + 1, dtype=np.float64) / A_HEADS)).astype(np.float32)[:, None]
    heads = np.arange(A_HEADS)[:, None]
    bmask = (np.arange(KVW)[None, :] // HEAD_DIM == heads // A_GROUP).astype(np.float32)
    n = np.arange(PAST_SEGS)[None, :]
    c_end = n * CMP_STRIDE + (CMP_BLOCK - 1)
    basec = np.where(n < PAST_SEGS - 1, slopes * (c_end - PAST_LEN), NEG).astype(np.float32)
    bases = (slopes * (np.arange(PAGE_SIZE)[None, :] - PAST_LEN)).astype(np.float32)
    r = np.arange(WINDOW)[None, :]
    basew = np.where(r >= 1, slopes * (r - WINDOW), NEG).astype(np.float32)
    slope = np.broadcast_to(slopes, (A_HEADS, PAGE_SIZE)).astype(np.float32)
    grp = (heads // A_GROUP == heads.T // A_GROUP).astype(np.float32)
    c_start = np.arange(PAST_SEGS) * CMP_STRIDE
    s_start = np.arange(PAST_SEGS) * SLC_BLOCK
    ov = ((c_start[:, None] < s_start[None, :] + SLC_BLOCK) & (c_start[:, None] + CMP_BLOCK - 1 >= s_start[None, :])
          & (np.arange(PAST_SEGS)[:, None] < PAST_SEGS - 1) & (np.arange(PAST_SEGS)[None, :] < NS_SAMPLE))
    expand = (np.arange(PAST_LEN)[None, :] // SLC_BLOCK == np.arange(PAST_SEGS)[:, None])
    as_bf = lambda a: jnp.asarray(a.astype(np.float32)).astype(jnp.bfloat16)
    return (jnp.asarray(bmask), jnp.asarray(basec), jnp.asarray(bases), jnp.asarray(basew), jnp.asarray(slope),
            as_bf(grp), as_bf(ov), as_bf(expand))


def nsa_decode_pallas(qa, ga, kva, pool, page_table, win_buf, layer, w1, w2, pe_bias, win_into=None):
    bf = jnp.bfloat16
    B = qa.shape[0]
    depth, npool = pool.shape[:2]
    page_table = page_table + layer * npool
    poolT = pool.transpose(0, 1, 3, 4, 5, 2).reshape(depth * npool, ROW_W, PAGE_SIZE)
    winT = win_buf.transpose(0, 1, 3, 4, 5, 2).reshape(depth * B, 2 * KVW, WINDOW)
    w1r = w1.reshape(2, CMP_RATIO, CMP_STRIDE * HEAD_DIM, CMP_HIDDEN).transpose(0, 2, 1, 3)
    w1r = w1r.reshape(2, CMP_STRIDE * HEAD_DIM, CMP_RATIO * CMP_HIDDEN).astype(bf)
    w2bd = jnp.einsum('kl,ced->ckeld', jnp.eye(A_KV_HEADS, dtype=w2.dtype), w2)
    w2bd = w2bd.reshape(2, A_KV_HEADS * CMP_HIDDEN, KVW).astype(bf)
    consts = _decode_constants()
    full = lambda shape: pl.BlockSpec(shape, lambda b, pt: (0,) * len(shape))
    page_specs = [pl.BlockSpec((1, ROW_W, PAGE_SIZE), functools.partial(lambda p, b, pt: (pt[b, p], 0, 0), p))
                  for p in range(N_PAGES)]
    grid_spec = pltpu.PrefetchScalarGridSpec(
        num_scalar_prefetch=1,
        grid=(B,),
        in_specs=[
            pl.BlockSpec((1, A_HEADS, HEAD_DIM), lambda b, pt: (b, 0, 0)),
            pl.BlockSpec((1, A_HEADS, 3), lambda b, pt: (b, 0, 0)),
            pl.BlockSpec((1, 1, N_KV_BRANCH * KVW), lambda b, pt: (b, 0, 0)),
            *page_specs,
            pl.BlockSpec((1, 2 * KVW, WINDOW), lambda b, pt: (layer * B + b, 0, 0)),
            full(w1r.shape), full((2, 1, CMP_HIDDEN)), full(w2bd.shape),
            *[full(c.shape) for c in consts],
            *([] if win_into is None else [pl.BlockSpec(memory_space=pl.ANY)]),
        ],
        out_specs=[
            pl.BlockSpec((1, A_HEADS, HEAD_DIM), lambda b, pt: (b, 0, 0)),
            pl.BlockSpec((1, 2 * KVW, WINDOW), lambda b, pt: (layer * B + b, 0, 0)),
        ],
        scratch_shapes=[pltpu.VMEM((2 * PAST_SEGS * SEG_PITCH, 2 * HEAD_DIM), jnp.float32),
                        pltpu.VMEM((A_KV_HEADS * PAST_SEGS, CMP_STRIDE * HEAD_DIM), jnp.float32)],
    )
    args = (page_table, qa.reshape(B, A_HEADS, HEAD_DIM), ga.reshape(B, A_HEADS, 3), kva.reshape(B, 1, N_KV_BRANCH * KVW),
            *([poolT] * N_PAGES), winT, w1r, pe_bias.reshape(2, 1, CMP_HIDDEN), w2bd, *consts)
    aliases = {}
    if win_into is not None:
        args += (win_into,)
        aliases = {len(args) - 1: 1}
    o, wout = pl.pallas_call(
        functools.partial(_nsa_decode_kernel, has_into=win_into is not None),
        grid_spec=grid_spec,
        out_shape=(jax.ShapeDtypeStruct((B, A_HEADS, HEAD_DIM), jnp.float32),
                   jax.ShapeDtypeStruct((depth * B, 2 * KVW, WINDOW), jnp.float32)),
        input_output_aliases=aliases,
        compiler_params=_params("arbitrary"),
        name="nsa_decode",
    )(*args)
    return o.reshape(B, A_WIDTH), wout


MLSTM_L = 128
VAUG = 2 * V_DIM


def _mlstm_prompt_kernel(q_ref, k_ref, v_ref, o_ref, fsrc_ref, ifT_ref, y_ref, c_ref, m_ref, *, f_off):
    f32, bf = jnp.float32, jnp.bfloat16
    L = MLSTM_L
    f_ref = fsrc_ref.at[:, f_off:f_off + B_HEADS]
    iT_ref = ifT_ref.at[0:B_HEADS, :]
    fT_ref = ifT_ref.at[B_HEADS:2 * B_HEADS, :]

    @pl.when(pl.program_id(0) == 0)
    def _():
        c_ref[...] = jnp.zeros_like(c_ref)
        m_ref[...] = jnp.zeros_like(m_ref)

    kT_all = k_ref[...].T * (QK_DIM ** -0.5)
    rr = lax.broadcasted_iota(jnp.int32, (L, L), 0)
    cc = lax.broadcasted_iota(jnp.int32, (L, L), 1)
    lower = rr >= cc
    tril = jnp.where(lower, 1.0, 0.0).astype(bf)
    triu = jnp.where(rr <= cc, 1.0, 0.0).astype(bf)
    b_col = sum(jnp.dot(tril, p, preferred_element_type=f32) for p in _split3(jax.nn.log_sigmoid(f_ref[...])))
    b_row = _dot3(jax.nn.log_sigmoid(fT_ref[...]), triu)
    a_row = iT_ref[...] - b_row
    ones_col = jnp.where(lax.broadcasted_iota(jnp.int32, (L, V_DIM), 1) == 0, 1.0, 0.0)

    for h in range(B_HEADS):
        m_prev = m_ref[h, 0:1, 0:1]
        a = a_row[h:h + 1, :]
        amat = jnp.where(lower, jnp.broadcast_to(a, (L, L)), -jnp.inf)
        big_m = jnp.maximum(m_prev, jnp.max(amat, axis=1, keepdims=True))
        dmat = jnp.exp(amat - big_m)
        inter = jnp.exp(m_prev - big_m)
        q = q_ref[:, h * QK_DIM:(h + 1) * QK_DIM].astype(bf)
        kT = kT_all[h * QK_DIM:(h + 1) * QK_DIM, :]
        vaug = jnp.concatenate([v_ref[:, h * V_DIM:(h + 1) * V_DIM], ones_col], axis=1).astype(bf)
        s = jnp.dot(q, kT.astype(bf), preferred_element_type=f32) * dmat
        r = (jnp.dot(s.astype(bf), vaug, preferred_element_type=f32)
             + inter * jnp.dot(q, c_ref[h].astype(bf), preferred_element_type=f32))
        m_new = b_col[:, h:h + 1] + big_m
        den = jnp.maximum(jnp.abs(r[:, V_DIM:V_DIM + 1]), jnp.exp(-m_new))
        y_ref[:, h * V_DIM:(h + 1) * V_DIM] = (jax.nn.sigmoid(o_ref[:, h * V_DIM:(h + 1) * V_DIM])
                                               * (r[:, 0:V_DIM] / den))
        m_end = big_m[L - 1:L, :]
        w_end = jnp.exp(a - m_end)
        kw_end = kT * w_end
        c_new = inter[L - 1:L, :] * c_ref[h] + jnp.dot(kw_end.astype(bf), vaug, preferred_element_type=f32)
        n_new = inter[L - 1:L, :] * c_ref[h, :, V_DIM:V_DIM + 1] + jnp.sum(kw_end, axis=1, keepdims=True)
        c_ref[h] = jnp.where(lax.broadcasted_iota(jnp.int32, (QK_DIM, VAUG), 1) == V_DIM, n_new, c_new)
        m_ref[h] = jnp.broadcast_to(m_new[L - 1:L, :], m_ref.shape[1:])


def mlstm_prompt_pallas(q_src, k_src, v_src, o_src, f_src, ifT, T):
    L = MLSTM_L
    (q_arr, q_cb), (k_arr, k_cb), (v_arr, v_cb), (o_arr, o_cb), (f_arr, f_cb, f_off) = q_src, k_src, v_src, o_src, f_src
    y, caug, m = pl.pallas_call(
        functools.partial(_mlstm_prompt_kernel, f_off=f_off),
        grid=(T // L,),
        in_specs=[
            pl.BlockSpec((L, B_HEADS * QK_DIM), lambda c: (c, q_cb)),
            pl.BlockSpec((L, B_HEADS * QK_DIM), lambda c: (c, k_cb)),
            pl.BlockSpec((L, B_WIDTH), lambda c: (c, v_cb)),
            pl.BlockSpec((L, B_WIDTH), lambda c: (c, o_cb)),
            pl.BlockSpec((L, f_arr.shape[1] if f_arr.shape[1] < 128 else 128), lambda c: (c, f_cb)),
            pl.BlockSpec((2 * B_HEADS, L), lambda c: (0, c)),
        ],
        out_specs=[
            pl.BlockSpec((L, B_WIDTH), lambda c: (c, 0)),
            pl.BlockSpec((B_HEADS, QK_DIM, VAUG), lambda c: (0, 0, 0)),
            pl.BlockSpec((B_HEADS, 8, 128), lambda c: (0, 0, 0)),
        ],
        out_shape=(jax.ShapeDtypeStruct((T, B_WIDTH), jnp.float32),
                   jax.ShapeDtypeStruct((B_HEADS, QK_DIM, VAUG), jnp.float32),
                   jax.ShapeDtypeStruct((B_HEADS, 8, 128), jnp.float32)),
        compiler_params=_params("arbitrary"),
        name="mlstm_prompt",
    )(q_arr, k_arr, v_arr, o_arr, f_arr, ifT)
    return y, caug[:, :, 0:V_DIM], caug[:, :, V_DIM], m[:, 0, 0]


def _mlstm_sample_kernel(q_ref, k_ref, qT_ref, kT_ref, v_ref, o_ref, i_ref, f_ref, m_ref, n_ref, c_ref,
                         y_ref, cn_ref, nn_ref, mn_ref):
    bb = q_ref.shape[0]
    scale = QK_DIM ** -0.5
    for b in range(bb):
        logf = jax.nn.log_sigmoid(f_ref[b])
        m_new = jnp.maximum(logf + m_ref[b], i_ref[b])
        d_all = jnp.exp(i_ref[b] - m_new)
        inter_all = jnp.exp(logf + m_ref[b] - m_new)
        floor_all = jnp.exp(-m_new)
        mn_ref[b] = m_new
        qk_all = jnp.sum(q_ref[b] * k_ref[b], axis=1, keepdims=True) * scale
        qn_all = jnp.sum(q_ref[b] * n_ref[b], axis=1, keepdims=True)
        for h in range(B_HEADS):
            d = d_all[:, h:h + 1]
            inter = inter_all[:, h:h + 1]
            s = qk_all[h:h + 1, :] * d
            c = c_ref[b, h]
            v = v_ref[b, h:h + 1, :]
            qc = qT_ref[b, :, h:h + 1]
            kc = kT_ref[b, :, h:h + 1] * scale
            num = inter * jnp.sum(qc * c, axis=0, keepdims=True) + s * v
            den = inter * qn_all[h:h + 1, :] + s
            hout = num / jnp.maximum(jnp.abs(den), floor_all[:, h:h + 1])
            y_ref[b, h:h + 1, :] = jax.nn.sigmoid(o_ref[b, h:h + 1, :]) * hout
            cn_ref[b, h] = inter * c + d * (kc * v)
            nn_ref[b, h:h + 1, :] = inter * n_ref[b, h:h + 1, :] + d * (k_ref[b, h:h + 1, :] * scale)


def mlstm_sample_pallas(qb, kb, vb, ib, fb, ob, state_C, state_n, state_m, layer, *, bb=8):
    B = qb.shape[0]
    nb = B // bb
    q3 = qb.reshape(B, B_HEADS, QK_DIM)
    k3 = kb.reshape(B, B_HEADS, QK_DIM)
    row8 = lambda x: x.reshape(-1, 1, B_HEADS)
    lay = layer * nb
    y, cn, nn, mn = pl.pallas_call(
        _mlstm_sample_kernel,
        grid=(nb,),
        in_specs=[
            pl.BlockSpec((bb, B_HEADS, QK_DIM), lambda i: (i, 0, 0)),
            pl.BlockSpec((bb, B_HEADS, QK_DIM), lambda i: (i, 0, 0)),
            pl.BlockSpec((bb, QK_DIM, B_HEADS), lambda i: (i, 0, 0)),
            pl.BlockSpec((bb, QK_DIM, B_HEADS), lambda i: (i, 0, 0)),
            pl.BlockSpec((bb, B_HEADS, V_DIM), lambda i: (i, 0, 0)),
            pl.BlockSpec((bb, B_HEADS, V_DIM), lambda i: (i, 0, 0)),
            pl.BlockSpec((bb, 1, B_HEADS), lambda i: (i, 0, 0)),
            pl.BlockSpec((bb, 1, B_HEADS), lambda i: (i, 0, 0)),
            pl.BlockSpec((bb, 1, B_HEADS), lambda i: (lay + i, 0, 0)),
            pl.BlockSpec((bb, B_HEADS, QK_DIM), lambda i: (lay + i, 0, 0)),
            pl.BlockSpec((bb, B_HEADS, QK_DIM, V_DIM), lambda i: (lay + i, 0, 0, 0)),
        ],
        out_specs=[
            pl.BlockSpec((bb, B_HEADS, V_DIM), lambda i: (i, 0, 0)),
            pl.BlockSpec((bb, B_HEADS, QK_DIM, V_DIM), lambda i: (i, 0, 0, 0)),
            pl.BlockSpec((bb, B_HEADS, QK_DIM), lambda i: (i, 0, 0)),
            pl.BlockSpec((bb, 1, B_HEADS), lambda i: (i, 0, 0)),
        ],
        out_shape=(jax.ShapeDtypeStruct((B, B_HEADS, V_DIM), jnp.float32),
                   jax.ShapeDtypeStruct((B, B_HEADS, QK_DIM, V_DIM), jnp.float32),
                   jax.ShapeDtypeStruct((B, B_HEADS, QK_DIM), jnp.float32),
                   jax.ShapeDtypeStruct((B, 1, B_HEADS), jnp.float32)),
        compiler_params=_params("arbitrary"),
        name="mlstm_sample",
    )(q3, k3, q3.transpose(0, 2, 1), k3.transpose(0, 2, 1), vb.reshape(B, B_HEADS, V_DIM),
      ob.reshape(B, B_HEADS, V_DIM), row8(ib), row8(fb), row8(state_m),
      state_n.reshape(-1, B_HEADS, QK_DIM), state_C.reshape(-1, B_HEADS, QK_DIM, V_DIM))
    return y.reshape(B, B_WIDTH), cn, nn, mn.reshape(B, B_HEADS)


(G_QA, G_KVA, G_GA, G_QB, G_KB, G_VB, G_IB, G_FB, G_OB, G_GM) = range(10)
WIDE_ORDER = (G_QA, G_VB, G_OB, G_KVA, G_QB, G_KB, G_GM)
NARROW_ORDER = (G_GA, G_IB, G_FB)
WIDE_OFF = dict(zip(WIDE_ORDER, np.cumsum((0,) + tuple(SPLITS[g] for g in WIDE_ORDER))[:-1].tolist()))
NARROW_OFF = dict(zip(NARROW_ORDER, np.cumsum((0,) + tuple(SPLITS[g] for g in NARROW_ORDER))[:-1].tolist()))
NARROW_W = 128


def project_in(h, w_in, b_in, layer):
    w_t = jnp.swapaxes(w_in, 1, 2)
    b_in = b_in[layer]

    def gather(order, pad_to):
        rows = jnp.concatenate([w_t[layer, BOUNDS[g]:BOUNDS[g + 1]] for g in order], axis=0)
        bias = jnp.concatenate([b_in[BOUNDS[g]:BOUNDS[g + 1]] for g in order], axis=0)
        pad = pad_to - rows.shape[0]
        return jnp.pad(rows, ((0, pad), (0, 0))), jnp.pad(bias, (0, pad))

    n_wide = sum(SPLITS[g] for g in WIDE_ORDER)
    wide = matmul_act(h, *gather(WIDE_ORDER, n_wide), nt=True)
    narrow = matmul_act(h, *gather(NARROW_ORDER, NARROW_W), nt=True, tn=NARROW_W)
    return wide, narrow


def kernel(x_prompt, x_sample, cache_nsa_kv, cache_win_kv, state_C, state_n, state_m, page_table,
           norm_g, w_in, b_in, cmp_pe, cmp_w1, cmp_b1, cmp_w2, w_up_a, w_up_b, w_out, w_mlp1, w_mlp2):
    Bp, Tp = x_prompt.shape[:2]
    Bs, Ts = x_sample.shape[:2]
    assert Bp == 1 and Ts == 1 and Tp >= WINDOW, (x_prompt.shape, x_sample.shape)
    x = jnp.concatenate([x_prompt.reshape(Tp, D_MODEL), x_sample.reshape(Bs, D_MODEL)], axis=0)
    kv_p, kv_s, win_p, win_all = [], [], [], None
    C_p, C_s, n_p, n_s, m_p, m_s = [], [], [], [], [], []
    kv4 = 4 * KVW
    w_out_bf = w_out.astype(jnp.bfloat16)
    w_mlp2_bf = w_mlp2.astype(jnp.bfloat16)
    for l in range(DEPTH):
        wide, narrow = project_in(rmsnorm_cast(x, norm_g[l, 0]), w_in, b_in, l)
        col = lambda g, rows: wide[rows, WIDE_OFF[g]:WIDE_OFF[g] + SPLITS[g]]
        ncol = lambda g, rows: narrow[rows, NARROW_OFF[g]:NARROW_OFF[g] + SPLITS[g]]
        pr, sr = slice(0, Tp), slice(Tp, Tp + Bs)

        kva = lax.optimization_barrier(col(G_KVA, pr))
        ya_p, pe_bias = nsa_prompt_pallas((wide, WIDE_OFF[G_QA]), ncol(G_GA, pr), kva, cmp_pe[l], cmp_w1[l], cmp_b1[l],
                                          cmp_w2[l])
        yb_p, C, n, m = mlstm_prompt_pallas(
            (wide, WIDE_OFF[G_QB] // (B_HEADS * QK_DIM)), (wide, WIDE_OFF[G_KB] // (B_HEADS * QK_DIM)),
            (wide, WIDE_OFF[G_VB] // B_WIDTH),
            (wide, WIDE_OFF[G_OB] // B_WIDTH), (narrow, 0, NARROW_OFF[G_FB]),
            narrow[pr, NARROW_OFF[G_IB]:NARROW_OFF[G_IB] + 2 * B_HEADS].T, Tp)
        kv_p.append(kva[:, 0:kv4].reshape(Bp, Tp, 4, A_KV_HEADS, HEAD_DIM))
        win_p.append(kva[Tp - WINDOW:, kv4:].reshape(Bp, WINDOW, 2, A_KV_HEADS, HEAD_DIM))
        C_p.append(C[None])
        n_p.append(n[None])
        m_p.append(m[None])

        kva = col(G_KVA, sr)
        ya_s, win_all = nsa_decode_pallas(col(G_QA, sr), ncol(G_GA, sr), kva, cache_nsa_kv, page_table, cache_win_kv, l,
                                          cmp_w1[l], cmp_w2[l], pe_bias, win_into=win_all)
        yb_s, C, n, m = mlstm_sample_pallas(col(G_QB, sr), col(G_KB, sr), col(G_VB, sr), ncol(G_IB, sr), ncol(G_FB, sr),
                                            col(G_OB, sr), state_C, state_n, state_m, l)
        kv_s.append(kva[:, 0:kv4].reshape(Bs, Ts, 4, A_KV_HEADS, HEAD_DIM))
        C_s.append(C)
        n_s.append(n)
        m_s.append(m)

        gm_src = (wide, WIDE_OFF[G_GM])
        mix = mix_matmul(ya_p, yb_p, w_up_a, w_up_b, gm_src, row0=0, rows_total=Tp + Bs, layer=l)
        mix = mix_matmul(ya_s, yb_s, w_up_a, w_up_b, gm_src, row0=Tp, rows_total=Tp + Bs, into=mix, layer=l)
        x = matmul_norm_res(mix, w_out_bf, x, norm_g[l, 1], layer=l)
        hid = matmul_act(rmsnorm_cast(x, norm_g[l, 2]), w_mlp1, jnp.zeros((D_FF,), jnp.float32), nt=False,
                         act="relu2", out_dtype=jnp.bfloat16, layer=l)
        x = matmul_norm_res(hid, w_mlp2_bf, x, norm_g[l, 3], layer=l)

    win_s = win_all.reshape(DEPTH, Bs, 2, A_KV_HEADS, HEAD_DIM, WINDOW).transpose(0, 1, 5, 2, 3, 4)
    return (x[:Tp].reshape(Bp, Tp, D_MODEL), x[Tp:].reshape(Bs, Ts, D_MODEL),
            jnp.stack(kv_p), jnp.stack(kv_s), jnp.stack(win_p), win_s,
            jnp.stack(C_p), jnp.stack(C_s), jnp.stack(n_p), jnp.stack(n_s), jnp.stack(m_p), jnp.stack(m_s))
```

```python
import functools

import jax
import jax.numpy as jnp
import numpy as np
from jax import lax
from jax.experimental import pallas as pl
from jax.experimental.pallas import tpu as pltpu

D_MODEL = 2048
DEPTH = 2
PAST_LEN = 2048
A_HEADS = 16
A_KV_HEADS = 4
A_GROUP = A_HEADS // A_KV_HEADS
HEAD_DIM = 64
A_WIDTH = A_HEADS * HEAD_DIM
CMP_BLOCK = 32
CMP_STRIDE = 16
CMP_RATIO = CMP_BLOCK // CMP_STRIDE
CMP_HIDDEN = 128
SLC_BLOCK = 64
SLC_TOP = 16
WINDOW = 512
Q_BLOCK = 128
N_KV_BRANCH = 6
B_HEADS = 8
QK_DIM = 64
V_DIM = 128
B_WIDTH = B_HEADS * V_DIM
D_FF = 4 * D_MODEL
EPS = 1e-6
FORCE = 1e4
NEG = -1e30

SPLITS = (A_WIDTH, N_KV_BRANCH * A_KV_HEADS * HEAD_DIM, 3 * A_HEADS,
          B_HEADS * QK_DIM, B_HEADS * QK_DIM, B_WIDTH, B_HEADS, B_HEADS, B_WIDTH, 2 * D_MODEL)
BOUNDS = tuple(int(b) for b in np.cumsum((0,) + SPLITS))

VMEM_LIMIT_BYTES = 56 * 1024 * 1024


def _params(*sem):
    return pltpu.CompilerParams(dimension_semantics=sem, vmem_limit_bytes=VMEM_LIMIT_BYTES)


def _row_tile(m, want):
    for t in range(min(want, m), 15, -1):
        if m % t == 0 and t % 16 == 0:
            return t
    return m


def _rmsnorm_kernel(x_ref, g_ref, o_ref):
    x = x_ref[...]
    r = lax.rsqrt(jnp.mean(x * x, axis=-1, keepdims=True) + EPS)
    o_ref[...] = (x * r * g_ref[...]).astype(o_ref.dtype)


def rmsnorm_cast(x, g, *, tm=640):
    M, K = x.shape
    tm = _row_tile(M, tm)
    return pl.pallas_call(
        _rmsnorm_kernel,
        grid=(M // tm,),
        in_specs=[pl.BlockSpec((tm, K), lambda i: (i, 0)), pl.BlockSpec((1, K), lambda i: (0, 0))],
        out_specs=pl.BlockSpec((tm, K), lambda i: (i, 0)),
        out_shape=jax.ShapeDtypeStruct((M, K), jnp.bfloat16),
        compiler_params=_params("parallel"),
        name="rmsnorm_cast",
    )(x, g.reshape(1, K))


def _matmul_act_kernel(h_ref, w_ref, b_ref, o_ref, *, nt, act):
    w = w_ref[...].astype(jnp.bfloat16)
    z = (_nt(h_ref[...], w) if nt else jnp.dot(h_ref[...], w, preferred_element_type=jnp.float32)) + b_ref[...]
    if act == "relu2":
        z = jnp.square(jnp.maximum(z, 0.0))
    o_ref[...] = z.astype(o_ref.dtype)


def _layer_spec(w, layer, block, index):
    if w.ndim == 2:
        return pl.BlockSpec(block, index)
    return pl.BlockSpec((None,) + block, lambda *g: (layer,) + index(*g))


def matmul_act(h, w, b, *, nt, act=None, out_dtype=jnp.float32, tm=1664, tn=512, layer=None):
    M, K = h.shape
    N = w.shape[-2] if nt else w.shape[-1]
    tm = _row_tile(M, tm)
    tn = min(tn, N)
    assert N % tn == 0, (N, tn)
    w_spec = (_layer_spec(w, layer, (tn, K), lambda i, j: (j, 0)) if nt
              else _layer_spec(w, layer, (K, tn), lambda i, j: (0, j)))
    return pl.pallas_call(
        functools.partial(_matmul_act_kernel, nt=nt, act=act),
        grid=(M // tm, N // tn),
        in_specs=[pl.BlockSpec((tm, K), lambda i, j: (i, 0)), w_spec, pl.BlockSpec((1, tn), lambda i, j: (0, j))],
        out_specs=pl.BlockSpec((tm, tn), lambda i, j: (i, j)),
        out_shape=jax.ShapeDtypeStruct((M, N), out_dtype),
        compiler_params=_params("parallel", "arbitrary"),
        name="matmul_act",
    )(h, w, b.reshape(1, N))


def _mix_kernel(ya_ref, yb_ref, wa_ref, wb_ref, ga_ref, gb_ref, o_ref):
    bf = jnp.bfloat16
    a = jnp.dot(ya_ref[...].astype(bf), wa_ref[...].astype(bf), preferred_element_type=jnp.float32)
    b = jnp.dot(yb_ref[...].astype(bf), wb_ref[...].astype(bf), preferred_element_type=jnp.float32)
    o_ref[...] = (jax.nn.sigmoid(ga_ref[...]) * a + jax.nn.sigmoid(gb_ref[...]) * b).astype(o_ref.dtype)


def _mix_into_kernel(ya_ref, yb_ref, wa_ref, wb_ref, ga_ref, gb_ref, prev_ref, o_ref):
    del prev_ref
    _mix_kernel(ya_ref, yb_ref, wa_ref, wb_ref, ga_ref, gb_ref, o_ref)


def mix_matmul(ya, yb, wa, wb, gm_src, *, row0, rows_total, into=None, tm=1024, tn=512, layer=None):
    M, Ka = ya.shape
    Kb = yb.shape[1]
    N = wa.shape[-1]
    gm, off = gm_src
    tm = _row_tile(M, tm)
    nb = N // tn
    assert off % tn == 0 and N % tn == 0 and row0 % tm == 0, (off, N, tn, row0, tm)
    ja, jb, i0 = off // tn, off // tn + nb, row0 // tm
    in_specs = [
        pl.BlockSpec((tm, Ka), lambda i, j: (i, 0)),
        pl.BlockSpec((tm, Kb), lambda i, j: (i, 0)),
        _layer_spec(wa, layer, (Ka, tn), lambda i, j: (0, j)),
        _layer_spec(wb, layer, (Kb, tn), lambda i, j: (0, j)),
        pl.BlockSpec((tm, tn), lambda i, j: (i0 + i, ja + j)),
        pl.BlockSpec((tm, tn), lambda i, j: (i0 + i, jb + j)),
    ]
    args = (ya, yb, wa, wb, gm, gm)
    aliases = {}
    if into is not None:
        in_specs.append(pl.BlockSpec(memory_space=pl.ANY))
        args += (into,)
        aliases = {len(args) - 1: 0}
    return pl.pallas_call(
        _mix_kernel if into is None else _mix_into_kernel,
        grid=(M // tm, nb),
        in_specs=in_specs,
        out_specs=pl.BlockSpec((tm, tn), lambda i, j: (i0 + i, j)),
        out_shape=jax.ShapeDtypeStruct((rows_total, N), jnp.bfloat16),
        input_output_aliases=aliases,
        compiler_params=_params("parallel", "arbitrary"),
        name="mix_matmul",
    )(*args)


def _matmul_norm_res_kernel(a_ref, w_ref, x_ref, g_ref, o_ref, acc_ref):
    k = pl.program_id(1)

    @pl.when(k == 0)
    def _():
        acc_ref[...] = jnp.zeros_like(acc_ref)

    a = a_ref[...]
    for n0 in range(0, acc_ref.shape[1], NORM_RES_COLS):
        cols = slice(n0, n0 + NORM_RES_COLS)
        acc_ref[:, cols] += jnp.dot(a, w_ref[:, cols].astype(jnp.bfloat16), preferred_element_type=jnp.float32)

    @pl.when(k == pl.num_programs(1) - 1)
    def _():
        y = acc_ref[...]
        r = lax.rsqrt(jnp.mean(y * y, axis=-1, keepdims=True) + EPS)
        o_ref[...] = x_ref[...] + y * r * g_ref[...]


NORM_RES_COLS = 512


def matmul_norm_res(a, w, x, g, *, tm=640, tk=2048, layer=None):
    M, K = a.shape
    N = w.shape[-1]
    tm = _row_tile(M, tm)
    tk = min(tk, K)
    return pl.pallas_call(
        _matmul_norm_res_kernel,
        grid=(M // tm, K // tk),
        in_specs=[
            pl.BlockSpec((tm, tk), lambda i, k: (i, k)),
            _layer_spec(w, layer, (tk, N), lambda i, k: (k, 0)),
            pl.BlockSpec((tm, N), lambda i, k: (i, 0)),
            pl.BlockSpec((1, N), lambda i, k: (0, 0)),
        ],
        out_specs=pl.BlockSpec((tm, N), lambda i, k: (i, 0)),
        out_shape=jax.ShapeDtypeStruct((M, N), jnp.float32),
        scratch_shapes=[pltpu.VMEM((tm, N), jnp.float32)],
        compiler_params=_params("parallel", "arbitrary"),
        name="matmul_norm_res",
    )(a, w, x, g.reshape(1, N))


NSA_LANES = A_GROUP * Q_BLOCK
KEY_CHUNK = 512
WIN_CHUNK = 128
WIN_SPAN = WINDOW + Q_BLOCK
SLC_SLOTS = 128
POS_PIECES = 6
AUG = 128
MASK_BIG = 30000.0


def _pos_pieces(pos):
    pos = np.asarray(pos, np.int64)
    hi = (pos // 64) * 64
    lo = pos % 64
    return np.stack([hi, lo] * 3, axis=-1).astype(np.float32)


def _slope_rows():
    slopes = 2.0 ** (-8.0 * np.arange(1, A_HEADS + 1, dtype=np.float64) / A_HEADS)
    s = jnp.asarray(slopes.astype(np.float32))
    hi = s.astype(jnp.bfloat16)
    r1 = s - hi.astype(jnp.float32)
    mid = r1.astype(jnp.bfloat16)
    lo = (r1 - mid.astype(jnp.float32)).astype(jnp.bfloat16)
    rows = jnp.stack([hi, hi, mid, mid, lo, lo], axis=0)
    rows = rows.reshape(POS_PIECES, A_KV_HEADS, A_GROUP).transpose(1, 0, 2)
    rows = jnp.repeat(rows, Q_BLOCK, axis=-1)
    return jnp.pad(rows, ((0, 0), (0, 16 - POS_PIECES), (0, 0)))


def _split3(x):
    hi = x.astype(jnp.bfloat16)
    r1 = x - hi.astype(jnp.float32)
    mid = r1.astype(jnp.bfloat16)
    lo = (r1 - mid.astype(jnp.float32)).astype(jnp.bfloat16)
    return hi, mid, lo


def _tile4(x):
    return jnp.concatenate([x] * A_GROUP, axis=1)


def _softmax_step(carry, s, vT):
    m, l, acc = carry
    m_new = jnp.maximum(m, jnp.max(s, axis=0, keepdims=True))
    alpha = jnp.exp(m - m_new)
    p = jnp.exp(s - m_new)
    l = alpha * l + jnp.sum(p, axis=0, keepdims=True)
    acc = alpha * acc + jnp.dot(vT, p.astype(jnp.bfloat16), preferred_element_type=jnp.float32)
    return m_new, l, acc


def _nsa_prompt_kernel(q_ref, slope_ref, gT_ref, ck_ref, cvT_ref, ovT_ref, ks_ref, vsT_ref,
                       kw_ref, vwT_ref, o_ref, qa_ref, flag_ref):
    f32, bf = jnp.float32, jnp.bfloat16
    i = pl.program_id(1)
    s0 = i * Q_BLOCK
    ncp = ck_ref.shape[1]

    qT = q_ref[...].T * (HEAD_DIM ** -0.5)
    qcat = jnp.concatenate([qT[g * HEAD_DIM:(g + 1) * HEAD_DIM, :] for g in range(A_GROUP)], axis=1)
    qa_ref[0:HEAD_DIM, :] = qcat.astype(bf)
    qa_ref[HEAD_DIM:HEAD_DIM + 16, :] = slope_ref[0]
    qa_ref[HEAD_DIM + 16:AUG, :] = jnp.zeros((AUG - HEAD_DIM - 16, NSA_LANES), bf)
    qc = qa_ref[0:AUG, :]


    def compressed_and_select(nc, ns):
        sc = jnp.dot(ck_ref[0, 0:nc, :], qc, preferred_element_type=f32)
        n_idx = lax.broadcasted_iota(jnp.int32, (nc, Q_BLOCK), 0)
        t_idx = s0 + lax.broadcasted_iota(jnp.int32, (nc, Q_BLOCK), 1)
        valid = _tile4(n_idx * CMP_STRIDE + (CMP_BLOCK - 1) <= t_idx)
        sc = jnp.where(valid, sc, NEG)
        ec = jnp.where(valid, jnp.exp(sc - jnp.max(sc, axis=0, keepdims=True)), 0.0)
        lc = jnp.sum(ec, axis=0, keepdims=True)
        pc = ec * jnp.where(lc > 0.0, 1.0 / lc, 0.0)
        o_c = jnp.dot(cvT_ref[0, :, 0:nc], pc.astype(bf), preferred_element_type=f32)

        pg = pc[:, 0:Q_BLOCK]
        for g in range(1, A_GROUP):
            pg = pg + pc[:, g * Q_BLOCK:(g + 1) * Q_BLOCK]
        imp = sum(jnp.dot(ovT_ref[0:ns, 0:nc], piece, preferred_element_type=f32) for piece in _split3(pg))

        blk = lax.broadcasted_iota(jnp.int32, (ns, Q_BLOCK), 0)
        cur = (s0 + lax.broadcasted_iota(jnp.int32, (ns, Q_BLOCK), 1)) // SLC_BLOCK
        forced = (blk == 0) | (blk == cur) | (blk == cur - 1)
        causal = blk <= cur
        score = jnp.where(forced, FORCE, jnp.where(causal, imp, -FORCE))
        blk_f = blk.astype(f32)
        sel = jnp.zeros((ns, Q_BLOCK), f32)
        for _ in range(SLC_TOP):
            top = jnp.max(score, axis=0, keepdims=True)
            first = jnp.min(jnp.where(score == top, blk_f, float(SLC_SLOTS)), axis=0, keepdims=True)
            pick = blk_f == first
            sel = jnp.where(pick, 1.0, sel)
            score = jnp.where(pick, -jnp.inf, score)
        picked = jnp.where((sel > 0.0) & causal, 1.0, 0.0)
        if ns < SLC_SLOTS:
            picked = jnp.concatenate([picked, jnp.zeros((SLC_SLOTS - ns, Q_BLOCK), f32)], axis=0)
        return o_c, picked

    nqb = ks_ref.shape[1] // Q_BLOCK
    if ncp % 256 == 0 and nqb % 8 == 0 and nqb < SLC_SLOTS:
        o_c, picked = lax.cond(i < nqb // 2, lambda: compressed_and_select(ncp // 2, nqb),
                               lambda: compressed_and_select(ncp, SLC_SLOTS))
    else:
        o_c, picked = compressed_and_select(ncp, SLC_SLOTS)
    qa_ref[AUG:AUG + SLC_SLOTS, :] = _tile4(jnp.where(picked > 0.0, 0.0, -MASK_BIG)).astype(bf)

    def sel_scores(c):
        kc = ks_ref[0, pl.ds(pl.multiple_of(c * KEY_CHUNK, KEY_CHUNK), KEY_CHUNK), :]
        return jnp.dot(kc, qa_ref[...], preferred_element_type=f32)

    init = (jnp.full((1, NSA_LANES), NEG, f32), jnp.zeros((1, NSA_LANES), f32),
            jnp.zeros((HEAD_DIM, NSA_LANES), f32))
    c_last = (i * Q_BLOCK) // KEY_CHUNK
    bpc = KEY_CHUNK // SLC_BLOCK
    for c in range(SLC_SLOTS // bpc):
        flag_ref[c] = (jnp.max(picked[c * bpc:(c + 1) * bpc, :]) > 0.0).astype(jnp.int32)

    def sel_chunk(c, cr):
        return lax.cond(flag_ref[c] > 0, lambda x: _softmax_step(x, sel_scores(c), vsT_ref[0, c]), lambda x: x, cr)

    carry = lax.fori_loop(0, c_last, sel_chunk, init)
    j_last = c_last * KEY_CHUNK + lax.broadcasted_iota(jnp.int32, (KEY_CHUNK, Q_BLOCK), 0)
    t_last = s0 + lax.broadcasted_iota(jnp.int32, (KEY_CHUNK, Q_BLOCK), 1)
    m_s, l_s, acc_s = _softmax_step(carry, jnp.where(_tile4(j_last <= t_last), sel_scores(c_last), NEG),
                                    vsT_ref[0, c_last])
    o_s = acc_s * (1.0 / l_s)

    w0 = jnp.maximum(i - WINDOW // WIN_CHUNK, 0)
    kw = kw_ref[0, pl.ds(pl.multiple_of(w0 * WIN_CHUNK, WIN_CHUNK), WIN_SPAN), :]
    sw = jnp.dot(kw, qc, preferred_element_type=f32)
    jw = w0 * WIN_CHUNK + lax.broadcasted_iota(jnp.int32, (WIN_SPAN, Q_BLOCK), 0)
    tw = s0 + lax.broadcasted_iota(jnp.int32, (WIN_SPAN, Q_BLOCK), 1)
    sw = jnp.where(_tile4((jw <= tw) & (jw > tw - WINDOW)), sw, NEG)
    pw = jnp.exp(sw - jnp.max(sw, axis=0, keepdims=True))
    l_w = jnp.sum(pw, axis=0, keepdims=True)
    pw = pw.astype(bf)
    acc_w = jnp.zeros((HEAD_DIM, NSA_LANES), f32)
    for d in range(WIN_SPAN // WIN_CHUNK):
        acc_w = acc_w + jnp.dot(vwT_ref[0, w0 + d], pw[d * WIN_CHUNK:(d + 1) * WIN_CHUNK, :], preferred_element_type=f32)
    o_w = acc_w * (1.0 / l_w)

    gate = jax.nn.sigmoid(gT_ref[0])
    outs = []
    for g in range(A_GROUP):
        cols = slice(g * Q_BLOCK, (g + 1) * Q_BLOCK)
        outs.append(gate[3 * g:3 * g + 1, :] * o_c[:, cols] + gate[3 * g + 1:3 * g + 2, :] * o_s[:, cols]
                    + gate[3 * g + 2:3 * g + 3, :] * o_w[:, cols])
    o_ref[...] = jnp.concatenate(outs, axis=0).T


def _cmp_prompt_kernel(x_ref, pe_ref, w1_ref, b1_ref, w2T_ref, ck_ref, cvT_ref, bias_ref):
    f32, bf = jnp.float32, jnp.bfloat16
    nseg = ck_ref.shape[1]
    row_w = 2 * A_KV_HEADS * HEAD_DIM
    for c in range(2):
        bias = jnp.dot(pe_ref[c], w1_ref[c], preferred_element_type=f32,
                       precision=lax.Precision.HIGHEST) + b1_ref[c]
        bias_ref[c] = bias
        for k in range(A_KV_HEADS):
            col = (c * A_KV_HEADS + k) * HEAD_DIM
            y0 = jnp.zeros((nseg, CMP_HIDDEN), f32)
            y1 = jnp.zeros((nseg, CMP_HIDDEN), f32)
            for j in range(CMP_STRIDE):
                xj = x_ref[:, j * row_w + col:j * row_w + col + HEAD_DIM].astype(bf)
                w0 = w1_ref[c, j * HEAD_DIM:(j + 1) * HEAD_DIM, :].astype(bf)
                w1 = w1_ref[c, (CMP_STRIDE + j) * HEAD_DIM:(CMP_STRIDE + j + 1) * HEAD_DIM, :].astype(bf)
                y0 = y0 + jnp.dot(xj, w0, preferred_element_type=f32)
                y1 = y1 + jnp.dot(xj, w1, preferred_element_type=f32)
            pre = y0 + pltpu.roll(y1, nseg - 1, 0)
            hid = jnp.maximum(pre + bias, 0.0).astype(bf)
            if c == 0:
                ck_ref[k] = lax.dot_general(hid, w2T_ref[c].astype(bf), (((1,), (1,)), ((), ())),
                                            preferred_element_type=f32)
            else:
                cvT_ref[k] = lax.dot_general(w2T_ref[c].astype(bf), hid, (((1,), (1,)), ((), ())),
                                             preferred_element_type=f32)


def compress_prompt(x_cmp, pe, w1, b1, w2):
    T = x_cmp.shape[0]
    nseg = T // CMP_STRIDE
    return pl.pallas_call(
        _cmp_prompt_kernel,
        out_shape=(jax.ShapeDtypeStruct((A_KV_HEADS, nseg, HEAD_DIM), jnp.float32),
                   jax.ShapeDtypeStruct((A_KV_HEADS, HEAD_DIM, nseg), jnp.float32),
                   jax.ShapeDtypeStruct((2, 1, CMP_HIDDEN), jnp.float32)),
        compiler_params=pltpu.CompilerParams(vmem_limit_bytes=VMEM_LIMIT_BYTES),
        name="compress_prompt",
    )(x_cmp.reshape(nseg, CMP_STRIDE * x_cmp.shape[1]), pe.reshape(2, 1, CMP_BLOCK * HEAD_DIM), w1, b1.reshape(2, 1, CMP_HIDDEN), w2.transpose(0, 2, 1))


def nsa_prompt_pallas(q_src, ga, kva, pe, w1, b1, w2):
    bf = jnp.bfloat16
    q_arr, q_off = q_src
    q_cb = q_off // (A_GROUP * HEAD_DIM)
    T = kva.shape[0]
    nqb = T // Q_BLOCK
    assert T % KEY_CHUNK == 0 and T >= WIN_SPAN, T
    ncp = T // CMP_STRIDE
    kvw = A_KV_HEADS * HEAD_DIM
    ck, cvT, pe_bias = compress_prompt(kva[:, 0:2 * kvw], pe, w1, b1, w2)

    def heads(x):
        return x.reshape(T, A_KV_HEADS, HEAD_DIM).transpose(1, 0, 2)

    def chunksT(x, chunk):
        return x.reshape(T // chunk, chunk, A_KV_HEADS, HEAD_DIM).transpose(2, 0, 3, 1).astype(bf)

    pos = np.arange(T)
    key_pos = jnp.broadcast_to(jnp.asarray(_pos_pieces(pos)), (A_KV_HEADS, T, POS_PIECES))
    zpad = jnp.zeros((A_KV_HEADS, T, AUG - HEAD_DIM - POS_PIECES), jnp.float32)
    onehot = jnp.broadcast_to(jnp.asarray((pos[:, None] // SLC_BLOCK == np.arange(SLC_SLOTS)[None, :]).astype(np.float32)),
                              (A_KV_HEADS, T, SLC_SLOTS))
    ks_aug = jnp.concatenate([heads(kva[:, 2 * kvw:3 * kvw]), key_pos, zpad, onehot], axis=-1).astype(bf)
    kw_aug = jnp.concatenate([heads(kva[:, 4 * kvw:5 * kvw]), key_pos, zpad], axis=-1).astype(bf)
    c_end = np.arange(ncp) * CMP_STRIDE + (CMP_BLOCK - 1)
    ck_aug = jnp.concatenate([ck, jnp.broadcast_to(jnp.asarray(_pos_pieces(c_end)), (A_KV_HEADS, ncp, POS_PIECES)),
                              jnp.zeros((A_KV_HEADS, ncp, AUG - HEAD_DIM - POS_PIECES), jnp.float32)], axis=-1).astype(bf)
    vsT = chunksT(kva[:, 3 * kvw:4 * kvw], KEY_CHUNK)
    vwT = chunksT(kva[:, 5 * kvw:6 * kvw], WIN_CHUNK)
    c_start = c_end - (CMP_BLOCK - 1)
    s_start = np.arange(SLC_SLOTS) * SLC_BLOCK
    ovT = ((c_start[None, :] < s_start[:, None] + SLC_BLOCK) & (c_end[None, :] >= s_start[:, None])
           & (np.arange(ncp)[None, :] < ncp - 1))
    ovT = jnp.asarray(ovT.astype(np.float32)).astype(bf)
    gT = jnp.pad(ga.T.reshape(A_KV_HEADS, 3 * A_GROUP, T), ((0, 0), (0, 16 - 3 * A_GROUP), (0, 0)))

    y = pl.pallas_call(
        _nsa_prompt_kernel,
        grid=(A_KV_HEADS, nqb),
        in_specs=[
            pl.BlockSpec((Q_BLOCK, A_GROUP * HEAD_DIM), lambda k, i: (i, q_cb + k)),
            pl.BlockSpec((1, 16, NSA_LANES), lambda k, i: (k, 0, 0)),
            pl.BlockSpec((1, 16, Q_BLOCK), lambda k, i: (k, 0, i)),
            pl.BlockSpec((1, ncp, AUG), lambda k, i: (k, 0, 0)),
            pl.BlockSpec((1, HEAD_DIM, ncp), lambda k, i: (k, 0, 0)),
            pl.BlockSpec((SLC_SLOTS, ncp), lambda k, i: (0, 0)),
            pl.BlockSpec((1, T, AUG + SLC_SLOTS), lambda k, i: (k, 0, 0)),
            pl.BlockSpec((1, T // KEY_CHUNK, HEAD_DIM, KEY_CHUNK), lambda k, i: (k, 0, 0, 0)),
            pl.BlockSpec((1, T, AUG), lambda k, i: (k, 0, 0)),
            pl.BlockSpec((1, T // WIN_CHUNK, HEAD_DIM, WIN_CHUNK), lambda k, i: (k, 0, 0, 0)),
        ],
        out_specs=pl.BlockSpec((Q_BLOCK, A_GROUP * HEAD_DIM), lambda k, i: (i, k)),
        out_shape=jax.ShapeDtypeStruct((T, A_WIDTH), jnp.float32),
        scratch_shapes=[pltpu.VMEM((AUG + SLC_SLOTS, NSA_LANES), bf),
                        pltpu.SMEM((SLC_SLOTS * SLC_BLOCK // KEY_CHUNK,), jnp.int32)],
        compiler_params=_params("parallel", "arbitrary"),
        name="nsa_prompt",
    )(q_arr, _slope_rows(), gT, ck_aug, cvT.astype(bf), ovT, ks_aug, vsT, kw_aug, vwT)
    return y, pe_bias


PAGE_SIZE = 128
N_PAGES = PAST_LEN // PAGE_SIZE
PAGE_SEGS = PAGE_SIZE // CMP_STRIDE
PAST_SEGS = PAST_LEN // CMP_STRIDE
ROW_W = 4 * A_KV_HEADS * HEAD_DIM
KVW = A_KV_HEADS * HEAD_DIM
NS_SAMPLE = PAST_LEN // SLC_BLOCK + 1


def _nt(a, b):
    return lax.dot_general(a, b, (((1,), (1,)), ((), ())), preferred_element_type=jnp.float32)


def _dot3(x, w):
    return sum(jnp.dot(p, w, preferred_element_type=jnp.float32) for p in _split3(x))


SEG_PITCH = 24


def _nsa_decode_kernel(pt_ref, q_ref, g_ref, new_ref, *refs, has_into):
    del pt_ref
    pages = refs[:N_PAGES]
    rest = refs[N_PAGES:]
    (win_ref, w1_ref, bias_ref, w2bd_ref, bmask_ref, basec_ref, bases_ref, basew_ref, slope_ref,
     grp_ref, ov_ref, exp_ref) = rest[:12]
    o_ref, wout_ref, tr_ref, xc_ref = rest[12 + int(has_into):]
    f32, bf = jnp.float32, jnp.bfloat16

    q = q_ref[0] * (HEAD_DIM ** -0.5)
    qbd = jnp.where(bmask_ref[...] > 0.0, jnp.concatenate([q] * A_KV_HEADS, axis=1), 0.0).astype(bf)
    qbd_f = qbd.astype(f32)
    new = new_ref[0]

    ckv = []
    for c in range(2):
        for p in range(N_PAGES):
            for kp in range(2):
                r0 = c * KVW + kp * 2 * HEAD_DIM
                t = pages[p][0, r0:r0 + 2 * HEAD_DIM, :].T
                for s in range(PAGE_SEGS):
                    t0 = (kp * PAST_SEGS + p * PAGE_SEGS + s) * SEG_PITCH
                    tr_ref[t0:t0 + CMP_STRIDE, :] = t[s * CMP_STRIDE:(s + 1) * CMP_STRIDE, :]
        for j in range(CMP_STRIDE):
            for kp in range(2):
                piece = tr_ref[pl.ds(kp * PAST_SEGS * SEG_PITCH + j, PAST_SEGS, stride=SEG_PITCH), :]
                for kk in range(2):
                    k = 2 * kp + kk
                    xc_ref[k * PAST_SEGS:(k + 1) * PAST_SEGS, j * HEAD_DIM:(j + 1) * HEAD_DIM] = (
                        piece[:, kk * HEAD_DIM:(kk + 1) * HEAD_DIM])
        y = jnp.dot(xc_ref[...].astype(bf), w1_ref[c], preferred_element_type=f32)
        pre = y[:, 0:CMP_HIDDEN] + pltpu.roll(y[:, CMP_HIDDEN:2 * CMP_HIDDEN], A_KV_HEADS * PAST_SEGS - 1, 0)
        hid = jnp.maximum(pre + bias_ref[c], 0.0).astype(bf)
        hid = jnp.concatenate([hid[k * PAST_SEGS:(k + 1) * PAST_SEGS] for k in range(A_KV_HEADS)], axis=1)
        ckv.append(jnp.dot(hid, w2bd_ref[c], preferred_element_type=f32).astype(bf))
    ck, cv = ckv

    sc = _nt(qbd, ck) + basec_ref[...]
    ec = jnp.exp(sc - jnp.max(sc, axis=1, keepdims=True))
    pc = ec / jnp.sum(ec, axis=1, keepdims=True)
    o_c = jnp.dot(pc.astype(bf), cv, preferred_element_type=f32)

    pov = _dot3(pc, ov_ref[...])
    imp = sum(jnp.dot(grp_ref[...], p, preferred_element_type=f32) for p in _split3(pov))

    lane = lax.broadcasted_iota(jnp.int32, (A_HEADS, PAST_SEGS), 1)
    score = jnp.where((lane == 0) | (lane == NS_SAMPLE - 1) | (lane == NS_SAMPLE - 2), FORCE,
                      jnp.where(lane < NS_SAMPLE, imp, -jnp.inf))
    rr = lax.broadcasted_iota(jnp.int32, (PAST_SEGS, PAST_SEGS), 0)
    cc = lax.broadcasted_iota(jnp.int32, (PAST_SEGS, PAST_SEGS), 1)
    row_id = lax.broadcasted_iota(jnp.int32, (A_HEADS, PAST_SEGS), 0)
    selrows = jnp.zeros((A_HEADS, PAST_SEGS), f32)
    for k in range(A_KV_HEADS):
        row = jnp.broadcast_to(score[A_GROUP * k:A_GROUP * k + 1, :], (PAST_SEGS, PAST_SEGS))
        col = jnp.max(jnp.where(rr == cc, row, -jnp.inf), axis=1, keepdims=True)
        ahead = (col > row) | ((col == row) & (rr < cc))
        rank = jnp.sum(jnp.where(ahead, 1.0, 0.0), axis=0, keepdims=True)
        sel_k = jnp.where(rank < float(SLC_TOP), 1.0, 0.0)
        selrows = jnp.where(row_id // A_GROUP == k, jnp.broadcast_to(sel_k, (A_HEADS, PAST_SEGS)), selrows)
    selpos = jnp.dot(selrows.astype(bf), exp_ref[...], preferred_element_type=f32)

    slope = slope_ref[...]
    scores = []
    for p in range(N_PAGES):
        kp_ = pages[p][0, 2 * KVW:3 * KVW, :].astype(bf)
        mask = jnp.where(selpos[:, p * PAGE_SIZE:(p + 1) * PAGE_SIZE] > 0.5, 0.0, NEG)
        scores.append(jnp.dot(qbd, kp_, preferred_element_type=f32) + (bases_ref[...] + slope * float(p * PAGE_SIZE)) + mask)
    s_new = jnp.sum(qbd_f * new[:, 2 * KVW:3 * KVW], axis=1, keepdims=True)
    mx = scores[0]
    for s in scores[1:]:
        mx = jnp.maximum(mx, s)
    m = jnp.maximum(jnp.max(mx, axis=1, keepdims=True), s_new)
    p_new = jnp.exp(s_new - m)
    acc = p_new * new[:, 3 * KVW:4 * KVW]
    psum = jnp.zeros((A_HEADS, PAGE_SIZE), f32)
    for p in range(N_PAGES):
        pr = jnp.exp(scores[p] - m)
        psum = psum + pr
        acc = acc + _nt(pr.astype(bf), pages[p][0, 3 * KVW:4 * KVW, :].astype(bf))
    o_s = acc / (jnp.sum(psum, axis=1, keepdims=True) + p_new)

    sw = jnp.dot(qbd, win_ref[0, 0:KVW, :].astype(bf), preferred_element_type=f32) + basew_ref[...]
    s_neww = jnp.sum(qbd_f * new[:, 4 * KVW:5 * KVW], axis=1, keepdims=True)
    m = jnp.maximum(jnp.max(sw, axis=1, keepdims=True), s_neww)
    pw = jnp.exp(sw - m)
    p_new = jnp.exp(s_neww - m)
    o_w = (_nt(pw.astype(bf), win_ref[0, KVW:2 * KVW, :].astype(bf)) + p_new * new[:, 5 * KVW:6 * KVW]) / (
        jnp.sum(pw, axis=1, keepdims=True) + p_new)

    gate = jax.nn.sigmoid(g_ref[0])
    o = gate[:, 0:1] * o_c + gate[:, 1:2] * o_s + gate[:, 2:3] * o_w
    for k in range(A_KV_HEADS):
        o_ref[0, A_GROUP * k:A_GROUP * (k + 1), :] = o[A_GROUP * k:A_GROUP * (k + 1), k * HEAD_DIM:(k + 1) * HEAD_DIM]

    wr = lax.broadcasted_iota(jnp.int32, (WINDOW, 2 * KVW), 0)
    wc = lax.broadcasted_iota(jnp.int32, (WINDOW, 2 * KVW), 1)
    new_col = jnp.sum(jnp.where(wr == wc, jnp.broadcast_to(new[:, 4 * KVW:6 * KVW], (WINDOW, 2 * KVW)), 0.0),
                      axis=1, keepdims=True)
    wout_ref[0] = jnp.where(wc == WINDOW - 1, new_col, pltpu.roll(win_ref[0], WINDOW - 1, 1))


def _decode_constants():
    slopes = (2.0 ** (-8.0 * np.arange(1, A_HEADS + 1, dtype=np.float64) / A_HEADS)).astype(np.float32)[:, None]
    heads = np.arange(A_HEADS)[:, None]
    bmask = (np.arange(KVW)[None, :] // HEAD_DIM == heads // A_GROUP).astype(np.float32)
    n = np.arange(PAST_SEGS)[None, :]
    c_end = n * CMP_STRIDE + (CMP_BLOCK - 1)
    basec = np.where(n < PAST_SEGS - 1, slopes * (c_end - PAST_LEN), NEG).astype(np.float32)
    bases = (slopes * (np.arange(PAGE_SIZE)[None, :] - PAST_LEN)).astype(np.float32)
    r = np.arange(WINDOW)[None, :]
    basew = np.where(r >= 1, slopes * (r - WINDOW), NEG).astype(np.float32)
    slope = np.broadcast_to(slopes, (A_HEADS, PAGE_SIZE)).astype(np.float32)
    grp = (heads // A_GROUP == heads.T // A_GROUP).astype(np.float32)
    c_start = np.arange(PAST_SEGS) * CMP_STRIDE
    s_start = np.arange(PAST_SEGS) * SLC_BLOCK
    ov = ((c_start[:, None] < s_start[None, :] + SLC_BLOCK) & (c_start[:, None] + CMP_BLOCK - 1 >= s_start[None, :])
          & (np.arange(PAST_SEGS)[:, None] < PAST_SEGS - 1) & (np.arange(PAST_SEGS)[None, :] < NS_SAMPLE))
    expand = (np.arange(PAST_LEN)[None, :] // SLC_BLOCK == np.arange(PAST_SEGS)[:, None])
    as_bf = lambda a: jnp.asarray(a.astype(np.float32)).astype(jnp.bfloat16)
    return (jnp.asarray(bmask), jnp.asarray(basec), jnp.asarray(bases), jnp.asarray(basew), jnp.asarray(slope),
            as_bf(grp), as_bf(ov), as_bf(expand))


def nsa_decode_pallas(qa, ga, kva, pool, page_table, win_buf, layer, w1, w2, pe_bias, win_into=None):
    bf = jnp.bfloat16
    B = qa.shape[0]
    depth, npool = pool.shape[:2]
    page_table = page_table + layer * npool
    poolT = pool.transpose(0, 1, 3, 4, 5, 2).reshape(depth * npool, ROW_W, PAGE_SIZE)
    winT = win_buf.transpose(0, 1, 3, 4, 5, 2).reshape(depth * B, 2 * KVW, WINDOW)
    w1r = w1.reshape(2, CMP_RATIO, CMP_STRIDE * HEAD_DIM, CMP_HIDDEN).transpose(0, 2, 1, 3)
    w1r = w1r.reshape(2, CMP_STRIDE * HEAD_DIM, CMP_RATIO * CMP_HIDDEN).astype(bf)
    w2bd = jnp.einsum('kl,ced->ckeld', jnp.eye(A_KV_HEADS, dtype=w2.dtype), w2)
    w2bd = w2bd.reshape(2, A_KV_HEADS * CMP_HIDDEN, KVW).astype(bf)
    consts = _decode_constants()
    full = lambda shape: pl.BlockSpec(shape, lambda b, pt: (0,) * len(shape))
    page_specs = [pl.BlockSpec((1, ROW_W, PAGE_SIZE), functools.partial(lambda p, b, pt: (pt[b, p], 0, 0), p))
                  for p in range(N_PAGES)]
    grid_spec = pltpu.PrefetchScalarGridSpec(
        num_scalar_prefetch=1,
        grid=(B,),
        in_specs=[
            pl.BlockSpec((1, A_HEADS, HEAD_DIM), lambda b, pt: (b, 0, 0)),
            pl.BlockSpec((1, A_HEADS, 3), lambda b, pt: (b, 0, 0)),
            pl.BlockSpec((1, 1, N_KV_BRANCH * KVW), lambda b, pt: (b, 0, 0)),
            *page_specs,
            pl.BlockSpec((1, 2 * KVW, WINDOW), lambda b, pt: (layer * B + b, 0, 0)),
            full(w1r.shape), full((2, 1, CMP_HIDDEN)), full(w2bd.shape),
            *[full(c.shape) for c in consts],
            *([] if win_into is None else [pl.BlockSpec(memory_space=pl.ANY)]),
        ],
        out_specs=[
            pl.BlockSpec((1, A_HEADS, HEAD_DIM), lambda b, pt: (b, 0, 0)),
            pl.BlockSpec((1, 2 * KVW, WINDOW), lambda b, pt: (layer * B + b, 0, 0)),
        ],
        scratch_shapes=[pltpu.VMEM((2 * PAST_SEGS * SEG_PITCH, 2 * HEAD_DIM), jnp.float32),
                        pltpu.VMEM((A_KV_HEADS * PAST_SEGS, CMP_STRIDE * HEAD_DIM), jnp.float32)],
    )
    args = (page_table, qa.reshape(B, A_HEADS, HEAD_DIM), ga.reshape(B, A_HEADS, 3), kva.reshape(B, 1, N_KV_BRANCH * KVW),
            *([poolT] * N_PAGES), winT, w1r, pe_bias.reshape(2, 1, CMP_HIDDEN), w2bd, *consts)
    aliases = {}
    if win_into is not None:
        args += (win_into,)
        aliases = {len(args) - 1: 1}
    o, wout = pl.pallas_call(
        functools.partial(_nsa_decode_kernel, has_into=win_into is not None),
        grid_spec=grid_spec,
        out_shape=(jax.ShapeDtypeStruct((B, A_HEADS, HEAD_DIM), jnp.float32),
                   jax.ShapeDtypeStruct((depth * B, 2 * KVW, WINDOW), jnp.float32)),
        input_output_aliases=aliases,
        compiler_params=_params("arbitrary"),
        name="nsa_decode",
    )(*args)
    return o.reshape(B, A_WIDTH), wout


MLSTM_L = 128
VAUG = 2 * V_DIM


def _mlstm_prompt_kernel(q_ref, k_ref, v_ref, o_ref, fsrc_ref, ifT_ref, y_ref, c_ref, m_ref, *, f_off):
    f32, bf = jnp.float32, jnp.bfloat16
    L = MLSTM_L
    f_ref = fsrc_ref.at[:, f_off:f_off + B_HEADS]
    iT_ref = ifT_ref.at[0:B_HEADS, :]
    fT_ref = ifT_ref.at[B_HEADS:2 * B_HEADS, :]

    @pl.when(pl.program_id(0) == 0)
    def _():
        c_ref[...] = jnp.zeros_like(c_ref)
        m_ref[...] = jnp.zeros_like(m_ref)

    kT_all = k_ref[...].T * (QK_DIM ** -0.5)
    rr = lax.broadcasted_iota(jnp.int32, (L, L), 0)
    cc = lax.broadcasted_iota(jnp.int32, (L, L), 1)
    lower = rr >= cc
    tril = jnp.where(lower, 1.0, 0.0).astype(bf)
    triu = jnp.where(rr <= cc, 1.0, 0.0).astype(bf)
    b_col = sum(jnp.dot(tril, p, preferred_element_type=f32) for p in _split3(jax.nn.log_sigmoid(f_ref[...])))
    b_row = _dot3(jax.nn.log_sigmoid(fT_ref[...]), triu)
    a_row = iT_ref[...] - b_row
    ones_col = jnp.where(lax.broadcasted_iota(jnp.int32, (L, V_DIM), 1) == 0, 1.0, 0.0)

    for h in range(B_HEADS):
        m_prev = m_ref[h, 0:1, 0:1]
        a = a_row[h:h + 1, :]
        amat = jnp.where(lower, jnp.broadcast_to(a, (L, L)), -jnp.inf)
        big_m = jnp.maximum(m_prev, jnp.max(amat, axis=1, keepdims=True))
        dmat = jnp.exp(amat - big_m)
        inter = jnp.exp(m_prev - big_m)
        q = q_ref[:, h * QK_DIM:(h + 1) * QK_DIM].astype(bf)
        kT = kT_all[h * QK_DIM:(h + 1) * QK_DIM, :]
        vaug = jnp.concatenate([v_ref[:, h * V_DIM:(h + 1) * V_DIM], ones_col], axis=1).astype(bf)
        s = jnp.dot(q, kT.astype(bf), preferred_element_type=f32) * dmat
        r = (jnp.dot(s.astype(bf), vaug, preferred_element_type=f32)
             + inter * jnp.dot(q, c_ref[h].astype(bf), preferred_element_type=f32))
        m_new = b_col[:, h:h + 1] + big_m
        den = jnp.maximum(jnp.abs(r[:, V_DIM:V_DIM + 1]), jnp.exp(-m_new))
        y_ref[:, h * V_DIM:(h + 1) * V_DIM] = (jax.nn.sigmoid(o_ref[:, h * V_DIM:(h + 1) * V_DIM])
                                               * (r[:, 0:V_DIM] / den))
        m_end = big_m[L - 1:L, :]
        w_end = jnp.exp(a - m_end)
        kw_end = kT * w_end
        c_new = inter[L - 1:L, :] * c_ref[h] + jnp.dot(kw_end.astype(bf), vaug, preferred_element_type=f32)
        n_new = inter[L - 1:L, :] * c_ref[h, :, V_DIM:V_DIM + 1] + jnp.sum(kw_end, axis=1, keepdims=True)
        c_ref[h] = jnp.where(lax.broadcasted_iota(jnp.int32, (QK_DIM, VAUG), 1) == V_DIM, n_new, c_new)
        m_ref[h] = jnp.broadcast_to(m_new[L - 1:L, :], m_ref.shape[1:])


def mlstm_prompt_pallas(q_src, k_src, v_src, o_src, f_src, ifT, T):
    L = MLSTM_L
    (q_arr, q_cb), (k_arr, k_cb), (v_arr, v_cb), (o_arr, o_cb), (f_arr, f_cb, f_off) = q_src, k_src, v_src, o_src, f_src
    y, caug, m = pl.pallas_call(
        functools.partial(_mlstm_prompt_kernel, f_off=f_off),
        grid=(T // L,),
        in_specs=[
            pl.BlockSpec((L, B_HEADS * QK_DIM), lambda c: (c, q_cb)),
            pl.BlockSpec((L, B_HEADS * QK_DIM), lambda c: (c, k_cb)),
            pl.BlockSpec((L, B_WIDTH), lambda c: (c, v_cb)),
            pl.BlockSpec((L, B_WIDTH), lambda c: (c, o_cb)),
            pl.BlockSpec((L, f_arr.shape[1] if f_arr.shape[1] < 128 else 128), lambda c: (c, f_cb)),
            pl.BlockSpec((2 * B_HEADS, L), lambda c: (0, c)),
        ],
        out_specs=[
            pl.BlockSpec((L, B_WIDTH), lambda c: (c, 0)),
            pl.BlockSpec((B_HEADS, QK_DIM, VAUG), lambda c: (0, 0, 0)),
            pl.BlockSpec((B_HEADS, 8, 128), lambda c: (0, 0, 0)),
        ],
        out_shape=(jax.ShapeDtypeStruct((T, B_WIDTH), jnp.float32),
                   jax.ShapeDtypeStruct((B_HEADS, QK_DIM, VAUG), jnp.float32),
                   jax.ShapeDtypeStruct((B_HEADS, 8, 128), jnp.float32)),
        compiler_params=_params("arbitrary"),
        name="mlstm_prompt",
    )(q_arr, k_arr, v_arr, o_arr, f_arr, ifT)
    return y, caug[:, :, 0:V_DIM], caug[:, :, V_DIM], m[:, 0, 0]


def _mlstm_sample_kernel(q_ref, k_ref, qT_ref, kT_ref, v_ref, o_ref, i_ref, f_ref, m_ref, n_ref, c_ref,
                         y_ref, cn_ref, nn_ref, mn_ref):
    bb = q_ref.shape[0]
    scale = QK_DIM ** -0.5
    for b in range(bb):
        logf = jax.nn.log_sigmoid(f_ref[b])
        m_new = jnp.maximum(logf + m_ref[b], i_ref[b])
        d_all = jnp.exp(i_ref[b] - m_new)
        inter_all = jnp.exp(logf + m_ref[b] - m_new)
        floor_all = jnp.exp(-m_new)
        mn_ref[b] = m_new
        qk_all = jnp.sum(q_ref[b] * k_ref[b], axis=1, keepdims=True) * scale
        qn_all = jnp.sum(q_ref[b] * n_ref[b], axis=1, keepdims=True)
        for h in range(B_HEADS):
            d = d_all[:, h:h + 1]
            inter = inter_all[:, h:h + 1]
            s = qk_all[h:h + 1, :] * d
            c = c_ref[b, h]
            v = v_ref[b, h:h + 1, :]
            qc = qT_ref[b, :, h:h + 1]
            kc = kT_ref[b, :, h:h + 1] * scale
            num = inter * jnp.sum(qc * c, axis=0, keepdims=True) + s * v
            den = inter * qn_all[h:h + 1, :] + s
            hout = num / jnp.maximum(jnp.abs(den), floor_all[:, h:h + 1])
            y_ref[b, h:h + 1, :] = jax.nn.sigmoid(o_ref[b, h:h + 1, :]) * hout
            cn_ref[b, h] = inter * c + d * (kc * v)
            nn_ref[b, h:h + 1, :] = inter * n_ref[b, h:h + 1, :] + d * (k_ref[b, h:h + 1, :] * scale)


def mlstm_sample_pallas(qb, kb, vb, ib, fb, ob, state_C, state_n, state_m, layer, *, bb=8):
    B = qb.shape[0]
    nb = B // bb
    q3 = qb.reshape(B, B_HEADS, QK_DIM)
    k3 = kb.reshape(B, B_HEADS, QK_DIM)
    row8 = lambda x: x.reshape(-1, 1, B_HEADS)
    lay = layer * nb
    y, cn, nn, mn = pl.pallas_call(
        _mlstm_sample_kernel,
        grid=(nb,),
        in_specs=[
            pl.BlockSpec((bb, B_HEADS, QK_DIM), lambda i: (i, 0, 0)),
            pl.BlockSpec((bb, B_HEADS, QK_DIM), lambda i: (i, 0, 0)),
            pl.BlockSpec((bb, QK_DIM, B_HEADS), lambda i: (i, 0, 0)),
            pl.BlockSpec((bb, QK_DIM, B_HEADS), lambda i: (i, 0, 0)),
            pl.BlockSpec((bb, B_HEADS, V_DIM), lambda i: (i, 0, 0)),
            pl.BlockSpec((bb, B_HEADS, V_DIM), lambda i: (i, 0, 0)),
            pl.BlockSpec((bb, 1, B_HEADS), lambda i: (i, 0, 0)),
            pl.BlockSpec((bb, 1, B_HEADS), lambda i: (i, 0, 0)),
            pl.BlockSpec((bb, 1, B_HEADS), lambda i: (lay + i, 0, 0)),
            pl.BlockSpec((bb, B_HEADS, QK_DIM), lambda i: (lay + i, 0, 0)),
            pl.BlockSpec((bb, B_HEADS, QK_DIM, V_DIM), lambda i: (lay + i, 0, 0, 0)),
        ],
        out_specs=[
            pl.BlockSpec((bb, B_HEADS, V_DIM), lambda i: (i, 0, 0)),
            pl.BlockSpec((bb, B_HEADS, QK_DIM, V_DIM), lambda i: (i, 0, 0, 0)),
            pl.BlockSpec((bb, B_HEADS, QK_DIM), lambda i: (i, 0, 0)),
            pl.BlockSpec((bb, 1, B_HEADS), lambda i: (i, 0, 0)),
        ],
        out_shape=(jax.ShapeDtypeStruct((B, B_HEADS, V_DIM), jnp.float32),
                   jax.ShapeDtypeStruct((B, B_HEADS, QK_DIM, V_DIM), jnp.float32),
                   jax.ShapeDtypeStruct((B, B_HEADS, QK_DIM), jnp.float32),
                   jax.ShapeDtypeStruct((B, 1, B_HEADS), jnp.float32)),
        compiler_params=_params("arbitrary"),
        name="mlstm_sample",
    )(q3, k3, q3.transpose(0, 2, 1), k3.transpose(0, 2, 1), vb.reshape(B, B_HEADS, V_DIM),
      ob.reshape(B, B_HEADS, V_DIM), row8(ib), row8(fb), row8(state_m),
      state_n.reshape(-1, B_HEADS, QK_DIM), state_C.reshape(-1, B_HEADS, QK_DIM, V_DIM))
    return y.reshape(B, B_WIDTH), cn, nn, mn.reshape(B, B_HEADS)


(G_QA, G_KVA, G_GA, G_QB, G_KB, G_VB, G_IB, G_FB, G_OB, G_GM) = range(10)
WIDE_ORDER = (G_QA, G_VB, G_OB, G_KVA, G_QB, G_KB, G_GM)
NARROW_ORDER = (G_GA, G_IB, G_FB)
WIDE_OFF = dict(zip(WIDE_ORDER, np.cumsum((0,) + tuple(SPLITS[g] for g in WIDE_ORDER))[:-1].tolist()))
NARROW_OFF = dict(zip(NARROW_ORDER, np.cumsum((0,) + tuple(SPLITS[g] for g in NARROW_ORDER))[:-1].tolist()))
NARROW_W = 128


def project_in(h, w_in, b_in, layer):
    w_t = jnp.swapaxes(w_in, 1, 2)
    b_in = b_in[layer]

    def gather(order, pad_to):
        rows = jnp.concatenate([w_t[layer, BOUNDS[g]:BOUNDS[g + 1]] for g in order], axis=0)
        bias = jnp.concatenate([b_in[BOUNDS[g]:BOUNDS[g + 1]] for g in order], axis=0)
        pad = pad_to - rows.shape[0]
        return jnp.pad(rows, ((0, pad), (0, 0))), jnp.pad(bias, (0, pad))

    n_wide = sum(SPLITS[g] for g in WIDE_ORDER)
    wide = matmul_act(h, *gather(WIDE_ORDER, n_wide), nt=True)
    narrow = matmul_act(h, *gather(NARROW_ORDER, NARROW_W), nt=True, tn=NARROW_W)
    return wide, narrow


def kernel(x_prompt, x_sample, cache_nsa_kv, cache_win_kv, state_C, state_n, state_m, page_table,
           norm_g, w_in, b_in, cmp_pe, cmp_w1, cmp_b1, cmp_w2, w_up_a, w_up_b, w_out, w_mlp1, w_mlp2):
    Bp, Tp = x_prompt.shape[:2]
    Bs, Ts = x_sample.shape[:2]
    assert Bp == 1 and Ts == 1 and Tp >= WINDOW, (x_prompt.shape, x_sample.shape)
    x = jnp.concatenate([x_prompt.reshape(Tp, D_MODEL), x_sample.reshape(Bs, D_MODEL)], axis=0)
    kv_p, kv_s, win_p, win_all = [], [], [], None
    C_p, C_s, n_p, n_s, m_p, m_s = [], [], [], [], [], []
    kv4 = 4 * KVW
    w_out_bf = w_out.astype(jnp.bfloat16)
    w_mlp2_bf = w_mlp2.astype(jnp.bfloat16)
    for l in range(DEPTH):
        wide, narrow = project_in(rmsnorm_cast(x, norm_g[l, 0]), w_in, b_in, l)
        col = lambda g, rows: wide[rows, WIDE_OFF[g]:WIDE_OFF[g] + SPLITS[g]]
        ncol = lambda g, rows: narrow[rows, NARROW_OFF[g]:NARROW_OFF[g] + SPLITS[g]]
        pr, sr = slice(0, Tp), slice(Tp, Tp + Bs)

        kva = lax.optimization_barrier(col(G_KVA, pr))
        ya_p, pe_bias = nsa_prompt_pallas((wide, WIDE_OFF[G_QA]), ncol(G_GA, pr), kva, cmp_pe[l], cmp_w1[l], cmp_b1[l],
                                          cmp_w2[l])
        yb_p, C, n, m = mlstm_prompt_pallas(
            (wide, WIDE_OFF[G_QB] // (B_HEADS * QK_DIM)), (wide, WIDE_OFF[G_KB] // (B_HEADS * QK_DIM)),
            (wide, WIDE_OFF[G_VB] // B_WIDTH),
            (wide, WIDE_OFF[G_OB] // B_WIDTH), (narrow, 0, NARROW_OFF[G_FB]),
            narrow[pr, NARROW_OFF[G_IB]:NARROW_OFF[G_IB] + 2 * B_HEADS].T, Tp)
        kv_p.append(kva[:, 0:kv4].reshape(Bp, Tp, 4, A_KV_HEADS, HEAD_DIM))
        win_p.append(kva[Tp - WINDOW:, kv4:].reshape(Bp, WINDOW, 2, A_KV_HEADS, HEAD_DIM))
        C_p.append(C[None])
        n_p.append(n[None])
        m_p.append(m[None])

        kva = col(G_KVA, sr)
        ya_s, win_all = nsa_decode_pallas(col(G_QA, sr), ncol(G_GA, sr), kva, cache_nsa_kv, page_table, cache_win_kv, l,
                                          cmp_w1[l], cmp_w2[l], pe_bias, win_into=win_all)
        yb_s, C, n, m = mlstm_sample_pallas(col(G_QB, sr), col(G_KB, sr), col(G_VB, sr), ncol(G_IB, sr), ncol(G_FB, sr),
                                            col(G_OB, sr), state_C, state_n, state_m, l)
        kv_s.append(kva[:, 0:kv4].reshape(Bs, Ts, 4, A_KV_HEADS, HEAD_DIM))
        C_s.append(C)
        n_s.append(n)
        m_s.append(m)

        gm_src = (wide, WIDE_OFF[G_GM])
        mix = mix_matmul(ya_p, yb_p, w_up_a, w_up_b, gm_src, row0=0, rows_total=Tp + Bs, layer=l)
        mix = mix_matmul(ya_s, yb_s, w_up_a, w_up_b, gm_src, row0=Tp, rows_total=Tp + Bs, into=mix, layer=l)
        x = matmul_norm_res(mix, w_out_bf, x, norm_g[l, 1], layer=l)
        hid = matmul_act(rmsnorm_cast(x, norm_g[l, 2]), w_mlp1, jnp.zeros((D_FF,), jnp.float32), nt=False,
                         act="relu2", out_dtype=jnp.bfloat16, layer=l)
        x = matmul_norm_res(hid, w_mlp2_bf, x, norm_g[l, 3], layer=l)

    win_s = win_all.reshape(DEPTH, Bs, 2, A_KV_HEADS, HEAD_DIM, WINDOW).transpose(0, 1, 5, 2, 3, 4)
    return (x[:Tp].reshape(Bp, Tp, D_MODEL), x[Tp:].reshape(Bs, Ts, D_MODEL),
            jnp.stack(kv_p), jnp.stack(kv_s), jnp.stack(win_p), win_s,
            jnp.stack(C_p), jnp.stack(C_s), jnp.stack(n_p), jnp.stack(n_s), jnp.stack(m_p), jnp.stack(m_s))
```

```python
import functools

import jax
import jax.numpy as jnp
import numpy as np
from jax import lax
from jax.experimental import pallas as pl
from jax.experimental.pallas import tpu as pltpu

D_MODEL = 2048
DEPTH = 2
PAST_LEN = 2048
A_HEADS = 16
A_KV_HEADS = 4
A_GROUP = A_HEADS // A_KV_HEADS
HEAD_DIM = 64
A_WIDTH = A_HEADS * HEAD_DIM
CMP_BLOCK = 32
CMP_STRIDE = 16
CMP_RATIO = CMP_BLOCK // CMP_STRIDE
CMP_HIDDEN = 128
SLC_BLOCK = 64
SLC_TOP = 16
WINDOW = 512
Q_BLOCK = 128
N_KV_BRANCH = 6
B_HEADS = 8
QK_DIM = 64
V_DIM = 128
B_WIDTH = B_HEADS * V_DIM
D_FF = 4 * D_MODEL
EPS = 1e-6
FORCE = 1e4
NEG = -1e30

SPLITS = (A_WIDTH, N_KV_BRANCH * A_KV_HEADS * HEAD_DIM, 3 * A_HEADS,
          B_HEADS * QK_DIM, B_HEADS * QK_DIM, B_WIDTH, B_HEADS, B_HEADS, B_WIDTH, 2 * D_MODEL)
BOUNDS = tuple(int(b) for b in np.cumsum((0,) + SPLITS))

VMEM_LIMIT_BYTES = 56 * 1024 * 1024


def _params(*sem):
    return pltpu.CompilerParams(dimension_semantics=sem, vmem_limit_bytes=VMEM_LIMIT_BYTES)


def _row_tile(m, want):
    for t in range(min(want, m), 15, -1):
        if m % t == 0 and t % 16 == 0:
            return t
    return m


def _rmsnorm_kernel(x_ref, g_ref, o_ref):
    x = x_ref[...]
    r = lax.rsqrt(jnp.mean(x * x, axis=-1, keepdims=True) + EPS)
    o_ref[...] = (x * r * g_ref[...]).astype(o_ref.dtype)


def rmsnorm_cast(x, g, *, tm=640):
    M, K = x.shape
    tm = _row_tile(M, tm)
    return pl.pallas_call(
        _rmsnorm_kernel,
        grid=(M // tm,),
        in_specs=[pl.BlockSpec((tm, K), lambda i: (i, 0)), pl.BlockSpec((1, K), lambda i: (0, 0))],
        out_specs=pl.BlockSpec((tm, K), lambda i: (i, 0)),
        out_shape=jax.ShapeDtypeStruct((M, K), jnp.bfloat16),
        compiler_params=_params("parallel"),
        name="rmsnorm_cast",
    )(x, g.reshape(1, K))


def _matmul_act_kernel(h_ref, w_ref, b_ref, o_ref, *, nt, act):
    w = w_ref[...].astype(jnp.bfloat16)
    z = (_nt(h_ref[...], w) if nt else jnp.dot(h_ref[...], w, preferred_element_type=jnp.float32)) + b_ref[...]
    if act == "relu2":
        z = jnp.square(jnp.maximum(z, 0.0))
    o_ref[...] = z.astype(o_ref.dtype)


def _layer_spec(w, layer, block, index):
    if w.ndim == 2:
        return pl.BlockSpec(block, index)
    return pl.BlockSpec((None,) + block, lambda *g: (layer,) + index(*g))


def matmul_act(h, w, b, *, nt, act=None, out_dtype=jnp.float32, tm=1664, tn=512, layer=None):
    M, K = h.shape
    N = w.shape[-2] if nt else w.shape[-1]
    tm = _row_tile(M, tm)
    tn = min(tn, N)
    assert N % tn == 0, (N, tn)
    w_spec = (_layer_spec(w, layer, (tn, K), lambda i, j: (j, 0)) if nt
              else _layer_spec(w, layer, (K, tn), lambda i, j: (0, j)))
    return pl.pallas_call(
        functools.partial(_matmul_act_kernel, nt=nt, act=act),
        grid=(M // tm, N // tn),
        in_specs=[pl.BlockSpec((tm, K), lambda i, j: (i, 0)), w_spec, pl.BlockSpec((1, tn), lambda i, j: (0, j))],
        out_specs=pl.BlockSpec((tm, tn), lambda i, j: (i, j)),
        out_shape=jax.ShapeDtypeStruct((M, N), out_dtype),
        compiler_params=_params("parallel", "arbitrary"),
        name="matmul_act",
    )(h, w, b.reshape(1, N))


def _mix_kernel(ya_ref, yb_ref, wa_ref, wb_ref, ga_ref, gb_ref, o_ref):
    bf = jnp.bfloat16
    a = jnp.dot(ya_ref[...].astype(bf), wa_ref[...].astype(bf), preferred_element_type=jnp.float32)
    b = jnp.dot(yb_ref[...].astype(bf), wb_ref[...].astype(bf), preferred_element_type=jnp.float32)
    o_ref[...] = (jax.nn.sigmoid(ga_ref[...]) * a + jax.nn.sigmoid(gb_ref[...]) * b).astype(o_ref.dtype)


def _mix_into_kernel(ya_ref, yb_ref, wa_ref, wb_ref, ga_ref, gb_ref, prev_ref, o_ref):
    del prev_ref
    _mix_kernel(ya_ref, yb_ref, wa_ref, wb_ref, ga_ref, gb_ref, o_ref)


def mix_matmul(ya, yb, wa, wb, gm_src, *, row0, rows_total, into=None, tm=1024, tn=512, layer=None):
    M, Ka = ya.shape
    Kb = yb.shape[1]
    N = wa.shape[-1]
    gm, off = gm_src
    tm = _row_tile(M, tm)
    nb = N // tn
    assert off % tn == 0 and N % tn == 0 and row0 % tm == 0, (off, N, tn, row0, tm)
    ja, jb, i0 = off // tn, off // tn + nb, row0 // tm
    in_specs = [
        pl.BlockSpec((tm, Ka), lambda i, j: (i, 0)),
        pl.BlockSpec((tm, Kb), lambda i, j: (i, 0)),
        _layer_spec(wa, layer, (Ka, tn), lambda i, j: (0, j)),
        _layer_spec(wb, layer, (Kb, tn), lambda i, j: (0, j)),
        pl.BlockSpec((tm, tn), lambda i, j: (i0 + i, ja + j)),
        pl.BlockSpec((tm, tn), lambda i, j: (i0 + i, jb + j)),
    ]
    args = (ya, yb, wa, wb, gm, gm)
    aliases = {}
    if into is not None:
        in_specs.append(pl.BlockSpec(memory_space=pl.ANY))
        args += (into,)
        aliases = {len(args) - 1: 0}
    return pl.pallas_call(
        _mix_kernel if into is None else _mix_into_kernel,
        grid=(M // tm, nb),
        in_specs=in_specs,
        out_specs=pl.BlockSpec((tm, tn), lambda i, j: (i0 + i, j)),
        out_shape=jax.ShapeDtypeStruct((rows_total, N), jnp.bfloat16),
        input_output_aliases=aliases,
        compiler_params=_params("parallel", "arbitrary"),
        name="mix_matmul",
    )(*args)


def _matmul_norm_res_kernel(a_ref, w_ref, x_ref, g_ref, gn_ref, o_ref, h_ref, acc_ref):
    k = pl.program_id(1)

    @pl.when(k == 0)
    def _():
        acc_ref[...] = jnp.zeros_like(acc_ref)

    a = a_ref[...]
    for n0 in range(0, acc_ref.shape[1], NORM_RES_COLS):
        cols = slice(n0, n0 + NORM_RES_COLS)
        acc_ref[:, cols] += jnp.dot(a, w_ref[:, cols].astype(jnp.bfloat16), preferred_element_type=jnp.float32)

    @pl.when(k == pl.num_programs(1) - 1)
    def _():
        y = acc_ref[...]
        r = lax.rsqrt(jnp.mean(y * y, axis=-1, keepdims=True) + EPS)
        o = x_ref[...] + y * r * g_ref[...]
        o_ref[...] = o
        rn = lax.rsqrt(jnp.mean(o * o, axis=-1, keepdims=True) + EPS)
        h_ref[...] = (o * rn * gn_ref[...]).astype(h_ref.dtype)


NORM_RES_COLS = 512


def matmul_norm_res(a, w, x, g, g_next, *, tm=640, tk=2048, layer=None):
    M, K = a.shape
    N = w.shape[-1]
    tm = _row_tile(M, tm)
    tk = min(tk, K)
    return pl.pallas_call(
        _matmul_norm_res_kernel,
        grid=(M // tm, K // tk),
        in_specs=[
            pl.BlockSpec((tm, tk), lambda i, k: (i, k)),
            _layer_spec(w, layer, (tk, N), lambda i, k: (k, 0)),
            pl.BlockSpec((tm, N), lambda i, k: (i, 0)),
            pl.BlockSpec((1, N), lambda i, k: (0, 0)),
            pl.BlockSpec((1, N), lambda i, k: (0, 0)),
        ],
        out_specs=[pl.BlockSpec((tm, N), lambda i, k: (i, 0)), pl.BlockSpec((tm, N), lambda i, k: (i, 0))],
        out_shape=(jax.ShapeDtypeStruct((M, N), jnp.float32), jax.ShapeDtypeStruct((M, N), jnp.bfloat16)),
        scratch_shapes=[pltpu.VMEM((tm, N), jnp.float32)],
        compiler_params=_params("parallel", "arbitrary"),
        name="matmul_norm_res",
    )(a, w, x, g.reshape(1, N), g_next.reshape(1, N))


NSA_LANES = A_GROUP * Q_BLOCK
KEY_CHUNK = 512
WIN_CHUNK = 128
WIN_SPAN = WINDOW + Q_BLOCK
SLC_SLOTS = 128
POS_PIECES = 6
AUG = 128
MASK_BIG = 30000.0
N_FORCED = 3


def _pos_pieces(pos):
    pos = np.asarray(pos, np.int64)
    hi = (pos // 64) * 64
    lo = pos % 64
    return np.stack([hi, lo] * 3, axis=-1).astype(np.float32)


def _slope_rows():
    slopes = 2.0 ** (-8.0 * np.arange(1, A_HEADS + 1, dtype=np.float64) / A_HEADS)
    s = jnp.asarray(slopes.astype(np.float32))
    hi = s.astype(jnp.bfloat16)
    r1 = s - hi.astype(jnp.float32)
    mid = r1.astype(jnp.bfloat16)
    lo = (r1 - mid.astype(jnp.float32)).astype(jnp.bfloat16)
    rows = jnp.stack([hi, hi, mid, mid, lo, lo], axis=0)
    rows = rows.reshape(POS_PIECES, A_KV_HEADS, A_GROUP).transpose(1, 0, 2)
    rows = jnp.repeat(rows, Q_BLOCK, axis=-1)
    return jnp.pad(rows, ((0, 0), (0, 16 - POS_PIECES), (0, 0)))


def _split3(x):
    hi = x.astype(jnp.bfloat16)
    r1 = x - hi.astype(jnp.float32)
    mid = r1.astype(jnp.bfloat16)
    lo = (r1 - mid.astype(jnp.float32)).astype(jnp.bfloat16)
    return hi, mid, lo


def _tile4(x):
    return jnp.concatenate([x] * A_GROUP, axis=1)


def _softmax_step(carry, s, vT):
    m, l, acc = carry
    m_new = jnp.maximum(m, jnp.max(s, axis=0, keepdims=True))
    alpha = jnp.exp(m - m_new)
    p = jnp.exp(s - m_new)
    l = alpha * l + jnp.sum(p, axis=0, keepdims=True)
    acc = alpha * acc + jnp.dot(vT, p.astype(jnp.bfloat16), preferred_element_type=jnp.float32)
    return m_new, l, acc


def _nsa_prompt_kernel(q_ref, slope_ref, gT_ref, ck_ref, cvT_ref, ovT_ref, ks_ref, vsT_ref,
                       kw_ref, vwT_ref, o_ref, qa_ref, flag_ref):
    f32, bf = jnp.float32, jnp.bfloat16
    i = pl.program_id(1)
    s0 = i * Q_BLOCK
    ncp = ck_ref.shape[1]

    qT = q_ref[...].T * (HEAD_DIM ** -0.5)
    qcat = jnp.concatenate([qT[g * HEAD_DIM:(g + 1) * HEAD_DIM, :] for g in range(A_GROUP)], axis=1)
    qa_ref[0:HEAD_DIM, :] = qcat.astype(bf)
    qa_ref[HEAD_DIM:HEAD_DIM + 16, :] = slope_ref[0]
    qa_ref[HEAD_DIM + 16:AUG, :] = jnp.zeros((AUG - HEAD_DIM - 16, NSA_LANES), bf)
    qc = qa_ref[0:AUG, :]


    def compressed_and_select(nc, ns):
        sc = jnp.dot(ck_ref[0, 0:nc, :], qc, preferred_element_type=f32)
        n_idx = lax.broadcasted_iota(jnp.int32, (nc, Q_BLOCK), 0)
        t_idx = s0 + lax.broadcasted_iota(jnp.int32, (nc, Q_BLOCK), 1)
        valid = _tile4(n_idx * CMP_STRIDE + (CMP_BLOCK - 1) <= t_idx)
        sc = jnp.where(valid, sc, NEG)
        ec = jnp.where(valid, jnp.exp(sc - jnp.max(sc, axis=0, keepdims=True)), 0.0)
        lc = jnp.sum(ec, axis=0, keepdims=True)
        pc = ec * jnp.where(lc > 0.0, 1.0 / lc, 0.0)
        o_c = jnp.dot(cvT_ref[0, :, 0:nc], pc.astype(bf), preferred_element_type=f32)

        pg = pc[:, 0:Q_BLOCK]
        for g in range(1, A_GROUP):
            pg = pg + pc[:, g * Q_BLOCK:(g + 1) * Q_BLOCK]
        imp = sum(jnp.dot(ovT_ref[0:ns, 0:nc], piece, preferred_element_type=f32) for piece in _split3(pg))

        blk = lax.broadcasted_iota(jnp.int32, (ns, Q_BLOCK), 0)
        cur = (s0 + lax.broadcasted_iota(jnp.int32, (ns, Q_BLOCK), 1)) // SLC_BLOCK
        forced = (blk == 0) | (blk == cur) | (blk == cur - 1)
        causal = blk <= cur
        score = jnp.where(forced, -jnp.inf, jnp.where(causal, imp, -FORCE))
        blk_f = blk.astype(f32)
        sel = jnp.where(forced, 1.0, 0.0)
        for _ in range(SLC_TOP - N_FORCED):
            top = jnp.max(score, axis=0, keepdims=True)
            first = jnp.min(jnp.where(score == top, blk_f, float(SLC_SLOTS)), axis=0, keepdims=True)
            pick = blk_f == first
            sel = jnp.where(pick, 1.0, sel)
            score = jnp.where(pick, -jnp.inf, score)
        picked = jnp.where((sel > 0.0) & causal, 1.0, 0.0)
        if ns < SLC_SLOTS:
            picked = jnp.concatenate([picked, jnp.zeros((SLC_SLOTS - ns, Q_BLOCK), f32)], axis=0)
        return o_c, picked

    nqb = ks_ref.shape[1] // Q_BLOCK
    if ncp % 256 == 0 and nqb % 8 == 0 and nqb < SLC_SLOTS:
        o_c, picked = lax.cond(i < nqb // 2, lambda: compressed_and_select(ncp // 2, nqb),
                               lambda: compressed_and_select(ncp, SLC_SLOTS))
    else:
        o_c, picked = compressed_and_select(ncp, SLC_SLOTS)
    qa_ref[AUG:AUG + SLC_SLOTS, :] = _tile4(jnp.where(picked > 0.0, 0.0, -MASK_BIG)).astype(bf)

    def sel_scores(c):
        kc = ks_ref[0, pl.ds(pl.multiple_of(c * KEY_CHUNK, KEY_CHUNK), KEY_CHUNK), :]
        return jnp.dot(kc, qa_ref[...], preferred_element_type=f32)

    init = (jnp.full((1, NSA_LANES), NEG, f32), jnp.zeros((1, NSA_LANES), f32),
            jnp.zeros((HEAD_DIM, NSA_LANES), f32))
    c_last = (i * Q_BLOCK) // KEY_CHUNK
    bpc = KEY_CHUNK // SLC_BLOCK
    for c in range(SLC_SLOTS // bpc):
        flag_ref[c] = (jnp.max(picked[c * bpc:(c + 1) * bpc, :]) > 0.0).astype(jnp.int32)

    def sel_chunk(c, cr):
        return lax.cond(flag_ref[c] > 0, lambda x: _softmax_step(x, sel_scores(c), vsT_ref[0, c]), lambda x: x, cr)

    carry = lax.fori_loop(0, c_last, sel_chunk, init)
    j_last = c_last * KEY_CHUNK + lax.broadcasted_iota(jnp.int32, (KEY_CHUNK, Q_BLOCK), 0)
    t_last = s0 + lax.broadcasted_iota(jnp.int32, (KEY_CHUNK, Q_BLOCK), 1)
    m_s, l_s, acc_s = _softmax_step(carry, jnp.where(_tile4(j_last <= t_last), sel_scores(c_last), NEG),
                                    vsT_ref[0, c_last])
    o_s = acc_s * (1.0 / l_s)

    w0 = jnp.maximum(i - WINDOW // WIN_CHUNK, 0)
    kw = kw_ref[0, pl.ds(pl.multiple_of(w0 * WIN_CHUNK, WIN_CHUNK), WIN_SPAN), :]
    sw = jnp.dot(kw, qc, preferred_element_type=f32)
    jw = w0 * WIN_CHUNK + lax.broadcasted_iota(jnp.int32, (WIN_SPAN, Q_BLOCK), 0)
    tw = s0 + lax.broadcasted_iota(jnp.int32, (WIN_SPAN, Q_BLOCK), 1)
    sw = jnp.where(_tile4((jw <= tw) & (jw > tw - WINDOW)), sw, NEG)
    pw = jnp.exp(sw - jnp.max(sw, axis=0, keepdims=True))
    l_w = jnp.sum(pw, axis=0, keepdims=True)
    pw = pw.astype(bf)
    acc_w = jnp.zeros((HEAD_DIM, NSA_LANES), f32)
    for d in range(WIN_SPAN // WIN_CHUNK):
        acc_w = acc_w + jnp.dot(vwT_ref[0, w0 + d], pw[d * WIN_CHUNK:(d + 1) * WIN_CHUNK, :], preferred_element_type=f32)
    o_w = acc_w * (1.0 / l_w)

    gate = jax.nn.sigmoid(gT_ref[0])
    outs = []
    for g in range(A_GROUP):
        cols = slice(g * Q_BLOCK, (g + 1) * Q_BLOCK)
        outs.append(gate[3 * g:3 * g + 1, :] * o_c[:, cols] + gate[3 * g + 1:3 * g + 2, :] * o_s[:, cols]
                    + gate[3 * g + 2:3 * g + 3, :] * o_w[:, cols])
    o_ref[...] = jnp.concatenate(outs, axis=0).T


def _cmp_prompt_kernel(x_ref, pe_ref, w1_ref, b1_ref, w2T_ref, ck_ref, cvT_ref, bias_ref):
    f32, bf = jnp.float32, jnp.bfloat16
    nseg = ck_ref.shape[1]
    row_w = 2 * A_KV_HEADS * HEAD_DIM
    for c in range(2):
        bias = jnp.dot(pe_ref[c], w1_ref[c], preferred_element_type=f32,
                       precision=lax.Precision.HIGHEST) + b1_ref[c]
        bias_ref[c] = bias
        for k in range(A_KV_HEADS):
            col = (c * A_KV_HEADS + k) * HEAD_DIM
            y0 = jnp.zeros((nseg, CMP_HIDDEN), f32)
            y1 = jnp.zeros((nseg, CMP_HIDDEN), f32)
            for j in range(CMP_STRIDE):
                xj = x_ref[:, j * row_w + col:j * row_w + col + HEAD_DIM].astype(bf)
                w0 = w1_ref[c, j * HEAD_DIM:(j + 1) * HEAD_DIM, :].astype(bf)
                w1 = w1_ref[c, (CMP_STRIDE + j) * HEAD_DIM:(CMP_STRIDE + j + 1) * HEAD_DIM, :].astype(bf)
                y0 = y0 + jnp.dot(xj, w0, preferred_element_type=f32)
                y1 = y1 + jnp.dot(xj, w1, preferred_element_type=f32)
            pre = y0 + pltpu.roll(y1, nseg - 1, 0)
            hid = jnp.maximum(pre + bias, 0.0).astype(bf)
            if c == 0:
                ck_ref[k] = lax.dot_general(hid, w2T_ref[c].astype(bf), (((1,), (1,)), ((), ())),
                                            preferred_element_type=f32)
            else:
                cvT_ref[k] = lax.dot_general(w2T_ref[c].astype(bf), hid, (((1,), (1,)), ((), ())),
                                             preferred_element_type=f32)


def compress_prompt(x_cmp, pe, w1, b1, w2):
    T = x_cmp.shape[0]
    nseg = T // CMP_STRIDE
    return pl.pallas_call(
        _cmp_prompt_kernel,
        out_shape=(jax.ShapeDtypeStruct((A_KV_HEADS, nseg, HEAD_DIM), jnp.float32),
                   jax.ShapeDtypeStruct((A_KV_HEADS, HEAD_DIM, nseg), jnp.float32),
                   jax.ShapeDtypeStruct((2, 1, CMP_HIDDEN), jnp.float32)),
        compiler_params=pltpu.CompilerParams(vmem_limit_bytes=VMEM_LIMIT_BYTES),
        name="compress_prompt",
    )(x_cmp.reshape(nseg, CMP_STRIDE * x_cmp.shape[1]), pe.reshape(2, 1, CMP_BLOCK * HEAD_DIM), w1, b1.reshape(2, 1, CMP_HIDDEN), w2.transpose(0, 2, 1))


def nsa_prompt_pallas(q_src, ga, kva, pe, w1, b1, w2):
    bf = jnp.bfloat16
    q_arr, q_off = q_src
    q_cb = q_off // (A_GROUP * HEAD_DIM)
    T = kva.shape[0]
    nqb = T // Q_BLOCK
    assert T % KEY_CHUNK == 0 and T >= WIN_SPAN, T
    ncp = T // CMP_STRIDE
    kvw = A_KV_HEADS * HEAD_DIM
    ck, cvT, pe_bias = compress_prompt(kva[:, 0:2 * kvw], pe, w1, b1, w2)

    def heads(x):
        return x.reshape(T, A_KV_HEADS, HEAD_DIM).transpose(1, 0, 2)

    def chunksT(x, chunk):
        return x.reshape(T // chunk, chunk, A_KV_HEADS, HEAD_DIM).transpose(2, 0, 3, 1).astype(bf)

    pos = np.arange(T)
    key_pos = jnp.broadcast_to(jnp.asarray(_pos_pieces(pos)), (A_KV_HEADS, T, POS_PIECES))
    zpad = jnp.zeros((A_KV_HEADS, T, AUG - HEAD_DIM - POS_PIECES), jnp.float32)
    onehot = jnp.broadcast_to(jnp.asarray((pos[:, None] // SLC_BLOCK == np.arange(SLC_SLOTS)[None, :]).astype(np.float32)),
                              (A_KV_HEADS, T, SLC_SLOTS))
    ks_aug = jnp.concatenate([heads(kva[:, 2 * kvw:3 * kvw]), key_pos, zpad, onehot], axis=-1).astype(bf)
    kw_aug = jnp.concatenate([heads(kva[:, 4 * kvw:5 * kvw]), key_pos, zpad], axis=-1).astype(bf)
    c_end = np.arange(ncp) * CMP_STRIDE + (CMP_BLOCK - 1)
    ck_aug = jnp.concatenate([ck, jnp.broadcast_to(jnp.asarray(_pos_pieces(c_end)), (A_KV_HEADS, ncp, POS_PIECES)),
                              jnp.zeros((A_KV_HEADS, ncp, AUG - HEAD_DIM - POS_PIECES), jnp.float32)], axis=-1).astype(bf)
    vsT = chunksT(kva[:, 3 * kvw:4 * kvw], KEY_CHUNK)
    vwT = chunksT(kva[:, 5 * kvw:6 * kvw], WIN_CHUNK)
    c_start = c_end - (CMP_BLOCK - 1)
    s_start = np.arange(SLC_SLOTS) * SLC_BLOCK
    ovT = ((c_start[None, :] < s_start[:, None] + SLC_BLOCK) & (c_end[None, :] >= s_start[:, None])
           & (np.arange(ncp)[None, :] < ncp - 1))
    ovT = jnp.asarray(ovT.astype(np.float32)).astype(bf)
    gT = jnp.pad(ga.T.reshape(A_KV_HEADS, 3 * A_GROUP, T), ((0, 0), (0, 16 - 3 * A_GROUP), (0, 0)))

    y = pl.pallas_call(
        _nsa_prompt_kernel,
        grid=(A_KV_HEADS, nqb),
        in_specs=[
            pl.BlockSpec((Q_BLOCK, A_GROUP * HEAD_DIM), lambda k, i: (i, q_cb + k)),
            pl.BlockSpec((1, 16, NSA_LANES), lambda k, i: (k, 0, 0)),
            pl.BlockSpec((1, 16, Q_BLOCK), lambda k, i: (k, 0, i)),
            pl.BlockSpec((1, ncp, AUG), lambda k, i: (k, 0, 0)),
            pl.BlockSpec((1, HEAD_DIM, ncp), lambda k, i: (k, 0, 0)),
            pl.BlockSpec((SLC_SLOTS, ncp), lambda k, i: (0, 0)),
            pl.BlockSpec((1, T, AUG + SLC_SLOTS), lambda k, i: (k, 0, 0)),
            pl.BlockSpec((1, T // KEY_CHUNK, HEAD_DIM, KEY_CHUNK), lambda k, i: (k, 0, 0, 0)),
            pl.BlockSpec((1, T, AUG), lambda k, i: (k, 0, 0)),
            pl.BlockSpec((1, T // WIN_CHUNK, HEAD_DIM, WIN_CHUNK), lambda k, i: (k, 0, 0, 0)),
        ],
        out_specs=pl.BlockSpec((Q_BLOCK, A_GROUP * HEAD_DIM), lambda k, i: (i, k)),
        out_shape=jax.ShapeDtypeStruct((T, A_WIDTH), jnp.float32),
        scratch_shapes=[pltpu.VMEM((AUG + SLC_SLOTS, NSA_LANES), bf),
                        pltpu.SMEM((SLC_SLOTS * SLC_BLOCK // KEY_CHUNK,), jnp.int32)],
        compiler_params=_params("parallel", "arbitrary"),
        name="nsa_prompt",
    )(q_arr, _slope_rows(), gT, ck_aug, cvT.astype(bf), ovT, ks_aug, vsT, kw_aug, vwT)
    return y, pe_bias


PAGE_SIZE = 128
N_PAGES = PAST_LEN // PAGE_SIZE
PAGE_SEGS = PAGE_SIZE // CMP_STRIDE
PAST_SEGS = PAST_LEN // CMP_STRIDE
ROW_W = 4 * A_KV_HEADS * HEAD_DIM
KVW = A_KV_HEADS * HEAD_DIM
NS_SAMPLE = PAST_LEN // SLC_BLOCK + 1


def _nt(a, b):
    return lax.dot_general(a, b, (((1,), (1,)), ((), ())), preferred_element_type=jnp.float32)


def _dot3(x, w):
    return sum(jnp.dot(p, w, preferred_element_type=jnp.float32) for p in _split3(x))


SEG_PITCH = 24


def _nsa_decode_kernel(pt_ref, q_ref, g_ref, new_ref, *refs, has_into):
    del pt_ref
    pages = refs[:N_PAGES]
    rest = refs[N_PAGES:]
    (win_ref, w1_ref, bias_ref, w2bd_ref, bmask_ref, basec_ref, bases_ref, basew_ref, slope_ref,
     grp_ref, ov_ref, exp_ref) = rest[:12]
    o_ref, wout_ref, tr_ref, xc_ref = rest[12 + int(has_into):]
    f32, bf = jnp.float32, jnp.bfloat16

    q = q_ref[0] * (HEAD_DIM ** -0.5)
    qbd = jnp.where(bmask_ref[...] > 0.0, jnp.concatenate([q] * A_KV_HEADS, axis=1), 0.0).astype(bf)
    qbd_f = qbd.astype(f32)
    new = new_ref[0]

    ckv = []
    for c in range(2):
        for p in range(N_PAGES):
            for kp in range(2):
                r0 = c * KVW + kp * 2 * HEAD_DIM
                t = pages[p][0, r0:r0 + 2 * HEAD_DIM, :].T
                for s in range(PAGE_SEGS):
                    t0 = (kp * PAST_SEGS + p * PAGE_SEGS + s) * SEG_PITCH
                    tr_ref[t0:t0 + CMP_STRIDE, :] = t[s * CMP_STRIDE:(s + 1) * CMP_STRIDE, :]
        for j in range(CMP_STRIDE):
            for kp in range(2):
                piece = tr_ref[pl.ds(kp * PAST_SEGS * SEG_PITCH + j, PAST_SEGS, stride=SEG_PITCH), :]
                for kk in range(2):
                    k = 2 * kp + kk
                    xc_ref[k * PAST_SEGS:(k + 1) * PAST_SEGS, j * HEAD_DIM:(j + 1) * HEAD_DIM] = (
                        piece[:, kk * HEAD_DIM:(kk + 1) * HEAD_DIM])
        y = jnp.dot(xc_ref[...].astype(bf), w1_ref[c], preferred_element_type=f32)
        pre = y[:, 0:CMP_HIDDEN] + pltpu.roll(y[:, CMP_HIDDEN:2 * CMP_HIDDEN], A_KV_HEADS * PAST_SEGS - 1, 0)
        hid = jnp.maximum(pre + bias_ref[c], 0.0).astype(bf)
        hid = jnp.concatenate([hid[k * PAST_SEGS:(k + 1) * PAST_SEGS] for k in range(A_KV_HEADS)], axis=1)
        ckv.append(jnp.dot(hid, w2bd_ref[c], preferred_element_type=f32).astype(bf))
    ck, cv = ckv

    sc = _nt(qbd, ck) + basec_ref[...]
    ec = jnp.exp(sc - jnp.max(sc, axis=1, keepdims=True))
    pc = ec / jnp.sum(ec, axis=1, keepdims=True)
    o_c = jnp.dot(pc.astype(bf), cv, preferred_element_type=f32)

    pov = _dot3(pc, ov_ref[...])
    imp = sum(jnp.dot(grp_ref[...], p, preferred_element_type=f32) for p in _split3(pov))

    lane = lax.broadcasted_iota(jnp.int32, (A_HEADS, PAST_SEGS), 1)
    score = jnp.where((lane == 0) | (lane == NS_SAMPLE - 1) | (lane == NS_SAMPLE - 2), FORCE,
                      jnp.where(lane < NS_SAMPLE, imp, -jnp.inf))
    rr = lax.broadcasted_iota(jnp.int32, (PAST_SEGS, PAST_SEGS), 0)
    cc = lax.broadcasted_iota(jnp.int32, (PAST_SEGS, PAST_SEGS), 1)
    row_id = lax.broadcasted_iota(jnp.int32, (A_HEADS, PAST_SEGS), 0)
    selrows = jnp.zeros((A_HEADS, PAST_SEGS), f32)
    for k in range(A_KV_HEADS):
        row = jnp.broadcast_to(score[A_GROUP * k:A_GROUP * k + 1, :], (PAST_SEGS, PAST_SEGS))
        col = jnp.max(jnp.where(rr == cc, row, -jnp.inf), axis=1, keepdims=True)
        ahead = (col > row) | ((col == row) & (rr < cc))
        rank = jnp.sum(jnp.where(ahead, 1.0, 0.0), axis=0, keepdims=True)
        sel_k = jnp.where(rank < float(SLC_TOP), 1.0, 0.0)
        selrows = jnp.where(row_id // A_GROUP == k, jnp.broadcast_to(sel_k, (A_HEADS, PAST_SEGS)), selrows)
    selpos = jnp.dot(selrows.astype(bf), exp_ref[...], preferred_element_type=f32)

    slope = slope_ref[...]
    scores = []
    for p in range(N_PAGES):
        kp_ = pages[p][0, 2 * KVW:3 * KVW, :].astype(bf)
        mask = jnp.where(selpos[:, p * PAGE_SIZE:(p + 1) * PAGE_SIZE] > 0.5, 0.0, NEG)
        scores.append(jnp.dot(qbd, kp_, preferred_element_type=f32) + (bases_ref[...] + slope * float(p * PAGE_SIZE)) + mask)
    s_new = jnp.sum(qbd_f * new[:, 2 * KVW:3 * KVW], axis=1, keepdims=True)
    mx = scores[0]
    for s in scores[1:]:
        mx = jnp.maximum(mx, s)
    m = jnp.maximum(jnp.max(mx, axis=1, keepdims=True), s_new)
    p_new = jnp.exp(s_new - m)
    acc = p_new * new[:, 3 * KVW:4 * KVW]
    psum = jnp.zeros((A_HEADS, PAGE_SIZE), f32)
    for p in range(N_PAGES):
        pr = jnp.exp(scores[p] - m)
        psum = psum + pr
        acc = acc + _nt(pr.astype(bf), pages[p][0, 3 * KVW:4 * KVW, :].astype(bf))
    o_s = acc / (jnp.sum(psum, axis=1, keepdims=True) + p_new)

    sw = jnp.dot(qbd, win_ref[0, 0:KVW, :].astype(bf), preferred_element_type=f32) + basew_ref[...]
    s_neww = jnp.sum(qbd_f * new[:, 4 * KVW:5 * KVW], axis=1, keepdims=True)
    m = jnp.maximum(jnp.max(sw, axis=1, keepdims=True), s_neww)
    pw = jnp.exp(sw - m)
    p_new = jnp.exp(s_neww - m)
    o_w = (_nt(pw.astype(bf), win_ref[0, KVW:2 * KVW, :].astype(bf)) + p_new * new[:, 5 * KVW:6 * KVW]) / (
        jnp.sum(pw, axis=1, keepdims=True) + p_new)

    gate = jax.nn.sigmoid(g_ref[0])
    o = gate[:, 0:1] * o_c + gate[:, 1:2] * o_s + gate[:, 2:3] * o_w
    for k in range(A_KV_HEADS):
        o_ref[0, A_GROUP * k:A_GROUP * (k + 1), :] = o[A_GROUP * k:A_GROUP * (k + 1), k * HEAD_DIM:(k + 1) * HEAD_DIM]

    wr = lax.broadcasted_iota(jnp.int32, (WINDOW, 2 * KVW), 0)
    wc = lax.broadcasted_iota(jnp.int32, (WINDOW, 2 * KVW), 1)
    new_col = jnp.sum(jnp.where(wr == wc, jnp.broadcast_to(new[:, 4 * KVW:6 * KVW], (WINDOW, 2 * KVW)), 0.0),
                      axis=1, keepdims=True)
    wout_ref[0] = jnp.where(wc == WINDOW - 1, new_col, pltpu.roll(win_ref[0], WINDOW - 1, 1))


def _decode_constants():
    slopes = (2.0 ** (-8.0 * np.arange(1, A_HEADS + 1, dtype=np.float64) / A_HEADS)).astype(np.float32)[:, None]
    heads = np.arange(A_HEADS)[:, None]
    bmask = (np.arange(KVW)[None, :] // HEAD_DIM == heads // A_GROUP).astype(np.float32)
    n = np.arange(PAST_SEGS)[None, :]
    c_end = n * CMP_STRIDE + (CMP_BLOCK - 1)
    basec = np.where(n < PAST_SEGS - 1, slopes * (c_end - PAST_LEN), NEG).astype(np.float32)
    bases = (slopes * (np.arange(PAGE_SIZE)[None, :] - PAST_LEN)).astype(np.float32)
    r = np.arange(WINDOW)[None, :]
    basew = np.where(r >= 1, slopes * (r - WINDOW), NEG).astype(np.float32)
    slope = np.broadcast_to(slopes, (A_HEADS, PAGE_SIZE)).astype(np.float32)
    grp = (heads // A_GROUP == heads.T // A_GROUP).astype(np.float32)
    c_start = np.arange(PAST_SEGS) * CMP_STRIDE
    s_start = np.arange(PAST_SEGS) * SLC_BLOCK
    ov = ((c_start[:, None] < s_start[None, :] + SLC_BLOCK) & (c_start[:, None] + CMP_BLOCK - 1 >= s_start[None, :])
          & (np.arange(PAST_SEGS)[:, None] < PAST_SEGS - 1) & (np.arange(PAST_SEGS)[None, :] < NS_SAMPLE))
    expand = (np.arange(PAST_LEN)[None, :] // SLC_BLOCK == np.arange(PAST_SEGS)[:, None])
    as_bf = lambda a: jnp.asarray(a.astype(np.float32)).astype(jnp.bfloat16)
    return (jnp.asarray(bmask), jnp.asarray(basec), jnp.asarray(bases), jnp.asarray(basew), jnp.asarray(slope),
            as_bf(grp), as_bf(ov), as_bf(expand))


def nsa_decode_pallas(qa, ga, kva, pool, page_table, win_buf, layer, w1, w2, pe_bias, win_into=None):
    bf = jnp.bfloat16
    B = qa.shape[0]
    depth, npool = pool.shape[:2]
    page_table = page_table + layer * npool
    poolT = pool.transpose(0, 1, 3, 4, 5, 2).reshape(depth * npool, ROW_W, PAGE_SIZE)
    winT = win_buf.transpose(0, 1, 3, 4, 5, 2).reshape(depth * B, 2 * KVW, WINDOW)
    w1r = w1.reshape(2, CMP_RATIO, CMP_STRIDE * HEAD_DIM, CMP_HIDDEN).transpose(0, 2, 1, 3)
    w1r = w1r.reshape(2, CMP_STRIDE * HEAD_DIM, CMP_RATIO * CMP_HIDDEN).astype(bf)
    w2bd = jnp.einsum('kl,ced->ckeld', jnp.eye(A_KV_HEADS, dtype=w2.dtype), w2)
    w2bd = w2bd.reshape(2, A_KV_HEADS * CMP_HIDDEN, KVW).astype(bf)
    consts = _decode_constants()
    full = lambda shape: pl.BlockSpec(shape, lambda b, pt: (0,) * len(shape))
    page_specs = [pl.BlockSpec((1, ROW_W, PAGE_SIZE), functools.partial(lambda p, b, pt: (pt[b, p], 0, 0), p))
                  for p in range(N_PAGES)]
    grid_spec = pltpu.PrefetchScalarGridSpec(
        num_scalar_prefetch=1,
        grid=(B,),
        in_specs=[
            pl.BlockSpec((1, A_HEADS, HEAD_DIM), lambda b, pt: (b, 0, 0)),
            pl.BlockSpec((1, A_HEADS, 3), lambda b, pt: (b, 0, 0)),
            pl.BlockSpec((1, 1, N_KV_BRANCH * KVW), lambda b, pt: (b, 0, 0)),
            *page_specs,
            pl.BlockSpec((1, 2 * KVW, WINDOW), lambda b, pt: (layer * B + b, 0, 0)),
            full(w1r.shape), full((2, 1, CMP_HIDDEN)), full(w2bd.shape),
            *[full(c.shape) for c in consts],
            *([] if win_into is None else [pl.BlockSpec(memory_space=pl.ANY)]),
        ],
        out_specs=[
            pl.BlockSpec((1, A_HEADS, HEAD_DIM), lambda b, pt: (b, 0, 0)),
            pl.BlockSpec((1, 2 * KVW, WINDOW), lambda b, pt: (layer * B + b, 0, 0)),
        ],
        scratch_shapes=[pltpu.VMEM((2 * PAST_SEGS * SEG_PITCH, 2 * HEAD_DIM), jnp.float32),
                        pltpu.VMEM((A_KV_HEADS * PAST_SEGS, CMP_STRIDE * HEAD_DIM), jnp.float32)],
    )
    args = (page_table, qa.reshape(B, A_HEADS, HEAD_DIM), ga.reshape(B, A_HEADS, 3), kva.reshape(B, 1, N_KV_BRANCH * KVW),
            *([poolT] * N_PAGES), winT, w1r, pe_bias.reshape(2, 1, CMP_HIDDEN), w2bd, *consts)
    aliases = {}
    if win_into is not None:
        args += (win_into,)
        aliases = {len(args) - 1: 1}
    o, wout = pl.pallas_call(
        functools.partial(_nsa_decode_kernel, has_into=win_into is not None),
        grid_spec=grid_spec,
        out_shape=(jax.ShapeDtypeStruct((B, A_HEADS, HEAD_DIM), jnp.float32),
                   jax.ShapeDtypeStruct((depth * B, 2 * KVW, WINDOW), jnp.float32)),
        input_output_aliases=aliases,
        compiler_params=_params("arbitrary"),
        name="nsa_decode",
    )(*args)
    return o.reshape(B, A_WIDTH), wout


MLSTM_L = 128
VAUG = 2 * V_DIM


def _mlstm_prompt_kernel(q_ref, k_ref, v_ref, o_ref, fsrc_ref, ifT_ref, y_ref, c_ref, m_ref, *, f_off):
    f32, bf = jnp.float32, jnp.bfloat16
    L = MLSTM_L
    f_ref = fsrc_ref.at[:, f_off:f_off + B_HEADS]
    iT_ref = ifT_ref.at[0:B_HEADS, :]
    fT_ref = ifT_ref.at[B_HEADS:2 * B_HEADS, :]

    @pl.when(pl.program_id(0) == 0)
    def _():
        c_ref[...] = jnp.zeros_like(c_ref)
        m_ref[...] = jnp.zeros_like(m_ref)

    kT_all = k_ref[...].T * (QK_DIM ** -0.5)
    rr = lax.broadcasted_iota(jnp.int32, (L, L), 0)
    cc = lax.broadcasted_iota(jnp.int32, (L, L), 1)
    lower = rr >= cc
    tril = jnp.where(lower, 1.0, 0.0).astype(bf)
    triu = jnp.where(rr <= cc, 1.0, 0.0).astype(bf)
    b_col = sum(jnp.dot(tril, p, preferred_element_type=f32) for p in _split3(jax.nn.log_sigmoid(f_ref[...])))
    b_row = _dot3(jax.nn.log_sigmoid(fT_ref[...]), triu)
    a_row = iT_ref[...] - b_row
    ones_col = jnp.where(lax.broadcasted_iota(jnp.int32, (L, V_DIM), 1) == 0, 1.0, 0.0)

    for h in range(B_HEADS):
        m_prev = m_ref[h, 0:1, 0:1]
        a = a_row[h:h + 1, :]
        amat = jnp.where(lower, jnp.broadcast_to(a, (L, L)), -jnp.inf)
        big_m = jnp.maximum(m_prev, jnp.max(amat, axis=1, keepdims=True))
        dmat = jnp.exp(amat - big_m)
        inter = jnp.exp(m_prev - big_m)
        q = q_ref[:, h * QK_DIM:(h + 1) * QK_DIM].astype(bf)
        kT = kT_all[h * QK_DIM:(h + 1) * QK_DIM, :]
        vaug = jnp.concatenate([v_ref[:, h * V_DIM:(h + 1) * V_DIM], ones_col], axis=1).astype(bf)
        s = jnp.dot(q, kT.astype(bf), preferred_element_type=f32) * dmat
        r = (jnp.dot(s.astype(bf), vaug, preferred_element_type=f32)
             + inter * jnp.dot(q, c_ref[h].astype(bf), preferred_element_type=f32))
        m_new = b_col[:, h:h + 1] + big_m
        den = jnp.maximum(jnp.abs(r[:, V_DIM:V_DIM + 1]), jnp.exp(-m_new))
        y_ref[:, h * V_DIM:(h + 1) * V_DIM] = (jax.nn.sigmoid(o_ref[:, h * V_DIM:(h + 1) * V_DIM])
                                               * (r[:, 0:V_DIM] / den))
        m_end = big_m[L - 1:L, :]
        w_end = jnp.exp(a - m_end)
        kw_end = kT * w_end
        c_new = inter[L - 1:L, :] * c_ref[h] + jnp.dot(kw_end.astype(bf), vaug, preferred_element_type=f32)
        n_new = inter[L - 1:L, :] * c_ref[h, :, V_DIM:V_DIM + 1] + jnp.sum(kw_end, axis=1, keepdims=True)
        c_ref[h] = jnp.where(lax.broadcasted_iota(jnp.int32, (QK_DIM, VAUG), 1) == V_DIM, n_new, c_new)
        m_ref[h] = jnp.broadcast_to(m_new[L - 1:L, :], m_ref.shape[1:])


def mlstm_prompt_pallas(q_src, k_src, v_src, o_src, f_src, ifT, T):
    L = MLSTM_L
    (q_arr, q_cb), (k_arr, k_cb), (v_arr, v_cb), (o_arr, o_cb), (f_arr, f_cb, f_off) = q_src, k_src, v_src, o_src, f_src
    y, caug, m = pl.pallas_call(
        functools.partial(_mlstm_prompt_kernel, f_off=f_off),
        grid=(T // L,),
        in_specs=[
            pl.BlockSpec((L, B_HEADS * QK_DIM), lambda c: (c, q_cb)),
            pl.BlockSpec((L, B_HEADS * QK_DIM), lambda c: (c, k_cb)),
            pl.BlockSpec((L, B_WIDTH), lambda c: (c, v_cb)),
            pl.BlockSpec((L, B_WIDTH), lambda c: (c, o_cb)),
            pl.BlockSpec((L, f_arr.shape[1] if f_arr.shape[1] < 128 else 128), lambda c: (c, f_cb)),
            pl.BlockSpec((2 * B_HEADS, L), lambda c: (0, c)),
        ],
        out_specs=[
            pl.BlockSpec((L, B_WIDTH), lambda c: (c, 0)),
            pl.BlockSpec((B_HEADS, QK_DIM, VAUG), lambda c: (0, 0, 0)),
            pl.BlockSpec((B_HEADS, 8, 128), lambda c: (0, 0, 0)),
        ],
        out_shape=(jax.ShapeDtypeStruct((T, B_WIDTH), jnp.float32),
                   jax.ShapeDtypeStruct((B_HEADS, QK_DIM, VAUG), jnp.float32),
                   jax.ShapeDtypeStruct((B_HEADS, 8, 128), jnp.float32)),
        compiler_params=_params("arbitrary"),
        name="mlstm_prompt",
    )(q_arr, k_arr, v_arr, o_arr, f_arr, ifT)
    return y, caug[:, :, 0:V_DIM], caug[:, :, V_DIM], m[:, 0, 0]


def _mlstm_sample_kernel(q_ref, k_ref, qT_ref, kT_ref, v_ref, o_ref, i_ref, f_ref, m_ref, n_ref, c_ref,
                         y_ref, cn_ref, nn_ref, mn_ref):
    bb = q_ref.shape[0]
    scale = QK_DIM ** -0.5
    for b in range(bb):
        logf = jax.nn.log_sigmoid(f_ref[b])
        m_new = jnp.maximum(logf + m_ref[b], i_ref[b])
        d_all = jnp.exp(i_ref[b] - m_new)
        inter_all = jnp.exp(logf + m_ref[b] - m_new)
        floor_all = jnp.exp(-m_new)
        mn_ref[b] = m_new
        qk_all = jnp.sum(q_ref[b] * k_ref[b], axis=1, keepdims=True) * scale
        qn_all = jnp.sum(q_ref[b] * n_ref[b], axis=1, keepdims=True)
        for h in range(B_HEADS):
            d = d_all[:, h:h + 1]
            inter = inter_all[:, h:h + 1]
            s = qk_all[h:h + 1, :] * d
            c = c_ref[b, h]
            v = v_ref[b, h:h + 1, :]
            qc = qT_ref[b, :, h:h + 1]
            kc = kT_ref[b, :, h:h + 1] * scale
            num = inter * jnp.sum(qc * c, axis=0, keepdims=True) + s * v
            den = inter * qn_all[h:h + 1, :] + s
            hout = num / jnp.maximum(jnp.abs(den), floor_all[:, h:h + 1])
            y_ref[b, h:h + 1, :] = jax.nn.sigmoid(o_ref[b, h:h + 1, :]) * hout
            cn_ref[b, h] = inter * c + d * (kc * v)
            nn_ref[b, h:h + 1, :] = inter * n_ref[b, h:h + 1, :] + d * (k_ref[b, h:h + 1, :] * scale)


def mlstm_sample_pallas(qb, kb, vb, ib, fb, ob, state_C, state_n, state_m, layer, *, bb=8):
    B = qb.shape[0]
    nb = B // bb
    q3 = qb.reshape(B, B_HEADS, QK_DIM)
    k3 = kb.reshape(B, B_HEADS, QK_DIM)
    row8 = lambda x: x.reshape(-1, 1, B_HEADS)
    lay = layer * nb
    y, cn, nn, mn = pl.pallas_call(
        _mlstm_sample_kernel,
        grid=(nb,),
        in_specs=[
            pl.BlockSpec((bb, B_HEADS, QK_DIM), lambda i: (i, 0, 0)),
            pl.BlockSpec((bb, B_HEADS, QK_DIM), lambda i: (i, 0, 0)),
            pl.BlockSpec((bb, QK_DIM, B_HEADS), lambda i: (i, 0, 0)),
            pl.BlockSpec((bb, QK_DIM, B_HEADS), lambda i: (i, 0, 0)),
            pl.BlockSpec((bb, B_HEADS, V_DIM), lambda i: (i, 0, 0)),
            pl.BlockSpec((bb, B_HEADS, V_DIM), lambda i: (i, 0, 0)),
            pl.BlockSpec((bb, 1, B_HEADS), lambda i: (i, 0, 0)),
            pl.BlockSpec((bb, 1, B_HEADS), lambda i: (i, 0, 0)),
            pl.BlockSpec((bb, 1, B_HEADS), lambda i: (lay + i, 0, 0)),
            pl.BlockSpec((bb, B_HEADS, QK_DIM), lambda i: (lay + i, 0, 0)),
            pl.BlockSpec((bb, B_HEADS, QK_DIM, V_DIM), lambda i: (lay + i, 0, 0, 0)),
        ],
        out_specs=[
            pl.BlockSpec((bb, B_HEADS, V_DIM), lambda i: (i, 0, 0)),
            pl.BlockSpec((bb, B_HEADS, QK_DIM, V_DIM), lambda i: (i, 0, 0, 0)),
            pl.BlockSpec((bb, B_HEADS, QK_DIM), lambda i: (i, 0, 0)),
            pl.BlockSpec((bb, 1, B_HEADS), lambda i: (i, 0, 0)),
        ],
        out_shape=(jax.ShapeDtypeStruct((B, B_HEADS, V_DIM), jnp.float32),
                   jax.ShapeDtypeStruct((B, B_HEADS, QK_DIM, V_DIM), jnp.float32),
                   jax.ShapeDtypeStruct((B, B_HEADS, QK_DIM), jnp.float32),
                   jax.ShapeDtypeStruct((B, 1, B_HEADS), jnp.float32)),
        compiler_params=_params("arbitrary"),
        name="mlstm_sample",
    )(q3, k3, q3.transpose(0, 2, 1), k3.transpose(0, 2, 1), vb.reshape(B, B_HEADS, V_DIM),
      ob.reshape(B, B_HEADS, V_DIM), row8(ib), row8(fb), row8(state_m),
      state_n.reshape(-1, B_HEADS, QK_DIM), state_C.reshape(-1, B_HEADS, QK_DIM, V_DIM))
    return y.reshape(B, B_WIDTH), cn, nn, mn.reshape(B, B_HEADS)


(G_QA, G_KVA, G_GA, G_QB, G_KB, G_VB, G_IB, G_FB, G_OB, G_GM) = range(10)
WIDE_ORDER = (G_QA, G_VB, G_OB, G_KVA, G_QB, G_KB, G_GM)
NARROW_ORDER = (G_GA, G_IB, G_FB)
WIDE_OFF = dict(zip(WIDE_ORDER, np.cumsum((0,) + tuple(SPLITS[g] for g in WIDE_ORDER))[:-1].tolist()))
NARROW_OFF = dict(zip(NARROW_ORDER, np.cumsum((0,) + tuple(SPLITS[g] for g in NARROW_ORDER))[:-1].tolist()))
NARROW_W = 128


def project_in(h, w_in, b_in, layer):
    w_t = jnp.swapaxes(w_in, 1, 2)
    b_in = b_in[layer]

    def gather(order, pad_to):
        rows = jnp.concatenate([w_t[layer, BOUNDS[g]:BOUNDS[g + 1]] for g in order], axis=0)
        bias = jnp.concatenate([b_in[BOUNDS[g]:BOUNDS[g + 1]] for g in order], axis=0)
        pad = pad_to - rows.shape[0]
        return jnp.pad(rows, ((0, pad), (0, 0))), jnp.pad(bias, (0, pad))

    n_wide = sum(SPLITS[g] for g in WIDE_ORDER)
    wide = matmul_act(h, *gather(WIDE_ORDER, n_wide), nt=True)
    narrow = matmul_act(h, *gather(NARROW_ORDER, NARROW_W), nt=True, tn=NARROW_W)
    return wide, narrow


def kernel(x_prompt, x_sample, cache_nsa_kv, cache_win_kv, state_C, state_n, state_m, page_table,
           norm_g, w_in, b_in, cmp_pe, cmp_w1, cmp_b1, cmp_w2, w_up_a, w_up_b, w_out, w_mlp1, w_mlp2):
    Bp, Tp = x_prompt.shape[:2]
    Bs, Ts = x_sample.shape[:2]
    assert Bp == 1 and Ts == 1 and Tp >= WINDOW, (x_prompt.shape, x_sample.shape)
    x = jnp.concatenate([x_prompt.reshape(Tp, D_MODEL), x_sample.reshape(Bs, D_MODEL)], axis=0)
    kv_p, kv_s, win_p, win_all = [], [], [], None
    C_p, C_s, n_p, n_s, m_p, m_s = [], [], [], [], [], []
    kv4 = 4 * KVW
    w_out_bf = w_out.astype(jnp.bfloat16)
    w_mlp2_bf = w_mlp2.astype(jnp.bfloat16)
    h_in = rmsnorm_cast(x, norm_g[0, 0])
    for l in range(DEPTH):
        wide, narrow = project_in(h_in, w_in, b_in, l)
        col = lambda g, rows: wide[rows, WIDE_OFF[g]:WIDE_OFF[g] + SPLITS[g]]
        ncol = lambda g, rows: narrow[rows, NARROW_OFF[g]:NARROW_OFF[g] + SPLITS[g]]
        pr, sr = slice(0, Tp), slice(Tp, Tp + Bs)

        kva = lax.optimization_barrier(col(G_KVA, pr))
        ya_p, pe_bias = nsa_prompt_pallas((wide, WIDE_OFF[G_QA]), ncol(G_GA, pr), kva, cmp_pe[l], cmp_w1[l], cmp_b1[l],
                                          cmp_w2[l])
        yb_p, C, n, m = mlstm_prompt_pallas(
            (wide, WIDE_OFF[G_QB] // (B_HEADS * QK_DIM)), (wide, WIDE_OFF[G_KB] // (B_HEADS * QK_DIM)),
            (wide, WIDE_OFF[G_VB] // B_WIDTH),
            (wide, WIDE_OFF[G_OB] // B_WIDTH), (narrow, 0, NARROW_OFF[G_FB]),
            narrow[pr, NARROW_OFF[G_IB]:NARROW_OFF[G_IB] + 2 * B_HEADS].T, Tp)
        kv_p.append(kva[:, 0:kv4].reshape(Bp, Tp, 4, A_KV_HEADS, HEAD_DIM))
        win_p.append(kva[Tp - WINDOW:, kv4:].reshape(Bp, WINDOW, 2, A_KV_HEADS, HEAD_DIM))
        C_p.append(C[None])
        n_p.append(n[None])
        m_p.append(m[None])

        kva = col(G_KVA, sr)
        ya_s, win_all = nsa_decode_pallas(col(G_QA, sr), ncol(G_GA, sr), kva, cache_nsa_kv, page_table, cache_win_kv, l,
                                          cmp_w1[l], cmp_w2[l], pe_bias, win_into=win_all)
        yb_s, C, n, m = mlstm_sample_pallas(col(G_QB, sr), col(G_KB, sr), col(G_VB, sr), ncol(G_IB, sr), ncol(G_FB, sr),
                                            col(G_OB, sr), state_C, state_n, state_m, l)
        kv_s.append(kva[:, 0:kv4].reshape(Bs, Ts, 4, A_KV_HEADS, HEAD_DIM))
        C_s.append(C)
        n_s.append(n)
        m_s.append(m)

        gm_src = (wide, WIDE_OFF[G_GM])
        mix = mix_matmul(ya_p, yb_p, w_up_a, w_up_b, gm_src, row0=0, rows_total=Tp + Bs, layer=l)
        mix = mix_matmul(ya_s, yb_s, w_up_a, w_up_b, gm_src, row0=Tp, rows_total=Tp + Bs, into=mix, layer=l)
        x, h_mlp = matmul_norm_res(mix, w_out_bf, x, norm_g[l, 1], norm_g[l, 2], layer=l)
        hid = matmul_act(h_mlp, w_mlp1, jnp.zeros((D_FF,), jnp.float32), nt=False, act="relu2",
                         out_dtype=jnp.bfloat16, layer=l)
        x, h_in = matmul_norm_res(hid, w_mlp2_bf, x, norm_g[l, 3], norm_g[(l + 1) % DEPTH, 0], layer=l)

    win_s = win_all.reshape(DEPTH, Bs, 2, A_KV_HEADS, HEAD_DIM, WINDOW).transpose(0, 1, 5, 2, 3, 4)
    return (x[:Tp].reshape(Bp, Tp, D_MODEL), x[Tp:].reshape(Bs, Ts, D_MODEL),
            jnp.stack(kv_p), jnp.stack(kv_s), jnp.stack(win_p), win_s,
            jnp.stack(C_p), jnp.stack(C_s), jnp.stack(n_p), jnp.stack(n_s), jnp.stack(m_p), jnp.stack(m_s))
```
